```python
import math
import jax
import jax.numpy as jnp
from jax import lax
import numpy as np

D_MODEL = 2048
BATCH = 4
SEQ = 2048
DEPTH = 1
DEC_BATCH = 128
DEC_SEQ = 8
PAST_LEN = 16384
PAGE_SIZE = 128

MIX_WIDTH = D_MODEL
GDN_WIDTH = MIX_WIDTH // 2
GDN_HEADS = 8
GDN_DK = GDN_WIDTH // GDN_HEADS
GDN_DV = GDN_DK
GDN_CONV = 4
GDN_CHUNK = 64
RWKV_WIDTH = MIX_WIDTH - GDN_WIDTH
RWKV_HEAD = 64
RWKV_HEADS = RWKV_WIDTH // RWKV_HEAD
RWKV_LORA_W = 64
RWKV_LORA_A = 64
RWKV_LORA_G = 128
RWKV_PROJ = 3 * RWKV_WIDTH + RWKV_LORA_W + RWKV_LORA_A + RWKV_LORA_G
D_FF = 5632
FFN_CONV = 3
RMS_EPS = 1e-6
L2_EPS = 1e-12
GN_EPS = 64e-5
OFF_GDN_Z = 3 * GDN_WIDTH
OFF_GDN_B = 4 * GDN_WIDTH
OFF_GDN_A = OFF_GDN_B + GDN_HEADS
OFF_RWKV = OFF_GDN_A + GDN_HEADS
IN_WIDTH = OFF_RWKV + RWKV_PROJ

kernel_name = 'hymba_gdn_rwkv7_convffn_step'


def rms_norm(x, g):
    xf = x.astype(jnp.float32)
    y = xf * lax.rsqrt(jnp.mean(xf * xf, axis=-1, keepdims=True) + RMS_EPS)
    return (y * g.astype(jnp.float32)).astype(x.dtype)


def l2_normalize(x):
    return x * lax.rsqrt(jnp.sum(x * x, axis=-1, keepdims=True) + L2_EPS)


def causal_dwconv(x, buf, w):
    width = w.shape[0]
    t = x.shape[1]
    xp = jnp.concatenate([buf.astype(x.dtype), x], axis=1)
    w = w.astype(x.dtype)
    y = xp[:, width - 1:] * w[width - 1]
    for i in range(width - 1):
        y = y + xp[:, i:i + t] * w[i]
    return y, xp[:, t:]


def chunk_gated_delta(q, k, v, beta, g, s0):
    b, t, h, dk = q.shape
    dv = v.shape[-1]
    c = min(GDN_CHUNK, t)
    n = -(-t // c)
    pad = n * c - t

    def blocks(z):
        z = jnp.pad(z, [(0, 0), (0, pad)] + [(0, 0)] * (z.ndim - 2))
        z = z.reshape((b, n, c) + z.shape[2:])
        return jnp.moveaxis(z, 3, 2)

    qc, kc, vc, bc, gc = (blocks(z) for z in (q, k, v, beta, g))
    G = jnp.cumsum(gc, axis=-1)
    idx = jnp.arange(c)
    causal = idx[:, None] >= idx[None, :]
    strict = idx[:, None] > idx[None, :]
    diff = G[..., :, None] - G[..., None, :]
    decay = jnp.where(causal, jnp.exp(jnp.where(causal, diff, 0.0)), 0.0)
    kkt = jnp.einsum('bnhid,bnhjd->bnhij', kc, kc)
    lower = jnp.where(strict, bc[..., :, None] * kkt * decay, 0.0) + jnp.eye(c, dtype=q.dtype)
    gamma = jnp.exp(G)
    rhs = jnp.concatenate([(bc * gamma)[..., None] * kc, bc[..., None] * vc], axis=-1)
    sol = lax.linalg.triangular_solve(lower, rhs, left_side=True, lower=True, unit_diagonal=True)
    w_c, u0_c = sol[..., :dk], sol[..., dk:]
    qk = jnp.einsum('bnhid,bnhjd->bnhij', qc, kc) * decay
    qg = qc * gamma[..., None]
    kt = kc * jnp.exp(G[..., -1:] - G)[..., None]
    gl = jnp.exp(G[..., -1])

    def step(s, inp):
        w_, u0_, qk_, qg_, kt_, gl_ = inp
        u = u0_ - jnp.einsum('bhcd,bhdv->bhcv', w_, s)
        o = jnp.einsum('bhcd,bhdv->bhcv', qg_, s) + jnp.einsum('bhij,bhjv->bhiv', qk_, u)
        s = gl_[..., None, None] * s + jnp.einsum('bhcd,bhcv->bhdv', kt_, u)
        return s, o

    xs = tuple(jnp.moveaxis(z, 1, 0) for z in (w_c, u0_c, qk, qg, kt, gl))
    s, o = lax.scan(step, s0, xs)
    o = jnp.transpose(o, (1, 0, 3, 2, 4)).reshape(b, n * c, h, dv)[:, :t]
    return o, s


def gdn_mixer(p_qkv, p_z, p_beta, p_a, conv_buf, s0, conv_w, a_log, dt_bias, norm_g):
    f32 = jnp.float32
    b, t, _ = p_qkv.shape
    qkv, conv_new = causal_dwconv(p_qkv, conv_buf, conv_w)
    qkv = jax.nn.silu(qkv.astype(f32)).reshape(b, t, 3, GDN_HEADS, GDN_DK)
    q = l2_normalize(qkv[:, :, 0]) * (GDN_DK ** -0.5)
    k = l2_normalize(qkv[:, :, 1])
    v = qkv[:, :, 2]
    beta = jax.nn.sigmoid(p_beta.astype(f32))
    g = -jnp.exp(a_log.astype(f32)) * jax.nn.softplus(p_a.astype(f32) + dt_bias.astype(f32))
    o, s_new = chunk_gated_delta(q, k, v, beta, g, s0.astype(f32))
    o = o * lax.rsqrt(jnp.mean(o * o, axis=-1, keepdims=True) + RMS_EPS) * norm_g.astype(f32)
    o = o * jax.nn.silu(p_z.astype(f32).reshape(b, t, GDN_HEADS, GDN_DV))
    return o.reshape(b, t, GDN_WIDTH), conv_new, s_new


def rwkv7_scan(r, decay, k, v, kk, kka, s0):
    def step(s, inp):
        r_t, w_t, k_t, v_t, kk_t, kka_t = inp
        sk = jnp.einsum('bhvk,bhk->bhv', s, kk_t)
        s = s * w_t[:, :, None, :] - sk[..., None] * kka_t[:, :, None, :] + v_t[..., None] * k_t[:, :, None, :]
        return s, jnp.einsum('bhvk,bhk->bhv', s, r_t)

    xs = tuple(jnp.moveaxis(z, 1, 0) for z in (r, decay, k, v, kk, kka))
    s, y = lax.scan(step, s0, xs)
    return jnp.moveaxis(y, 0, 1), s


def rwkv_mixer(p, shift_buf, s0, mu, w0, w_b, a0, a_b, g_b, k_k, k_a, r_k, gn_w, gn_b):
    f32 = jnp.float32
    b, t, _ = p.shape
    prev = jnp.concatenate([shift_buf[:, None].astype(p.dtype), p[:, :-1]], axis=1)
    xs = (p + (prev - p) * mu.astype(p.dtype)).astype(f32)
    splits = [RWKV_WIDTH, 2 * RWKV_WIDTH, 3 * RWKV_WIDTH, 3 * RWKV_WIDTH + RWKV_LORA_W,
              3 * RWKV_WIDTH + RWKV_LORA_W + RWKV_LORA_A]
    r, k, v, wd, ad, gd = jnp.split(xs, splits, axis=-1)
    w = -jax.nn.softplus(-(w0.astype(f32) + jnp.tanh(wd) @ w_b.astype(f32))) - 0.5
    decay = jnp.exp(-jnp.exp(w))
    a = jax.nn.sigmoid(a0.astype(f32) + ad @ a_b.astype(f32))
    gate = jax.nn.sigmoid(gd) @ g_b.astype(f32)

    def heads(z):
        return z.reshape(b, t, RWKV_HEADS, RWKV_HEAD)

    kk = l2_normalize(heads(k * k_k.astype(f32)))
    k = k * (1.0 + (a - 1.0) * k_a.astype(f32))
    r, k, v, decay, a = heads(r), heads(k), heads(v), heads(decay), heads(a)
    y, s_new = rwkv7_scan(r, decay, k, v, kk, kk * a, s0.astype(f32))
    mean = jnp.mean(y, axis=-1, keepdims=True)
    var = jnp.mean(jnp.square(y - mean), axis=-1, keepdims=True)
    y = (y - mean) * lax.rsqrt(var + GN_EPS) * gn_w.astype(f32).reshape(RWKV_HEADS, RWKV_HEAD) \
        + gn_b.astype(f32).reshape(RWKV_HEADS, RWKV_HEAD)
    y = y + jnp.sum(r * k * r_k.astype(f32), axis=-1, keepdims=True) * v
    return y.reshape(b, t, RWKV_WIDTH) * gate, p[:, -1], s_new


def conv_ffn(x, buf, w_up, conv_w, w_down):
    h, buf_new = causal_dwconv(x @ w_up, buf, conv_w)
    gate, up = jnp.split(h, 2, axis=-1)
    return (jax.nn.silu(gate) * up) @ w_down, buf_new


def layer(x, s_gdn, s_gconv, s_rwkv, s_shift, s_ffn,
          ln1_g, w_in, gdn_conv_w, gdn_a_log, gdn_dt_bias, gdn_norm_g,
          rwkv_mu, rwkv_w0, rwkv_w_b, rwkv_a0, rwkv_a_b, rwkv_g_b, rwkv_k_k, rwkv_k_a, rwkv_r_k,
          rwkv_gn_w, rwkv_gn_b, w_o, ln2_g, ffn_w_up, ffn_conv_w, ffn_w_down):
    proj = rms_norm(x, ln1_g) @ w_in
    o_a, gconv_new, gdn_new = gdn_mixer(
        proj[..., :OFF_GDN_Z], proj[..., OFF_GDN_Z:OFF_GDN_B], proj[..., OFF_GDN_B:OFF_GDN_A],
        proj[..., OFF_GDN_A:OFF_RWKV], s_gconv, s_gdn, gdn_conv_w, gdn_a_log, gdn_dt_bias, gdn_norm_g)
    o_b, shift_new, rwkv_new = rwkv_mixer(
        proj[..., OFF_RWKV:], s_shift, s_rwkv, rwkv_mu, rwkv_w0, rwkv_w_b, rwkv_a0, rwkv_a_b,
        rwkv_g_b, rwkv_k_k, rwkv_k_a, rwkv_r_k, rwkv_gn_w, rwkv_gn_b)
    mixed = jnp.concatenate([o_a, o_b], axis=-1).astype(x.dtype)
    x = x + mixed @ w_o
    f, ffn_new = conv_ffn(rms_norm(x, ln2_g), s_ffn, ffn_w_up, ffn_conv_w, ffn_w_down)
    return x + f, gdn_new, gconv_new, rwkv_new, shift_new, ffn_new


def setup_inputs(seed: int = 0) -> dict:
    key = jax.random.key(seed)
    keys = iter(jax.random.split(key, 40))

    def nrm(shape, scale):
        return scale * jax.random.normal(next(keys), shape, jnp.float32)

    def uni(shape, lo, hi):
        return jax.random.uniform(next(keys), shape, jnp.float32, lo, hi)

    L = DEPTH
    dt = jnp.exp(uni((L, GDN_HEADS), math.log(1e-3), math.log(1e-1)))
    return {
        'x_prompt': nrm((BATCH, SEQ, D_MODEL), 1.0),
        'x_sample': nrm((DEC_BATCH, DEC_SEQ, D_MODEL), 1.0),
        'state_gdn': nrm((L, DEC_BATCH, GDN_HEADS, GDN_DK, GDN_DV), 0.1),
        'state_gdn_conv': nrm((L, DEC_BATCH, GDN_CONV - 1, 3 * GDN_WIDTH), 1.0),
        'state_rwkv': nrm((L, DEC_BATCH, RWKV_HEADS, RWKV_HEAD, RWKV_HEAD), 0.1),
        'state_rwkv_shift': nrm((L, DEC_BATCH, RWKV_PROJ), 1.0),
        'state_ffn_conv': nrm((L, DEC_BATCH, FFN_CONV - 1, 2 * D_FF), 1.0),
        'ln1_g': 1.0 + nrm((L, D_MODEL), 0.02),
        'w_in': nrm((L, D_MODEL, IN_WIDTH), D_MODEL ** -0.5),
        'gdn_conv_w': nrm((L, GDN_CONV, 3 * GDN_WIDTH), GDN_CONV ** -0.5),
        'gdn_a_log': jnp.log(uni((L, GDN_HEADS), 1.0, 16.0)),
        'gdn_dt_bias': dt + jnp.log(-jnp.expm1(-dt)),
        'gdn_norm_g': 1.0 + nrm((L, GDN_DV), 0.02),
        'rwkv_mu': uni((L, RWKV_PROJ), 0.0, 1.0),
        'rwkv_w0': uni((L, RWKV_WIDTH), -6.0, 1.0),
        'rwkv_w_b': nrm((L, RWKV_LORA_W, RWKV_WIDTH), 0.5 * RWKV_LORA_W ** -0.5),
        'rwkv_a0': nrm((L, RWKV_WIDTH), 0.5),
        'rwkv_a_b': nrm((L, RWKV_LORA_A, RWKV_WIDTH), RWKV_LORA_A ** -0.5),
        'rwkv_g_b': nrm((L, RWKV_LORA_G, RWKV_WIDTH), RWKV_LORA_G ** -0.5),
        'rwkv_k_k': 0.85 + nrm((L, RWKV_WIDTH), 0.02),
        'rwkv_k_a': 1.0 + nrm((L, RWKV_WIDTH), 0.02),
        'rwkv_r_k': nrm((L, RWKV_HEADS, RWKV_HEAD), 0.1),
        'rwkv_gn_w': 1.0 + nrm((L, RWKV_WIDTH), 0.02),
        'rwkv_gn_b': nrm((L, RWKV_WIDTH), 0.02),
        'w_o': nrm((L, MIX_WIDTH, D_MODEL), MIX_WIDTH ** -0.5),
        'ln2_g': 1.0 + nrm((L, D_MODEL), 0.02),
        'ffn_w_up': nrm((L, D_MODEL, 2 * D_FF), D_MODEL ** -0.5),
        'ffn_conv_w': nrm((L, FFN_CONV, 2 * D_FF), FFN_CONV ** -0.5),
        'ffn_w_down': nrm((L, D_FF, D_MODEL), D_FF ** -0.5),
        'final_g': 1.0 + nrm((D_MODEL,), 0.02),
    }


def reference(x_prompt, x_sample, state_gdn, state_gdn_conv, state_rwkv, state_rwkv_shift, state_ffn_conv,
              ln1_g, w_in, gdn_conv_w, gdn_a_log, gdn_dt_bias, gdn_norm_g,
              rwkv_mu, rwkv_w0, rwkv_w_b, rwkv_a0, rwkv_a_b, rwkv_g_b, rwkv_k_k, rwkv_k_a, rwkv_r_k,
              rwkv_gn_w, rwkv_gn_b, w_o, ln2_g, ffn_w_up, ffn_conv_w, ffn_w_down, final_g):
    params = (ln1_g, w_in, gdn_conv_w, gdn_a_log, gdn_dt_bias, gdn_norm_g,
              rwkv_mu, rwkv_w0, rwkv_w_b, rwkv_a0, rwkv_a_b, rwkv_g_b, rwkv_k_k, rwkv_k_a, rwkv_r_k,
              rwkv_gn_w, rwkv_gn_b, w_o, ln2_g, ffn_w_up, ffn_conv_w, ffn_w_down)

    def trunk(x, states):
        new = [[] for _ in states]
        for l in range(DEPTH):
            x, *st = layer(x, *[s[l] for s in states], *[p[l] for p in params])
            for acc, s in zip(new, st):
                acc.append(s)
        return rms_norm(x, final_g), [jnp.stack(acc) for acc in new]

    sample_states = (state_gdn, state_gdn_conv, state_rwkv, state_rwkv_shift, state_ffn_conv)
    prompt_states = tuple(jnp.zeros((DEPTH, BATCH) + s.shape[2:], jnp.float32) for s in sample_states)
    y_prompt, (gdn_p, gconv_p, rwkv_p, shift_p, ffn_p) = trunk(x_prompt, prompt_states)
    y_sample, (gdn_s, gconv_s, rwkv_s, shift_s, ffn_s) = trunk(x_sample, sample_states)
    return (y_prompt, y_sample, gdn_p, gconv_p, rwkv_p, shift_p, ffn_p, gdn_s, gconv_s, rwkv_s, shift_s, ffn_s)
```

```python
import functools

import jax
import jax.numpy as jnp
from jax import lax
from jax.experimental import pallas as pl
from jax.experimental.pallas import tpu as pltpu

F32 = jnp.float32
BF16 = jnp.bfloat16

D_MODEL = 2048
GDN_WIDTH = 1024
GDN_HEADS = 8
GDN_DK = 128
GDN_CONV = 4
RWKV_WIDTH = 1024
RWKV_HEAD = 64
RWKV_HEADS = 16
RWKV_LORA_W = 64
RWKV_LORA_A = 64
RWKV_LORA_G = 128
RWKV_LORA = RWKV_LORA_W + RWKV_LORA_A + RWKV_LORA_G
RWKV_PROJ = 3 * RWKV_WIDTH + RWKV_LORA
D_FF = 5632
FFN_CONV = 3
RMS_EPS = 1e-6
L2_EPS = 1e-12
GN_EPS = 64e-5

REF_OFF_Z = 3 * GDN_WIDTH
REF_OFF_B = 4 * GDN_WIDTH
REF_OFF_RWKV = REF_OFF_B + 2 * GDN_HEADS
REF_IN_WIDTH = REF_OFF_RWKV + RWKV_PROJ

LANES = 128
SUBLANES = 8
COL_QKV = 0
COL_RKV = 3 * GDN_WIDTH
COL_Z = COL_RKV + 3 * RWKV_WIDTH
COL_LORA = COL_Z + GDN_WIDTH
COL_BA = COL_LORA + RWKV_LORA
PROJ_WIDTH = 7680
PROJ_TN = 1280

GDN_CHUNK = 64
RWKV_CHUNK = 64

NN = (((1,), (0,)), ((), ()))
NT = (((1,), (1,)), ((), ()))
TN = (((0,), (0,)), ((), ()))

VMEM_LIMIT = 56 * 1024 * 1024


def _dot(a, b, dims=NN, exact=False):
    if exact:
        return lax.dot_general(a, b, dims, precision=lax.Precision.HIGHEST,
                               preferred_element_type=F32)
    return lax.dot_general(a.astype(BF16), b.astype(BF16), dims, preferred_element_type=F32)


def _sigmoid(x):
    return 1.0 / (1.0 + jnp.exp(-x))


def _silu(x):
    return x * _sigmoid(x)


def _softplus(x):
    return jnp.maximum(x, 0.0) + jnp.log(1.0 + jnp.exp(-jnp.abs(x)))


def _tri(n, strict):
    r = lax.broadcasted_iota(jnp.int32, (n, n), 0)
    c = lax.broadcasted_iota(jnp.int32, (n, n), 1)
    return (r > c) if strict else (r >= c)


def _unit_lower_inverse(a, n):
    eye = jnp.where(_tri(n, False) & ~_tri(n, True), 1.0, 0.0).astype(F32)
    x = -a
    t = eye + x
    p = x
    k = 1
    while 2 * k < n:
        p = _dot(p, p, exact=True)
        t = t + _dot(t, p, exact=True)
        k *= 2
    return t


def _inproj_kernel(x_ref, g_ref, w_ref, o_ref, xn_ref):
    @pl.when(pl.program_id(1) == 0)
    def _():
        x = x_ref[...]
        ms = jnp.mean(x * x, axis=-1, keepdims=True)
        xn_ref[...] = (x * lax.rsqrt(ms + RMS_EPS) * g_ref[...]).astype(BF16)

    o_ref[...] = jnp.dot(xn_ref[...], w_ref[...], preferred_element_type=F32)


def _inproj(x2d, g_row, w_bf16, tm):
    m = x2d.shape[0]
    return pl.pallas_call(
        _inproj_kernel,
        grid=(m // tm, PROJ_WIDTH // PROJ_TN),
        in_specs=[
            pl.BlockSpec((tm, D_MODEL), lambda i, j: (i, 0)),
            pl.BlockSpec((1, D_MODEL), lambda i, j: (0, 0)),
            pl.BlockSpec((D_MODEL, PROJ_TN), lambda i, j: (0, j)),
        ],
        out_specs=pl.BlockSpec((tm, PROJ_TN), lambda i, j: (i, j)),
        out_shape=jax.ShapeDtypeStruct((m, PROJ_WIDTH), F32),
        scratch_shapes=[pltpu.VMEM((tm, D_MODEL), BF16)],
        compiler_params=pltpu.CompilerParams(
            dimension_semantics=("arbitrary", "arbitrary"), vmem_limit_bytes=VMEM_LIMIT),
        name="inproj",
    )(x2d, g_row, w_bf16)


def _gdn_kernel(qkv_ref, z_ref, ba_ref, cbuf_ref, s0_ref, convw_ref, alog_r_ref, dtb_r_ref,
                alog_c_ref, dtb_c_ref, ng_ref, o_ref, s_ref, xp_ref, *, chunk):
    C = chunk
    c = pl.program_id(1)
    width = 3 * GDN_WIDTH

    @pl.when(c == 0)
    def _():
        xp_ref[0:SUBLANES, :] = jnp.zeros((SUBLANES, width), F32)
        xp_ref[SUBLANES - (GDN_CONV - 1):SUBLANES, :] = cbuf_ref[...]
        s_ref[...] = s0_ref[...]

    @pl.when(c > 0)
    def _():
        xp_ref[0:SUBLANES, :] = xp_ref[C:C + SUBLANES, :]

    xp_ref[SUBLANES:SUBLANES + C, :] = qkv_ref[...]

    cw = convw_ref[...]
    acc = qkv_ref[...] * cw[GDN_CONV - 1:GDN_CONV, :]
    for i in range(GDN_CONV - 1):
        off = SUBLANES - (GDN_CONV - 1) + i
        acc = acc + xp_ref[off:off + C, :] * cw[i:i + 1, :]
    qkv = _silu(acc)

    ba = ba_ref[...]
    ba_t = ba.T
    beta_c = _sigmoid(ba)
    g_c = -jnp.exp(alog_r_ref[...]) * _softplus(ba + dtb_r_ref[...])
    g_r = -jnp.exp(alog_c_ref[...]) * _softplus(ba_t + dtb_c_ref[...])
    lower = jnp.where(_tri(C, False), 1.0, 0.0).astype(F32)
    gc_all = _dot(lower, g_c, exact=True)
    gr_all = _dot(g_r, lower, NT, exact=True)

    causal = _tri(C, False)
    strict = _tri(C, True)
    ng = ng_ref[...]
    for h in range(GDN_HEADS):
        lo = h * GDN_DK
        q = qkv[:, lo:lo + GDN_DK]
        k = qkv[:, GDN_WIDTH + lo:GDN_WIDTH + lo + GDN_DK]
        v = qkv[:, 2 * GDN_WIDTH + lo:2 * GDN_WIDTH + lo + GDN_DK]
        q = q * lax.rsqrt(jnp.sum(q * q, axis=-1, keepdims=True) + L2_EPS) * (GDN_DK ** -0.5)
        k = k * lax.rsqrt(jnp.sum(k * k, axis=-1, keepdims=True) + L2_EPS)
        beta = beta_c[:, h:h + 1]
        gcol = gc_all[:, GDN_HEADS + h:GDN_HEADS + h + 1]
        grow = gr_all[GDN_HEADS + h:GDN_HEADS + h + 1, :]
        glast = gcol[C - 1:C, :]
        decay = jnp.where(causal, jnp.exp(jnp.where(causal, gcol - grow, 0.0)), 0.0)
        kq = _dot(jnp.concatenate([k, q], axis=0), k, NT, exact=True)
        a_mat = jnp.where(strict, beta * kq[:C] * decay, 0.0)
        t_inv = _unit_lower_inverse(a_mat, C)
        gamma = jnp.exp(gcol)
        rhs = jnp.concatenate([(beta * gamma) * k, beta * v], axis=1)
        sol = _dot(t_inv, rhs, exact=True)
        w_c = sol[:, :GDN_DK]
        u0 = sol[:, GDN_DK:]
        qk = kq[C:] * decay
        qg = q * gamma
        kt = k * jnp.exp(glast - gcol)
        s = s_ref[h]
        ws = _dot(jnp.concatenate([w_c, qg], axis=0), s, exact=True)
        u = u0 - ws[:C]
        o = ws[C:] + _dot(qk, u, exact=True)
        s_ref[h] = jnp.exp(glast) * s + _dot(kt, u, TN, exact=True)
        o = o * lax.rsqrt(jnp.mean(o * o, axis=-1, keepdims=True) + RMS_EPS) * ng
        o = o * _silu(z_ref[:, lo:lo + GDN_DK])
        o_ref[:, lo:lo + GDN_DK] = o.astype(o_ref.dtype)


def _gdn_mixer(proj, cbuf, s0, convw, alog_r, dtb_r, alog_c, dtb_c, ng, chunk):
    b, t, _ = proj.shape
    n = t // chunk
    width = 3 * GDN_WIDTH
    const2 = lambda i, c: (0, 0)
    return pl.pallas_call(
        functools.partial(_gdn_kernel, chunk=chunk),
        grid=(b, n),
        in_specs=[
            pl.BlockSpec((None, chunk, width), lambda i, c: (i, c, COL_QKV // width)),
            pl.BlockSpec((None, chunk, GDN_WIDTH), lambda i, c: (i, c, COL_Z // GDN_WIDTH)),
            pl.BlockSpec((None, chunk, LANES), lambda i, c: (i, c, COL_BA // LANES)),
            pl.BlockSpec((None, GDN_CONV - 1, width), lambda i, c: (i, 0, 0)),
            pl.BlockSpec((None, GDN_HEADS, GDN_DK, GDN_DK), lambda i, c: (i, 0, 0, 0)),
            pl.BlockSpec((GDN_CONV, width), const2),
            pl.BlockSpec((1, LANES), const2),
            pl.BlockSpec((1, LANES), const2),
            pl.BlockSpec((LANES, 1), const2),
            pl.BlockSpec((LANES, 1), const2),
            pl.BlockSpec((1, GDN_DK), const2),
        ],
        out_specs=[
            pl.BlockSpec((None, chunk, GDN_WIDTH), lambda i, c: (i, c, 0)),
            pl.BlockSpec((None, GDN_HEADS, GDN_DK, GDN_DK), lambda i, c: (i, 0, 0, 0)),
        ],
        out_shape=[
            jax.ShapeDtypeStruct((b, t, GDN_WIDTH), BF16),
            jax.ShapeDtypeStruct((b, GDN_HEADS, GDN_DK, GDN_DK), F32),
        ],
        scratch_shapes=[pltpu.VMEM((SUBLANES + chunk, width), F32)],
        compiler_params=pltpu.CompilerParams(
            dimension_semantics=("arbitrary", "arbitrary"), vmem_limit_bytes=VMEM_LIMIT),
        name="gdn_mixer",
    )(proj, proj, proj, cbuf, s0, convw, alog_r, dtb_r, alog_c, dtb_c, ng)


def _rwkv_kernel(rkv_ref, lora_ref, sh_rkv_ref, sh_lora_ref, s0_ref, mu_rkv_ref, mu_lora_ref,
                 w0_ref, a0_ref, wab_ref, gb_ref, kk_ref, ka_ref, rk_ref, gnw_ref, gnb_ref,
                 o_ref, s_ref, xr_ref, xl_ref, *, chunk):
    C = chunk
    c = pl.program_id(1)
    W = RWKV_WIDTH

    @pl.when(c == 0)
    def _():
        xr_ref[0:SUBLANES, :] = jnp.zeros((SUBLANES, 3 * W), F32)
        xl_ref[0:SUBLANES, :] = jnp.zeros((SUBLANES, RWKV_LORA), F32)
        xr_ref[SUBLANES - 1:SUBLANES, :] = sh_rkv_ref[...]
        xl_ref[SUBLANES - 1:SUBLANES, :] = sh_lora_ref[...]
        s_ref[...] = s0_ref[...]

    @pl.when(c > 0)
    def _():
        xr_ref[0:SUBLANES, :] = xr_ref[C:C + SUBLANES, :]
        xl_ref[0:SUBLANES, :] = xl_ref[C:C + SUBLANES, :]

    xr_ref[SUBLANES:SUBLANES + C, :] = rkv_ref[...]
    xl_ref[SUBLANES:SUBLANES + C, :] = lora_ref[...]

    p = rkv_ref[...]
    xs = p + (xr_ref[SUBLANES - 1:SUBLANES - 1 + C, :] - p) * mu_rkv_ref[...]
    pl_ = lora_ref[...]
    xl = pl_ + (xl_ref[SUBLANES - 1:SUBLANES - 1 + C, :] - pl_) * mu_lora_ref[...]
    r = xs[:, :W]
    k = xs[:, W:2 * W]
    v = xs[:, 2 * W:]

    wa_in = xl[:, :LANES]
    lane = lax.broadcasted_iota(jnp.int32, (C, LANES), 1)
    wa_in = jnp.where(lane < RWKV_LORA_W, jnp.tanh(wa_in), wa_in)
    wa = _dot(wa_in, wab_ref[...])
    w = -_softplus(-(w0_ref[...] + wa[:, :W])) - 0.5
    lw = -jnp.exp(w)
    a = _sigmoid(a0_ref[...] + wa[:, W:])
    gate = _dot(_sigmoid(xl[:, LANES:]), gb_ref[...])
    kk_raw = k * kk_ref[...]
    k2 = k * (1.0 + (a - 1.0) * ka_ref[...])
    rkr = r * k2 * rk_ref[...]

    lower = jnp.where(_tri(C, False), 1.0, 0.0).astype(F32)
    lc = _dot(lower, lw, exact=True)
    llast = lc[C - 1:C, :]
    e_inc = jnp.exp(lc)
    e_exc = jnp.exp(lc - lw)
    e_inv = jnp.exp(-lc)
    e_rem = jnp.exp(llast - lc)
    p_all = jnp.exp(llast)

    causal = _tri(C, False)
    strict = _tri(C, True)
    for h in range(RWKV_HEADS):
        sl = slice(h * RWKV_HEAD, (h + 1) * RWKV_HEAD)
        kk = kk_raw[:, sl]
        kk = kk * lax.rsqrt(jnp.sum(kk * kk, axis=-1, keepdims=True) + L2_EPS)
        kka = kk * a[:, sl]
        k2h = k2[:, sl]
        vh = v[:, sl]
        x = jnp.concatenate([kk * e_exc[:, sl], r[:, sl] * e_inc[:, sl]], axis=0)
        y = jnp.concatenate([kka * e_inv[:, sl], k2h * e_inv[:, sl]], axis=0)
        aa = _dot(x, y, NT, exact=True)
        s0 = s_ref[h]
        xm = _dot(x, s0, NT, exact=True)
        a_bc = jnp.where(strict, aa[:C, :C], 0.0)
        a_kc = jnp.where(strict, aa[:C, C:], 0.0)
        q_bk = jnp.concatenate([jnp.where(causal, aa[C:, :C], 0.0),
                                jnp.where(causal, aa[C:, C:], 0.0)], axis=1)
        t_inv = _unit_lower_inverse(a_bc, C)
        u = _dot(t_inv, -(xm[:C] + _dot(a_kc, vh, exact=True)), exact=True)
        uv = jnp.concatenate([u, vh], axis=0)
        yh = xm[C:] + _dot(q_bk, uv, exact=True)
        bk = jnp.concatenate([kka * e_rem[:, sl], k2h * e_rem[:, sl]], axis=0)
        s_ref[h] = s0 * p_all[:, sl] + _dot(uv, bk, TN, exact=True)
        mean = jnp.mean(yh, axis=-1, keepdims=True)
        yc = yh - mean
        var = jnp.mean(yc * yc, axis=-1, keepdims=True)
        yn = yc * lax.rsqrt(var + GN_EPS) * gnw_ref[:, sl] + gnb_ref[:, sl]
        yn = yn + jnp.sum(rkr[:, sl], axis=-1, keepdims=True) * vh
        o_ref[:, sl] = (yn * gate[:, sl]).astype(o_ref.dtype)


def _rwkv_mixer(proj, sh_rkv, sh_lora, s0, mu_rkv, mu_lora, w0, a0, wab, gb, kk, ka, rk, gnw, gnb,
                chunk):
    b, t, _ = proj.shape
    n = t // chunk
    W = RWKV_WIDTH
    const2 = lambda i, c: (0, 0)
    row = lambda width: pl.BlockSpec((1, width), const2)
    return pl.pallas_call(
        functools.partial(_rwkv_kernel, chunk=chunk),
        grid=(b, n),
        in_specs=[
            pl.BlockSpec((None, chunk, 3 * W), lambda i, c: (i, c, COL_RKV // (3 * W))),
            pl.BlockSpec((None, chunk, RWKV_LORA), lambda i, c: (i, c, COL_LORA // RWKV_LORA)),
            pl.BlockSpec((None, 1, 3 * W), lambda i, c: (i, 0, 0)),
            pl.BlockSpec((None, 1, RWKV_LORA), lambda i, c: (i, 0, 0)),
            pl.BlockSpec((None, RWKV_HEADS, RWKV_HEAD, RWKV_HEAD), lambda i, c: (i, 0, 0, 0)),
            row(3 * W), row(RWKV_LORA), row(W), row(W),
            pl.BlockSpec((LANES, 2 * W), const2),
            pl.BlockSpec((RWKV_LORA_G, W), const2),
            row(W), row(W), row(W), row(W), row(W),
        ],
        out_specs=[
            pl.BlockSpec((None, chunk, W), lambda i, c: (i, c, 0)),
            pl.BlockSpec((None, RWKV_HEADS, RWKV_HEAD, RWKV_HEAD), lambda i, c: (i, 0, 0, 0)),
        ],
        out_shape=[
            jax.ShapeDtypeStruct((b, t, W), BF16),
            jax.ShapeDtypeStruct((b, RWKV_HEADS, RWKV_HEAD, RWKV_HEAD), F32),
        ],
        scratch_shapes=[pltpu.VMEM((SUBLANES + chunk, 3 * W), F32),
                        pltpu.VMEM((SUBLANES + chunk, RWKV_LORA), F32)],
        compiler_params=pltpu.CompilerParams(
            dimension_semantics=("arbitrary", "arbitrary"), vmem_limit_bytes=VMEM_LIMIT),
        name="rwkv_mixer",
    )(proj, proj, sh_rkv, sh_lora, s0, mu_rkv, mu_lora, w0, a0, wab, gb, kk, ka, rk, gnw, gnb)


def _outproj_kernel(x_ref, oa_ref, ob_ref, wa_ref, wb_ref, g_ref, x1_ref, hn_ref):
    x1 = (x_ref[...] + jnp.dot(oa_ref[...], wa_ref[...], preferred_element_type=F32)
          + jnp.dot(ob_ref[...], wb_ref[...], preferred_element_type=F32))
    x1_ref[...] = x1
    ms = jnp.mean(x1 * x1, axis=-1, keepdims=True)
    hn_ref[...] = (x1 * lax.rsqrt(ms + RMS_EPS) * g_ref[...]).astype(BF16)


def _outproj(x2d, oa, ob, wo_bf16, g_row, tm):
    m = x2d.shape[0]
    return pl.pallas_call(
        _outproj_kernel,
        grid=(m // tm,),
        in_specs=[
            pl.BlockSpec((tm, D_MODEL), lambda i: (i, 0)),
            pl.BlockSpec((tm, GDN_WIDTH), lambda i: (i, 0)),
            pl.BlockSpec((tm, RWKV_WIDTH), lambda i: (i, 0)),
            pl.BlockSpec((GDN_WIDTH, D_MODEL), lambda i: (0, 0)),
            pl.BlockSpec((RWKV_WIDTH, D_MODEL), lambda i: (1, 0)),
            pl.BlockSpec((1, D_MODEL), lambda i: (0, 0)),
        ],
        out_specs=[pl.BlockSpec((tm, D_MODEL), lambda i: (i, 0)),
                   pl.BlockSpec((tm, D_MODEL), lambda i: (i, 0))],
        out_shape=[jax.ShapeDtypeStruct((m, D_MODEL), F32),
                   jax.ShapeDtypeStruct((m, D_MODEL), BF16)],
        compiler_params=pltpu.CompilerParams(
            dimension_semantics=("arbitrary",), vmem_limit_bytes=VMEM_LIMIT),
        name="outproj",
    )(x2d, oa, ob, wo_bf16, wo_bf16, g_row)


FFN_TF = 512
FFN_NF = D_FF // FFN_TF


def _ffn_tail(f, act, wd_ref, x1_ref, fg_ref, y_ref, acc_ref):
    contrib = jnp.dot(act.astype(BF16), wd_ref[...], preferred_element_type=F32)

    @pl.when(f == 0)
    def _():
        acc_ref[...] = contrib

    @pl.when(f > 0)
    def _():
        acc_ref[...] += contrib

    @pl.when(f == FFN_NF - 1)
    def _():
        xo = x1_ref[...] + acc_ref[...]
        ms = jnp.mean(xo * xo, axis=-1, keepdims=True)
        y_ref[...] = xo * lax.rsqrt(ms + RMS_EPS) * fg_ref[...]


def _ffn_long_kernel(hn_ref, x1_ref, wg_ref, wu_ref, cwg_ref, cwu_ref, wd_ref, fg_ref,
                     y_ref, ng_ref, nu_ref, acc_ref, carry_ref, hbuf_ref, *, tt):
    ti = pl.program_id(1)
    f = pl.program_id(2)
    hn = hn_ref[...]
    convs = []
    for j, (w_ref, cw_ref, n_ref) in enumerate(((wg_ref, cwg_ref, ng_ref), (wu_ref, cwu_ref, nu_ref))):
        h = jnp.dot(hn, w_ref[...], preferred_element_type=F32)
        hbuf_ref[j, SUBLANES:SUBLANES + tt, :] = h
        prev = carry_ref[f, j]
        hbuf_ref[j, 0:SUBLANES, :] = jnp.where(ti == 0, jnp.zeros_like(prev), prev)
        cw = cw_ref[...]
        conv = h * cw[FFN_CONV - 1:FFN_CONV, :]
        for i in range(FFN_CONV - 1):
            off = SUBLANES - (FFN_CONV - 1) + i
            conv = conv + hbuf_ref[j, off:off + tt, :] * cw[i:i + 1, :]
        carry_ref[f, j] = h[tt - SUBLANES:, :]
        n_ref[...] = h[tt - (FFN_CONV - 1):, :]
        convs.append(conv)
    act = _silu(convs[0]) * convs[1]
    _ffn_tail(f, act, wd_ref, x1_ref, fg_ref, y_ref, acc_ref)


def _ffn_long(hn, x1, wup, cw, wdown, fg_row, tt):
    b, t, _ = hn.shape
    tf = FFN_TF
    nf = FFN_NF
    return pl.pallas_call(
        functools.partial(_ffn_long_kernel, tt=tt),
        grid=(b, t // tt, nf),
        in_specs=[
            pl.BlockSpec((None, tt, D_MODEL), lambda i, s, f: (i, s, 0)),
            pl.BlockSpec((None, tt, D_MODEL), lambda i, s, f: (i, s, 0)),
            pl.BlockSpec((D_MODEL, tf), lambda i, s, f: (0, f)),
            pl.BlockSpec((D_MODEL, tf), lambda i, s, f: (0, nf + f)),
            pl.BlockSpec((FFN_CONV, tf), lambda i, s, f: (0, f)),
            pl.BlockSpec((FFN_CONV, tf), lambda i, s, f: (0, nf + f)),
            pl.BlockSpec((tf, D_MODEL), lambda i, s, f: (f, 0)),
            pl.BlockSpec((1, D_MODEL), lambda i, s, f: (0, 0)),
        ],
        out_specs=[
            pl.BlockSpec((None, tt, D_MODEL), lambda i, s, f: (i, s, 0)),
            pl.BlockSpec((None, None, FFN_CONV - 1, tf), lambda i, s, f: (i, s, 0, f)),
            pl.BlockSpec((None, None, FFN_CONV - 1, tf), lambda i, s, f: (i, s, 0, f)),
        ],
        out_shape=[
            jax.ShapeDtypeStruct((b, t, D_MODEL), F32),
            jax.ShapeDtypeStruct((b, t // tt, FFN_CONV - 1, D_FF), F32),
            jax.ShapeDtypeStruct((b, t // tt, FFN_CONV - 1, D_FF), F32),
        ],
        scratch_shapes=[
            pltpu.VMEM((tt, D_MODEL), F32),
            pltpu.VMEM((nf, 2, SUBLANES, tf), F32),
            pltpu.VMEM((2, SUBLANES + tt, tf), F32),
        ],
        compiler_params=pltpu.CompilerParams(
            dimension_semantics=("arbitrary", "arbitrary", "arbitrary"),
            vmem_limit_bytes=VMEM_LIMIT),
        name="ffn_long",
    )(hn, x1, wup, wup, cw, cw, wdown, fg_row)


def _ffn_short_kernel(hn_ref, x1_ref, wg_ref, wu_ref, cwg_ref, cwu_ref, wd_ref, fg_ref,
                      b0g_ref, b1g_ref, b0u_ref, b1u_ref,
                      y_ref, n0g_ref, n1g_ref, n0u_ref, n1u_ref, acc_ref, z_ref, hb_ref, *, tt, seq):
    f = pl.program_id(1)
    nseq = tt // seq
    hn = hn_ref[...]
    t_in_seq = lax.broadcasted_iota(jnp.int32, (tt, FFN_TF), 0) % seq
    convs = []
    groups = ((wg_ref, cwg_ref, b0g_ref, b1g_ref, n0g_ref, n1g_ref),
              (wu_ref, cwu_ref, b0u_ref, b1u_ref, n0u_ref, n1u_ref))
    for w_ref, cw_ref, b0_ref, b1_ref, n0_ref, n1_ref in groups:
        h = jnp.dot(hn, w_ref[...], preferred_element_type=F32)
        z_ref[...] = jnp.zeros(z_ref.shape, F32)
        for lb in range(FFN_TF // LANES):
            cols = slice(lb * LANES, (lb + 1) * LANES)
            z_ref[lb, pl.ds(0, nseq, stride=seq), :] = b0_ref[:, cols]
            z_ref[lb, pl.ds(1, nseq, stride=seq), :] = b1_ref[:, cols]
            hb_ref[lb] = h[:, cols]
            n0_ref[:, cols] = hb_ref[lb, pl.ds(seq - 2, nseq, stride=seq), :]
            n1_ref[:, cols] = hb_ref[lb, pl.ds(seq - 1, nseq, stride=seq), :]
        z = jnp.concatenate([z_ref[lb] for lb in range(FFN_TF // LANES)], axis=1)
        s1 = jnp.where(t_in_seq == 0, pltpu.roll(z, tt - 1, 0), pltpu.roll(h, 1, 0))
        s2 = jnp.where(t_in_seq < 2, z, pltpu.roll(h, 2, 0))
        cw = cw_ref[...]
        convs.append(h * cw[2:3, :] + s1 * cw[1:2, :] + s2 * cw[0:1, :])
    act = _silu(convs[0]) * convs[1]
    _ffn_tail(f, act, wd_ref, x1_ref, fg_ref, y_ref, acc_ref)


def _ffn_short(hn, x1, wup, cw, wdown, fg_row, buf0, buf1, tt, seq):
    m = hn.shape[0]
    tf = FFN_TF
    nf = FFN_NF
    nseq = tt // seq
    st_g = pl.BlockSpec((nseq, tf), lambda i, f: (i, f))
    st_u = pl.BlockSpec((nseq, tf), lambda i, f: (i, nf + f))
    new = pl.BlockSpec((nseq, tf), lambda i, f: (i, f))
    new_shape = jax.ShapeDtypeStruct((m // seq, D_FF), F32)
    return pl.pallas_call(
        functools.partial(_ffn_short_kernel, tt=tt, seq=seq),
        grid=(m // tt, nf),
        in_specs=[
            pl.BlockSpec((tt, D_MODEL), lambda i, f: (i, 0)),
            pl.BlockSpec((tt, D_MODEL), lambda i, f: (i, 0)),
            pl.BlockSpec((D_MODEL, tf), lambda i, f: (0, f)),
            pl.BlockSpec((D_MODEL, tf), lambda i, f: (0, nf + f)),
            pl.BlockSpec((FFN_CONV, tf), lambda i, f: (0, f)),
            pl.BlockSpec((FFN_CONV, tf), lambda i, f: (0, nf + f)),
            pl.BlockSpec((tf, D_MODEL), lambda i, f: (f, 0)),
            pl.BlockSpec((1, D_MODEL), lambda i, f: (0, 0)),
            st_g, st_g, st_u, st_u,
        ],
        out_specs=[pl.BlockSpec((tt, D_MODEL), lambda i, f: (i, 0)), new, new, new, new],
        out_shape=[jax.ShapeDtypeStruct((m, D_MODEL), F32), new_shape, new_shape, new_shape, new_shape],
        scratch_shapes=[
            pltpu.VMEM((tt, D_MODEL), F32),
            pltpu.VMEM((tf // LANES, tt, LANES), F32),
            pltpu.VMEM((tf // LANES, tt, LANES), F32),
        ],
        compiler_params=pltpu.CompilerParams(
            dimension_semantics=("arbitrary", "arbitrary"), vmem_limit_bytes=VMEM_LIMIT),
        name="ffn_short",
    )(hn, x1, wup, wup, cw, cw, wdown, fg_row, buf0, buf1, buf0, buf1)


def _pad_lanes(vec, offset):
    out = jnp.zeros((LANES,), F32)
    return out.at[offset:offset + vec.shape[0]].set(vec.astype(F32))


def _trunk(x, s_gdn, s_gconv, s_rwkv, s_shift, s_ffn, prm, *, long_seq):
    b, t, _ = x.shape
    m = b * t
    x2d = x.reshape(m, D_MODEL)
    tm = min(512, m)
    proj = _inproj(x2d, prm["ln1_g"], prm["w_in"], tm).reshape(b, t, PROJ_WIDTH)

    gdn_chunk = min(GDN_CHUNK, t)
    o_a, gdn_new = _gdn_mixer(proj, s_gconv, s_gdn, prm["gdn_conv_w"], prm["alog_r"], prm["dtb_r"],
                              prm["alog_c"], prm["dtb_c"], prm["gdn_norm_g"], gdn_chunk)
    rwkv_chunk = min(RWKV_CHUNK, t)
    sh_rkv = s_shift[:, None, :3 * RWKV_WIDTH]
    sh_lora = s_shift[:, None, 3 * RWKV_WIDTH:]
    o_b, rwkv_new = _rwkv_mixer(proj, sh_rkv, sh_lora, s_rwkv, prm["mu_rkv"], prm["mu_lora"],
                                prm["rwkv_w0"], prm["rwkv_a0"], prm["rwkv_wab"], prm["rwkv_g_b"],
                                prm["rwkv_k_k"], prm["rwkv_k_a"], prm["rwkv_r_k"], prm["rwkv_gn_w"],
                                prm["rwkv_gn_b"], rwkv_chunk)

    x1, hn = _outproj(x2d, o_a.reshape(m, GDN_WIDTH), o_b.reshape(m, RWKV_WIDTH), prm["w_o"],
                      prm["ln2_g"], tm)
    if long_seq:
        tt = min(512, t if long_seq else m)
        y, n_g, n_u = _ffn_long(hn.reshape(b, t, D_MODEL), x1.reshape(b, t, D_MODEL), prm["ffn_w_up"],
                                prm["ffn_conv_w"], prm["ffn_w_down"], prm["final_g"], tt)
        ffn_new = jnp.concatenate([n_g[:, -1], n_u[:, -1]], axis=-1)
    else:
        tt = min(512, t if long_seq else m)
        y, n0g, n1g, n0u, n1u = _ffn_short(hn, x1, prm["ffn_w_up"], prm["ffn_conv_w"],
                                            prm["ffn_w_down"], prm["final_g"],
                                            s_ffn[:, 0], s_ffn[:, 1], tt, t)
        y = y.reshape(b, t, D_MODEL)
        ffn_new = jnp.stack([jnp.concatenate([n0g, n0u], axis=-1),
                             jnp.concatenate([n1g, n1u], axis=-1)], axis=1)

    gconv_new = proj[:, t - (GDN_CONV - 1):, COL_QKV:COL_QKV + 3 * GDN_WIDTH]
    shift_new = jnp.concatenate([proj[:, t - 1, COL_RKV:COL_RKV + 3 * RWKV_WIDTH],
                                 proj[:, t - 1, COL_LORA:COL_LORA + RWKV_LORA]], axis=-1)
    return y, gdn_new[None], gconv_new[None], rwkv_new[None], shift_new[None], ffn_new[None]


def kernel(x_prompt, x_sample, state_gdn, state_gdn_conv, state_rwkv, state_rwkv_shift, state_ffn_conv, ln1_g, w_in, gdn_conv_w, gdn_a_log, gdn_dt_bias, gdn_norm_g, rwkv_mu, rwkv_w0, rwkv_w_b, rwkv_a0, rwkv_a_b, rwkv_g_b, rwkv_k_k, rwkv_k_a, rwkv_r_k, rwkv_gn_w, rwkv_gn_b, w_o, ln2_g, ffn_w_up, ffn_conv_w, ffn_w_down, final_g):
    assert ln1_g.shape[0] == 1, "single-layer trunk"
    w = w_in[0]
    rw = REF_OFF_RWKV
    w_perm = jnp.concatenate([
        w[:, :REF_OFF_Z],
        w[:, rw:rw + 3 * RWKV_WIDTH],
        w[:, REF_OFF_Z:REF_OFF_B],
        w[:, rw + 3 * RWKV_WIDTH:],
        w[:, REF_OFF_B:REF_OFF_RWKV],
        jnp.zeros((D_MODEL, PROJ_WIDTH - COL_BA - 2 * GDN_HEADS), w.dtype),
    ], axis=1).astype(BF16)
    mu = rwkv_mu[0]
    zeros_w = jnp.zeros((RWKV_LORA_W, RWKV_WIDTH), F32)
    wab = jnp.concatenate([
        jnp.concatenate([rwkv_w_b[0], zeros_w], axis=1),
        jnp.concatenate([zeros_w, rwkv_a_b[0]], axis=1)], axis=0).astype(BF16)
    alog = _pad_lanes(gdn_a_log[0], GDN_HEADS)
    dtb = _pad_lanes(gdn_dt_bias[0], GDN_HEADS)
    prm = {
        "ln1_g": ln1_g[0][None], "w_in": w_perm, "gdn_conv_w": gdn_conv_w[0],
        "alog_r": alog[None], "dtb_r": dtb[None], "alog_c": alog[:, None], "dtb_c": dtb[:, None],
        "gdn_norm_g": gdn_norm_g[0][None],
        "mu_rkv": mu[None, :3 * RWKV_WIDTH], "mu_lora": mu[None, 3 * RWKV_WIDTH:],
        "rwkv_w0": rwkv_w0[0][None], "rwkv_a0": rwkv_a0[0][None], "rwkv_wab": wab,
        "rwkv_g_b": rwkv_g_b[0].astype(BF16), "rwkv_k_k": rwkv_k_k[0][None],
        "rwkv_k_a": rwkv_k_a[0][None], "rwkv_r_k": rwkv_r_k[0].reshape(1, RWKV_WIDTH),
        "rwkv_gn_w": rwkv_gn_w[0][None], "rwkv_gn_b": rwkv_gn_b[0][None],
        "w_o": w_o[0].astype(BF16), "ln2_g": ln2_g[0][None],
        "ffn_w_up": ffn_w_up[0].astype(BF16), "ffn_conv_w": ffn_conv_w[0],
        "ffn_w_down": ffn_w_down[0].astype(BF16), "final_g": final_g[None],
    }

    bp = x_prompt.shape[0]
    zero_states = (
        jnp.zeros((bp,) + state_gdn.shape[2:], F32),
        jnp.zeros((bp,) + state_gdn_conv.shape[2:], F32),
        jnp.zeros((bp,) + state_rwkv.shape[2:], F32),
        jnp.zeros((bp,) + state_rwkv_shift.shape[2:], F32),
        None,
    )
    outs_p = _trunk(x_prompt, *zero_states, prm, long_seq=True)
    outs_s = _trunk(x_sample, state_gdn[0], state_gdn_conv[0], state_rwkv[0], state_rwkv_shift[0],
                    state_ffn_conv[0], prm, long_seq=False)
    return (outs_p[0], outs_s[0]) + tuple(outs_p[1:]) + tuple(outs_s[1:])
```

```python
import functools

import jax
import jax.numpy as jnp
from jax import lax
from jax.experimental import pallas as pl
from jax.experimental.pallas import tpu as pltpu

F32 = jnp.float32
BF16 = jnp.bfloat16

D_MODEL = 2048
GDN_WIDTH = 1024
GDN_HEADS = 8
GDN_DK = 128
GDN_CONV = 4
RWKV_WIDTH = 1024
RWKV_HEAD = 64
RWKV_HEADS = 16
RWKV_LORA_W = 64
RWKV_LORA_A = 64
RWKV_LORA_G = 128
RWKV_LORA = RWKV_LORA_W + RWKV_LORA_A + RWKV_LORA_G
RWKV_PROJ = 3 * RWKV_WIDTH + RWKV_LORA
D_FF = 5632
FFN_CONV = 3
RMS_EPS = 1e-6
L2_EPS = 1e-12
GN_EPS = 64e-5

REF_OFF_Z = 3 * GDN_WIDTH
REF_OFF_B = 4 * GDN_WIDTH
REF_OFF_RWKV = REF_OFF_B + 2 * GDN_HEADS
REF_IN_WIDTH = REF_OFF_RWKV + RWKV_PROJ

LANES = 128
SUBLANES = 8
COL_QKV = 0
COL_RKV = 3 * GDN_WIDTH
COL_Z = COL_RKV + 3 * RWKV_WIDTH
COL_LORA = COL_Z + GDN_WIDTH
COL_BA = COL_LORA + RWKV_LORA
PROJ_WIDTH = 7680
PROJ_TN = 1280

MIX_ROWS = 64

NN = (((1,), (0,)), ((), ()))
NT = (((1,), (1,)), ((), ()))
TN = (((0,), (0,)), ((), ()))

VMEM_LIMIT = 56 * 1024 * 1024

P_GRAM = "x1"
P_INV = "x1"
P_SOLVE = "x1"
P_STATE = "x1"
P_OUT = "x1"


def _split(x):
    hi = x.astype(BF16)
    return hi, (x - hi.astype(F32)).astype(BF16)


def _dot(a, b, dims=NN, mode="x1"):
    if mode == "hi":
        return lax.dot_general(a, b, dims, precision=lax.Precision.HIGHEST,
                               preferred_element_type=F32)
    if mode == "x3":
        a_hi, a_lo = _split(a)
        b_hi, b_lo = _split(b)
        d = lambda u, v: lax.dot_general(u, v, dims, preferred_element_type=F32)
        return d(a_hi, b_hi) + (d(a_hi, b_lo) + d(a_lo, b_hi))
    return lax.dot_general(a.astype(BF16), b.astype(BF16), dims, preferred_element_type=F32)


def _ones_dot(ones_mat, x, dims=NN):
    x1 = x.astype(BF16)
    r1 = x - x1.astype(F32)
    x2 = r1.astype(BF16)
    x3 = (r1 - x2.astype(F32)).astype(BF16)
    m = ones_mat.astype(BF16)
    if dims == NN:
        d = lambda v: lax.dot_general(m, v, dims, preferred_element_type=F32)
    else:
        d = lambda v: lax.dot_general(v, m, dims, preferred_element_type=F32)
    return d(x1) + (d(x2) + d(x3))


def _sigmoid(x):
    return 1.0 / (1.0 + jnp.exp(-x))


def _silu(x):
    return x * _sigmoid(x)


def _softplus(x):
    return jnp.maximum(x, 0.0) + jnp.log(1.0 + jnp.exp(-jnp.abs(x)))


def _seq_masks(rows, seq_len):
    r = lax.broadcasted_iota(jnp.int32, (rows, rows), 0)
    c = lax.broadcasted_iota(jnp.int32, (rows, rows), 1)
    if seq_len >= rows:
        return None, r >= c, r > c
    shift = seq_len.bit_length() - 1
    assert 1 << shift == seq_len
    same = jnp.right_shift(r, shift) == jnp.right_shift(c, shift)
    return same, same & (r >= c), same & (r > c)


def _unit_lower_inverses(mats, rows, nilpotency):
    r = lax.broadcasted_iota(jnp.int32, (rows, rows), 0)
    c = lax.broadcasted_iota(jnp.int32, (rows, rows), 1)
    eye = jnp.where(r == c, 1.0, 0.0).astype(F32)
    ps = [-a for a in mats]
    ts = [eye + p for p in ps]
    k = 1
    while 2 * k < nilpotency:
        ps = [_dot(p, p, mode=P_INV) for p in ps]
        ts = [t + _dot(t, p, mode=P_INV) for t, p in zip(ts, ps)]
        k *= 2
    return ts


def _inproj_kernel(x_ref, g_ref, w_ref, o_ref, xn_ref):
    @pl.when(pl.program_id(1) == 0)
    def _():
        x = x_ref[...]
        ms = jnp.mean(x * x, axis=-1, keepdims=True)
        xn_ref[...] = (x * lax.rsqrt(ms + RMS_EPS) * g_ref[...]).astype(BF16)

    o_ref[...] = jnp.dot(xn_ref[...], w_ref[...], preferred_element_type=F32)


def _inproj(x2d, g_row, w_bf16, tm):
    m = x2d.shape[0]
    return pl.pallas_call(
        _inproj_kernel,
        grid=(m // tm, PROJ_WIDTH // PROJ_TN),
        in_specs=[
            pl.BlockSpec((tm, D_MODEL), lambda i, j: (i, 0)),
            pl.BlockSpec((1, D_MODEL), lambda i, j: (0, 0)),
            pl.BlockSpec((D_MODEL, PROJ_TN), lambda i, j: (0, j)),
        ],
        out_specs=pl.BlockSpec((tm, PROJ_TN), lambda i, j: (i, j)),
        out_shape=jax.ShapeDtypeStruct((m, PROJ_WIDTH), F32),
        scratch_shapes=[pltpu.VMEM((tm, D_MODEL), BF16)],
        compiler_params=pltpu.CompilerParams(
            dimension_semantics=("arbitrary", "arbitrary"), vmem_limit_bytes=VMEM_LIMIT),
        name="inproj",
    )(x2d, g_row, w_bf16)


def _mixer_geometry(seq_len):
    rows = MIX_ROWS
    length = min(seq_len, rows)
    assert rows % length == 0 and seq_len % length == 0 and length % SUBLANES == 0
    return rows, length, rows // length, seq_len // length


def _gdn_kernel(qkv_ref, z_ref, ba_ref, cbuf_ref, s0_ref, convw_ref, alog_r_ref, dtb_r_ref,
                alog_c_ref, dtb_c_ref, ng_ref, o_ref, s_ref, xp_ref, *, seq_len):
    R, L, G, _ = _mixer_geometry(seq_len)
    c = pl.program_id(1)
    width = 3 * GDN_WIDTH
    hist = GDN_CONV - 1
    cw = convw_ref[...]

    @pl.when(c == 0)
    def _():
        s_ref[...] = s0_ref[...]

    if G == 1:
        @pl.when(c == 0)
        def _():
            xp_ref[0:SUBLANES, :] = jnp.zeros((SUBLANES, width), F32)
            xp_ref[SUBLANES - hist:SUBLANES, :] = cbuf_ref[0]

        @pl.when(c > 0)
        def _():
            xp_ref[0:SUBLANES, :] = xp_ref[R:R + SUBLANES, :]

        xp_ref[SUBLANES:SUBLANES + R, :] = qkv_ref[...]
        acc = qkv_ref[...] * cw[hist:hist + 1, :]
        for i in range(hist):
            off = SUBLANES - hist + i
            acc = acc + xp_ref[off:off + R, :] * cw[i:i + 1, :]
    else:
        pieces = []
        for g in range(G):
            rows = slice(g * L, (g + 1) * L)
            xp_ref[g, SUBLANES - hist:SUBLANES, :] = cbuf_ref[g]
            xp_ref[g, SUBLANES:SUBLANES + L, :] = qkv_ref[rows, :]
            piece = qkv_ref[rows, :] * cw[hist:hist + 1, :]
            for i in range(hist):
                off = SUBLANES - hist + i
                piece = piece + xp_ref[g, off:off + L, :] * cw[i:i + 1, :]
            pieces.append(piece)
        acc = jnp.concatenate(pieces, axis=0)
    qkv = _silu(acc)

    same, causal, strict = _seq_masks(R, L)
    causal01 = jnp.where(causal, 1.0, 0.0).astype(F32)

    ba = ba_ref[...]
    ba_t = ba.T
    beta_c = _sigmoid(ba)
    g_c = -jnp.exp(alog_r_ref[...]) * _softplus(ba + dtb_r_ref[...])
    g_r = -jnp.exp(alog_c_ref[...]) * _softplus(ba_t + dtb_c_ref[...])
    gc_all = _ones_dot(causal01, g_c)
    gr_all = _ones_dot(causal01, g_r, NT)
    if G == 1:
        gtot_all = jnp.broadcast_to(gc_all[R - 1:R, :], (R, LANES))
    else:
        gtot_all = _ones_dot(jnp.where(same, 1.0, 0.0).astype(F32), g_c)

    ng = ng_ref[...]
    heads = range(GDN_HEADS)
    seqs = range(G)
    seq_rows = [slice(g * L, (g + 1) * L) for g in seqs]
    qs, ks, vs, betas, gcols, gtots, decays = [], [], [], [], [], [], []
    for h in heads:
        lo = h * GDN_DK
        q = qkv[:, lo:lo + GDN_DK]
        k = qkv[:, GDN_WIDTH + lo:GDN_WIDTH + lo + GDN_DK]
        qs.append(q * lax.rsqrt(jnp.sum(q * q, axis=-1, keepdims=True) + L2_EPS) * (GDN_DK ** -0.5))
        ks.append(k * lax.rsqrt(jnp.sum(k * k, axis=-1, keepdims=True) + L2_EPS))
        vs.append(qkv[:, 2 * GDN_WIDTH + lo:2 * GDN_WIDTH + lo + GDN_DK])
        betas.append(beta_c[:, h:h + 1])
        gcol = gc_all[:, GDN_HEADS + h:GDN_HEADS + h + 1]
        grow = gr_all[GDN_HEADS + h:GDN_HEADS + h + 1, :]
        gcols.append(gcol)
        gtots.append(gtot_all[:, GDN_HEADS + h:GDN_HEADS + h + 1])
        decays.append(jnp.where(causal, jnp.exp(jnp.where(causal, gcol - grow, 0.0)), 0.0))
    kqs = [_dot(jnp.concatenate([ks[h], qs[h]], axis=0), ks[h], NT, mode=P_GRAM) for h in heads]
    t_invs = _unit_lower_inverses(
        [jnp.where(strict, betas[h] * kqs[h][:R] * decays[h], 0.0) for h in heads], R, L)
    gammas = [jnp.exp(gcols[h]) for h in heads]
    sols = [_dot(t_invs[h], jnp.concatenate([(betas[h] * gammas[h]) * ks[h], betas[h] * vs[h]], axis=1),
                 mode=P_SOLVE) for h in heads]
    states = [[s_ref[g, h] for g in seqs] for h in heads]
    wss = [[_dot(jnp.concatenate([sols[h][rows, :GDN_DK], (qs[h] * gammas[h])[rows]], axis=0),
                 states[h][g], mode=P_STATE) for g, rows in enumerate(seq_rows)] for h in heads]
    us = [jnp.concatenate([sols[h][rows, GDN_DK:] - wss[h][g][:L] for g, rows in enumerate(seq_rows)],
                          axis=0) for h in heads]
    outs = [jnp.concatenate([wss[h][g][L:] for g in seqs], axis=0)
            + _dot(kqs[h][R:] * decays[h], us[h], mode=P_OUT) for h in heads]
    for h in heads:
        kt = ks[h] * jnp.exp(gtots[h] - gcols[h])
        for g, rows in enumerate(seq_rows):
            gl = jnp.exp(gtots[h][g * L:g * L + 1, :])
            s_ref[g, h] = gl * states[h][g] + _dot(kt[rows], us[h][rows], TN, mode=P_STATE)
    for h in heads:
        lo = h * GDN_DK
        o = outs[h]
        o = o * lax.rsqrt(jnp.mean(o * o, axis=-1, keepdims=True) + RMS_EPS) * ng
        o = o * _silu(z_ref[:, lo:lo + GDN_DK])
        o_ref[:, lo:lo + GDN_DK] = o.astype(o_ref.dtype)


def _gdn_mixer(proj2d, cbuf, s0, convw, alog_r, dtb_r, alog_c, dtb_c, ng, seq_len):
    m = proj2d.shape[0]
    R, L, G, sps = _mixer_geometry(seq_len)
    nseq = m // seq_len
    width = 3 * GDN_WIDTH
    const2 = lambda i, c: (0, 0)
    rows_map = lambda col: (lambda i, c: (i * sps + c, col))
    xp_shape = (SUBLANES + R, width) if G == 1 else (G, SUBLANES + L, width)
    return pl.pallas_call(
        functools.partial(_gdn_kernel, seq_len=seq_len),
        grid=(nseq // G, sps),
        in_specs=[
            pl.BlockSpec((R, width), rows_map(COL_QKV // width)),
            pl.BlockSpec((R, GDN_WIDTH), rows_map(COL_Z // GDN_WIDTH)),
            pl.BlockSpec((R, LANES), rows_map(COL_BA // LANES)),
            pl.BlockSpec((G, GDN_CONV - 1, width), lambda i, c: (i, 0, 0)),
            pl.BlockSpec((G, GDN_HEADS, GDN_DK, GDN_DK), lambda i, c: (i, 0, 0, 0)),
            pl.BlockSpec((GDN_CONV, width), const2),
            pl.BlockSpec((1, LANES), const2),
            pl.BlockSpec((1, LANES), const2),
            pl.BlockSpec((LANES, 1), const2),
            pl.BlockSpec((LANES, 1), const2),
            pl.BlockSpec((1, GDN_DK), const2),
        ],
        out_specs=[
            pl.BlockSpec((R, GDN_WIDTH), rows_map(0)),
            pl.BlockSpec((G, GDN_HEADS, GDN_DK, GDN_DK), lambda i, c: (i, 0, 0, 0)),
        ],
        out_shape=[
            jax.ShapeDtypeStruct((m, GDN_WIDTH), BF16),
            jax.ShapeDtypeStruct((nseq, GDN_HEADS, GDN_DK, GDN_DK), F32),
        ],
        scratch_shapes=[pltpu.VMEM(xp_shape, F32)],
        compiler_params=pltpu.CompilerParams(
            dimension_semantics=("arbitrary", "arbitrary"), vmem_limit_bytes=VMEM_LIMIT),
        name="gdn_mixer",
    )(proj2d, proj2d, proj2d, cbuf, s0, convw, alog_r, dtb_r, alog_c, dtb_c, ng)


def _shifted_rows(x_ref, prev_ref, carry_ref, c, R, L, G):
    x = x_ref[...]
    if G == 1:
        width = x.shape[1]

        @pl.when(c == 0)
        def _():
            carry_ref[0:SUBLANES, :] = jnp.zeros((SUBLANES, width), F32)
            carry_ref[SUBLANES - 1:SUBLANES, :] = prev_ref[0]

        @pl.when(c > 0)
        def _():
            carry_ref[0:SUBLANES, :] = carry_ref[R:R + SUBLANES, :]

        carry_ref[SUBLANES:SUBLANES + R, :] = x
        return x, carry_ref[SUBLANES - 1:SUBLANES - 1 + R, :]
    row = lax.broadcasted_iota(jnp.int32, (L, x.shape[1]), 0)
    pieces = []
    for g in range(G):
        xg = x[g * L:(g + 1) * L]
        pieces.append(jnp.where(row == 0, prev_ref[g], pltpu.roll(xg, 1, 0)))
    return x, jnp.concatenate(pieces, axis=0)


def _rwkv_kernel(rkv_ref, lora_ref, sh_rkv_ref, sh_lora_ref, s0_ref, mu_rkv_ref, mu_lora_ref,
                 w0_ref, a0_ref, wab_ref, gb_ref, kk_ref, ka_ref, rk_ref, gnw_ref, gnb_ref,
                 o_ref, s_ref, xr_ref, xl_ref, *, seq_len):
    R, L, G, _ = _mixer_geometry(seq_len)
    c = pl.program_id(1)
    W = RWKV_WIDTH

    @pl.when(c == 0)
    def _():
        s_ref[...] = s0_ref[...]

    p, p_prev = _shifted_rows(rkv_ref, sh_rkv_ref, xr_ref, c, R, L, G)
    xs = p + (p_prev - p) * mu_rkv_ref[...]
    pl_, pl_prev = _shifted_rows(lora_ref, sh_lora_ref, xl_ref, c, R, L, G)
    xl = pl_ + (pl_prev - pl_) * mu_lora_ref[...]
    r = xs[:, :W]
    k = xs[:, W:2 * W]
    v = xs[:, 2 * W:]

    wa_in = xl[:, :LANES]
    lane = lax.broadcasted_iota(jnp.int32, (R, LANES), 1)
    wa_in = jnp.where(lane < RWKV_LORA_W, jnp.tanh(wa_in), wa_in)
    wa = _dot(wa_in, wab_ref[...])
    w = -_softplus(-(w0_ref[...] + wa[:, :W])) - 0.5
    lw = -jnp.exp(w)
    a = _sigmoid(a0_ref[...] + wa[:, W:])
    gate = _dot(_sigmoid(xl[:, LANES:]), gb_ref[...])
    kk_raw = k * kk_ref[...]
    k2 = k * (1.0 + (a - 1.0) * ka_ref[...])
    rkr = r * k2 * rk_ref[...]

    same, causal, strict = _seq_masks(R, L)
    lc = _ones_dot(jnp.where(causal, 1.0, 0.0).astype(F32), lw)
    if G == 1:
        ltot = jnp.broadcast_to(lc[R - 1:R, :], (R, W))
    else:
        ltot = _ones_dot(jnp.where(same, 1.0, 0.0).astype(F32), lw)
    e_inc = jnp.exp(lc)
    e_exc = jnp.exp(lc - lw)
    e_inv = jnp.exp(-lc)
    e_rem = jnp.exp(ltot - lc)
    p_all = jnp.exp(ltot)

    heads = range(RWKV_HEADS)
    seqs = range(G)
    seq_rows = [slice(g * L, (g + 1) * L) for g in seqs]
    lanes = [slice(h * RWKV_HEAD, (h + 1) * RWKV_HEAD) for h in heads]
    cts, rts, bhs, khs, bbs, kbs, vhs = [], [], [], [], [], [], []
    for sl in lanes:
        kk = kk_raw[:, sl]
        kk = kk * lax.rsqrt(jnp.sum(kk * kk, axis=-1, keepdims=True) + L2_EPS)
        kka = kk * a[:, sl]
        k2h = k2[:, sl]
        cts.append(kk * e_exc[:, sl])
        rts.append(r[:, sl] * e_inc[:, sl])
        bhs.append(kka * e_inv[:, sl])
        khs.append(k2h * e_inv[:, sl])
        bbs.append(kka * e_rem[:, sl])
        kbs.append(k2h * e_rem[:, sl])
        vhs.append(v[:, sl])
    aas = [_dot(jnp.concatenate([cts[h], rts[h]], axis=0),
                jnp.concatenate([bhs[h], khs[h]], axis=0), NT, mode=P_GRAM) for h in heads]
    states = [[s_ref[g, h] for g in seqs] for h in heads]
    xms = [[_dot(jnp.concatenate([cts[h][rows], rts[h][rows]], axis=0), states[h][g], NT, mode=P_STATE)
            for g, rows in enumerate(seq_rows)] for h in heads]
    t_invs = _unit_lower_inverses([jnp.where(strict, aas[h][:R, :R], 0.0) for h in heads], R, L)
    akvs = [_dot(jnp.where(strict, aas[h][:R, R:], 0.0), vhs[h], mode=P_OUT) for h in heads]
    us = [_dot(t_invs[h], -(jnp.concatenate([xms[h][g][:L] for g in seqs], axis=0) + akvs[h]),
               mode=P_SOLVE) for h in heads]
    yhs = [jnp.concatenate([xms[h][g][L:] for g in seqs], axis=0)
           + _dot(jnp.concatenate([jnp.where(causal, aas[h][R:, :R], 0.0),
                                   jnp.where(causal, aas[h][R:, R:], 0.0)], axis=1),
                  jnp.concatenate([us[h], vhs[h]], axis=0), mode=P_OUT) for h in heads]
    for h in heads:
        for g, rows in enumerate(seq_rows):
            uv_g = jnp.concatenate([us[h][rows], vhs[h][rows]], axis=0)
            bk_g = jnp.concatenate([bbs[h][rows], kbs[h][rows]], axis=0)
            s_ref[g, h] = (states[h][g] * p_all[g * L:g * L + 1, lanes[h]]
                           + _dot(uv_g, bk_g, TN, mode=P_STATE))
    for h in heads:
        sl = lanes[h]
        yh = yhs[h]
        mean = jnp.mean(yh, axis=-1, keepdims=True)
        yc = yh - mean
        var = jnp.mean(yc * yc, axis=-1, keepdims=True)
        yn = yc * lax.rsqrt(var + GN_EPS) * gnw_ref[:, sl] + gnb_ref[:, sl]
        yn = yn + jnp.sum(rkr[:, sl], axis=-1, keepdims=True) * vhs[h]
        o_ref[:, sl] = (yn * gate[:, sl]).astype(o_ref.dtype)


def _rwkv_mixer(proj2d, sh_rkv, sh_lora, s0, mu_rkv, mu_lora, w0, a0, wab, gb, kk, ka, rk, gnw, gnb,
                seq_len):
    m = proj2d.shape[0]
    R, L, G, sps = _mixer_geometry(seq_len)
    nseq = m // seq_len
    W = RWKV_WIDTH
    const2 = lambda i, c: (0, 0)
    row = lambda width: pl.BlockSpec((1, width), const2)
    rows_map = lambda col: (lambda i, c: (i * sps + c, col))
    state_spec = pl.BlockSpec((G, RWKV_HEADS, RWKV_HEAD, RWKV_HEAD), lambda i, c: (i, 0, 0, 0))
    carry = lambda width: pltpu.VMEM((SUBLANES + R, width) if G == 1 else (SUBLANES, LANES), F32)
    return pl.pallas_call(
        functools.partial(_rwkv_kernel, seq_len=seq_len),
        grid=(nseq // G, sps),
        in_specs=[
            pl.BlockSpec((R, 3 * W), rows_map(COL_RKV // (3 * W))),
            pl.BlockSpec((R, RWKV_LORA), rows_map(COL_LORA // RWKV_LORA)),
            pl.BlockSpec((G, 1, 3 * W), lambda i, c: (i, 0, 0)),
            pl.BlockSpec((G, 1, RWKV_LORA), lambda i, c: (i, 0, 0)),
            state_spec,
            row(3 * W), row(RWKV_LORA), row(W), row(W),
            pl.BlockSpec((LANES, 2 * W), const2),
            pl.BlockSpec((RWKV_LORA_G, W), const2),
            row(W), row(W), row(W), row(W), row(W),
        ],
        out_specs=[pl.BlockSpec((R, W), rows_map(0)), state_spec],
        out_shape=[
            jax.ShapeDtypeStruct((m, W), BF16),
            jax.ShapeDtypeStruct((nseq, RWKV_HEADS, RWKV_HEAD, RWKV_HEAD), F32),
        ],
        scratch_shapes=[carry(3 * W), carry(RWKV_LORA)],
        compiler_params=pltpu.CompilerParams(
            dimension_semantics=("arbitrary", "arbitrary"), vmem_limit_bytes=VMEM_LIMIT),
        name="rwkv_mixer",
    )(proj2d, proj2d, sh_rkv, sh_lora, s0, mu_rkv, mu_lora, w0, a0, wab, gb, kk, ka, rk, gnw, gnb)


def _outproj_kernel(x_ref, oa_ref, ob_ref, wa_ref, wb_ref, g_ref, x1_ref, hn_ref):
    x1 = (x_ref[...] + jnp.dot(oa_ref[...], wa_ref[...], preferred_element_type=F32)
          + jnp.dot(ob_ref[...], wb_ref[...], preferred_element_type=F32))
    x1_ref[...] = x1
    ms = jnp.mean(x1 * x1, axis=-1, keepdims=True)
    hn_ref[...] = (x1 * lax.rsqrt(ms + RMS_EPS) * g_ref[...]).astype(BF16)


def _outproj(x2d, oa, ob, wo_bf16, g_row, tm):
    m = x2d.shape[0]
    return pl.pallas_call(
        _outproj_kernel,
        grid=(m // tm,),
        in_specs=[
            pl.BlockSpec((tm, D_MODEL), lambda i: (i, 0)),
            pl.BlockSpec((tm, GDN_WIDTH), lambda i: (i, 0)),
            pl.BlockSpec((tm, RWKV_WIDTH), lambda i: (i, 0)),
            pl.BlockSpec((GDN_WIDTH, D_MODEL), lambda i: (0, 0)),
            pl.BlockSpec((RWKV_WIDTH, D_MODEL), lambda i: (1, 0)),
            pl.BlockSpec((1, D_MODEL), lambda i: (0, 0)),
        ],
        out_specs=[pl.BlockSpec((tm, D_MODEL), lambda i: (i, 0)),
                   pl.BlockSpec((tm, D_MODEL), lambda i: (i, 0))],
        out_shape=[jax.ShapeDtypeStruct((m, D_MODEL), F32),
                   jax.ShapeDtypeStruct((m, D_MODEL), BF16)],
        compiler_params=pltpu.CompilerParams(
            dimension_semantics=("arbitrary",), vmem_limit_bytes=VMEM_LIMIT),
        name="outproj",
    )(x2d, oa, ob, wo_bf16, wo_bf16, g_row)


FFN_TF = 512
FFN_NF = D_FF // FFN_TF


def _ffn_tail(f, act, wd_ref, x1_ref, fg_ref, y_ref, acc_ref):
    contrib = jnp.dot(act.astype(BF16), wd_ref[...], preferred_element_type=F32)

    @pl.when(f == 0)
    def _():
        acc_ref[...] = contrib

    @pl.when(f > 0)
    def _():
        acc_ref[...] += contrib

    @pl.when(f == FFN_NF - 1)
    def _():
        xo = x1_ref[...] + acc_ref[...]
        ms = jnp.mean(xo * xo, axis=-1, keepdims=True)
        y_ref[...] = xo * lax.rsqrt(ms + RMS_EPS) * fg_ref[...]


def _ffn_long_kernel(hn_ref, x1_ref, wg_ref, wu_ref, cwg_ref, cwu_ref, wd_ref, fg_ref,
                     y_ref, ng_ref, nu_ref, acc_ref, carry_ref, hbuf_ref, *, tt):
    ti = pl.program_id(1)
    f = pl.program_id(2)
    hn = hn_ref[...]
    convs = []
    for j, (w_ref, cw_ref, n_ref) in enumerate(((wg_ref, cwg_ref, ng_ref), (wu_ref, cwu_ref, nu_ref))):
        h = jnp.dot(hn, w_ref[...], preferred_element_type=F32)
        hbuf_ref[j, SUBLANES:SUBLANES + tt, :] = h
        prev = carry_ref[f, j]
        hbuf_ref[j, 0:SUBLANES, :] = jnp.where(ti == 0, jnp.zeros_like(prev), prev)
        cw = cw_ref[...]
        conv = h * cw[FFN_CONV - 1:FFN_CONV, :]
        for i in range(FFN_CONV - 1):
            off = SUBLANES - (FFN_CONV - 1) + i
            conv = conv + hbuf_ref[j, off:off + tt, :] * cw[i:i + 1, :]
        carry_ref[f, j] = h[tt - SUBLANES:, :]
        n_ref[...] = h[tt - (FFN_CONV - 1):, :]
        convs.append(conv)
    act = _silu(convs[0]) * convs[1]
    _ffn_tail(f, act, wd_ref, x1_ref, fg_ref, y_ref, acc_ref)


def _ffn_long(hn, x1, wup, cw, wdown, fg_row, tt):
    b, t, _ = hn.shape
    tf = FFN_TF
    nf = FFN_NF
    return pl.pallas_call(
        functools.partial(_ffn_long_kernel, tt=tt),
        grid=(b, t // tt, nf),
        in_specs=[
            pl.BlockSpec((None, tt, D_MODEL), lambda i, s, f: (i, s, 0)),
            pl.BlockSpec((None, tt, D_MODEL), lambda i, s, f: (i, s, 0)),
            pl.BlockSpec((D_MODEL, tf), lambda i, s, f: (0, f)),
            pl.BlockSpec((D_MODEL, tf), lambda i, s, f: (0, nf + f)),
            pl.BlockSpec((FFN_CONV, tf), lambda i, s, f: (0, f)),
            pl.BlockSpec((FFN_CONV, tf), lambda i, s, f: (0, nf + f)),
            pl.BlockSpec((tf, D_MODEL), lambda i, s, f: (f, 0)),
            pl.BlockSpec((1, D_MODEL), lambda i, s, f: (0, 0)),
        ],
        out_specs=[
            pl.BlockSpec((None, tt, D_MODEL), lambda i, s, f: (i, s, 0)),
            pl.BlockSpec((None, None, FFN_CONV - 1, tf), lambda i, s, f: (i, s, 0, f)),
            pl.BlockSpec((None, None, FFN_CONV - 1, tf), lambda i, s, f: (i, s, 0, f)),
        ],
        out_shape=[
            jax.ShapeDtypeStruct((b, t, D_MODEL), F32),
            jax.ShapeDtypeStruct((b, t // tt, FFN_CONV - 1, D_FF), F32),
            jax.ShapeDtypeStruct((b, t // tt, FFN_CONV - 1, D_FF), F32),
        ],
        scratch_shapes=[
            pltpu.VMEM((tt, D_MODEL), F32),
            pltpu.VMEM((nf, 2, SUBLANES, tf), F32),
            pltpu.VMEM((2, SUBLANES + tt, tf), F32),
        ],
        compiler_params=pltpu.CompilerParams(
            dimension_semantics=("arbitrary", "arbitrary", "arbitrary"),
            vmem_limit_bytes=VMEM_LIMIT),
        name="ffn_long",
    )(hn, x1, wup, wup, cw, cw, wdown, fg_row)


def _ffn_short_kernel(hn_ref, x1_ref, wg_ref, wu_ref, cwg_ref, cwu_ref, wd_ref, fg_ref,
                      b0g_ref, b1g_ref, b0u_ref, b1u_ref,
                      y_ref, n0g_ref, n1g_ref, n0u_ref, n1u_ref, acc_ref, z_ref, hb_ref, *, tt, seq):
    f = pl.program_id(1)
    nseq = tt // seq
    hn = hn_ref[...]
    t_in_seq = lax.broadcasted_iota(jnp.int32, (tt, FFN_TF), 0) % seq
    convs = []
    groups = ((wg_ref, cwg_ref, b0g_ref, b1g_ref, n0g_ref, n1g_ref),
              (wu_ref, cwu_ref, b0u_ref, b1u_ref, n0u_ref, n1u_ref))
    for w_ref, cw_ref, b0_ref, b1_ref, n0_ref, n1_ref in groups:
        h = jnp.dot(hn, w_ref[...], preferred_element_type=F32)
        z_ref[...] = jnp.zeros(z_ref.shape, F32)
        for lb in range(FFN_TF // LANES):
            cols = slice(lb * LANES, (lb + 1) * LANES)
            z_ref[lb, pl.ds(0, nseq, stride=seq), :] = b0_ref[:, cols]
            z_ref[lb, pl.ds(1, nseq, stride=seq), :] = b1_ref[:, cols]
            hb_ref[lb] = h[:, cols]
            n0_ref[:, cols] = hb_ref[lb, pl.ds(seq - 2, nseq, stride=seq), :]
            n1_ref[:, cols] = hb_ref[lb, pl.ds(seq - 1, nseq, stride=seq), :]
        z = jnp.concatenate([z_ref[lb] for lb in range(FFN_TF // LANES)], axis=1)
        s1 = jnp.where(t_in_seq == 0, pltpu.roll(z, tt - 1, 0), pltpu.roll(h, 1, 0))
        s2 = jnp.where(t_in_seq < 2, z, pltpu.roll(h, 2, 0))
        cw = cw_ref[...]
        convs.append(h * cw[2:3, :] + s1 * cw[1:2, :] + s2 * cw[0:1, :])
    act = _silu(convs[0]) * convs[1]
    _ffn_tail(f, act, wd_ref, x1_ref, fg_ref, y_ref, acc_ref)


def _ffn_short(hn, x1, wup, cw, wdown, fg_row, buf0, buf1, tt, seq):
    m = hn.shape[0]
    tf = FFN_TF
    nf = FFN_NF
    nseq = tt // seq
    st_g = pl.BlockSpec((nseq, tf), lambda i, f: (i, f))
    st_u = pl.BlockSpec((nseq, tf), lambda i, f: (i, nf + f))
    new = pl.BlockSpec((nseq, tf), lambda i, f: (i, f))
    new_shape = jax.ShapeDtypeStruct((m // seq, D_FF), F32)
    return pl.pallas_call(
        functools.partial(_ffn_short_kernel, tt=tt, seq=seq),
        grid=(m // tt, nf),
        in_specs=[
            pl.BlockSpec((tt, D_MODEL), lambda i, f: (i, 0)),
            pl.BlockSpec((tt, D_MODEL), lambda i, f: (i, 0)),
            pl.BlockSpec((D_MODEL, tf), lambda i, f: (0, f)),
            pl.BlockSpec((D_MODEL, tf), lambda i, f: (0, nf + f)),
            pl.BlockSpec((FFN_CONV, tf), lambda i, f: (0, f)),
            pl.BlockSpec((FFN_CONV, tf), lambda i, f: (0, nf + f)),
            pl.BlockSpec((tf, D_MODEL), lambda i, f: (f, 0)),
            pl.BlockSpec((1, D_MODEL), lambda i, f: (0, 0)),
            st_g, st_g, st_u, st_u,
        ],
        out_specs=[pl.BlockSpec((tt, D_MODEL), lambda i, f: (i, 0)), new, new, new, new],
        out_shape=[jax.ShapeDtypeStruct((m, D_MODEL), F32), new_shape, new_shape, new_shape, new_shape],
        scratch_shapes=[
            pltpu.VMEM((tt, D_MODEL), F32),
            pltpu.VMEM((tf // LANES, tt, LANES), F32),
            pltpu.VMEM((tf // LANES, tt, LANES), F32),
        ],
        compiler_params=pltpu.CompilerParams(
            dimension_semantics=("arbitrary", "arbitrary"), vmem_limit_bytes=VMEM_LIMIT),
        name="ffn_short",
    )(hn, x1, wup, wup, cw, cw, wdown, fg_row, buf0, buf1, buf0, buf1)


def _pad_lanes(vec, offset):
    out = jnp.zeros((LANES,), F32)
    return out.at[offset:offset + vec.shape[0]].set(vec.astype(F32))


def _trunk(x, s_gdn, s_gconv, s_rwkv, s_shift, s_ffn, prm, *, long_seq):
    b, t, _ = x.shape
    m = b * t
    x2d = x.reshape(m, D_MODEL)
    tm = min(512, m)
    proj = _inproj(x2d, prm["ln1_g"], prm["w_in"], tm)

    o_a, gdn_new = _gdn_mixer(proj, s_gconv, s_gdn, prm["gdn_conv_w"], prm["alog_r"], prm["dtb_r"],
                              prm["alog_c"], prm["dtb_c"], prm["gdn_norm_g"], t)
    sh_rkv = s_shift[:, None, :3 * RWKV_WIDTH]
    sh_lora = s_shift[:, None, 3 * RWKV_WIDTH:]
    o_b, rwkv_new = _rwkv_mixer(proj, sh_rkv, sh_lora, s_rwkv, prm["mu_rkv"], prm["mu_lora"],
                                prm["rwkv_w0"], prm["rwkv_a0"], prm["rwkv_wab"], prm["rwkv_g_b"],
                                prm["rwkv_k_k"], prm["rwkv_k_a"], prm["rwkv_r_k"], prm["rwkv_gn_w"],
                                prm["rwkv_gn_b"], t)

    x1, hn = _outproj(x2d, o_a, o_b, prm["w_o"], prm["ln2_g"], tm)
    if long_seq:
        tt = min(512, t)
        y, n_g, n_u = _ffn_long(hn.reshape(b, t, D_MODEL), x1.reshape(b, t, D_MODEL), prm["ffn_w_up"],
                                prm["ffn_conv_w"], prm["ffn_w_down"], prm["final_g"], tt)
        ffn_new = jnp.concatenate([n_g[:, -1], n_u[:, -1]], axis=-1)
    else:
        tt = min(512, m)
        y, n0g, n1g, n0u, n1u = _ffn_short(hn, x1, prm["ffn_w_up"], prm["ffn_conv_w"],
                                            prm["ffn_w_down"], prm["final_g"],
                                            s_ffn[:, 0], s_ffn[:, 1], tt, t)
        y = y.reshape(b, t, D_MODEL)
        ffn_new = jnp.stack([jnp.concatenate([n0g, n0u], axis=-1),
                             jnp.concatenate([n1g, n1u], axis=-1)], axis=1)

    proj = proj.reshape(b, t, PROJ_WIDTH)
    gconv_new = proj[:, t - (GDN_CONV - 1):, COL_QKV:COL_QKV + 3 * GDN_WIDTH]
    shift_new = jnp.concatenate([proj[:, t - 1, COL_RKV:COL_RKV + 3 * RWKV_WIDTH],
                                 proj[:, t - 1, COL_LORA:COL_LORA + RWKV_LORA]], axis=-1)
    return y, gdn_new[None], gconv_new[None], rwkv_new[None], shift_new[None], ffn_new[None]


def kernel(x_prompt, x_sample, state_gdn, state_gdn_conv, state_rwkv, state_rwkv_shift, state_ffn_conv, ln1_g, w_in, gdn_conv_w, gdn_a_log, gdn_dt_bias, gdn_norm_g, rwkv_mu, rwkv_w0, rwkv_w_b, rwkv_a0, rwkv_a_b, rwkv_g_b, rwkv_k_k, rwkv_k_a, rwkv_r_k, rwkv_gn_w, rwkv_gn_b, w_o, ln2_g, ffn_w_up, ffn_conv_w, ffn_w_down, final_g):
    assert ln1_g.shape[0] == 1, "single-layer trunk"
    w = w_in[0]
    rw = REF_OFF_RWKV
    w_perm = jnp.concatenate([
        w[:, :REF_OFF_Z],
        w[:, rw:rw + 3 * RWKV_WIDTH],
        w[:, REF_OFF_Z:REF_OFF_B],
        w[:, rw + 3 * RWKV_WIDTH:],
        w[:, REF_OFF_B:REF_OFF_RWKV],
        jnp.zeros((D_MODEL, PROJ_WIDTH - COL_BA - 2 * GDN_HEADS), w.dtype),
    ], axis=1).astype(BF16)
    mu = rwkv_mu[0]
    zeros_w = jnp.zeros((RWKV_LORA_W, RWKV_WIDTH), F32)
    wab = jnp.concatenate([
        jnp.concatenate([rwkv_w_b[0], zeros_w], axis=1),
        jnp.concatenate([zeros_w, rwkv_a_b[0]], axis=1)], axis=0).astype(BF16)
    alog = _pad_lanes(gdn_a_log[0], GDN_HEADS)
    dtb = _pad_lanes(gdn_dt_bias[0], GDN_HEADS)
    prm = {
        "ln1_g": ln1_g[0][None], "w_in": w_perm, "gdn_conv_w": gdn_conv_w[0],
        "alog_r": alog[None], "dtb_r": dtb[None], "alog_c": alog[:, None], "dtb_c": dtb[:, None],
        "gdn_norm_g": gdn_norm_g[0][None],
        "mu_rkv": mu[None, :3 * RWKV_WIDTH], "mu_lora": mu[None, 3 * RWKV_WIDTH:],
        "rwkv_w0": rwkv_w0[0][None], "rwkv_a0": rwkv_a0[0][None], "rwkv_wab": wab,
        "rwkv_g_b": rwkv_g_b[0].astype(BF16), "rwkv_k_k": rwkv_k_k[0][None],
        "rwkv_k_a": rwkv_k_a[0][None], "rwkv_r_k": rwkv_r_k[0].reshape(1, RWKV_WIDTH),
        "rwkv_gn_w": rwkv_gn_w[0][None], "rwkv_gn_b": rwkv_gn_b[0][None],
        "w_o": w_o[0].astype(BF16), "ln2_g": ln2_g[0][None],
        "ffn_w_up": ffn_w_up[0].astype(BF16), "ffn_conv_w": ffn_conv_w[0],
        "ffn_w_down": ffn_w_down[0].astype(BF16), "final_g": final_g[None],
    }

    bp = x_prompt.shape[0]
    zero_states = (
        jnp.zeros((bp,) + state_gdn.shape[2:], F32),
        jnp.zeros((bp,) + state_gdn_conv.shape[2:], F32),
        jnp.zeros((bp,) + state_rwkv.shape[2:], F32),
        jnp.zeros((bp,) + state_rwkv_shift.shape[2:], F32),
        None,
    )
    outs_p = _trunk(x_prompt, *zero_states, prm, long_seq=True)
    outs_s = _trunk(x_sample, state_gdn[0], state_gdn_conv[0], state_rwkv[0], state_rwkv_shift[0],
                    state_ffn_conv[0], prm, long_seq=False)
    return (outs_p[0], outs_s[0]) + tuple(outs_p[1:]) + tuple(outs_s[1:])
```

```python
import functools

import jax
import jax.numpy as jnp
from jax import lax
from jax.experimental import pallas as pl
from jax.experimental.pallas import tpu as pltpu

F32 = jnp.float32
BF16 = jnp.bfloat16

D_MODEL = 2048
GDN_WIDTH = 1024
GDN_HEADS = 8
GDN_DK = 128
GDN_CONV = 4
RWKV_WIDTH = 1024
RWKV_HEAD = 64
RWKV_HEADS = 16
RWKV_LORA_W = 64
RWKV_LORA_A = 64
RWKV_LORA_G = 128
RWKV_LORA = RWKV_LORA_W + RWKV_LORA_A + RWKV_LORA_G
RWKV_PROJ = 3 * RWKV_WIDTH + RWKV_LORA
D_FF = 5632
FFN_CONV = 3
RMS_EPS = 1e-6
L2_EPS = 1e-12
GN_EPS = 64e-5

REF_OFF_Z = 3 * GDN_WIDTH
REF_OFF_B = 4 * GDN_WIDTH
REF_OFF_RWKV = REF_OFF_B + 2 * GDN_HEADS
REF_IN_WIDTH = REF_OFF_RWKV + RWKV_PROJ

LANES = 128
SUBLANES = 8
COL_QKV = 0
COL_RKV = 3 * GDN_WIDTH
COL_Z = COL_RKV + 3 * RWKV_WIDTH
COL_LORA = COL_Z + GDN_WIDTH
COL_BA = COL_LORA + RWKV_LORA
PROJ_WIDTH = 7680
PROJ_TN = 1280

MIX_ROWS = 64

NN = (((1,), (0,)), ((), ()))
NT = (((1,), (1,)), ((), ()))
TN = (((0,), (0,)), ((), ()))

VMEM_LIMIT = 56 * 1024 * 1024

P_GRAM = "x1"
P_INV = "x1"
P_SOLVE = "x1"
P_STATE = "x1"
P_OUT = "x1"


def _split(x):
    hi = x.astype(BF16)
    return hi, (x - hi.astype(F32)).astype(BF16)


def _dot(a, b, dims=NN, mode="x1"):
    if mode == "hi":
        return lax.dot_general(a, b, dims, precision=lax.Precision.HIGHEST,
                               preferred_element_type=F32)
    if mode == "x3":
        a_hi, a_lo = _split(a)
        b_hi, b_lo = _split(b)
        d = lambda u, v: lax.dot_general(u, v, dims, preferred_element_type=F32)
        return d(a_hi, b_hi) + (d(a_hi, b_lo) + d(a_lo, b_hi))
    return lax.dot_general(a.astype(BF16), b.astype(BF16), dims, preferred_element_type=F32)


def _ones_dot(ones_mat, x, dims=NN):
    x1 = x.astype(BF16)
    r1 = x - x1.astype(F32)
    x2 = r1.astype(BF16)
    x3 = (r1 - x2.astype(F32)).astype(BF16)
    m = ones_mat.astype(BF16)
    if dims == NN:
        d = lambda v: lax.dot_general(m, v, dims, preferred_element_type=F32)
    else:
        d = lambda v: lax.dot_general(v, m, dims, preferred_element_type=F32)
    return d(x1) + (d(x2) + d(x3))


def _sigmoid(x):
    return 1.0 / (1.0 + jnp.exp(-x))


def _silu(x):
    return x * _sigmoid(x)


def _softplus(x):
    return jnp.maximum(x, 0.0) + jnp.log(1.0 + jnp.exp(-jnp.abs(x)))


def _seq_masks(rows, seq_len):
    r = lax.broadcasted_iota(jnp.int32, (rows, rows), 0)
    c = lax.broadcasted_iota(jnp.int32, (rows, rows), 1)
    if seq_len >= rows:
        return None, r >= c, r > c
    shift = seq_len.bit_length() - 1
    assert 1 << shift == seq_len
    same = jnp.right_shift(r, shift) == jnp.right_shift(c, shift)
    return same, same & (r >= c), same & (r > c)


def _unit_lower_inverses(mats, rows, nilpotency):
    r = lax.broadcasted_iota(jnp.int32, (rows, rows), 0)
    c = lax.broadcasted_iota(jnp.int32, (rows, rows), 1)
    eye = jnp.where(r == c, 1.0, 0.0).astype(F32)
    ps = [-a for a in mats]
    ts = [eye + p for p in ps]
    k = 1
    while 2 * k < nilpotency:
        ps = [_dot(p, p, mode=P_INV) for p in ps]
        ts = [t + _dot(t, p, mode=P_INV) for t, p in zip(ts, ps)]
        k *= 2
    return ts


def _inproj_kernel(x_ref, g_ref, w_ref, o_ref, xn_ref):
    @pl.when(pl.program_id(1) == 0)
    def _():
        x = x_ref[...]
        ms = jnp.mean(x * x, axis=-1, keepdims=True)
        xn_ref[...] = (x * lax.rsqrt(ms + RMS_EPS) * g_ref[...]).astype(BF16)

    o_ref[...] = jnp.dot(xn_ref[...], w_ref[...], preferred_element_type=F32)


def _inproj(x2d, g_row, w_bf16, tm):
    m = x2d.shape[0]
    return pl.pallas_call(
        _inproj_kernel,
        grid=(m // tm, PROJ_WIDTH // PROJ_TN),
        in_specs=[
            pl.BlockSpec((tm, D_MODEL), lambda i, j: (i, 0)),
            pl.BlockSpec((1, D_MODEL), lambda i, j: (0, 0)),
            pl.BlockSpec((D_MODEL, PROJ_TN), lambda i, j: (0, j)),
        ],
        out_specs=pl.BlockSpec((tm, PROJ_TN), lambda i, j: (i, j)),
        out_shape=jax.ShapeDtypeStruct((m, PROJ_WIDTH), F32),
        scratch_shapes=[pltpu.VMEM((tm, D_MODEL), BF16)],
        compiler_params=pltpu.CompilerParams(
            dimension_semantics=("arbitrary", "arbitrary"), vmem_limit_bytes=VMEM_LIMIT),
        name="inproj",
    )(x2d, g_row, w_bf16)


def _mixer_geometry(seq_len):
    rows = MIX_ROWS
    length = min(seq_len, rows)
    assert rows % length == 0 and seq_len % length == 0 and length % SUBLANES == 0
    return rows, length, rows // length, seq_len // length


def _gdn_kernel(qkv_ref, z_ref, ba_ref, cbuf_ref, s0_ref, convw_ref, alog_r_ref, dtb_r_ref,
                alog_c_ref, dtb_c_ref, ng_ref, o_ref, s_ref, xp_ref, *, seq_len):
    R, L, G, _ = _mixer_geometry(seq_len)
    c = pl.program_id(1)
    width = 3 * GDN_WIDTH
    hist = GDN_CONV - 1
    cw = convw_ref[...]

    @pl.when(c == 0)
    def _():
        s_ref[...] = s0_ref[...]

    if G == 1:
        @pl.when(c == 0)
        def _():
            xp_ref[0:SUBLANES, :] = jnp.zeros((SUBLANES, width), F32)
            xp_ref[SUBLANES - hist:SUBLANES, :] = cbuf_ref[0]

        @pl.when(c > 0)
        def _():
            xp_ref[0:SUBLANES, :] = xp_ref[R:R + SUBLANES, :]

        xp_ref[SUBLANES:SUBLANES + R, :] = qkv_ref[...]
        acc = qkv_ref[...] * cw[hist:hist + 1, :]
        for i in range(hist):
            off = SUBLANES - hist + i
            acc = acc + xp_ref[off:off + R, :] * cw[i:i + 1, :]
    else:
        pieces = []
        for g in range(G):
            rows = slice(g * L, (g + 1) * L)
            xp_ref[g, SUBLANES - hist:SUBLANES, :] = cbuf_ref[g]
            xp_ref[g, SUBLANES:SUBLANES + L, :] = qkv_ref[rows, :]
            piece = qkv_ref[rows, :] * cw[hist:hist + 1, :]
            for i in range(hist):
                off = SUBLANES - hist + i
                piece = piece + xp_ref[g, off:off + L, :] * cw[i:i + 1, :]
            pieces.append(piece)
        acc = jnp.concatenate(pieces, axis=0)
    qkv = _silu(acc)

    same, causal, strict = _seq_masks(R, L)
    causal01 = jnp.where(causal, 1.0, 0.0).astype(F32)

    ba = ba_ref[...]
    ba_t = ba.T
    beta_c = _sigmoid(ba)
    g_c = -jnp.exp(alog_r_ref[...]) * _softplus(ba + dtb_r_ref[...])
    g_r = -jnp.exp(alog_c_ref[...]) * _softplus(ba_t + dtb_c_ref[...])
    gc_all = _ones_dot(causal01, g_c)
    gr_all = _ones_dot(causal01, g_r, NT)
    if G == 1:
        gtot_all = jnp.broadcast_to(gc_all[R - 1:R, :], (R, LANES))
    else:
        gtot_all = _ones_dot(jnp.where(same, 1.0, 0.0).astype(F32), g_c)

    ng = ng_ref[...]
    heads = range(GDN_HEADS)
    seqs = range(G)
    seq_rows = [slice(g * L, (g + 1) * L) for g in seqs]
    qs, ks, vs, betas, gcols, gtots, decays = [], [], [], [], [], [], []
    for h in heads:
        lo = h * GDN_DK
        q = qkv[:, lo:lo + GDN_DK]
        k = qkv[:, GDN_WIDTH + lo:GDN_WIDTH + lo + GDN_DK]
        qs.append(q * lax.rsqrt(jnp.sum(q * q, axis=-1, keepdims=True) + L2_EPS) * (GDN_DK ** -0.5))
        ks.append(k * lax.rsqrt(jnp.sum(k * k, axis=-1, keepdims=True) + L2_EPS))
        vs.append(qkv[:, 2 * GDN_WIDTH + lo:2 * GDN_WIDTH + lo + GDN_DK])
        betas.append(beta_c[:, h:h + 1])
        gcol = gc_all[:, GDN_HEADS + h:GDN_HEADS + h + 1]
        grow = gr_all[GDN_HEADS + h:GDN_HEADS + h + 1, :]
        gcols.append(gcol)
        gtots.append(gtot_all[:, GDN_HEADS + h:GDN_HEADS + h + 1])
        decays.append(jnp.where(causal, jnp.exp(jnp.where(causal, gcol - grow, 0.0)), 0.0))
    kqs = [_dot(jnp.concatenate([ks[h], qs[h]], axis=0), ks[h], NT, mode=P_GRAM) for h in heads]
    t_invs = _unit_lower_inverses(
        [jnp.where(strict, betas[h] * kqs[h][:R] * decays[h], 0.0) for h in heads], R, L)
    gammas = [jnp.exp(gcols[h]) for h in heads]
    sols = [_dot(t_invs[h], jnp.concatenate([(betas[h] * gammas[h]) * ks[h], betas[h] * vs[h]], axis=1),
                 mode=P_SOLVE) for h in heads]
    states = [[s_ref[g, h] for g in seqs] for h in heads]
    wss = [[_dot(jnp.concatenate([sols[h][rows, :GDN_DK], (qs[h] * gammas[h])[rows]], axis=0),
                 states[h][g], mode=P_STATE) for g, rows in enumerate(seq_rows)] for h in heads]
    us = [jnp.concatenate([sols[h][rows, GDN_DK:] - wss[h][g][:L] for g, rows in enumerate(seq_rows)],
                          axis=0) for h in heads]
    outs = [jnp.concatenate([wss[h][g][L:] for g in seqs], axis=0)
            + _dot(kqs[h][R:] * decays[h], us[h], mode=P_OUT) for h in heads]
    for h in heads:
        kt = ks[h] * jnp.exp(gtots[h] - gcols[h])
        for g, rows in enumerate(seq_rows):
            gl = jnp.exp(gtots[h][g * L:g * L + 1, :])
            s_ref[g, h] = gl * states[h][g] + _dot(kt[rows], us[h][rows], TN, mode=P_STATE)
    for h in heads:
        lo = h * GDN_DK
        o = outs[h]
        o = o * lax.rsqrt(jnp.mean(o * o, axis=-1, keepdims=True) + RMS_EPS) * ng
        o = o * _silu(z_ref[:, lo:lo + GDN_DK])
        o_ref[:, lo:lo + GDN_DK] = o.astype(o_ref.dtype)


def _gdn_mixer(proj2d, cbuf, s0, convw, alog_r, dtb_r, alog_c, dtb_c, ng, seq_len):
    m = proj2d.shape[0]
    R, L, G, sps = _mixer_geometry(seq_len)
    nseq = m // seq_len
    width = 3 * GDN_WIDTH
    const2 = lambda i, c: (0, 0)
    rows_map = lambda col: (lambda i, c: (i * sps + c, col))
    xp_shape = (SUBLANES + R, width) if G == 1 else (G, SUBLANES + L, width)
    return pl.pallas_call(
        functools.partial(_gdn_kernel, seq_len=seq_len),
        grid=(nseq // G, sps),
        in_specs=[
            pl.BlockSpec((R, width), rows_map(COL_QKV // width)),
            pl.BlockSpec((R, GDN_WIDTH), rows_map(COL_Z // GDN_WIDTH)),
            pl.BlockSpec((R, LANES), rows_map(COL_BA // LANES)),
            pl.BlockSpec((G, GDN_CONV - 1, width), lambda i, c: (i, 0, 0)),
            pl.BlockSpec((G, GDN_HEADS, GDN_DK, GDN_DK), lambda i, c: (i, 0, 0, 0)),
            pl.BlockSpec((GDN_CONV, width), const2),
            pl.BlockSpec((1, LANES), const2),
            pl.BlockSpec((1, LANES), const2),
            pl.BlockSpec((LANES, 1), const2),
            pl.BlockSpec((LANES, 1), const2),
            pl.BlockSpec((1, GDN_DK), const2),
        ],
        out_specs=[
            pl.BlockSpec((R, GDN_WIDTH), rows_map(0)),
            pl.BlockSpec((G, GDN_HEADS, GDN_DK, GDN_DK), lambda i, c: (i, 0, 0, 0)),
        ],
        out_shape=[
            jax.ShapeDtypeStruct((m, GDN_WIDTH), BF16),
            jax.ShapeDtypeStruct((nseq, GDN_HEADS, GDN_DK, GDN_DK), F32),
        ],
        scratch_shapes=[pltpu.VMEM(xp_shape, F32)],
        compiler_params=pltpu.CompilerParams(
            dimension_semantics=("arbitrary", "arbitrary"), vmem_limit_bytes=VMEM_LIMIT),
        name="gdn_mixer",
    )(proj2d, proj2d, proj2d, cbuf, s0, convw, alog_r, dtb_r, alog_c, dtb_c, ng)


def _shifted_rows(x_ref, prev_ref, carry_ref, c, R, L, G):
    x = x_ref[...]
    if G == 1:
        width = x.shape[1]

        @pl.when(c == 0)
        def _():
            carry_ref[0:SUBLANES, :] = jnp.zeros((SUBLANES, width), F32)
            carry_ref[SUBLANES - 1:SUBLANES, :] = prev_ref[0]

        @pl.when(c > 0)
        def _():
            carry_ref[0:SUBLANES, :] = carry_ref[R:R + SUBLANES, :]

        carry_ref[SUBLANES:SUBLANES + R, :] = x
        return x, carry_ref[SUBLANES - 1:SUBLANES - 1 + R, :]
    row = lax.broadcasted_iota(jnp.int32, (L, x.shape[1]), 0)
    pieces = []
    for g in range(G):
        xg = x[g * L:(g + 1) * L]
        pieces.append(jnp.where(row == 0, prev_ref[g], pltpu.roll(xg, 1, 0)))
    return x, jnp.concatenate(pieces, axis=0)


def _rwkv_kernel(rkv_ref, lora_ref, sh_rkv_ref, sh_lora_ref, s0_ref, mu_rkv_ref, mu_lora_ref,
                 w0_ref, a0_ref, wab_ref, gb_ref, kk_ref, ka_ref, rk_ref, gnw_ref, gnb_ref,
                 o_ref, s_ref, xr_ref, xl_ref, *, seq_len):
    R, L, G, _ = _mixer_geometry(seq_len)
    c = pl.program_id(1)
    W = RWKV_WIDTH

    @pl.when(c == 0)
    def _():
        s_ref[...] = s0_ref[...]

    p, p_prev = _shifted_rows(rkv_ref, sh_rkv_ref, xr_ref, c, R, L, G)
    xs = p + (p_prev - p) * mu_rkv_ref[...]
    pl_, pl_prev = _shifted_rows(lora_ref, sh_lora_ref, xl_ref, c, R, L, G)
    xl = pl_ + (pl_prev - pl_) * mu_lora_ref[...]
    r = xs[:, :W]
    k = xs[:, W:2 * W]
    v = xs[:, 2 * W:]

    wa_in = xl[:, :LANES]
    lane = lax.broadcasted_iota(jnp.int32, (R, LANES), 1)
    wa_in = jnp.where(lane < RWKV_LORA_W, jnp.tanh(wa_in), wa_in)
    wa = _dot(wa_in, wab_ref[...])
    w = -_softplus(-(w0_ref[...] + wa[:, :W])) - 0.5
    lw = -jnp.exp(w)
    a = _sigmoid(a0_ref[...] + wa[:, W:])
    gate = _dot(_sigmoid(xl[:, LANES:]), gb_ref[...])
    kk_raw = k * kk_ref[...]
    k2 = k * (1.0 + (a - 1.0) * ka_ref[...])
    rkr = r * k2 * rk_ref[...]

    same, causal, strict = _seq_masks(R, L)
    lc = _ones_dot(jnp.where(causal, 1.0, 0.0).astype(F32), lw)
    if G == 1:
        ltot = jnp.broadcast_to(lc[R - 1:R, :], (R, W))
    else:
        ltot = _ones_dot(jnp.where(same, 1.0, 0.0).astype(F32), lw)
    e_inc = jnp.exp(lc)
    e_exc = jnp.exp(lc - lw)
    e_inv = jnp.exp(-lc)
    e_rem = jnp.exp(ltot - lc)
    p_all = jnp.exp(ltot)

    heads = range(RWKV_HEADS)
    seqs = range(G)
    seq_rows = [slice(g * L, (g + 1) * L) for g in seqs]
    lanes = [slice(h * RWKV_HEAD, (h + 1) * RWKV_HEAD) for h in heads]
    cts, rts, bhs, khs, bbs, kbs, vhs = [], [], [], [], [], [], []
    for sl in lanes:
        kk = kk_raw[:, sl]
        kk = kk * lax.rsqrt(jnp.sum(kk * kk, axis=-1, keepdims=True) + L2_EPS)
        kka = kk * a[:, sl]
        k2h = k2[:, sl]
        cts.append(kk * e_exc[:, sl])
        rts.append(r[:, sl] * e_inc[:, sl])
        bhs.append(kka * e_inv[:, sl])
        khs.append(k2h * e_inv[:, sl])
        bbs.append(kka * e_rem[:, sl])
        kbs.append(k2h * e_rem[:, sl])
        vhs.append(v[:, sl])
    aas = [_dot(jnp.concatenate([cts[h], rts[h]], axis=0),
                jnp.concatenate([bhs[h], khs[h]], axis=0), NT, mode=P_GRAM) for h in heads]
    states = [[s_ref[g, h] for g in seqs] for h in heads]
    xms = [[_dot(jnp.concatenate([cts[h][rows], rts[h][rows]], axis=0), states[h][g], NT, mode=P_STATE)
            for g, rows in enumerate(seq_rows)] for h in heads]
    t_invs = _unit_lower_inverses([jnp.where(strict, aas[h][:R, :R], 0.0) for h in heads], R, L)
    akvs = [_dot(jnp.where(strict, aas[h][:R, R:], 0.0), vhs[h], mode=P_OUT) for h in heads]
    us = [_dot(t_invs[h], -(jnp.concatenate([xms[h][g][:L] for g in seqs], axis=0) + akvs[h]),
               mode=P_SOLVE) for h in heads]
    yhs = [jnp.concatenate([xms[h][g][L:] for g in seqs], axis=0)
           + _dot(jnp.concatenate([jnp.where(causal, aas[h][R:, :R], 0.0),
                                   jnp.where(causal, aas[h][R:, R:], 0.0)], axis=1),
                  jnp.concatenate([us[h], vhs[h]], axis=0), mode=P_OUT) for h in heads]
    for h in heads:
        for g, rows in enumerate(seq_rows):
            uv_g = jnp.concatenate([us[h][rows], vhs[h][rows]], axis=0)
            bk_g = jnp.concatenate([bbs[h][rows], kbs[h][rows]], axis=0)
            s_ref[g, h] = (states[h][g] * p_all[g * L:g * L + 1, lanes[h]]
                           + _dot(uv_g, bk_g, TN, mode=P_STATE))
    for h in heads:
        sl = lanes[h]
        yh = yhs[h]
        mean = jnp.mean(yh, axis=-1, keepdims=True)
        yc = yh - mean
        var = jnp.mean(yc * yc, axis=-1, keepdims=True)
        yn = yc * lax.rsqrt(var + GN_EPS) * gnw_ref[:, sl] + gnb_ref[:, sl]
        yn = yn + jnp.sum(rkr[:, sl], axis=-1, keepdims=True) * vhs[h]
        o_ref[:, sl] = (yn * gate[:, sl]).astype(o_ref.dtype)


def _rwkv_mixer(proj2d, sh_rkv, sh_lora, s0, mu_rkv, mu_lora, w0, a0, wab, gb, kk, ka, rk, gnw, gnb,
                seq_len):
    m = proj2d.shape[0]
    R, L, G, sps = _mixer_geometry(seq_len)
    nseq = m // seq_len
    W = RWKV_WIDTH
    const2 = lambda i, c: (0, 0)
    row = lambda width: pl.BlockSpec((1, width), const2)
    rows_map = lambda col: (lambda i, c: (i * sps + c, col))
    state_spec = pl.BlockSpec((G, RWKV_HEADS, RWKV_HEAD, RWKV_HEAD), lambda i, c: (i, 0, 0, 0))
    carry = lambda width: pltpu.VMEM((SUBLANES + R, width) if G == 1 else (SUBLANES, LANES), F32)
    return pl.pallas_call(
        functools.partial(_rwkv_kernel, seq_len=seq_len),
        grid=(nseq // G, sps),
        in_specs=[
            pl.BlockSpec((R, 3 * W), rows_map(COL_RKV // (3 * W))),
            pl.BlockSpec((R, RWKV_LORA), rows_map(COL_LORA // RWKV_LORA)),
            pl.BlockSpec((G, 1, 3 * W), lambda i, c: (i, 0, 0)),
            pl.BlockSpec((G, 1, RWKV_LORA), lambda i, c: (i, 0, 0)),
            state_spec,
            row(3 * W), row(RWKV_LORA), row(W), row(W),
            pl.BlockSpec((LANES, 2 * W), const2),
            pl.BlockSpec((RWKV_LORA_G, W), const2),
            row(W), row(W), row(W), row(W), row(W),
        ],
        out_specs=[pl.BlockSpec((R, W), rows_map(0)), state_spec],
        out_shape=[
            jax.ShapeDtypeStruct((m, W), BF16),
            jax.ShapeDtypeStruct((nseq, RWKV_HEADS, RWKV_HEAD, RWKV_HEAD), F32),
        ],
        scratch_shapes=[carry(3 * W), carry(RWKV_LORA)],
        compiler_params=pltpu.CompilerParams(
            dimension_semantics=("arbitrary", "arbitrary"), vmem_limit_bytes=VMEM_LIMIT),
        name="rwkv_mixer",
    )(proj2d, proj2d, sh_rkv, sh_lora, s0, mu_rkv, mu_lora, w0, a0, wab, gb, kk, ka, rk, gnw, gnb)


def _outproj_kernel(x_ref, oa_ref, ob_ref, wa_ref, wb_ref, g_ref, x1_ref, hn_ref):
    x1 = (x_ref[...] + jnp.dot(oa_ref[...], wa_ref[...], preferred_element_type=F32)
          + jnp.dot(ob_ref[...], wb_ref[...], preferred_element_type=F32))
    x1_ref[...] = x1
    ms = jnp.mean(x1 * x1, axis=-1, keepdims=True)
    hn_ref[...] = (x1 * lax.rsqrt(ms + RMS_EPS) * g_ref[...]).astype(BF16)


def _outproj(x2d, oa, ob, wo_bf16, g_row, tm):
    m = x2d.shape[0]
    return pl.pallas_call(
        _outproj_kernel,
        grid=(m // tm,),
        in_specs=[
            pl.BlockSpec((tm, D_MODEL), lambda i: (i, 0)),
            pl.BlockSpec((tm, GDN_WIDTH), lambda i: (i, 0)),
            pl.BlockSpec((tm, RWKV_WIDTH), lambda i: (i, 0)),
            pl.BlockSpec((GDN_WIDTH, D_MODEL), lambda i: (0, 0)),
            pl.BlockSpec((RWKV_WIDTH, D_MODEL), lambda i: (1, 0)),
            pl.BlockSpec((1, D_MODEL), lambda i: (0, 0)),
        ],
        out_specs=[pl.BlockSpec((tm, D_MODEL), lambda i: (i, 0)),
                   pl.BlockSpec((tm, D_MODEL), lambda i: (i, 0))],
        out_shape=[jax.ShapeDtypeStruct((m, D_MODEL), F32),
                   jax.ShapeDtypeStruct((m, D_MODEL), BF16)],
        compiler_params=pltpu.CompilerParams(
            dimension_semantics=("arbitrary",), vmem_limit_bytes=VMEM_LIMIT),
        name="outproj",
    )(x2d, oa, ob, wo_bf16, wo_bf16, g_row)


FFN_TF = 512
FFN_NF = D_FF // FFN_TF


def _ffn_zero_acc(f, acc_ref):
    @pl.when(f == 0)
    def _():
        acc_ref[...] = jnp.zeros(acc_ref.shape, F32)


def _ffn_tail(f, act, wd_ref, x1_ref, fg_ref, y_ref, acc_ref):
    acc_ref[...] += jnp.dot(act.astype(BF16), wd_ref[...], preferred_element_type=F32)

    @pl.when(f == FFN_NF - 1)
    def _():
        xo = x1_ref[...] + acc_ref[...]
        ms = jnp.mean(xo * xo, axis=-1, keepdims=True)
        y_ref[...] = xo * lax.rsqrt(ms + RMS_EPS) * fg_ref[...]


def _ffn_long_kernel(hn_ref, x1_ref, wg_ref, wu_ref, cwg_ref, cwu_ref, wd_ref, fg_ref,
                     y_ref, ng_ref, nu_ref, acc_ref, carry_ref, hbuf_ref, *, tt):
    ti = pl.program_id(1)
    f = pl.program_id(2)
    _ffn_zero_acc(f, acc_ref)
    hn = hn_ref[...]
    convs = []
    for j, (w_ref, cw_ref, n_ref) in enumerate(((wg_ref, cwg_ref, ng_ref), (wu_ref, cwu_ref, nu_ref))):
        h = jnp.dot(hn, w_ref[...], preferred_element_type=F32)
        hbuf_ref[j, SUBLANES:SUBLANES + tt, :] = h
        prev = carry_ref[f, j]
        hbuf_ref[j, 0:SUBLANES, :] = jnp.where(ti == 0, jnp.zeros_like(prev), prev)
        cw = cw_ref[...]
        conv = h * cw[FFN_CONV - 1:FFN_CONV, :]
        for i in range(FFN_CONV - 1):
            off = SUBLANES - (FFN_CONV - 1) + i
            conv = conv + hbuf_ref[j, off:off + tt, :] * cw[i:i + 1, :]
        carry_ref[f, j] = h[tt - SUBLANES:, :]
        n_ref[...] = h[tt - (FFN_CONV - 1):, :]
        convs.append(conv)
    act = _silu(convs[0]) * convs[1]
    _ffn_tail(f, act, wd_ref, x1_ref, fg_ref, y_ref, acc_ref)


def _ffn_long(hn, x1, wup, cw, wdown, fg_row, tt):
    b, t, _ = hn.shape
    tf = FFN_TF
    nf = FFN_NF
    return pl.pallas_call(
        functools.partial(_ffn_long_kernel, tt=tt),
        grid=(b, t // tt, nf),
        in_specs=[
            pl.BlockSpec((None, tt, D_MODEL), lambda i, s, f: (i, s, 0)),
            pl.BlockSpec((None, tt, D_MODEL), lambda i, s, f: (i, s, 0)),
            pl.BlockSpec((D_MODEL, tf), lambda i, s, f: (0, f)),
            pl.BlockSpec((D_MODEL, tf), lambda i, s, f: (0, nf + f)),
            pl.BlockSpec((FFN_CONV, tf), lambda i, s, f: (0, f)),
            pl.BlockSpec((FFN_CONV, tf), lambda i, s, f: (0, nf + f)),
            pl.BlockSpec((tf, D_MODEL), lambda i, s, f: (f, 0)),
            pl.BlockSpec((1, D_MODEL), lambda i, s, f: (0, 0)),
        ],
        out_specs=[
            pl.BlockSpec((None, tt, D_MODEL), lambda i, s, f: (i, s, 0)),
            pl.BlockSpec((None, None, FFN_CONV - 1, tf), lambda i, s, f: (i, s, 0, f)),
            pl.BlockSpec((None, None, FFN_CONV - 1, tf), lambda i, s, f: (i, s, 0, f)),
        ],
        out_shape=[
            jax.ShapeDtypeStruct((b, t, D_MODEL), F32),
            jax.ShapeDtypeStruct((b, t // tt, FFN_CONV - 1, D_FF), F32),
            jax.ShapeDtypeStruct((b, t // tt, FFN_CONV - 1, D_FF), F32),
        ],
        scratch_shapes=[
            pltpu.VMEM((tt, D_MODEL), F32),
            pltpu.VMEM((nf, 2, SUBLANES, tf), F32),
            pltpu.VMEM((2, SUBLANES + tt, tf), F32),
        ],
        compiler_params=pltpu.CompilerParams(
            dimension_semantics=("arbitrary", "arbitrary", "arbitrary"),
            vmem_limit_bytes=VMEM_LIMIT),
        name="ffn_long",
    )(hn, x1, wup, wup, cw, cw, wdown, fg_row)


def _ffn_short_kernel(hn_ref, x1_ref, wg_ref, wu_ref, cwg_ref, cwu_ref, wd_ref, fg_ref,
                      b0g_ref, b1g_ref, b0u_ref, b1u_ref,
                      y_ref, n0g_ref, n1g_ref, n0u_ref, n1u_ref, acc_ref, z_ref, hb_ref, *, tt, seq):
    f = pl.program_id(1)
    _ffn_zero_acc(f, acc_ref)
    nseq = tt // seq
    hn = hn_ref[...]
    t_in_seq = lax.broadcasted_iota(jnp.int32, (tt, FFN_TF), 0) % seq
    convs = []
    groups = ((wg_ref, cwg_ref, b0g_ref, b1g_ref, n0g_ref, n1g_ref),
              (wu_ref, cwu_ref, b0u_ref, b1u_ref, n0u_ref, n1u_ref))
    for w_ref, cw_ref, b0_ref, b1_ref, n0_ref, n1_ref in groups:
        h = jnp.dot(hn, w_ref[...], preferred_element_type=F32)
        z_ref[...] = jnp.zeros(z_ref.shape, F32)
        for lb in range(FFN_TF // LANES):
            cols = slice(lb * LANES, (lb + 1) * LANES)
            z_ref[lb, pl.ds(0, nseq, stride=seq), :] = b0_ref[:, cols]
            z_ref[lb, pl.ds(1, nseq, stride=seq), :] = b1_ref[:, cols]
            hb_ref[lb] = h[:, cols]
            n0_ref[:, cols] = hb_ref[lb, pl.ds(seq - 2, nseq, stride=seq), :]
            n1_ref[:, cols] = hb_ref[lb, pl.ds(seq - 1, nseq, stride=seq), :]
        z = jnp.concatenate([z_ref[lb] for lb in range(FFN_TF // LANES)], axis=1)
        s1 = jnp.where(t_in_seq == 0, pltpu.roll(z, tt - 1, 0), pltpu.roll(h, 1, 0))
        s2 = jnp.where(t_in_seq < 2, z, pltpu.roll(h, 2, 0))
        cw = cw_ref[...]
        convs.append(h * cw[2:3, :] + s1 * cw[1:2, :] + s2 * cw[0:1, :])
    act = _silu(convs[0]) * convs[1]
    _ffn_tail(f, act, wd_ref, x1_ref, fg_ref, y_ref, acc_ref)


def _ffn_short(hn, x1, wup, cw, wdown, fg_row, buf0, buf1, tt, seq):
    m = hn.shape[0]
    tf = FFN_TF
    nf = FFN_NF
    nseq = tt // seq
    st_g = pl.BlockSpec((nseq, tf), lambda i, f: (i, f))
    st_u = pl.BlockSpec((nseq, tf), lambda i, f: (i, nf + f))
    new = pl.BlockSpec((nseq, tf), lambda i, f: (i, f))
    new_shape = jax.ShapeDtypeStruct((m // seq, D_FF), F32)
    return pl.pallas_call(
        functools.partial(_ffn_short_kernel, tt=tt, seq=seq),
        grid=(m // tt, nf),
        in_specs=[
            pl.BlockSpec((tt, D_MODEL), lambda i, f: (i, 0)),
            pl.BlockSpec((tt, D_MODEL), lambda i, f: (i, 0)),
            pl.BlockSpec((D_MODEL, tf), lambda i, f: (0, f)),
            pl.BlockSpec((D_MODEL, tf), lambda i, f: (0, nf + f)),
            pl.BlockSpec((FFN_CONV, tf), lambda i, f: (0, f)),
            pl.BlockSpec((FFN_CONV, tf), lambda i, f: (0, nf + f)),
            pl.BlockSpec((tf, D_MODEL), lambda i, f: (f, 0)),
            pl.BlockSpec((1, D_MODEL), lambda i, f: (0, 0)),
            st_g, st_g, st_u, st_u,
        ],
        out_specs=[pl.BlockSpec((tt, D_MODEL), lambda i, f: (i, 0)), new, new, new, new],
        out_shape=[jax.ShapeDtypeStruct((m, D_MODEL), F32), new_shape, new_shape, new_shape, new_shape],
        scratch_shapes=[
            pltpu.VMEM((tt, D_MODEL), F32),
            pltpu.VMEM((tf // LANES, tt, LANES), F32),
            pltpu.VMEM((tf // LANES, tt, LANES), F32),
        ],
        compiler_params=pltpu.CompilerParams(
            dimension_semantics=("arbitrary", "arbitrary"), vmem_limit_bytes=VMEM_LIMIT),
        name="ffn_short",
    )(hn, x1, wup, wup, cw, cw, wdown, fg_row, buf0, buf1, buf0, buf1)


def _pad_lanes(vec, offset):
    out = jnp.zeros((LANES,), F32)
    return out.at[offset:offset + vec.shape[0]].set(vec.astype(F32))


def _trunk(x, s_gdn, s_gconv, s_rwkv, s_shift, s_ffn, prm, *, long_seq):
    b, t, _ = x.shape
    m = b * t
    x2d = x.reshape(m, D_MODEL)
    tm = min(512, m)
    proj = _inproj(x2d, prm["ln1_g"], prm["w_in"], min(1024, m))

    o_a, gdn_new = _gdn_mixer(proj, s_gconv, s_gdn, prm["gdn_conv_w"], prm["alog_r"], prm["dtb_r"],
                              prm["alog_c"], prm["dtb_c"], prm["gdn_norm_g"], t)
    sh_rkv = s_shift[:, None, :3 * RWKV_WIDTH]
    sh_lora = s_shift[:, None, 3 * RWKV_WIDTH:]
    o_b, rwkv_new = _rwkv_mixer(proj, sh_rkv, sh_lora, s_rwkv, prm["mu_rkv"], prm["mu_lora"],
                                prm["rwkv_w0"], prm["rwkv_a0"], prm["rwkv_wab"], prm["rwkv_g_b"],
                                prm["rwkv_k_k"], prm["rwkv_k_a"], prm["rwkv_r_k"], prm["rwkv_gn_w"],
                                prm["rwkv_gn_b"], t)

    x1, hn = _outproj(x2d, o_a, o_b, prm["w_o"], prm["ln2_g"], tm)
    if long_seq:
        tt = min(512, t)
        y, n_g, n_u = _ffn_long(hn.reshape(b, t, D_MODEL), x1.reshape(b, t, D_MODEL), prm["ffn_w_up"],
                                prm["ffn_conv_w"], prm["ffn_w_down"], prm["final_g"], tt)
        ffn_new = jnp.concatenate([n_g[:, -1], n_u[:, -1]], axis=-1)
    else:
        tt = min(512, m)
        y, n0g, n1g, n0u, n1u = _ffn_short(hn, x1, prm["ffn_w_up"], prm["ffn_conv_w"],
                                            prm["ffn_w_down"], prm["final_g"],
                                            s_ffn[:, 0], s_ffn[:, 1], tt, t)
        y = y.reshape(b, t, D_MODEL)
        ffn_new = jnp.stack([jnp.concatenate([n0g, n0u], axis=-1),
                             jnp.concatenate([n1g, n1u], axis=-1)], axis=1)

    proj = proj.reshape(b, t, PROJ_WIDTH)
    gconv_new = proj[:, t - (GDN_CONV - 1):, COL_QKV:COL_QKV + 3 * GDN_WIDTH]
    shift_new = jnp.concatenate([proj[:, t - 1, COL_RKV:COL_RKV + 3 * RWKV_WIDTH],
                                 proj[:, t - 1, COL_LORA:COL_LORA + RWKV_LORA]], axis=-1)
    return y, gdn_new[None], gconv_new[None], rwkv_new[None], shift_new[None], ffn_new[None]


def kernel(x_prompt, x_sample, state_gdn, state_gdn_conv, state_rwkv, state_rwkv_shift, state_ffn_conv, ln1_g, w_in, gdn_conv_w, gdn_a_log, gdn_dt_bias, gdn_norm_g, rwkv_mu, rwkv_w0, rwkv_w_b, rwkv_a0, rwkv_a_b, rwkv_g_b, rwkv_k_k, rwkv_k_a, rwkv_r_k, rwkv_gn_w, rwkv_gn_b, w_o, ln2_g, ffn_w_up, ffn_conv_w, ffn_w_down, final_g):
    assert ln1_g.shape[0] == 1, "single-layer trunk"
    w = w_in[0]
    rw = REF_OFF_RWKV
    w_perm = jnp.concatenate([
        w[:, :REF_OFF_Z],
        w[:, rw:rw + 3 * RWKV_WIDTH],
        w[:, REF_OFF_Z:REF_OFF_B],
        w[:, rw + 3 * RWKV_WIDTH:],
        w[:, REF_OFF_B:REF_OFF_RWKV],
        jnp.zeros((D_MODEL, PROJ_WIDTH - COL_BA - 2 * GDN_HEADS), w.dtype),
    ], axis=1).astype(BF16)
    mu = rwkv_mu[0]
    zeros_w = jnp.zeros((RWKV_LORA_W, RWKV_WIDTH), F32)
    wab = jnp.concatenate([
        jnp.concatenate([rwkv_w_b[0], zeros_w], axis=1),
        jnp.concatenate([zeros_w, rwkv_a_b[0]], axis=1)], axis=0).astype(BF16)
    alog = _pad_lanes(gdn_a_log[0], GDN_HEADS)
    dtb = _pad_lanes(gdn_dt_bias[0], GDN_HEADS)
    prm = {
        "ln1_g": ln1_g[0][None], "w_in": w_perm, "gdn_conv_w": gdn_conv_w[0],
        "alog_r": alog[None], "dtb_r": dtb[None], "alog_c": alog[:, None], "dtb_c": dtb[:, None],
        "gdn_norm_g": gdn_norm_g[0][None],
        "mu_rkv": mu[None, :3 * RWKV_WIDTH], "mu_lora": mu[None, 3 * RWKV_WIDTH:],
        "rwkv_w0": rwkv_w0[0][None], "rwkv_a0": rwkv_a0[0][None], "rwkv_wab": wab,
        "rwkv_g_b": rwkv_g_b[0].astype(BF16), "rwkv_k_k": rwkv_k_k[0][None],
        "rwkv_k_a": rwkv_k_a[0][None], "rwkv_r_k": rwkv_r_k[0].reshape(1, RWKV_WIDTH),
        "rwkv_gn_w": rwkv_gn_w[0][None], "rwkv_gn_b": rwkv_gn_b[0][None],
        "w_o": w_o[0].astype(BF16), "ln2_g": ln2_g[0][None],
        "ffn_w_up": ffn_w_up[0].astype(BF16), "ffn_conv_w": ffn_conv_w[0],
        "ffn_w_down": ffn_w_down[0].astype(BF16), "final_g": final_g[None],
    }

    bp = x_prompt.shape[0]
    zero_states = (
        jnp.zeros((bp,) + state_gdn.shape[2:], F32),
        jnp.zeros((bp,) + state_gdn_conv.shape[2:], F32),
        jnp.zeros((bp,) + state_rwkv.shape[2:], F32),
        jnp.zeros((bp,) + state_rwkv_shift.shape[2:], F32),
        None,
    )
    outs_p = _trunk(x_prompt, *zero_states, prm, long_seq=True)
    outs_s = _trunk(x_sample, state_gdn[0], state_gdn_conv[0], state_rwkv[0], state_rwkv_shift[0],
                    state_ffn_conv[0], prm, long_seq=False)
    return (outs_p[0], outs_s[0]) + tuple(outs_p[1:]) + tuple(outs_s[1:])
```

```python
import functools

import jax
import jax.numpy as jnp
from jax import lax
from jax.experimental import pallas as pl
from jax.experimental.pallas import tpu as pltpu

F32 = jnp.float32
BF16 = jnp.bfloat16

D_MODEL = 2048
GDN_WIDTH = 1024
GDN_HEADS = 8
GDN_DK = 128
GDN_CONV = 4
RWKV_WIDTH = 1024
RWKV_HEAD = 64
RWKV_HEADS = 16
RWKV_LORA_W = 64
RWKV_LORA_A = 64
RWKV_LORA_G = 128
RWKV_LORA = RWKV_LORA_W + RWKV_LORA_A + RWKV_LORA_G
RWKV_PROJ = 3 * RWKV_WIDTH + RWKV_LORA
D_FF = 5632
FFN_CONV = 3
RMS_EPS = 1e-6
L2_EPS = 1e-12
GN_EPS = 64e-5

REF_OFF_Z = 3 * GDN_WIDTH
REF_OFF_B = 4 * GDN_WIDTH
REF_OFF_RWKV = REF_OFF_B + 2 * GDN_HEADS
REF_IN_WIDTH = REF_OFF_RWKV + RWKV_PROJ

LANES = 128
SUBLANES = 8
COL_QKV = 0
COL_RKV = 3 * GDN_WIDTH
COL_Z = COL_RKV + 3 * RWKV_WIDTH
COL_LORA = COL_Z + GDN_WIDTH
COL_BA = COL_LORA + RWKV_LORA
PROJ_WIDTH = 7680
PROJ_TN = 1280

MIX_ROWS = 64
MIX_GROUPS_LONG = 4

NN = (((1,), (0,)), ((), ()))
NT = (((1,), (1,)), ((), ()))
TN = (((0,), (0,)), ((), ()))

VMEM_LIMIT = 56 * 1024 * 1024

P_GRAM = "x1"
P_INV = "x1"
P_SOLVE = "x1"
P_STATE = "x1"
P_OUT = "x1"


def _split(x):
    hi = x.astype(BF16)
    return hi, (x - hi.astype(F32)).astype(BF16)


def _dot(a, b, dims=NN, mode="x1"):
    if mode == "hi":
        return lax.dot_general(a, b, dims, precision=lax.Precision.HIGHEST,
                               preferred_element_type=F32)
    if mode == "x3":
        a_hi, a_lo = _split(a)
        b_hi, b_lo = _split(b)
        d = lambda u, v: lax.dot_general(u, v, dims, preferred_element_type=F32)
        return d(a_hi, b_hi) + (d(a_hi, b_lo) + d(a_lo, b_hi))
    return lax.dot_general(a.astype(BF16), b.astype(BF16), dims, preferred_element_type=F32)


def _ones_dot(ones_mat, x, dims=NN):
    x1 = x.astype(BF16)
    r1 = x - x1.astype(F32)
    x2 = r1.astype(BF16)
    x3 = (r1 - x2.astype(F32)).astype(BF16)
    m = ones_mat.astype(BF16)
    if dims == NN:
        d = lambda v: lax.dot_general(m, v, dims, preferred_element_type=F32)
    else:
        d = lambda v: lax.dot_general(v, m, dims, preferred_element_type=F32)
    return d(x1) + (d(x2) + d(x3))


def _sigmoid(x):
    return 1.0 / (1.0 + jnp.exp(-x))


def _silu(x):
    return x * _sigmoid(x)


def _softplus(x):
    return jnp.maximum(x, 0.0) + jnp.log(1.0 + jnp.exp(-jnp.abs(x)))


def _seq_masks(rows, seq_len):
    r = lax.broadcasted_iota(jnp.int32, (rows, rows), 0)
    c = lax.broadcasted_iota(jnp.int32, (rows, rows), 1)
    if seq_len >= rows:
        return None, r >= c, r > c
    shift = seq_len.bit_length() - 1
    assert 1 << shift == seq_len
    same = jnp.right_shift(r, shift) == jnp.right_shift(c, shift)
    return same, same & (r >= c), same & (r > c)


def _wide_masks(rows, seq_len):
    r = lax.broadcasted_iota(jnp.int32, (rows, 2 * rows), 0)
    c = lax.broadcasted_iota(jnp.int32, (rows, 2 * rows), 1)
    right = c >= rows
    cc = jnp.where(right, c - rows, c)
    if seq_len >= rows:
        return r >= cc, right & (r > cc)
    shift = seq_len.bit_length() - 1
    same = jnp.right_shift(r, shift) == jnp.right_shift(cc, shift)
    return same & (r >= cc), same & right & (r > cc)


def _head_sums(x):
    tile = 2 * LANES
    li = jnp.right_shift(lax.broadcasted_iota(jnp.int32, (tile, tile), 0), 6)
    lj = jnp.right_shift(lax.broadcasted_iota(jnp.int32, (tile, tile), 1), 6)
    ones = jnp.where(li == lj, 1.0, 0.0).astype(BF16)
    hi, lo = _split(x)
    d = lambda u: lax.dot_general(u, ones, NN, preferred_element_type=F32)
    return jnp.concatenate([d(hi[:, t:t + tile]) + d(lo[:, t:t + tile])
                            for t in range(0, x.shape[1], tile)], axis=1)


def _unit_lower_inverses(mats, rows, nilpotency):
    r = lax.broadcasted_iota(jnp.int32, (rows, rows), 0)
    c = lax.broadcasted_iota(jnp.int32, (rows, rows), 1)
    eye = jnp.where(r == c, 1.0, 0.0).astype(F32)
    qs = [-a for a in mats]
    ts = [eye + q for q in qs]
    n = 2
    if n < nilpotency:
        qs = [_dot(q, q, mode=P_INV) for q in qs]
    while n < nilpotency:
        if 2 * n < nilpotency:
            tq = [_dot(jnp.concatenate([t, q], axis=0), q, mode=P_INV) for t, q in zip(ts, qs)]
            ts = [t + p[:rows] for t, p in zip(ts, tq)]
            qs = [p[rows:] for p in tq]
        else:
            ts = [t + _dot(t, q, mode=P_INV) for t, q in zip(ts, qs)]
        n *= 2
    return ts


def _inproj_kernel(x_ref, g_ref, w_ref, o_ref, xn_ref):
    @pl.when(pl.program_id(1) == 0)
    def _():
        x = x_ref[...]
        ms = jnp.mean(x * x, axis=-1, keepdims=True)
        xn_ref[...] = (x * lax.rsqrt(ms + RMS_EPS) * g_ref[...]).astype(BF16)

    o_ref[...] = jnp.dot(xn_ref[...], w_ref[...], preferred_element_type=F32)


def _inproj(x2d, g_row, w_bf16, tm):
    m = x2d.shape[0]
    return pl.pallas_call(
        _inproj_kernel,
        grid=(m // tm, PROJ_WIDTH // PROJ_TN),
        in_specs=[
            pl.BlockSpec((tm, D_MODEL), lambda i, j: (i, 0)),
            pl.BlockSpec((1, D_MODEL), lambda i, j: (0, 0)),
            pl.BlockSpec((D_MODEL, PROJ_TN), lambda i, j: (0, j)),
        ],
        out_specs=pl.BlockSpec((tm, PROJ_TN), lambda i, j: (i, j)),
        out_shape=jax.ShapeDtypeStruct((m, PROJ_WIDTH), F32),
        scratch_shapes=[pltpu.VMEM((tm, D_MODEL), BF16)],
        compiler_params=pltpu.CompilerParams(
            dimension_semantics=("arbitrary", "arbitrary"), vmem_limit_bytes=VMEM_LIMIT),
        name="inproj",
    )(x2d, g_row, w_bf16)


def _mixer_geometry(seq_len, n_seq):
    rows = MIX_ROWS
    length = min(seq_len, rows)
    assert rows % length == 0 and seq_len % length == 0 and length % SUBLANES == 0
    per_group = rows // length
    groups = min(MIX_GROUPS_LONG, n_seq) if per_group == 1 else 1
    assert n_seq % (groups * per_group) == 0
    return rows, length, per_group, groups, seq_len // length


def _gdn_kernel(qkv_ref, z_ref, ba_ref, cbuf_ref, s0_ref, convw_ref, alog_r_ref, dtb_r_ref,
                alog_c_ref, dtb_c_ref, ng_ref, o_ref, s_ref, xp_ref, *, seq_len, n_seq):
    R, L, G, S, _ = _mixer_geometry(seq_len, n_seq)
    RT = S * R
    c = pl.program_id(1)
    width = 3 * GDN_WIDTH
    hist = GDN_CONV - 1
    cw = convw_ref[...]
    groups = range(S)
    seqs = range(G)

    @pl.when(c == 0)
    def _():
        s_ref[...] = s0_ref[...]

    pieces = []
    if G == 1:
        @pl.when(c == 0)
        def _():
            for s in groups:
                xp_ref[s, 0:SUBLANES, :] = jnp.zeros((SUBLANES, width), F32)
                xp_ref[s, SUBLANES - hist:SUBLANES, :] = cbuf_ref[s]

        @pl.when(c > 0)
        def _():
            for s in groups:
                xp_ref[s, 0:SUBLANES, :] = xp_ref[s, R:R + SUBLANES, :]

        for s in groups:
            xp_ref[s, SUBLANES:SUBLANES + R, :] = qkv_ref[s]
            piece = qkv_ref[s] * cw[hist:hist + 1, :]
            for i in range(hist):
                off = SUBLANES - hist + i
                piece = piece + xp_ref[s, off:off + R, :] * cw[i:i + 1, :]
            pieces.append(piece)
    else:
        for s in groups:
            for g in seqs:
                q = s * G + g
                rows = slice(g * L, (g + 1) * L)
                xp_ref[q, SUBLANES - hist:SUBLANES, :] = cbuf_ref[q]
                xp_ref[q, SUBLANES:SUBLANES + L, :] = qkv_ref[s, rows, :]
                piece = qkv_ref[s, rows, :] * cw[hist:hist + 1, :]
                for i in range(hist):
                    off = SUBLANES - hist + i
                    piece = piece + xp_ref[q, off:off + L, :] * cw[i:i + 1, :]
                pieces.append(piece)
    qkv = _silu(pieces[0] if len(pieces) == 1 else jnp.concatenate(pieces, axis=0))

    same_t, causal_t, _ = _seq_masks(RT, L)
    causal01 = jnp.where(causal_t, 1.0, 0.0).astype(F32)
    _, causal, strict = _seq_masks(R, L)

    ba = ba_ref[...].reshape(RT, LANES)
    ba_t = ba.T
    beta_c = _sigmoid(ba)
    g_c = -jnp.exp(alog_r_ref[...]) * _softplus(ba + dtb_r_ref[...])
    g_r = -jnp.exp(alog_c_ref[...]) * _softplus(ba_t + dtb_c_ref[...])
    gc_all = _ones_dot(causal01, g_c)
    gr_all = _ones_dot(causal01, g_r, NT)
    if same_t is None:
        gtot_all = jnp.broadcast_to(gc_all[RT - 1:RT, :], (RT, LANES))
    else:
        gtot_all = _ones_dot(jnp.where(same_t, 1.0, 0.0).astype(F32), g_c)

    ng = ng_ref[...]
    chains = [(s, h) for s in groups for h in range(GDN_HEADS)]
    seq_rows = [slice(g * L, (g + 1) * L) for g in seqs]
    qs, ks, vs, betas, gcols, gtots, decays = {}, {}, {}, {}, {}, {}, {}
    for s, h in chains:
        rs = slice(s * R, (s + 1) * R)
        lo = h * GDN_DK
        q = qkv[rs, lo:lo + GDN_DK]
        k = qkv[rs, GDN_WIDTH + lo:GDN_WIDTH + lo + GDN_DK]
        key = (s, h)
        qs[key] = q * lax.rsqrt(jnp.sum(q * q, axis=-1, keepdims=True) + L2_EPS) * (GDN_DK ** -0.5)
        ks[key] = k * lax.rsqrt(jnp.sum(k * k, axis=-1, keepdims=True) + L2_EPS)
        vs[key] = qkv[rs, 2 * GDN_WIDTH + lo:2 * GDN_WIDTH + lo + GDN_DK]
        betas[key] = beta_c[rs, h:h + 1]
        gcol = gc_all[rs, GDN_HEADS + h:GDN_HEADS + h + 1]
        grow = gr_all[GDN_HEADS + h:GDN_HEADS + h + 1, rs]
        gcols[key] = gcol
        gtots[key] = gtot_all[rs, GDN_HEADS + h:GDN_HEADS + h + 1]
        decays[key] = jnp.where(causal, jnp.exp(jnp.where(causal, gcol - grow, 0.0)), 0.0)
    kqs = {key: _dot(jnp.concatenate([ks[key], qs[key]], axis=0), ks[key], NT, mode=P_GRAM)
           for key in chains}
    t_list = _unit_lower_inverses(
        [jnp.where(strict, betas[key] * kqs[key][:R] * decays[key], 0.0) for key in chains], R, L)
    t_invs = dict(zip(chains, t_list))
    gammas = {key: jnp.exp(gcols[key]) for key in chains}
    sols = {key: _dot(t_invs[key],
                      jnp.concatenate([(betas[key] * gammas[key]) * ks[key], betas[key] * vs[key]], axis=1),
                      mode=P_SOLVE) for key in chains}
    states = {(s, h): [s_ref[s * G + g, h] for g in seqs] for s, h in chains}
    wss = {key: [_dot(jnp.concatenate([sols[key][rows, :GDN_DK], (qs[key] * gammas[key])[rows]], axis=0),
                      states[key][g], mode=P_STATE) for g, rows in enumerate(seq_rows)]
           for key in chains}
    us = {key: jnp.concatenate([sols[key][rows, GDN_DK:] - wss[key][g][:L]
                                for g, rows in enumerate(seq_rows)], axis=0) for key in chains}
    outs = {key: jnp.concatenate([wss[key][g][L:] for g in seqs], axis=0)
            + _dot(kqs[key][R:] * decays[key], us[key], mode=P_OUT) for key in chains}
    for s, h in chains:
        key = (s, h)
        kt = ks[key] * jnp.exp(gtots[key] - gcols[key])
        for g, rows in enumerate(seq_rows):
            gl = jnp.exp(gtots[key][g * L:g * L + 1, :])
            s_ref[s * G + g, h] = gl * states[key][g] + _dot(kt[rows], us[key][rows], TN, mode=P_STATE)
    for s, h in chains:
        lo = h * GDN_DK
        o = outs[(s, h)]
        o = o * lax.rsqrt(jnp.mean(o * o, axis=-1, keepdims=True) + RMS_EPS) * ng
        o = o * _silu(z_ref[s, :, lo:lo + GDN_DK])
        o_ref[s, :, lo:lo + GDN_DK] = o.astype(o_ref.dtype)


def _gdn_mixer(proj2d, cbuf, s0, convw, alog_r, dtb_r, alog_c, dtb_c, ng, seq_len):
    m = proj2d.shape[0]
    nseq = m // seq_len
    R, L, G, S, sps = _mixer_geometry(seq_len, nseq)
    width = 3 * GDN_WIDTH
    proj3d = proj2d.reshape(nseq // G, sps * R, PROJ_WIDTH)
    const2 = lambda i, c: (0, 0)
    rows_map = lambda col: (lambda i, c: (i, c, col))
    per_seq = lambda *dims: pl.BlockSpec((S * G,) + dims, lambda i, c: (i,) + (0,) * len(dims))
    xp_shape = (S, SUBLANES + R, width) if G == 1 else (S * G, SUBLANES + L, width)
    o, s_new = pl.pallas_call(
        functools.partial(_gdn_kernel, seq_len=seq_len, n_seq=nseq),
        grid=(nseq // (S * G), sps),
        in_specs=[
            pl.BlockSpec((S, R, width), rows_map(COL_QKV // width)),
            pl.BlockSpec((S, R, GDN_WIDTH), rows_map(COL_Z // GDN_WIDTH)),
            pl.BlockSpec((S, R, LANES), rows_map(COL_BA // LANES)),
            per_seq(GDN_CONV - 1, width),
            per_seq(GDN_HEADS, GDN_DK, GDN_DK),
            pl.BlockSpec((GDN_CONV, width), const2),
            pl.BlockSpec((1, LANES), const2),
            pl.BlockSpec((1, LANES), const2),
            pl.BlockSpec((LANES, 1), const2),
            pl.BlockSpec((LANES, 1), const2),
            pl.BlockSpec((1, GDN_DK), const2),
        ],
        out_specs=[
            pl.BlockSpec((S, R, GDN_WIDTH), rows_map(0)),
            per_seq(GDN_HEADS, GDN_DK, GDN_DK),
        ],
        out_shape=[
            jax.ShapeDtypeStruct((nseq // G, sps * R, GDN_WIDTH), BF16),
            jax.ShapeDtypeStruct((nseq, GDN_HEADS, GDN_DK, GDN_DK), F32),
        ],
        scratch_shapes=[pltpu.VMEM(xp_shape, F32)],
        compiler_params=pltpu.CompilerParams(
            dimension_semantics=("arbitrary", "arbitrary"), vmem_limit_bytes=VMEM_LIMIT),
        name="gdn_mixer",
    )(proj3d, proj3d, proj3d, cbuf, s0, convw, alog_r, dtb_r, alog_c, dtb_c, ng)
    return o.reshape(m, GDN_WIDTH), s_new


def _shifted_rows(x_ref, prev_ref, carry_ref, c, R, L, G, S):
    width = x_ref.shape[-1]
    groups = range(S)
    if G == 1:
        @pl.when(c == 0)
        def _():
            for s in groups:
                carry_ref[s, 0:SUBLANES, :] = jnp.zeros((SUBLANES, width), F32)
                carry_ref[s, SUBLANES - 1:SUBLANES, :] = prev_ref[s]

        @pl.when(c > 0)
        def _():
            for s in groups:
                carry_ref[s, 0:SUBLANES, :] = carry_ref[s, R:R + SUBLANES, :]

        xs, prevs = [], []
        for s in groups:
            carry_ref[s, SUBLANES:SUBLANES + R, :] = x_ref[s]
            xs.append(x_ref[s])
            prevs.append(carry_ref[s, SUBLANES - 1:SUBLANES - 1 + R, :])
    else:
        row = lax.broadcasted_iota(jnp.int32, (L, width), 0)
        xs, prevs = [], []
        for s in groups:
            x = x_ref[s]
            xs.append(x)
            for g in range(G):
                xg = x[g * L:(g + 1) * L]
                prevs.append(jnp.where(row == 0, prev_ref[s * G + g], pltpu.roll(xg, 1, 0)))
    cat = lambda parts: parts[0] if len(parts) == 1 else jnp.concatenate(parts, axis=0)
    return cat(xs), cat(prevs)


def _rwkv_kernel(rkv_ref, lora_ref, sh_rkv_ref, sh_lora_ref, s0_ref, mu_rkv_ref, mu_lora_ref,
                 w0_ref, a0_ref, wab_ref, gb_ref, kk_ref, ka_ref, rk_ref, gnw_ref, gnb_ref,
                 o_ref, s_ref, xr_ref, xl_ref, m_ref, *, seq_len, n_seq):
    R, L, G, S, sps = _mixer_geometry(seq_len, n_seq)
    RT = S * R
    NQ = S * G
    c = pl.program_id(1)
    W = RWKV_WIDTH
    HD = RWKV_HEAD
    pairs = range(RWKV_HEADS // 2)
    groups = range(S)
    seqs = range(G)

    @pl.when(c == 0)
    def _():
        zero = jnp.zeros((HD, HD), F32)
        for q in range(NQ):
            for j in pairs:
                vk = jnp.concatenate([jnp.concatenate([s0_ref[q, 2 * j], zero], axis=1),
                                      jnp.concatenate([zero, s0_ref[q, 2 * j + 1]], axis=1)], axis=0)
                m_ref[q, j] = vk.T

    p, p_prev = _shifted_rows(rkv_ref, sh_rkv_ref, xr_ref, c, R, L, G, S)
    xs = p + (p_prev - p) * mu_rkv_ref[...]
    pl_, pl_prev = _shifted_rows(lora_ref, sh_lora_ref, xl_ref, c, R, L, G, S)
    xl = pl_ + (pl_prev - pl_) * mu_lora_ref[...]
    r = xs[:, :W]
    k = xs[:, W:2 * W]
    v = xs[:, 2 * W:]

    wa_in = xl[:, :LANES]
    lane = lax.broadcasted_iota(jnp.int32, (RT, LANES), 1)
    wa_in = jnp.where(lane < RWKV_LORA_W, jnp.tanh(wa_in), wa_in)
    wa = _dot(wa_in, wab_ref[...])
    w = -_softplus(-(w0_ref[...] + wa[:, :W])) - 0.5
    lw = -jnp.exp(w)
    a = _sigmoid(a0_ref[...] + wa[:, W:])
    gate = _dot(_sigmoid(xl[:, LANES:]), gb_ref[...])
    kk_raw = k * kk_ref[...]
    k2 = k * (1.0 + (a - 1.0) * ka_ref[...])

    same_t, causal_t, _ = _seq_masks(RT, L)
    lc = _ones_dot(jnp.where(causal_t, 1.0, 0.0).astype(F32), lw)
    if same_t is None:
        ltot = jnp.broadcast_to(lc[RT - 1:RT, :], (RT, W))
    else:
        ltot = _ones_dot(jnp.where(same_t, 1.0, 0.0).astype(F32), lw)
    e_inv = jnp.exp(-lc)
    e_rem = jnp.exp(ltot - lc)

    kk = kk_raw * lax.rsqrt(_head_sums(kk_raw * kk_raw) + L2_EPS)
    kka = kk * a
    ct = kk * jnp.exp(lc - lw)
    rt = r * jnp.exp(lc)
    bh = kka * e_inv
    kh = k2 * e_inv
    bb = kka * e_rem
    kb = k2 * e_rem
    p_rows = [ltot[q * L:q * L + 1] for q in range(NQ)]
    p_rows = p_rows + [p_rows[0]] * (-NQ % SUBLANES)
    pt = jnp.exp(jnp.concatenate(p_rows, axis=0)).T

    _, _, strict = _seq_masks(R, L)
    causal_w, strict_right = _wide_masks(R, L)
    key_even = lax.broadcasted_iota(jnp.int32, (LANES, 2 * R), 0) < HD
    val_even = lax.broadcasted_iota(jnp.int32, (R, LANES), 1) < HD
    block_diag = ((lax.broadcasted_iota(jnp.int32, (LANES, LANES), 0) < HD)
                  == (lax.broadcasted_iota(jnp.int32, (LANES, LANES), 1) < HD))
    tile = lambda j: slice(j * LANES, (j + 1) * LANES)
    pick = lambda even, odd: jnp.where(val_even, even, odd)
    zeros_rv = jnp.zeros((R, LANES), F32)
    if G > 1:
        row2 = lax.broadcasted_iota(jnp.int32, (2 * R, LANES), 0)
        seq_of_row = jnp.right_shift(jnp.bitwise_and(row2, R - 1), L.bit_length() - 1)

    units = [(s, j) for s in groups for j in pairs]
    xps, vps, yts, bkts = {}, {}, {}, {}
    for s in groups:
        rs = slice(s * R, (s + 1) * R)
        x_s = jnp.concatenate([ct[rs], rt[rs]], axis=0)
        yt_s = jnp.concatenate([bh[rs], kh[rs]], axis=0).T
        bkt_s = jnp.concatenate([bb[rs], kb[rs]], axis=0).T
        for j in pairs:
            xps[(s, j)] = x_s[:, tile(j)]
            vps[(s, j)] = v[rs, tile(j)]
            yts[(s, j)] = yt_s[tile(j), :]
            bkts[(s, j)] = bkt_s[tile(j), :]
    aa_e = {u: _dot(xps[u], jnp.where(key_even, yts[u], 0.0), mode=P_GRAM) for u in units}
    aa_o = {u: _dot(xps[u], jnp.where(key_even, 0.0, yts[u]), mode=P_GRAM) for u in units}
    ms = {(s, j): [m_ref[s * G + g, j] for g in seqs] for s, j in units}
    xm_c, xm_r = {}, {}
    for u in units:
        if G == 1:
            xm = _dot(xps[u], ms[u][0], mode=P_STATE)
            xm_c[u], xm_r[u] = xm[:R], xm[R:]
        else:
            parts = [_dot(jnp.concatenate([xps[u][g * L:(g + 1) * L],
                                           xps[u][R + g * L:R + (g + 1) * L]], axis=0),
                          ms[u][g], mode=P_STATE) for g in seqs]
            xm_c[u] = jnp.concatenate([p_[:L] for p_ in parts], axis=0)
            xm_r[u] = jnp.concatenate([p_[L:] for p_ in parts], axis=0)
    t_list = _unit_lower_inverses(
        [jnp.where(strict, aa[u][:R, :R], 0.0) for u in units for aa in (aa_e, aa_o)], R, L)
    t_e = dict(zip(units, t_list[0::2]))
    t_o = dict(zip(units, t_list[1::2]))
    akvs = {}
    for u in units:
        zv = jnp.concatenate([zeros_rv, vps[u]], axis=0)
        akvs[u] = pick(_dot(jnp.where(strict_right, aa_e[u][:R], 0.0), zv, mode=P_OUT),
                       _dot(jnp.where(strict_right, aa_o[u][:R], 0.0), zv, mode=P_OUT))
    us = {}
    for u in units:
        rhs = -(xm_c[u] + akvs[u])
        us[u] = pick(_dot(t_e[u], rhs, mode=P_SOLVE), _dot(t_o[u], rhs, mode=P_SOLVE))
    uvs = {u: jnp.concatenate([us[u], vps[u]], axis=0) for u in units}
    ys = {u: xm_r[u] + pick(_dot(jnp.where(causal_w, aa_e[u][R:], 0.0), uvs[u], mode=P_OUT),
                            _dot(jnp.where(causal_w, aa_o[u][R:], 0.0), uvs[u], mode=P_OUT))
          for u in units}
    for s, j in units:
        u = (s, j)
        for g in seqs:
            q = s * G + g
            uv_g = uvs[u] if G == 1 else jnp.where(seq_of_row == g, uvs[u], 0.0)
            upd = _dot(bkts[u], uv_g, mode=P_STATE)
            m_ref[q, j] = pt[tile(j), q:q + 1] * ms[u][g] + jnp.where(block_diag, upd, 0.0)

    y_rows = [jnp.concatenate([ys[(s, j)] for j in pairs], axis=1) for s in groups]
    y_all = y_rows[0] if S == 1 else jnp.concatenate(y_rows, axis=0)
    mean = _head_sums(y_all) * (1.0 / HD)
    yc = y_all - mean
    var = _head_sums(yc * yc) * (1.0 / HD)
    yn = yc * lax.rsqrt(var + GN_EPS) * gnw_ref[...] + gnb_ref[...]
    yn = yn + _head_sums(r * k2 * rk_ref[...]) * v
    o_ref[...] = (yn * gate).reshape(S, R, W).astype(o_ref.dtype)

    @pl.when(c == sps - 1)
    def _():
        for q in range(NQ):
            for j in pairs:
                vk = m_ref[q, j].T
                s_ref[q, 2 * j] = vk[:HD, :HD]
                s_ref[q, 2 * j + 1] = vk[HD:, HD:]


def _rwkv_mixer(proj2d, sh_rkv, sh_lora, s0, mu_rkv, mu_lora, w0, a0, wab, gb, kk, ka, rk, gnw, gnb,
                seq_len):
    m = proj2d.shape[0]
    nseq = m // seq_len
    R, L, G, S, sps = _mixer_geometry(seq_len, nseq)
    W = RWKV_WIDTH
    proj3d = proj2d.reshape(nseq // G, sps * R, PROJ_WIDTH)
    const2 = lambda i, c: (0, 0)
    row = lambda width: pl.BlockSpec((1, width), const2)
    rows_map = lambda col: (lambda i, c: (i, c, col))
    per_seq = lambda *dims: pl.BlockSpec((S * G,) + dims, lambda i, c: (i,) + (0,) * len(dims))
    carry = lambda width: pltpu.VMEM((S, SUBLANES + R, width) if G == 1 else (SUBLANES, LANES), F32)
    o, s_new = pl.pallas_call(
        functools.partial(_rwkv_kernel, seq_len=seq_len, n_seq=nseq),
        grid=(nseq // (S * G), sps),
        in_specs=[
            pl.BlockSpec((S, R, 3 * W), rows_map(COL_RKV // (3 * W))),
            pl.BlockSpec((S, R, RWKV_LORA), rows_map(COL_LORA // RWKV_LORA)),
            per_seq(1, 3 * W),
            per_seq(1, RWKV_LORA),
            per_seq(RWKV_HEADS, RWKV_HEAD, RWKV_HEAD),
            row(3 * W), row(RWKV_LORA), row(W), row(W),
            pl.BlockSpec((LANES, 2 * W), const2),
            pl.BlockSpec((RWKV_LORA_G, W), const2),
            row(W), row(W), row(W), row(W), row(W),
        ],
        out_specs=[pl.BlockSpec((S, R, W), rows_map(0)), per_seq(RWKV_HEADS, RWKV_HEAD, RWKV_HEAD)],
        out_shape=[
            jax.ShapeDtypeStruct((nseq // G, sps * R, W), BF16),
            jax.ShapeDtypeStruct((nseq, RWKV_HEADS, RWKV_HEAD, RWKV_HEAD), F32),
        ],
        scratch_shapes=[carry(3 * W), carry(RWKV_LORA),
                        pltpu.VMEM((S * G, RWKV_HEADS // 2, LANES, LANES), F32)],
        compiler_params=pltpu.CompilerParams(
            dimension_semantics=("arbitrary", "arbitrary"), vmem_limit_bytes=VMEM_LIMIT),
        name="rwkv_mixer",
    )(proj3d, proj3d, sh_rkv, sh_lora, s0, mu_rkv, mu_lora, w0, a0, wab, gb, kk, ka, rk, gnw, gnb)
    return o.reshape(m, W), s_new


def _outproj_kernel(x_ref, oa_ref, ob_ref, wa_ref, wb_ref, g_ref, x1_ref, hn_ref):
    x1 = (x_ref[...] + jnp.dot(oa_ref[...], wa_ref[...], preferred_element_type=F32)
          + jnp.dot(ob_ref[...], wb_ref[...], preferred_element_type=F32))
    x1_ref[...] = x1
    ms = jnp.mean(x1 * x1, axis=-1, keepdims=True)
    hn_ref[...] = (x1 * lax.rsqrt(ms + RMS_EPS) * g_ref[...]).astype(BF16)


def _outproj(x2d, oa, ob, wo_bf16, g_row, tm):
    m = x2d.shape[0]
    return pl.pallas_call(
        _outproj_kernel,
        grid=(m // tm,),
        in_specs=[
            pl.BlockSpec((tm, D_MODEL), lambda i: (i, 0)),
            pl.BlockSpec((tm, GDN_WIDTH), lambda i: (i, 0)),
            pl.BlockSpec((tm, RWKV_WIDTH), lambda i: (i, 0)),
            pl.BlockSpec((GDN_WIDTH, D_MODEL), lambda i: (0, 0)),
            pl.BlockSpec((RWKV_WIDTH, D_MODEL), lambda i: (1, 0)),
            pl.BlockSpec((1, D_MODEL), lambda i: (0, 0)),
        ],
        out_specs=[pl.BlockSpec((tm, D_MODEL), lambda i: (i, 0)),
                   pl.BlockSpec((tm, D_MODEL), lambda i: (i, 0))],
        out_shape=[jax.ShapeDtypeStruct((m, D_MODEL), F32),
                   jax.ShapeDtypeStruct((m, D_MODEL), BF16)],
        compiler_params=pltpu.CompilerParams(
            dimension_semantics=("arbitrary",), vmem_limit_bytes=VMEM_LIMIT),
        name="outproj",
    )(x2d, oa, ob, wo_bf16, wo_bf16, g_row)


FFN_TF = 512
FFN_NF = D_FF // FFN_TF


def _ffn_zero_acc(f, acc_ref):
    @pl.when(f == 0)
    def _():
        acc_ref[...] = jnp.zeros(acc_ref.shape, F32)


def _ffn_tail(f, act, wd_ref, x1_ref, fg_ref, y_ref, acc_ref):
    acc_ref[...] += jnp.dot(act.astype(BF16), wd_ref[...], preferred_element_type=F32)

    @pl.when(f == FFN_NF - 1)
    def _():
        xo = x1_ref[...] + acc_ref[...]
        ms = jnp.mean(xo * xo, axis=-1, keepdims=True)
        y_ref[...] = xo * lax.rsqrt(ms + RMS_EPS) * fg_ref[...]


def _ffn_long_kernel(hn_ref, x1_ref, wg_ref, wu_ref, cwg_ref, cwu_ref, wd_ref, fg_ref,
                     y_ref, ng_ref, nu_ref, acc_ref, carry_ref, hbuf_ref, *, tt):
    ti = pl.program_id(1)
    f = pl.program_id(2)
    _ffn_zero_acc(f, acc_ref)
    hn = hn_ref[...]
    convs = []
    for j, (w_ref, cw_ref, n_ref) in enumerate(((wg_ref, cwg_ref, ng_ref), (wu_ref, cwu_ref, nu_ref))):
        h = jnp.dot(hn, w_ref[...], preferred_element_type=F32)
        hbuf_ref[j, SUBLANES:SUBLANES + tt, :] = h
        prev = carry_ref[f, j]
        hbuf_ref[j, 0:SUBLANES, :] = jnp.where(ti == 0, jnp.zeros_like(prev), prev)
        cw = cw_ref[...]
        conv = h * cw[FFN_CONV - 1:FFN_CONV, :]
        for i in range(FFN_CONV - 1):
            off = SUBLANES - (FFN_CONV - 1) + i
            conv = conv + hbuf_ref[j, off:off + tt, :] * cw[i:i + 1, :]
        carry_ref[f, j] = h[tt - SUBLANES:, :]
        n_ref[...] = h[tt - (FFN_CONV - 1):, :]
        convs.append(conv)
    act = _silu(convs[0]) * convs[1]
    _ffn_tail(f, act, wd_ref, x1_ref, fg_ref, y_ref, acc_ref)


def _ffn_long(hn, x1, wup, cw, wdown, fg_row, tt):
    b, t, _ = hn.shape
    tf = FFN_TF
    nf = FFN_NF
    return pl.pallas_call(
        functools.partial(_ffn_long_kernel, tt=tt),
        grid=(b, t // tt, nf),
        in_specs=[
            pl.BlockSpec((None, tt, D_MODEL), lambda i, s, f: (i, s, 0)),
            pl.BlockSpec((None, tt, D_MODEL), lambda i, s, f: (i, s, 0)),
            pl.BlockSpec((D_MODEL, tf), lambda i, s, f: (0, f)),
            pl.BlockSpec((D_MODEL, tf), lambda i, s, f: (0, nf + f)),
            pl.BlockSpec((FFN_CONV, tf), lambda i, s, f: (0, f)),
            pl.BlockSpec((FFN_CONV, tf), lambda i, s, f: (0, nf + f)),
            pl.BlockSpec((tf, D_MODEL), lambda i, s, f: (f, 0)),
            pl.BlockSpec((1, D_MODEL), lambda i, s, f: (0, 0)),
        ],
        out_specs=[
            pl.BlockSpec((None, tt, D_MODEL), lambda i, s, f: (i, s, 0)),
            pl.BlockSpec((None, None, FFN_CONV - 1, tf), lambda i, s, f: (i, s, 0, f)),
            pl.BlockSpec((None, None, FFN_CONV - 1, tf), lambda i, s, f: (i, s, 0, f)),
        ],
        out_shape=[
            jax.ShapeDtypeStruct((b, t, D_MODEL), F32),
            jax.ShapeDtypeStruct((b, t // tt, FFN_CONV - 1, D_FF), F32),
            jax.ShapeDtypeStruct((b, t // tt, FFN_CONV - 1, D_FF), F32),
        ],
        scratch_shapes=[
            pltpu.VMEM((tt, D_MODEL), F32),
            pltpu.VMEM((nf, 2, SUBLANES, tf), F32),
            pltpu.VMEM((2, SUBLANES + tt, tf), F32),
        ],
        compiler_params=pltpu.CompilerParams(
            dimension_semantics=("arbitrary", "arbitrary", "arbitrary"),
            vmem_limit_bytes=VMEM_LIMIT),
        name="ffn_long",
    )(hn, x1, wup, wup, cw, cw, wdown, fg_row)


def _ffn_short_kernel(hn_ref, x1_ref, wg_ref, wu_ref, cwg_ref, cwu_ref, wd_ref, fg_ref,
                      b0g_ref, b1g_ref, b0u_ref, b1u_ref,
                      y_ref, n0g_ref, n1g_ref, n0u_ref, n1u_ref, acc_ref, z_ref, hb_ref, *, tt, seq):
    f = pl.program_id(1)
    _ffn_zero_acc(f, acc_ref)
    nseq = tt // seq
    hn = hn_ref[...]
    t_in_seq = lax.broadcasted_iota(jnp.int32, (tt, FFN_TF), 0) % seq
    convs = []
    groups = ((wg_ref, cwg_ref, b0g_ref, b1g_ref, n0g_ref, n1g_ref),
              (wu_ref, cwu_ref, b0u_ref, b1u_ref, n0u_ref, n1u_ref))
    for w_ref, cw_ref, b0_ref, b1_ref, n0_ref, n1_ref in groups:
        h = jnp.dot(hn, w_ref[...], preferred_element_type=F32)
        z_ref[...] = jnp.zeros(z_ref.shape, F32)
        for lb in range(FFN_TF // LANES):
            cols = slice(lb * LANES, (lb + 1) * LANES)
            z_ref[lb, pl.ds(0, nseq, stride=seq), :] = b0_ref[:, cols]
            z_ref[lb, pl.ds(1, nseq, stride=seq), :] = b1_ref[:, cols]
            hb_ref[lb] = h[:, cols]
            n0_ref[:, cols] = hb_ref[lb, pl.ds(seq - 2, nseq, stride=seq), :]
            n1_ref[:, cols] = hb_ref[lb, pl.ds(seq - 1, nseq, stride=seq), :]
        z = jnp.concatenate([z_ref[lb] for lb in range(FFN_TF // LANES)], axis=1)
        s1 = jnp.where(t_in_seq == 0, pltpu.roll(z, tt - 1, 0), pltpu.roll(h, 1, 0))
        s2 = jnp.where(t_in_seq < 2, z, pltpu.roll(h, 2, 0))
        cw = cw_ref[...]
        convs.append(h * cw[2:3, :] + s1 * cw[1:2, :] + s2 * cw[0:1, :])
    act = _silu(convs[0]) * convs[1]
    _ffn_tail(f, act, wd_ref, x1_ref, fg_ref, y_ref, acc_ref)


def _ffn_short(hn, x1, wup, cw, wdown, fg_row, buf0, buf1, tt, seq):
    m = hn.shape[0]
    tf = FFN_TF
    nf = FFN_NF
    nseq = tt // seq
    st_g = pl.BlockSpec((nseq, tf), lambda i, f: (i, f))
    st_u = pl.BlockSpec((nseq, tf), lambda i, f: (i, nf + f))
    new = pl.BlockSpec((nseq, tf), lambda i, f: (i, f))
    new_shape = jax.ShapeDtypeStruct((m // seq, D_FF), F32)
    return pl.pallas_call(
        functools.partial(_ffn_short_kernel, tt=tt, seq=seq),
        grid=(m // tt, nf),
        in_specs=[
            pl.BlockSpec((tt, D_MODEL), lambda i, f: (i, 0)),
            pl.BlockSpec((tt, D_MODEL), lambda i, f: (i, 0)),
            pl.BlockSpec((D_MODEL, tf), lambda i, f: (0, f)),
            pl.BlockSpec((D_MODEL, tf), lambda i, f: (0, nf + f)),
            pl.BlockSpec((FFN_CONV, tf), lambda i, f: (0, f)),
            pl.BlockSpec((FFN_CONV, tf), lambda i, f: (0, nf + f)),
            pl.BlockSpec((tf, D_MODEL), lambda i, f: (f, 0)),
            pl.BlockSpec((1, D_MODEL), lambda i, f: (0, 0)),
            st_g, st_g, st_u, st_u,
        ],
        out_specs=[pl.BlockSpec((tt, D_MODEL), lambda i, f: (i, 0)), new, new, new, new],
        out_shape=[jax.ShapeDtypeStruct((m, D_MODEL), F32), new_shape, new_shape, new_shape, new_shape],
        scratch_shapes=[
            pltpu.VMEM((tt, D_MODEL), F32),
            pltpu.VMEM((tf // LANES, tt, LANES), F32),
            pltpu.VMEM((tf // LANES, tt, LANES), F32),
        ],
        compiler_params=pltpu.CompilerParams(
            dimension_semantics=("arbitrary", "arbitrary"), vmem_limit_bytes=VMEM_LIMIT),
        name="ffn_short",
    )(hn, x1, wup, wup, cw, cw, wdown, fg_row, buf0, buf1, buf0, buf1)


def _pad_lanes(vec, offset):
    out = jnp.zeros((LANES,), F32)
    return out.at[offset:offset + vec.shape[0]].set(vec.astype(F32))


def _trunk(x, s_gdn, s_gconv, s_rwkv, s_shift, s_ffn, prm, *, long_seq):
    b, t, _ = x.shape
    m = b * t
    x2d = x.reshape(m, D_MODEL)
    tm = min(512, m)
    proj = _inproj(x2d, prm["ln1_g"], prm["w_in"], min(1024, m))

    o_a, gdn_new = _gdn_mixer(proj, s_gconv, s_gdn, prm["gdn_conv_w"], prm["alog_r"], prm["dtb_r"],
                              prm["alog_c"], prm["dtb_c"], prm["gdn_norm_g"], t)
    sh_rkv = s_shift[:, None, :3 * RWKV_WIDTH]
    sh_lora = s_shift[:, None, 3 * RWKV_WIDTH:]
    o_b, rwkv_new = _rwkv_mixer(proj, sh_rkv, sh_lora, s_rwkv, prm["mu_rkv"], prm["mu_lora"],
                                prm["rwkv_w0"], prm["rwkv_a0"], prm["rwkv_wab"], prm["rwkv_g_b"],
                                prm["rwkv_k_k"], prm["rwkv_k_a"], prm["rwkv_r_k"], prm["rwkv_gn_w"],
                                prm["rwkv_gn_b"], t)

    x1, hn = _outproj(x2d, o_a, o_b, prm["w_o"], prm["ln2_g"], tm)
    if long_seq:
        tt = min(512, t)
        y, n_g, n_u = _ffn_long(hn.reshape(b, t, D_MODEL), x1.reshape(b, t, D_MODEL), prm["ffn_w_up"],
                                prm["ffn_conv_w"], prm["ffn_w_down"], prm["final_g"], tt)
        ffn_new = jnp.concatenate([n_g[:, -1], n_u[:, -1]], axis=-1)
    else:
        tt = min(512, m)
        y, n0g, n1g, n0u, n1u = _ffn_short(hn, x1, prm["ffn_w_up"], prm["ffn_conv_w"],
                                            prm["ffn_w_down"], prm["final_g"],
                                            s_ffn[:, 0], s_ffn[:, 1], tt, t)
        y = y.reshape(b, t, D_MODEL)
        ffn_new = jnp.stack([jnp.concatenate([n0g, n0u], axis=-1),
                             jnp.concatenate([n1g, n1u], axis=-1)], axis=1)

    proj = proj.reshape(b, t, PROJ_WIDTH)
    gconv_new = proj[:, t - (GDN_CONV - 1):, COL_QKV:COL_QKV + 3 * GDN_WIDTH]
    shift_new = jnp.concatenate([proj[:, t - 1, COL_RKV:COL_RKV + 3 * RWKV_WIDTH],
                                 proj[:, t - 1, COL_LORA:COL_LORA + RWKV_LORA]], axis=-1)
    return y, gdn_new[None], gconv_new[None], rwkv_new[None], shift_new[None], ffn_new[None]


def kernel(x_prompt, x_sample, state_gdn, state_gdn_conv, state_rwkv, state_rwkv_shift, state_ffn_conv, ln1_g, w_in, gdn_conv_w, gdn_a_log, gdn_dt_bias, gdn_norm_g, rwkv_mu, rwkv_w0, rwkv_w_b, rwkv_a0, rwkv_a_b, rwkv_g_b, rwkv_k_k, rwkv_k_a, rwkv_r_k, rwkv_gn_w, rwkv_gn_b, w_o, ln2_g, ffn_w_up, ffn_conv_w, ffn_w_down, final_g):
    assert ln1_g.shape[0] == 1, "single-layer trunk"
    w = w_in[0]
    rw = REF_OFF_RWKV
    w_perm = jnp.concatenate([
        w[:, :REF_OFF_Z],
        w[:, rw:rw + 3 * RWKV_WIDTH],
        w[:, REF_OFF_Z:REF_OFF_B],
        w[:, rw + 3 * RWKV_WIDTH:],
        w[:, REF_OFF_B:REF_OFF_RWKV],
        jnp.zeros((D_MODEL, PROJ_WIDTH - COL_BA - 2 * GDN_HEADS), w.dtype),
    ], axis=1).astype(BF16)
    mu = rwkv_mu[0]
    zeros_w = jnp.zeros((RWKV_LORA_W, RWKV_WIDTH), F32)
    wab = jnp.concatenate([
        jnp.concatenate([rwkv_w_b[0], zeros_w], axis=1),
        jnp.concatenate([zeros_w, rwkv_a_b[0]], axis=1)], axis=0).astype(BF16)
    alog = _pad_lanes(gdn_a_log[0], GDN_HEADS)
    dtb = _pad_lanes(gdn_dt_bias[0], GDN_HEADS)
    prm = {
        "ln1_g": ln1_g[0][None], "w_in": w_perm, "gdn_conv_w": gdn_conv_w[0],
        "alog_r": alog[None], "dtb_r": dtb[None], "alog_c": alog[:, None], "dtb_c": dtb[:, None],
        "gdn_norm_g": gdn_norm_g[0][None],
        "mu_rkv": mu[None, :3 * RWKV_WIDTH], "mu_lora": mu[None, 3 * RWKV_WIDTH:],
        "rwkv_w0": rwkv_w0[0][None], "rwkv_a0": rwkv_a0[0][None], "rwkv_wab": wab,
        "rwkv_g_b": rwkv_g_b[0].astype(BF16), "rwkv_k_k": rwkv_k_k[0][None],
        "rwkv_k_a": rwkv_k_a[0][None], "rwkv_r_k": rwkv_r_k[0].reshape(1, RWKV_WIDTH),
        "rwkv_gn_w": rwkv_gn_w[0][None], "rwkv_gn_b": rwkv_gn_b[0][None],
        "w_o": w_o[0].astype(BF16), "ln2_g": ln2_g[0][None],
        "ffn_w_up": ffn_w_up[0].astype(BF16), "ffn_conv_w": ffn_conv_w[0],
        "ffn_w_down": ffn_w_down[0].astype(BF16), "final_g": final_g[None],
    }

    bp = x_prompt.shape[0]
    zero_states = (
        jnp.zeros((bp,) + state_gdn.shape[2:], F32),
        jnp.zeros((bp,) + state_gdn_conv.shape[2:], F32),
        jnp.zeros((bp,) + state_rwkv.shape[2:], F32),
        jnp.zeros((bp,) + state_rwkv_shift.shape[2:], F32),
        None,
    )
    outs_p = _trunk(x_prompt, *zero_states, prm, long_seq=True)
    outs_s = _trunk(x_sample, state_gdn[0], state_gdn_conv[0], state_rwkv[0], state_rwkv_shift[0],
                    state_ffn_conv[0], prm, long_seq=False)
    return (outs_p[0], outs_s[0]) + tuple(outs_p[1:]) + tuple(outs_s[1:])
```

```python
import functools

import jax
import jax.numpy as jnp
from jax import lax
from jax.experimental import pallas as pl
from jax.experimental.pallas import tpu as pltpu

F32 = jnp.float32
BF16 = jnp.bfloat16

D_MODEL = 2048
GDN_WIDTH = 1024
GDN_HEADS = 8
GDN_DK = 128
GDN_CONV = 4
RWKV_WIDTH = 1024
RWKV_HEAD = 64
RWKV_HEADS = 16
RWKV_LORA_W = 64
RWKV_LORA_A = 64
RWKV_LORA_G = 128
RWKV_LORA = RWKV_LORA_W + RWKV_LORA_A + RWKV_LORA_G
RWKV_PROJ = 3 * RWKV_WIDTH + RWKV_LORA
D_FF = 5632
FFN_CONV = 3
RMS_EPS = 1e-6
L2_EPS = 1e-12
GN_EPS = 64e-5

REF_OFF_Z = 3 * GDN_WIDTH
REF_OFF_B = 4 * GDN_WIDTH
REF_OFF_RWKV = REF_OFF_B + 2 * GDN_HEADS
REF_IN_WIDTH = REF_OFF_RWKV + RWKV_PROJ

LANES = 128
SUBLANES = 8
COL_QKV = 0
COL_RKV = 3 * GDN_WIDTH
COL_Z = COL_RKV + 3 * RWKV_WIDTH
COL_LORA = COL_Z + GDN_WIDTH
COL_BA = COL_LORA + RWKV_LORA
PROJ_WIDTH = 7680
PROJ_TN = 1280

MIX_ROWS = 64
MIX_GROUPS_LONG = 4

NN = (((1,), (0,)), ((), ()))
NT = (((1,), (1,)), ((), ()))
TN = (((0,), (0,)), ((), ()))

VMEM_LIMIT = 56 * 1024 * 1024

P_GRAM = "x1"
P_INV = "x1"
P_SOLVE = "x1"
P_STATE = "x1"
P_OUT = "x1"


def _split(x):
    hi = x.astype(BF16)
    return hi, (x - hi.astype(F32)).astype(BF16)


def _dot(a, b, dims=NN, mode="x1"):
    if mode == "hi":
        return lax.dot_general(a, b, dims, precision=lax.Precision.HIGHEST,
                               preferred_element_type=F32)
    if mode == "x3":
        a_hi, a_lo = _split(a)
        b_hi, b_lo = _split(b)
        d = lambda u, v: lax.dot_general(u, v, dims, preferred_element_type=F32)
        return d(a_hi, b_hi) + (d(a_hi, b_lo) + d(a_lo, b_hi))
    return lax.dot_general(a.astype(BF16), b.astype(BF16), dims, preferred_element_type=F32)


def _ones_dot(ones_mat, x, dims=NN):
    x1 = x.astype(BF16)
    r1 = x - x1.astype(F32)
    x2 = r1.astype(BF16)
    x3 = (r1 - x2.astype(F32)).astype(BF16)
    m = ones_mat.astype(BF16)
    if dims == NN:
        d = lambda v: lax.dot_general(m, v, dims, preferred_element_type=F32)
    else:
        d = lambda v: lax.dot_general(v, m, dims, preferred_element_type=F32)
    return d(x1) + (d(x2) + d(x3))


def _sigmoid(x):
    return 1.0 / (1.0 + jnp.exp(-x))


def _silu(x):
    return x * _sigmoid(x)


def _softplus(x):
    return jnp.maximum(x, 0.0) + jnp.log(1.0 + jnp.exp(-jnp.abs(x)))


def _seq_masks(rows, seq_len):
    r = lax.broadcasted_iota(jnp.int32, (rows, rows), 0)
    c = lax.broadcasted_iota(jnp.int32, (rows, rows), 1)
    if seq_len >= rows:
        return None, r >= c, r > c
    shift = seq_len.bit_length() - 1
    assert 1 << shift == seq_len
    same = jnp.right_shift(r, shift) == jnp.right_shift(c, shift)
    return same, same & (r >= c), same & (r > c)


def _wide_masks(rows, seq_len):
    r = lax.broadcasted_iota(jnp.int32, (rows, 2 * rows), 0)
    c = lax.broadcasted_iota(jnp.int32, (rows, 2 * rows), 1)
    right = c >= rows
    cc = jnp.where(right, c - rows, c)
    if seq_len >= rows:
        return r >= cc, right & (r > cc)
    shift = seq_len.bit_length() - 1
    same = jnp.right_shift(r, shift) == jnp.right_shift(cc, shift)
    return same & (r >= cc), same & right & (r > cc)


def _head_sums(x):
    tile = 2 * LANES
    li = jnp.right_shift(lax.broadcasted_iota(jnp.int32, (tile, tile), 0), 6)
    lj = jnp.right_shift(lax.broadcasted_iota(jnp.int32, (tile, tile), 1), 6)
    ones = jnp.where(li == lj, 1.0, 0.0).astype(BF16)
    hi, lo = _split(x)
    d = lambda u: lax.dot_general(u, ones, NN, preferred_element_type=F32)
    return jnp.concatenate([d(hi[:, t:t + tile]) + d(lo[:, t:t + tile])
                            for t in range(0, x.shape[1], tile)], axis=1)


def _unit_lower_inverses(mats, rows, nilpotency):
    r = lax.broadcasted_iota(jnp.int32, (rows, rows), 0)
    c = lax.broadcasted_iota(jnp.int32, (rows, rows), 1)
    eye = jnp.where(r == c, 1.0, 0.0).astype(F32)
    qs = [-a for a in mats]
    ts = [eye + q for q in qs]
    n = 2
    if n < nilpotency:
        qs = [_dot(q, q, mode=P_INV) for q in qs]
    while n < nilpotency:
        if 2 * n < nilpotency:
            tq = [_dot(jnp.concatenate([t, q], axis=0), q, mode=P_INV) for t, q in zip(ts, qs)]
            ts = [t + p[:rows] for t, p in zip(ts, tq)]
            qs = [p[rows:] for p in tq]
        else:
            ts = [t + _dot(t, q, mode=P_INV) for t, q in zip(ts, qs)]
        n *= 2
    return ts


def _permute_win_kernel(w_ref, o_ref):
    rw = REF_OFF_RWKV
    rows = w_ref.shape[0]
    cast = lambda lo, hi: w_ref[:, lo:hi].astype(BF16)
    o_ref[:, COL_QKV:COL_RKV] = cast(0, REF_OFF_Z)
    o_ref[:, COL_RKV:COL_Z] = cast(rw, rw + 3 * RWKV_WIDTH)
    o_ref[:, COL_Z:COL_LORA] = cast(REF_OFF_Z, REF_OFF_B)
    o_ref[:, COL_LORA:COL_BA] = cast(rw + 3 * RWKV_WIDTH, REF_IN_WIDTH)
    tail = jnp.concatenate([w_ref[:, REF_OFF_B:REF_OFF_RWKV],
                            jnp.zeros((rows, PROJ_WIDTH - COL_BA - 2 * GDN_HEADS), F32)], axis=1)
    o_ref[:, COL_BA:] = tail.astype(BF16)


def _permute_win(w, tr=256):
    return pl.pallas_call(
        _permute_win_kernel,
        grid=(D_MODEL // tr,),
        in_specs=[pl.BlockSpec((tr, REF_IN_WIDTH), lambda i: (i, 0))],
        out_specs=pl.BlockSpec((tr, PROJ_WIDTH), lambda i: (i, 0)),
        out_shape=jax.ShapeDtypeStruct((D_MODEL, PROJ_WIDTH), BF16),
        compiler_params=pltpu.CompilerParams(
            dimension_semantics=("arbitrary",), vmem_limit_bytes=VMEM_LIMIT),
        name="permute_win",
    )(w)


def _inproj_kernel(x_ref, g_ref, w_ref, o_ref, xn_ref):
    @pl.when(pl.program_id(1) == 0)
    def _():
        x = x_ref[...]
        ms = jnp.mean(x * x, axis=-1, keepdims=True)
        xn_ref[...] = (x * lax.rsqrt(ms + RMS_EPS) * g_ref[...]).astype(BF16)

    o_ref[...] = jnp.dot(xn_ref[...], w_ref[...], preferred_element_type=F32)


def _inproj(x2d, g_row, w_bf16, tm):
    m = x2d.shape[0]
    return pl.pallas_call(
        _inproj_kernel,
        grid=(m // tm, PROJ_WIDTH // PROJ_TN),
        in_specs=[
            pl.BlockSpec((tm, D_MODEL), lambda i, j: (i, 0)),
            pl.BlockSpec((1, D_MODEL), lambda i, j: (0, 0)),
            pl.BlockSpec((D_MODEL, PROJ_TN), lambda i, j: (0, j)),
        ],
        out_specs=pl.BlockSpec((tm, PROJ_TN), lambda i, j: (i, j)),
        out_shape=jax.ShapeDtypeStruct((m, PROJ_WIDTH), F32),
        scratch_shapes=[pltpu.VMEM((tm, D_MODEL), BF16)],
        compiler_params=pltpu.CompilerParams(
            dimension_semantics=("arbitrary", "arbitrary"), vmem_limit_bytes=VMEM_LIMIT),
        name="inproj",
    )(x2d, g_row, w_bf16)


def _mixer_geometry(seq_len, n_seq):
    rows = MIX_ROWS
    length = min(seq_len, rows)
    assert rows % length == 0 and seq_len % length == 0 and length % SUBLANES == 0
    per_group = rows // length
    groups = min(MIX_GROUPS_LONG, n_seq) if per_group == 1 else 1
    assert n_seq % (groups * per_group) == 0
    return rows, length, per_group, groups, seq_len // length


def _gdn_kernel(qkv_ref, z_ref, ba_ref, cbuf_ref, s0_ref, convw_ref, alog_r_ref, dtb_r_ref,
                alog_c_ref, dtb_c_ref, ng_ref, o_ref, s_ref, xp_ref, *, seq_len, n_seq):
    R, L, G, S, _ = _mixer_geometry(seq_len, n_seq)
    RT = S * R
    c = pl.program_id(1)
    width = 3 * GDN_WIDTH
    hist = GDN_CONV - 1
    cw = convw_ref[...]
    groups = range(S)
    seqs = range(G)

    @pl.when(c == 0)
    def _():
        s_ref[...] = s0_ref[...]

    pieces = []
    if G == 1:
        @pl.when(c == 0)
        def _():
            for s in groups:
                xp_ref[s, 0:SUBLANES, :] = jnp.zeros((SUBLANES, width), F32)
                xp_ref[s, SUBLANES - hist:SUBLANES, :] = cbuf_ref[s]

        @pl.when(c > 0)
        def _():
            for s in groups:
                xp_ref[s, 0:SUBLANES, :] = xp_ref[s, R:R + SUBLANES, :]

        for s in groups:
            xp_ref[s, SUBLANES:SUBLANES + R, :] = qkv_ref[s]
            piece = qkv_ref[s] * cw[hist:hist + 1, :]
            for i in range(hist):
                off = SUBLANES - hist + i
                piece = piece + xp_ref[s, off:off + R, :] * cw[i:i + 1, :]
            pieces.append(piece)
    else:
        for s in groups:
            for g in seqs:
                q = s * G + g
                rows = slice(g * L, (g + 1) * L)
                xp_ref[q, SUBLANES - hist:SUBLANES, :] = cbuf_ref[q]
                xp_ref[q, SUBLANES:SUBLANES + L, :] = qkv_ref[s, rows, :]
                piece = qkv_ref[s, rows, :] * cw[hist:hist + 1, :]
                for i in range(hist):
                    off = SUBLANES - hist + i
                    piece = piece + xp_ref[q, off:off + L, :] * cw[i:i + 1, :]
                pieces.append(piece)
    qkv = _silu(pieces[0] if len(pieces) == 1 else jnp.concatenate(pieces, axis=0))

    same_t, causal_t, _ = _seq_masks(RT, L)
    causal01 = jnp.where(causal_t, 1.0, 0.0).astype(F32)
    _, causal, strict = _seq_masks(R, L)

    ba = ba_ref[...].reshape(RT, LANES)
    ba_t = ba.T
    beta_c = _sigmoid(ba)
    g_c = -jnp.exp(alog_r_ref[...]) * _softplus(ba + dtb_r_ref[...])
    g_r = -jnp.exp(alog_c_ref[...]) * _softplus(ba_t + dtb_c_ref[...])
    gc_all = _ones_dot(causal01, g_c)
    gr_all = _ones_dot(causal01, g_r, NT)
    if same_t is None:
        gtot_all = jnp.broadcast_to(gc_all[RT - 1:RT, :], (RT, LANES))
    else:
        gtot_all = _ones_dot(jnp.where(same_t, 1.0, 0.0).astype(F32), g_c)

    ng = ng_ref[...]
    chains = [(s, h) for s in groups for h in range(GDN_HEADS)]
    seq_rows = [slice(g * L, (g + 1) * L) for g in seqs]
    qs, ks, vs, betas, gcols, gtots, decays = {}, {}, {}, {}, {}, {}, {}
    for s, h in chains:
        rs = slice(s * R, (s + 1) * R)
        lo = h * GDN_DK
        q = qkv[rs, lo:lo + GDN_DK]
        k = qkv[rs, GDN_WIDTH + lo:GDN_WIDTH + lo + GDN_DK]
        key = (s, h)
        qs[key] = q * lax.rsqrt(jnp.sum(q * q, axis=-1, keepdims=True) + L2_EPS) * (GDN_DK ** -0.5)
        ks[key] = k * lax.rsqrt(jnp.sum(k * k, axis=-1, keepdims=True) + L2_EPS)
        vs[key] = qkv[rs, 2 * GDN_WIDTH + lo:2 * GDN_WIDTH + lo + GDN_DK]
        betas[key] = beta_c[rs, h:h + 1]
        gcol = gc_all[rs, GDN_HEADS + h:GDN_HEADS + h + 1]
        grow = gr_all[GDN_HEADS + h:GDN_HEADS + h + 1, rs]
        gcols[key] = gcol
        gtots[key] = gtot_all[rs, GDN_HEADS + h:GDN_HEADS + h + 1]
        decays[key] = jnp.where(causal, jnp.exp(jnp.where(causal, gcol - grow, 0.0)), 0.0)
    kqs = {key: _dot(jnp.concatenate([ks[key], qs[key]], axis=0), ks[key], NT, mode=P_GRAM)
           for key in chains}
    t_list = _unit_lower_inverses(
        [jnp.where(strict, betas[key] * kqs[key][:R] * decays[key], 0.0) for key in chains], R, L)
    t_invs = dict(zip(chains, t_list))
    gammas = {key: jnp.exp(gcols[key]) for key in chains}
    sols = {key: _dot(t_invs[key],
                      jnp.concatenate([(betas[key] * gammas[key]) * ks[key], betas[key] * vs[key]], axis=1),
                      mode=P_SOLVE) for key in chains}
    states = {(s, h): [s_ref[s * G + g, h] for g in seqs] for s, h in chains}
    wss = {key: [_dot(jnp.concatenate([sols[key][rows, :GDN_DK], (qs[key] * gammas[key])[rows]], axis=0),
                      states[key][g], mode=P_STATE) for g, rows in enumerate(seq_rows)]
           for key in chains}
    us = {key: jnp.concatenate([sols[key][rows, GDN_DK:] - wss[key][g][:L]
                                for g, rows in enumerate(seq_rows)], axis=0) for key in chains}
    outs = {key: jnp.concatenate([wss[key][g][L:] for g in seqs], axis=0)
            + _dot(kqs[key][R:] * decays[key], us[key], mode=P_OUT) for key in chains}
    for s, h in chains:
        key = (s, h)
        kt = ks[key] * jnp.exp(gtots[key] - gcols[key])
        for g, rows in enumerate(seq_rows):
            gl = jnp.exp(gtots[key][g * L:g * L + 1, :])
            s_ref[s * G + g, h] = gl * states[key][g] + _dot(kt[rows], us[key][rows], TN, mode=P_STATE)
    for s, h in chains:
        lo = h * GDN_DK
        o = outs[(s, h)]
        o = o * lax.rsqrt(jnp.mean(o * o, axis=-1, keepdims=True) + RMS_EPS) * ng
        o = o * _silu(z_ref[s, :, lo:lo + GDN_DK])
        o_ref[s, :, lo:lo + GDN_DK] = o.astype(o_ref.dtype)


def _gdn_mixer(proj2d, cbuf, s0, convw, alog_r, dtb_r, alog_c, dtb_c, ng, seq_len):
    m = proj2d.shape[0]
    nseq = m // seq_len
    R, L, G, S, sps = _mixer_geometry(seq_len, nseq)
    width = 3 * GDN_WIDTH
    proj3d = proj2d.reshape(nseq // G, sps * R, PROJ_WIDTH)
    const2 = lambda i, c: (0, 0)
    rows_map = lambda col: (lambda i, c: (i, c, col))
    per_seq = lambda *dims: pl.BlockSpec((S * G,) + dims, lambda i, c: (i,) + (0,) * len(dims))
    xp_shape = (S, SUBLANES + R, width) if G == 1 else (S * G, SUBLANES + L, width)
    o, s_new = pl.pallas_call(
        functools.partial(_gdn_kernel, seq_len=seq_len, n_seq=nseq),
        grid=(nseq // (S * G), sps),
        in_specs=[
            pl.BlockSpec((S, R, width), rows_map(COL_QKV // width)),
            pl.BlockSpec((S, R, GDN_WIDTH), rows_map(COL_Z // GDN_WIDTH)),
            pl.BlockSpec((S, R, LANES), rows_map(COL_BA // LANES)),
            per_seq(GDN_CONV - 1, width),
            per_seq(GDN_HEADS, GDN_DK, GDN_DK),
            pl.BlockSpec((GDN_CONV, width), const2),
            pl.BlockSpec((1, LANES), const2),
            pl.BlockSpec((1, LANES), const2),
            pl.BlockSpec((LANES, 1), const2),
            pl.BlockSpec((LANES, 1), const2),
            pl.BlockSpec((1, GDN_DK), const2),
        ],
        out_specs=[
            pl.BlockSpec((S, R, GDN_WIDTH), rows_map(0)),
            per_seq(GDN_HEADS, GDN_DK, GDN_DK),
        ],
        out_shape=[
            jax.ShapeDtypeStruct((nseq // G, sps * R, GDN_WIDTH), BF16),
            jax.ShapeDtypeStruct((nseq, GDN_HEADS, GDN_DK, GDN_DK), F32),
        ],
        scratch_shapes=[pltpu.VMEM(xp_shape, F32)],
        compiler_params=pltpu.CompilerParams(
            dimension_semantics=("arbitrary", "arbitrary"), vmem_limit_bytes=VMEM_LIMIT),
        name="gdn_mixer",
    )(proj3d, proj3d, proj3d, cbuf, s0, convw, alog_r, dtb_r, alog_c, dtb_c, ng)
    return o.reshape(m, GDN_WIDTH), s_new


def _shifted_rows(x_ref, prev_ref, carry_ref, c, R, L, G, S):
    width = x_ref.shape[-1]
    groups = range(S)
    if G == 1:
        @pl.when(c == 0)
        def _():
            for s in groups:
                carry_ref[s, 0:SUBLANES, :] = jnp.zeros((SUBLANES, width), F32)
                carry_ref[s, SUBLANES - 1:SUBLANES, :] = prev_ref[s]

        @pl.when(c > 0)
        def _():
            for s in groups:
                carry_ref[s, 0:SUBLANES, :] = carry_ref[s, R:R + SUBLANES, :]

        xs, prevs = [], []
        for s in groups:
            carry_ref[s, SUBLANES:SUBLANES + R, :] = x_ref[s]
            xs.append(x_ref[s])
            prevs.append(carry_ref[s, SUBLANES - 1:SUBLANES - 1 + R, :])
    else:
        row = lax.broadcasted_iota(jnp.int32, (L, width), 0)
        xs, prevs = [], []
        for s in groups:
            x = x_ref[s]
            xs.append(x)
            for g in range(G):
                xg = x[g * L:(g + 1) * L]
                prevs.append(jnp.where(row == 0, prev_ref[s * G + g], pltpu.roll(xg, 1, 0)))
    cat = lambda parts: parts[0] if len(parts) == 1 else jnp.concatenate(parts, axis=0)
    return cat(xs), cat(prevs)


def _rwkv_kernel(rkv_ref, lora_ref, sh_rkv_ref, sh_lora_ref, s0_ref, mu_rkv_ref, mu_lora_ref,
                 w0_ref, a0_ref, wab_ref, gb_ref, kk_ref, ka_ref, rk_ref, gnw_ref, gnb_ref,
                 o_ref, s_ref, xr_ref, xl_ref, m_ref, *, seq_len, n_seq):
    R, L, G, S, sps = _mixer_geometry(seq_len, n_seq)
    RT = S * R
    NQ = S * G
    c = pl.program_id(1)
    W = RWKV_WIDTH
    HD = RWKV_HEAD
    pairs = range(RWKV_HEADS // 2)
    groups = range(S)
    seqs = range(G)

    @pl.when(c == 0)
    def _():
        zero = jnp.zeros((HD, HD), F32)
        for q in range(NQ):
            for j in pairs:
                vk = jnp.concatenate([jnp.concatenate([s0_ref[q, 2 * j], zero], axis=1),
                                      jnp.concatenate([zero, s0_ref[q, 2 * j + 1]], axis=1)], axis=0)
                m_ref[q, j] = vk.T

    p, p_prev = _shifted_rows(rkv_ref, sh_rkv_ref, xr_ref, c, R, L, G, S)
    xs = p + (p_prev - p) * mu_rkv_ref[...]
    pl_, pl_prev = _shifted_rows(lora_ref, sh_lora_ref, xl_ref, c, R, L, G, S)
    xl = pl_ + (pl_prev - pl_) * mu_lora_ref[...]
    r = xs[:, :W]
    k = xs[:, W:2 * W]
    v = xs[:, 2 * W:]

    wa_in = xl[:, :LANES]
    lane = lax.broadcasted_iota(jnp.int32, (RT, LANES), 1)
    wa_in = jnp.where(lane < RWKV_LORA_W, jnp.tanh(wa_in), wa_in)
    wa = _dot(wa_in, wab_ref[...])
    w = -_softplus(-(w0_ref[...] + wa[:, :W])) - 0.5
    lw = -jnp.exp(w)
    a = _sigmoid(a0_ref[...] + wa[:, W:])
    gate = _dot(_sigmoid(xl[:, LANES:]), gb_ref[...])
    kk_raw = k * kk_ref[...]
    k2 = k * (1.0 + (a - 1.0) * ka_ref[...])

    same_t, causal_t, _ = _seq_masks(RT, L)
    lc = _ones_dot(jnp.where(causal_t, 1.0, 0.0).astype(F32), lw)
    if G == 1:
        ltot = jnp.concatenate([jnp.broadcast_to(lc[(s + 1) * R - 1:(s + 1) * R, :], (R, W))
                                for s in groups], axis=0) if S > 1 else \
            jnp.broadcast_to(lc[R - 1:R, :], (R, W))
    else:
        ltot = _ones_dot(jnp.where(same_t, 1.0, 0.0).astype(F32), lw)
    e_inv = jnp.exp(-lc)
    e_rem = jnp.exp(ltot - lc)

    kk = kk_raw * lax.rsqrt(_head_sums(kk_raw * kk_raw) + L2_EPS)
    kka = kk * a
    ct = kk * jnp.exp(lc - lw)
    rt = r * jnp.exp(lc)
    bh = kka * e_inv
    kh = k2 * e_inv
    bb = kka * e_rem
    kb = k2 * e_rem
    p_rows = [ltot[q * L:q * L + 1] for q in range(NQ)]
    p_rows = p_rows + [p_rows[0]] * (-NQ % SUBLANES)
    pt = jnp.exp(jnp.concatenate(p_rows, axis=0)).T

    lane_r = lax.broadcasted_iota(jnp.int32, (R, LANES), 1)
    row_r = lax.broadcasted_iota(jnp.int32, (R, LANES), 0)
    col_r = jnp.bitwise_and(lane_r, HD - 1)
    if L >= R:
        same2 = None
        causal2, strict2 = row_r >= col_r, row_r > col_r
    else:
        shift = L.bit_length() - 1
        same2 = jnp.right_shift(row_r, shift) == jnp.right_shift(col_r, shift)
        causal2, strict2 = same2 & (row_r >= col_r), same2 & (row_r > col_r)
    eye2 = jnp.where(row_r == col_r, 1.0, 0.0).astype(F32)
    even_half = lane_r < HD
    block_diag = ((lax.broadcasted_iota(jnp.int32, (LANES, LANES), 0) < HD)
                  == (lax.broadcasted_iota(jnp.int32, (LANES, LANES), 1) < HD))
    even_all = jnp.bitwise_and(lax.broadcasted_iota(jnp.int32, (R, W), 1), LANES - 1) < HD
    tile = lambda j: slice(j * LANES, (j + 1) * LANES)

    def bd(x):
        return jnp.concatenate([jnp.where(even_half, x, 0.0), jnp.where(even_half, 0.0, x)], axis=0)

    if G > 1:
        row2 = lax.broadcasted_iota(jnp.int32, (2 * R, LANES), 0)
        seq_of_row = jnp.right_shift(jnp.bitwise_and(row2, R - 1), L.bit_length() - 1)

    units = [(s, j) for s in groups for j in pairs]
    xps, vps, bdb, bdk, bkts = {}, {}, {}, {}, {}
    for s in groups:
        rs = slice(s * R, (s + 1) * R)
        x_s = jnp.concatenate([ct[rs], rt[rs]], axis=0)
        bt_s = jnp.concatenate([jnp.where(even_all, bh[rs], 0.0), jnp.where(even_all, 0.0, bh[rs])],
                               axis=0).T
        kt_s = jnp.concatenate([jnp.where(even_all, kh[rs], 0.0), jnp.where(even_all, 0.0, kh[rs])],
                               axis=0).T
        bkt_s = jnp.concatenate([bb[rs], kb[rs]], axis=0).T
        for j in pairs:
            xps[(s, j)] = x_s[:, tile(j)]
            vps[(s, j)] = v[rs, tile(j)]
            bdb[(s, j)] = bt_s[tile(j), :]
            bdk[(s, j)] = kt_s[tile(j), :]
            bkts[(s, j)] = bkt_s[tile(j), :]
    gb = {u: _dot(xps[u], bdb[u], mode=P_GRAM) for u in units}
    gk = {u: _dot(xps[u], bdk[u], mode=P_GRAM) for u in units}
    ms = {(s, j): [m_ref[s * G + g, j] for g in seqs] for s, j in units}
    xm_c, xm_r = {}, {}
    for u in units:
        if G == 1:
            xm = _dot(xps[u], ms[u][0], mode=P_STATE)
            xm_c[u], xm_r[u] = xm[:R], xm[R:]
        else:
            parts = [_dot(jnp.concatenate([xps[u][g * L:(g + 1) * L],
                                           xps[u][R + g * L:R + (g + 1) * L]], axis=0),
                          ms[u][g], mode=P_STATE) for g in seqs]
            xm_c[u] = jnp.concatenate([p_[:L] for p_ in parts], axis=0)
            xm_r[u] = jnp.concatenate([p_[L:] for p_ in parts], axis=0)
    qs = {u: jnp.where(strict2, -gb[u][:R], 0.0) for u in units}
    ts = {u: eye2 + qs[u] for u in units}
    n = 2
    if n < L:
        qs = {u: _dot(qs[u], bd(qs[u]), mode=P_INV) for u in units}
    while n < L:
        if 2 * n < L:
            tq = {u: _dot(jnp.concatenate([ts[u], qs[u]], axis=0), bd(qs[u]), mode=P_INV) for u in units}
            ts = {u: ts[u] + tq[u][:R] for u in units}
            qs = {u: tq[u][R:] for u in units}
        else:
            ts = {u: ts[u] + _dot(ts[u], bd(qs[u]), mode=P_INV) for u in units}
        n *= 2
    bdv = {u: bd(vps[u]) for u in units}
    akvs = {u: _dot(jnp.where(strict2, gk[u][:R], 0.0), bdv[u], mode=P_OUT) for u in units}
    us = {u: _dot(ts[u], bd(-(xm_c[u] + akvs[u])), mode=P_SOLVE) for u in units}
    uvs = {u: jnp.concatenate([us[u], vps[u]], axis=0) for u in units}
    ys = {u: xm_r[u] + _dot(
        jnp.concatenate([jnp.where(causal2, gb[u][R:], 0.0), jnp.where(causal2, gk[u][R:], 0.0)], axis=1),
        jnp.concatenate([bd(us[u]), bdv[u]], axis=0), mode=P_OUT) for u in units}
    for s, j in units:
        u = (s, j)
        for g in seqs:
            q = s * G + g
            uv_g = uvs[u] if G == 1 else jnp.where(seq_of_row == g, uvs[u], 0.0)
            upd = _dot(bkts[u], uv_g, mode=P_STATE)
            m_ref[q, j] = pt[tile(j), q:q + 1] * ms[u][g] + jnp.where(block_diag, upd, 0.0)

    y_rows = [jnp.concatenate([ys[(s, j)] for j in pairs], axis=1) for s in groups]
    y_all = y_rows[0] if S == 1 else jnp.concatenate(y_rows, axis=0)
    mean = _head_sums(y_all) * (1.0 / HD)
    yc = y_all - mean
    var = _head_sums(yc * yc) * (1.0 / HD)
    yn = yc * lax.rsqrt(var + GN_EPS) * gnw_ref[...] + gnb_ref[...]
    yn = yn + _head_sums(r * k2 * rk_ref[...]) * v
    o_ref[...] = (yn * gate).reshape(S, R, W).astype(o_ref.dtype)

    @pl.when(c == sps - 1)
    def _():
        for q in range(NQ):
            for j in pairs:
                vk = m_ref[q, j].T
                s_ref[q, 2 * j] = vk[:HD, :HD]
                s_ref[q, 2 * j + 1] = vk[HD:, HD:]


def _rwkv_mixer(proj2d, sh_rkv, sh_lora, s0, mu_rkv, mu_lora, w0, a0, wab, gb, kk, ka, rk, gnw, gnb,
                seq_len):
    m = proj2d.shape[0]
    nseq = m // seq_len
    R, L, G, S, sps = _mixer_geometry(seq_len, nseq)
    W = RWKV_WIDTH
    proj3d = proj2d.reshape(nseq // G, sps * R, PROJ_WIDTH)
    const2 = lambda i, c: (0, 0)
    row = lambda width: pl.BlockSpec((1, width), const2)
    rows_map = lambda col: (lambda i, c: (i, c, col))
    per_seq = lambda *dims: pl.BlockSpec((S * G,) + dims, lambda i, c: (i,) + (0,) * len(dims))
    carry = lambda width: pltpu.VMEM((S, SUBLANES + R, width) if G == 1 else (SUBLANES, LANES), F32)
    o, s_new = pl.pallas_call(
        functools.partial(_rwkv_kernel, seq_len=seq_len, n_seq=nseq),
        grid=(nseq // (S * G), sps),
        in_specs=[
            pl.BlockSpec((S, R, 3 * W), rows_map(COL_RKV // (3 * W))),
            pl.BlockSpec((S, R, RWKV_LORA), rows_map(COL_LORA // RWKV_LORA)),
            per_seq(1, 3 * W),
            per_seq(1, RWKV_LORA),
            per_seq(RWKV_HEADS, RWKV_HEAD, RWKV_HEAD),
            row(3 * W), row(RWKV_LORA), row(W), row(W),
            pl.BlockSpec((LANES, 2 * W), const2),
            pl.BlockSpec((RWKV_LORA_G, W), const2),
            row(W), row(W), row(W), row(W), row(W),
        ],
        out_specs=[pl.BlockSpec((S, R, W), rows_map(0)), per_seq(RWKV_HEADS, RWKV_HEAD, RWKV_HEAD)],
        out_shape=[
            jax.ShapeDtypeStruct((nseq // G, sps * R, W), BF16),
            jax.ShapeDtypeStruct((nseq, RWKV_HEADS, RWKV_HEAD, RWKV_HEAD), F32),
        ],
        scratch_shapes=[carry(3 * W), carry(RWKV_LORA),
                        pltpu.VMEM((S * G, RWKV_HEADS // 2, LANES, LANES), F32)],
        compiler_params=pltpu.CompilerParams(
            dimension_semantics=("arbitrary", "arbitrary"), vmem_limit_bytes=VMEM_LIMIT),
        name="rwkv_mixer",
    )(proj3d, proj3d, sh_rkv, sh_lora, s0, mu_rkv, mu_lora, w0, a0, wab, gb, kk, ka, rk, gnw, gnb)
    return o.reshape(m, W), s_new


def _outproj_kernel(x_ref, oa_ref, ob_ref, wa_ref, wb_ref, g_ref, x1_ref, hn_ref):
    x1 = (x_ref[...] + jnp.dot(oa_ref[...], wa_ref[...], preferred_element_type=F32)
          + jnp.dot(ob_ref[...], wb_ref[...], preferred_element_type=F32))
    x1_ref[...] = x1
    ms = jnp.mean(x1 * x1, axis=-1, keepdims=True)
    hn_ref[...] = (x1 * lax.rsqrt(ms + RMS_EPS) * g_ref[...]).astype(BF16)


def _outproj(x2d, oa, ob, wo_bf16, g_row, tm):
    m = x2d.shape[0]
    return pl.pallas_call(
        _outproj_kernel,
        grid=(m // tm,),
        in_specs=[
            pl.BlockSpec((tm, D_MODEL), lambda i: (i, 0)),
            pl.BlockSpec((tm, GDN_WIDTH), lambda i: (i, 0)),
            pl.BlockSpec((tm, RWKV_WIDTH), lambda i: (i, 0)),
            pl.BlockSpec((GDN_WIDTH, D_MODEL), lambda i: (0, 0)),
            pl.BlockSpec((RWKV_WIDTH, D_MODEL), lambda i: (1, 0)),
            pl.BlockSpec((1, D_MODEL), lambda i: (0, 0)),
        ],
        out_specs=[pl.BlockSpec((tm, D_MODEL), lambda i: (i, 0)),
                   pl.BlockSpec((tm, D_MODEL), lambda i: (i, 0))],
        out_shape=[jax.ShapeDtypeStruct((m, D_MODEL), F32),
                   jax.ShapeDtypeStruct((m, D_MODEL), BF16)],
        compiler_params=pltpu.CompilerParams(
            dimension_semantics=("arbitrary",), vmem_limit_bytes=VMEM_LIMIT),
        name="outproj",
    )(x2d, oa, ob, wo_bf16, wo_bf16, g_row)


FFN_TF = 512
FFN_NF = D_FF // FFN_TF


def _ffn_pipeline(f, up_fn, wd_ref, x1_ref, fg_ref, y_ref, acc_ref):
    @pl.when(f == 0)
    def _():
        acc_ref[...] = jnp.zeros(acc_ref.shape, F32)

    acc_ref[...] += jnp.dot(up_fn().astype(BF16), wd_ref[...], preferred_element_type=F32)

    @pl.when(f == FFN_NF - 1)
    def _():
        xo = x1_ref[...] + acc_ref[...]
        ms = jnp.mean(xo * xo, axis=-1, keepdims=True)
        y_ref[...] = xo * lax.rsqrt(ms + RMS_EPS) * fg_ref[...]


def _ffn_long_kernel(hn_ref, x1_ref, wg_ref, wu_ref, cwg_ref, cwu_ref, wd_ref, fg_ref,
                     y_ref, ng_ref, nu_ref, acc_ref, carry_ref, hbuf_ref, *, tt):
    ti = pl.program_id(1)
    f = pl.program_id(2)
    fc = f

    def up_fn():
        hn = hn_ref[...]
        convs = []
        for j, (w_ref, cw_ref, n_ref) in enumerate(((wg_ref, cwg_ref, ng_ref), (wu_ref, cwu_ref, nu_ref))):
            h = jnp.dot(hn, w_ref[...], preferred_element_type=F32)
            hbuf_ref[j, SUBLANES:SUBLANES + tt, :] = h
            prev = carry_ref[fc, j]
            hbuf_ref[j, 0:SUBLANES, :] = jnp.where(ti == 0, jnp.zeros_like(prev), prev)
            cw = cw_ref[...]
            conv = h * cw[FFN_CONV - 1:FFN_CONV, :]
            for i in range(FFN_CONV - 1):
                off = SUBLANES - (FFN_CONV - 1) + i
                conv = conv + hbuf_ref[j, off:off + tt, :] * cw[i:i + 1, :]
            carry_ref[fc, j] = h[tt - SUBLANES:, :]
            n_ref[...] = h[tt - (FFN_CONV - 1):, :]
            convs.append(conv)
        return _silu(convs[0]) * convs[1]

    _ffn_pipeline(f, up_fn, wd_ref, x1_ref, fg_ref, y_ref, acc_ref)


def _ffn_long(hn, x1, wup, cw, wdown, fg_row, tt):
    b, t, _ = hn.shape
    tf = FFN_TF
    nf = FFN_NF
    up = lambda f: f
    down = lambda f: f
    return pl.pallas_call(
        functools.partial(_ffn_long_kernel, tt=tt),
        grid=(b, t // tt, nf),
        in_specs=[
            pl.BlockSpec((None, tt, D_MODEL), lambda i, s, f: (i, s, 0)),
            pl.BlockSpec((None, tt, D_MODEL), lambda i, s, f: (i, s, 0)),
            pl.BlockSpec((D_MODEL, tf), lambda i, s, f: (0, up(f))),
            pl.BlockSpec((D_MODEL, tf), lambda i, s, f: (0, nf + up(f))),
            pl.BlockSpec((FFN_CONV, tf), lambda i, s, f: (0, up(f))),
            pl.BlockSpec((FFN_CONV, tf), lambda i, s, f: (0, nf + up(f))),
            pl.BlockSpec((tf, D_MODEL), lambda i, s, f: (down(f), 0)),
            pl.BlockSpec((1, D_MODEL), lambda i, s, f: (0, 0)),
        ],
        out_specs=[
            pl.BlockSpec((None, tt, D_MODEL), lambda i, s, f: (i, s, 0)),
            pl.BlockSpec((None, None, FFN_CONV - 1, tf), lambda i, s, f: (i, s, 0, up(f))),
            pl.BlockSpec((None, None, FFN_CONV - 1, tf), lambda i, s, f: (i, s, 0, up(f))),
        ],
        out_shape=[
            jax.ShapeDtypeStruct((b, t, D_MODEL), F32),
            jax.ShapeDtypeStruct((b, t // tt, FFN_CONV - 1, D_FF), F32),
            jax.ShapeDtypeStruct((b, t // tt, FFN_CONV - 1, D_FF), F32),
        ],
        scratch_shapes=[
            pltpu.VMEM((tt, D_MODEL), F32),
            pltpu.VMEM((nf, 2, SUBLANES, tf), F32),
            pltpu.VMEM((2, SUBLANES + tt, tf), F32),
        ],
        compiler_params=pltpu.CompilerParams(
            dimension_semantics=("arbitrary", "arbitrary", "arbitrary"),
            vmem_limit_bytes=VMEM_LIMIT),
        name="ffn_long",
    )(hn, x1, wup, wup, cw, cw, wdown, fg_row)


def _ffn_short_kernel(hn_ref, x1_ref, wg_ref, wu_ref, cwg_ref, cwu_ref, wd_ref, fg_ref,
                      b0g_ref, b1g_ref, b0u_ref, b1u_ref,
                      y_ref, n0g_ref, n1g_ref, n0u_ref, n1u_ref, acc_ref, z_ref, hb_ref, *, tt, seq):
    f = pl.program_id(1)
    nseq = tt // seq

    def up_fn():
        hn = hn_ref[...]
        t_in_seq = lax.broadcasted_iota(jnp.int32, (tt, FFN_TF), 0) % seq
        convs = []
        groups = ((wg_ref, cwg_ref, b0g_ref, b1g_ref, n0g_ref, n1g_ref),
                  (wu_ref, cwu_ref, b0u_ref, b1u_ref, n0u_ref, n1u_ref))
        for w_ref, cw_ref, b0_ref, b1_ref, n0_ref, n1_ref in groups:
            h = jnp.dot(hn, w_ref[...], preferred_element_type=F32)
            z_ref[...] = jnp.zeros(z_ref.shape, F32)
            for lb in range(FFN_TF // LANES):
                cols = slice(lb * LANES, (lb + 1) * LANES)
                z_ref[lb, pl.ds(0, nseq, stride=seq), :] = b0_ref[:, cols]
                z_ref[lb, pl.ds(1, nseq, stride=seq), :] = b1_ref[:, cols]
                hb_ref[lb] = h[:, cols]
                n0_ref[:, cols] = hb_ref[lb, pl.ds(seq - 2, nseq, stride=seq), :]
                n1_ref[:, cols] = hb_ref[lb, pl.ds(seq - 1, nseq, stride=seq), :]
            z = jnp.concatenate([z_ref[lb] for lb in range(FFN_TF // LANES)], axis=1)
            s1 = jnp.where(t_in_seq == 0, pltpu.roll(z, tt - 1, 0), pltpu.roll(h, 1, 0))
            s2 = jnp.where(t_in_seq < 2, z, pltpu.roll(h, 2, 0))
            cw = cw_ref[...]
            convs.append(h * cw[2:3, :] + s1 * cw[1:2, :] + s2 * cw[0:1, :])
        return _silu(convs[0]) * convs[1]

    _ffn_pipeline(f, up_fn, wd_ref, x1_ref, fg_ref, y_ref, acc_ref)


def _ffn_short(hn, x1, wup, cw, wdown, fg_row, buf0, buf1, tt, seq):
    m = hn.shape[0]
    tf = FFN_TF
    nf = FFN_NF
    nseq = tt // seq
    up = lambda f: f
    down = lambda f: f
    st_g = pl.BlockSpec((nseq, tf), lambda i, f: (i, up(f)))
    st_u = pl.BlockSpec((nseq, tf), lambda i, f: (i, nf + up(f)))
    new = pl.BlockSpec((nseq, tf), lambda i, f: (i, up(f)))
    new_shape = jax.ShapeDtypeStruct((m // seq, D_FF), F32)
    return pl.pallas_call(
        functools.partial(_ffn_short_kernel, tt=tt, seq=seq),
        grid=(m // tt, nf),
        in_specs=[
            pl.BlockSpec((tt, D_MODEL), lambda i, f: (i, 0)),
            pl.BlockSpec((tt, D_MODEL), lambda i, f: (i, 0)),
            pl.BlockSpec((D_MODEL, tf), lambda i, f: (0, up(f))),
            pl.BlockSpec((D_MODEL, tf), lambda i, f: (0, nf + up(f))),
            pl.BlockSpec((FFN_CONV, tf), lambda i, f: (0, up(f))),
            pl.BlockSpec((FFN_CONV, tf), lambda i, f: (0, nf + up(f))),
            pl.BlockSpec((tf, D_MODEL), lambda i, f: (down(f), 0)),
            pl.BlockSpec((1, D_MODEL), lambda i, f: (0, 0)),
            st_g, st_g, st_u, st_u,
        ],
        out_specs=[pl.BlockSpec((tt, D_MODEL), lambda i, f: (i, 0)), new, new, new, new],
        out_shape=[jax.ShapeDtypeStruct((m, D_MODEL), F32), new_shape, new_shape, new_shape, new_shape],
        scratch_shapes=[
            pltpu.VMEM((tt, D_MODEL), F32),
            pltpu.VMEM((tf // LANES, tt, LANES), F32),
            pltpu.VMEM((tf // LANES, tt, LANES), F32),
        ],
        compiler_params=pltpu.CompilerParams(
            dimension_semantics=("arbitrary", "arbitrary"), vmem_limit_bytes=VMEM_LIMIT),
        name="ffn_short",
    )(hn, x1, wup, wup, cw, cw, wdown, fg_row, buf0, buf1, buf0, buf1)


def _pad_lanes(vec, offset):
    out = jnp.zeros((LANES,), F32)
    return out.at[offset:offset + vec.shape[0]].set(vec.astype(F32))


def _trunk(x, s_gdn, s_gconv, s_rwkv, s_shift, s_ffn, prm, *, long_seq):
    b, t, _ = x.shape
    m = b * t
    x2d = x.reshape(m, D_MODEL)
    tm = min(512, m)
    proj = _inproj(x2d, prm["ln1_g"], prm["w_in"], min(1024, m))

    o_a, gdn_new = _gdn_mixer(proj, s_gconv, s_gdn, prm["gdn_conv_w"], prm["alog_r"], prm["dtb_r"],
                              prm["alog_c"], prm["dtb_c"], prm["gdn_norm_g"], t)
    sh_rkv = s_shift[:, None, :3 * RWKV_WIDTH]
    sh_lora = s_shift[:, None, 3 * RWKV_WIDTH:]
    o_b, rwkv_new = _rwkv_mixer(proj, sh_rkv, sh_lora, s_rwkv, prm["mu_rkv"], prm["mu_lora"],
                                prm["rwkv_w0"], prm["rwkv_a0"], prm["rwkv_wab"], prm["rwkv_g_b"],
                                prm["rwkv_k_k"], prm["rwkv_k_a"], prm["rwkv_r_k"], prm["rwkv_gn_w"],
                                prm["rwkv_gn_b"], t)

    x1, hn = _outproj(x2d, o_a, o_b, prm["w_o"], prm["ln2_g"], tm)
    if long_seq:
        tt = min(512, t)
        y, n_g, n_u = _ffn_long(hn.reshape(b, t, D_MODEL), x1.reshape(b, t, D_MODEL), prm["ffn_w_up"],
                                prm["ffn_conv_w"], prm["ffn_w_down"], prm["final_g"], tt)
        ffn_new = jnp.concatenate([n_g[:, -1], n_u[:, -1]], axis=-1)
    else:
        tt = min(512, m)
        y, n0g, n1g, n0u, n1u = _ffn_short(hn, x1, prm["ffn_w_up"], prm["ffn_conv_w"],
                                            prm["ffn_w_down"], prm["final_g"],
                                            s_ffn[:, 0], s_ffn[:, 1], tt, t)
        y = y.reshape(b, t, D_MODEL)
        ffn_new = jnp.stack([jnp.concatenate([n0g, n0u], axis=-1),
                             jnp.concatenate([n1g, n1u], axis=-1)], axis=1)

    proj = proj.reshape(b, t, PROJ_WIDTH)
    gconv_new = proj[:, t - (GDN_CONV - 1):, COL_QKV:COL_QKV + 3 * GDN_WIDTH]
    shift_new = jnp.concatenate([proj[:, t - 1, COL_RKV:COL_RKV + 3 * RWKV_WIDTH],
                                 proj[:, t - 1, COL_LORA:COL_LORA + RWKV_LORA]], axis=-1)
    return y, gdn_new[None], gconv_new[None], rwkv_new[None], shift_new[None], ffn_new[None]


def kernel(x_prompt, x_sample, state_gdn, state_gdn_conv, state_rwkv, state_rwkv_shift, state_ffn_conv, ln1_g, w_in, gdn_conv_w, gdn_a_log, gdn_dt_bias, gdn_norm_g, rwkv_mu, rwkv_w0, rwkv_w_b, rwkv_a0, rwkv_a_b, rwkv_g_b, rwkv_k_k, rwkv_k_a, rwkv_r_k, rwkv_gn_w, rwkv_gn_b, w_o, ln2_g, ffn_w_up, ffn_conv_w, ffn_w_down, final_g):
    assert ln1_g.shape[0] == 1, "single-layer trunk"
    w_perm = _permute_win(w_in[0])
    mu = rwkv_mu[0]
    zeros_w = jnp.zeros((RWKV_LORA_W, RWKV_WIDTH), F32)
    wab = jnp.concatenate([
        jnp.concatenate([rwkv_w_b[0], zeros_w], axis=1),
        jnp.concatenate([zeros_w, rwkv_a_b[0]], axis=1)], axis=0).astype(BF16)
    alog = _pad_lanes(gdn_a_log[0], GDN_HEADS)
    dtb = _pad_lanes(gdn_dt_bias[0], GDN_HEADS)
    prm = {
        "ln1_g": ln1_g[0][None], "w_in": w_perm, "gdn_conv_w": gdn_conv_w[0],
        "alog_r": alog[None], "dtb_r": dtb[None], "alog_c": alog[:, None], "dtb_c": dtb[:, None],
        "gdn_norm_g": gdn_norm_g[0][None],
        "mu_rkv": mu[None, :3 * RWKV_WIDTH], "mu_lora": mu[None, 3 * RWKV_WIDTH:],
        "rwkv_w0": rwkv_w0[0][None], "rwkv_a0": rwkv_a0[0][None], "rwkv_wab": wab,
        "rwkv_g_b": rwkv_g_b[0].astype(BF16), "rwkv_k_k": rwkv_k_k[0][None],
        "rwkv_k_a": rwkv_k_a[0][None], "rwkv_r_k": rwkv_r_k[0].reshape(1, RWKV_WIDTH),
        "rwkv_gn_w": rwkv_gn_w[0][None], "rwkv_gn_b": rwkv_gn_b[0][None],
        "w_o": w_o[0].astype(BF16), "ln2_g": ln2_g[0][None],
        "ffn_w_up": ffn_w_up[0].astype(BF16), "ffn_conv_w": ffn_conv_w[0],
        "ffn_w_down": ffn_w_down[0].astype(BF16), "final_g": final_g[None],
    }

    bp = x_prompt.shape[0]
    zero_states = (
        jnp.zeros((bp,) + state_gdn.shape[2:], F32),
        jnp.zeros((bp,) + state_gdn_conv.shape[2:], F32),
        jnp.zeros((bp,) + state_rwkv.shape[2:], F32),
        jnp.zeros((bp,) + state_rwkv_shift.shape[2:], F32),
        None,
    )
    outs_p = _trunk(x_prompt, *zero_states, prm, long_seq=True)
    outs_s = _trunk(x_sample, state_gdn[0], state_gdn_conv[0], state_rwkv[0], state_rwkv_shift[0],
                    state_ffn_conv[0], prm, long_seq=False)
    return (outs_p[0], outs_s[0]) + tuple(outs_p[1:]) + tuple(outs_s[1:])
```

```python
import functools

import jax
import jax.numpy as jnp
from jax import lax
from jax.experimental import pallas as pl
from jax.experimental.pallas import tpu as pltpu

F32 = jnp.float32
BF16 = jnp.bfloat16

D_MODEL = 2048
GDN_WIDTH = 1024
GDN_HEADS = 8
GDN_DK = 128
GDN_CONV = 4
RWKV_WIDTH = 1024
RWKV_HEAD = 64
RWKV_HEADS = 16
RWKV_LORA_W = 64
RWKV_LORA_A = 64
RWKV_LORA_G = 128
RWKV_LORA = RWKV_LORA_W + RWKV_LORA_A + RWKV_LORA_G
RWKV_PROJ = 3 * RWKV_WIDTH + RWKV_LORA
D_FF = 5632
FFN_CONV = 3
RMS_EPS = 1e-6
L2_EPS = 1e-12
GN_EPS = 64e-5

REF_OFF_Z = 3 * GDN_WIDTH
REF_OFF_B = 4 * GDN_WIDTH
REF_OFF_RWKV = REF_OFF_B + 2 * GDN_HEADS
REF_IN_WIDTH = REF_OFF_RWKV + RWKV_PROJ

LANES = 128
SUBLANES = 8
COL_QKV = 0
COL_RKV = 3 * GDN_WIDTH
COL_Z = COL_RKV + 3 * RWKV_WIDTH
COL_LORA = COL_Z + GDN_WIDTH
COL_BA = COL_LORA + RWKV_LORA
PROJ_WIDTH = 7680
PROJ_TN = 1280

MIX_ROWS = 64
MIX_GROUPS_LONG = 4

NN = (((1,), (0,)), ((), ()))
NT = (((1,), (1,)), ((), ()))
TN = (((0,), (0,)), ((), ()))

VMEM_LIMIT = 56 * 1024 * 1024

P_GRAM = "x1"
P_INV = "x1"
P_SOLVE = "x1"
P_STATE = "x1"
P_OUT = "x1"


def _split(x):
    hi = x.astype(BF16)
    return hi, (x - hi.astype(F32)).astype(BF16)


def _dot(a, b, dims=NN, mode="x1"):
    if mode == "hi":
        return lax.dot_general(a, b, dims, precision=lax.Precision.HIGHEST,
                               preferred_element_type=F32)
    if mode == "x3":
        a_hi, a_lo = _split(a)
        b_hi, b_lo = _split(b)
        d = lambda u, v: lax.dot_general(u, v, dims, preferred_element_type=F32)
        return d(a_hi, b_hi) + (d(a_hi, b_lo) + d(a_lo, b_hi))
    return lax.dot_general(a.astype(BF16), b.astype(BF16), dims, preferred_element_type=F32)


def _ones_dot(ones_mat, x, dims=NN):
    x1 = x.astype(BF16)
    r1 = x - x1.astype(F32)
    x2 = r1.astype(BF16)
    x3 = (r1 - x2.astype(F32)).astype(BF16)
    m = ones_mat.astype(BF16)
    if dims == NN:
        d = lambda v: lax.dot_general(m, v, dims, preferred_element_type=F32)
    else:
        d = lambda v: lax.dot_general(v, m, dims, preferred_element_type=F32)
    return d(x1) + (d(x2) + d(x3))


def _sigmoid(x):
    return 1.0 / (1.0 + jnp.exp(-x))


def _silu(x):
    return x * _sigmoid(x)


def _softplus(x):
    return jnp.maximum(x, 0.0) + jnp.log(1.0 + jnp.exp(-jnp.abs(x)))


def _seq_masks(rows, seq_len):
    r = lax.broadcasted_iota(jnp.int32, (rows, rows), 0)
    c = lax.broadcasted_iota(jnp.int32, (rows, rows), 1)
    if seq_len >= rows:
        return None, r >= c, r > c
    shift = seq_len.bit_length() - 1
    assert 1 << shift == seq_len
    same = jnp.right_shift(r, shift) == jnp.right_shift(c, shift)
    return same, same & (r >= c), same & (r > c)


def _wide_masks(rows, seq_len):
    r = lax.broadcasted_iota(jnp.int32, (rows, 2 * rows), 0)
    c = lax.broadcasted_iota(jnp.int32, (rows, 2 * rows), 1)
    right = c >= rows
    cc = jnp.where(right, c - rows, c)
    if seq_len >= rows:
        return r >= cc, right & (r > cc)
    shift = seq_len.bit_length() - 1
    same = jnp.right_shift(r, shift) == jnp.right_shift(cc, shift)
    return same & (r >= cc), same & right & (r > cc)


def _pair_masks(rows, seq_len):
    half = LANES // 2
    assert rows == half
    lane = lax.broadcasted_iota(jnp.int32, (rows, LANES), 1)
    row = lax.broadcasted_iota(jnp.int32, (rows, LANES), 0)
    col = jnp.bitwise_and(lane, half - 1)
    if seq_len >= rows:
        causal, strict = row >= col, row > col
    else:
        shift = seq_len.bit_length() - 1
        same = jnp.right_shift(row, shift) == jnp.right_shift(col, shift)
        causal, strict = same & (row >= col), same & (row > col)
    return causal, strict, jnp.where(row == col, 1.0, 0.0).astype(F32), lane < half


def _bd(x, first_half):
    return jnp.concatenate([jnp.where(first_half, x, 0.0), jnp.where(first_half, 0.0, x)], axis=0)


def _pair_inverses(neg_a, rows, nilpotency, eye2, first_half):
    qs = dict(neg_a)
    ts = {u: eye2 + q for u, q in qs.items()}
    n = 2
    if n < nilpotency:
        qs = {u: _dot(q, _bd(q, first_half), mode=P_INV) for u, q in qs.items()}
    while n < nilpotency:
        if 2 * n < nilpotency:
            tq = {u: _dot(jnp.concatenate([ts[u], qs[u]], axis=0), _bd(qs[u], first_half), mode=P_INV)
                  for u in qs}
            ts = {u: ts[u] + tq[u][:rows] for u in qs}
            qs = {u: tq[u][rows:] for u in qs}
        else:
            ts = {u: ts[u] + _dot(ts[u], _bd(qs[u], first_half), mode=P_INV) for u in qs}
        n *= 2
    return ts


def _lane_group_sums(x, group):
    tile = 2 * LANES
    shift = group.bit_length() - 1
    li = jnp.right_shift(lax.broadcasted_iota(jnp.int32, (tile, tile), 0), shift)
    lj = jnp.right_shift(lax.broadcasted_iota(jnp.int32, (tile, tile), 1), shift)
    ones = jnp.where(li == lj, 1.0, 0.0).astype(BF16)
    hi, lo = _split(x)
    d = lambda u: lax.dot_general(u, ones, NN, preferred_element_type=F32)
    return jnp.concatenate([d(hi[:, t:t + tile]) + d(lo[:, t:t + tile])
                            for t in range(0, x.shape[1], tile)], axis=1)


def _unit_lower_inverses(mats, rows, nilpotency):
    r = lax.broadcasted_iota(jnp.int32, (rows, rows), 0)
    c = lax.broadcasted_iota(jnp.int32, (rows, rows), 1)
    eye = jnp.where(r == c, 1.0, 0.0).astype(F32)
    qs = [-a for a in mats]
    ts = [eye + q for q in qs]
    n = 2
    if n < nilpotency:
        qs = [_dot(q, q, mode=P_INV) for q in qs]
    while n < nilpotency:
        if 2 * n < nilpotency:
            tq = [_dot(jnp.concatenate([t, q], axis=0), q, mode=P_INV) for t, q in zip(ts, qs)]
            ts = [t + p[:rows] for t, p in zip(ts, tq)]
            qs = [p[rows:] for p in tq]
        else:
            ts = [t + _dot(t, q, mode=P_INV) for t, q in zip(ts, qs)]
        n *= 2
    return ts


def _permute_win_kernel(w_ref, o_ref):
    rw = REF_OFF_RWKV
    rows = w_ref.shape[0]
    cast = lambda lo, hi: w_ref[:, lo:hi].astype(BF16)
    o_ref[:, COL_QKV:COL_RKV] = cast(0, REF_OFF_Z)
    o_ref[:, COL_RKV:COL_Z] = cast(rw, rw + 3 * RWKV_WIDTH)
    o_ref[:, COL_Z:COL_LORA] = cast(REF_OFF_Z, REF_OFF_B)
    o_ref[:, COL_LORA:COL_BA] = cast(rw + 3 * RWKV_WIDTH, REF_IN_WIDTH)
    tail = jnp.concatenate([w_ref[:, REF_OFF_B:REF_OFF_RWKV],
                            jnp.zeros((rows, PROJ_WIDTH - COL_BA - 2 * GDN_HEADS), F32)], axis=1)
    o_ref[:, COL_BA:] = tail.astype(BF16)


def _permute_win(w, tr=256):
    return pl.pallas_call(
        _permute_win_kernel,
        grid=(D_MODEL // tr,),
        in_specs=[pl.BlockSpec((tr, REF_IN_WIDTH), lambda i: (i, 0))],
        out_specs=pl.BlockSpec((tr, PROJ_WIDTH), lambda i: (i, 0)),
        out_shape=jax.ShapeDtypeStruct((D_MODEL, PROJ_WIDTH), BF16),
        compiler_params=pltpu.CompilerParams(
            dimension_semantics=("arbitrary",), vmem_limit_bytes=VMEM_LIMIT),
        name="permute_win",
    )(w)


def _inproj_kernel(x_ref, g_ref, w_ref, o_ref, xn_ref):
    @pl.when(pl.program_id(1) == 0)
    def _():
        x = x_ref[...]
        ms = jnp.mean(x * x, axis=-1, keepdims=True)
        xn_ref[...] = (x * lax.rsqrt(ms + RMS_EPS) * g_ref[...]).astype(BF16)

    o_ref[...] = jnp.dot(xn_ref[...], w_ref[...], preferred_element_type=F32)


def _inproj(x2d, g_row, w_bf16, tm):
    m = x2d.shape[0]
    return pl.pallas_call(
        _inproj_kernel,
        grid=(m // tm, PROJ_WIDTH // PROJ_TN),
        in_specs=[
            pl.BlockSpec((tm, D_MODEL), lambda i, j: (i, 0)),
            pl.BlockSpec((1, D_MODEL), lambda i, j: (0, 0)),
            pl.BlockSpec((D_MODEL, PROJ_TN), lambda i, j: (0, j)),
        ],
        out_specs=pl.BlockSpec((tm, PROJ_TN), lambda i, j: (i, j)),
        out_shape=jax.ShapeDtypeStruct((m, PROJ_WIDTH), F32),
        scratch_shapes=[pltpu.VMEM((tm, D_MODEL), BF16)],
        compiler_params=pltpu.CompilerParams(
            dimension_semantics=("arbitrary", "arbitrary"), vmem_limit_bytes=VMEM_LIMIT),
        name="inproj",
    )(x2d, g_row, w_bf16)


PROJ_TN_CONV = 768
PROJ_SUB_ROWS = 256
QKV_TILES = 3 * GDN_WIDTH // PROJ_TN_CONV


def _inproj_conv_kernel(x_ref, g_ref, w_ref, cw_ref, cbuf_ref, o_ref, tail_ref, xn_ref, carry_ref,
                        hbuf_ref, *, tm, tiles_per_seq):
    i = pl.program_id(0)
    j = pl.program_id(1)
    hist = GDN_CONV - 1

    @pl.when(j == 0)
    def _():
        x = x_ref[...]
        ms = jnp.mean(x * x, axis=-1, keepdims=True)
        xn_ref[...] = (x * lax.rsqrt(ms + RMS_EPS) * g_ref[...]).astype(BF16)

    @pl.when(j >= QKV_TILES)
    def _():
        o_ref[...] = jnp.dot(xn_ref[...], w_ref[...], preferred_element_type=F32)

    @pl.when(j < QKV_TILES)
    def _():
        jc = jnp.minimum(j, QKV_TILES - 1)
        first = i % tiles_per_seq == 0

        @pl.when(first)
        def _():
            hbuf_ref[...] = jnp.zeros((SUBLANES, PROJ_TN_CONV), F32)
            hbuf_ref[SUBLANES - hist:SUBLANES, :] = cbuf_ref[...]

        @pl.when(jnp.logical_not(first))
        def _():
            hbuf_ref[...] = carry_ref[jc]

        prev = hbuf_ref[...]
        row = lax.broadcasted_iota(jnp.int32, (SUBLANES, PROJ_TN_CONV), 0)
        cw = cw_ref[...]
        sub = min(PROJ_SUB_ROWS, tm)
        for sb in range(tm // sub):
            rows = slice(sb * sub, (sb + 1) * sub)
            acc = jnp.dot(xn_ref[rows, :], w_ref[...], preferred_element_type=F32)
            conv = acc * cw[hist:hist + 1, :]
            for t in range(hist):
                d = hist - t
                sh = pltpu.roll(acc, d, 0)
                head = jnp.where(row < d, pltpu.roll(prev, d, 0), sh[:SUBLANES])
                conv = conv + jnp.concatenate([head, sh[SUBLANES:]], axis=0) * cw[t:t + 1, :]
            prev = acc[sub - SUBLANES:, :]
            o_ref[rows, :] = _silu(conv)
        carry_ref[jc] = prev
        tail_ref[...] = prev


def _inproj_conv(x2d, g_row, w_bf16, convw, cbuf, tm, seq_len):
    m = x2d.shape[0]
    tn = PROJ_TN_CONV
    tps = seq_len // tm
    qkv_tile = lambda j: jnp.minimum(j, QKV_TILES - 1)
    return pl.pallas_call(
        functools.partial(_inproj_conv_kernel, tm=tm, tiles_per_seq=tps),
        grid=(m // tm, PROJ_WIDTH // tn),
        in_specs=[
            pl.BlockSpec((tm, D_MODEL), lambda i, j: (i, 0)),
            pl.BlockSpec((1, D_MODEL), lambda i, j: (0, 0)),
            pl.BlockSpec((D_MODEL, tn), lambda i, j: (0, j)),
            pl.BlockSpec((GDN_CONV, tn), lambda i, j: (0, qkv_tile(j))),
            pl.BlockSpec((None, GDN_CONV - 1, tn), lambda i, j: (i // tps, 0, qkv_tile(j))),
        ],
        out_specs=[
            pl.BlockSpec((tm, tn), lambda i, j: (i, j)),
            pl.BlockSpec((None, SUBLANES, tn), lambda i, j: (i, 0, qkv_tile(j))),
        ],
        out_shape=[
            jax.ShapeDtypeStruct((m, PROJ_WIDTH), F32),
            jax.ShapeDtypeStruct((m // tm, SUBLANES, 3 * GDN_WIDTH), F32),
        ],
        scratch_shapes=[
            pltpu.VMEM((tm, D_MODEL), BF16),
            pltpu.VMEM((QKV_TILES, SUBLANES, tn), F32),
            pltpu.VMEM((SUBLANES, tn), F32),
        ],
        compiler_params=pltpu.CompilerParams(
            dimension_semantics=("arbitrary", "arbitrary"), vmem_limit_bytes=VMEM_LIMIT),
        name="inproj_conv",
    )(x2d, g_row, w_bf16, convw, cbuf)


def _mixer_geometry(seq_len, n_seq):
    rows = MIX_ROWS
    length = min(seq_len, rows)
    assert rows % length == 0 and seq_len % length == 0 and length % SUBLANES == 0
    per_group = rows // length
    groups = min(MIX_GROUPS_LONG, n_seq) if per_group == 1 else 1
    assert n_seq % (groups * per_group) == 0
    return rows, length, per_group, groups, seq_len // length


def _gdn_kernel(qkv_ref, z_ref, ba_ref, cbuf_ref, s0_ref, convw_ref, alog_r_ref, dtb_r_ref,
                alog_c_ref, dtb_c_ref, ng_ref, o_ref, s_ref, xp_ref, *, seq_len, n_seq):
    R, L, G, S, _ = _mixer_geometry(seq_len, n_seq)
    RT = S * R
    c = pl.program_id(1)
    width = 3 * GDN_WIDTH
    hist = GDN_CONV - 1
    cw = convw_ref[...]
    groups = range(S)
    seqs = range(G)

    @pl.when(c == 0)
    def _():
        s_ref[...] = s0_ref[...]

    pieces = []
    if G == 1:
        qkv = qkv_ref[...].reshape(RT, width)
    else:
        for s in groups:
            for g in seqs:
                q = s * G + g
                rows = slice(g * L, (g + 1) * L)
                xp_ref[q, SUBLANES - hist:SUBLANES, :] = cbuf_ref[q]
                xp_ref[q, SUBLANES:SUBLANES + L, :] = qkv_ref[s, rows, :]
                piece = qkv_ref[s, rows, :] * cw[hist:hist + 1, :]
                for i in range(hist):
                    off = SUBLANES - hist + i
                    piece = piece + xp_ref[q, off:off + L, :] * cw[i:i + 1, :]
                pieces.append(piece)
        qkv = _silu(pieces[0] if len(pieces) == 1 else jnp.concatenate(pieces, axis=0))

    same_t, causal_t, _ = _seq_masks(RT, L)
    causal01 = jnp.where(causal_t, 1.0, 0.0).astype(F32)

    ba = ba_ref[...].reshape(RT, LANES)
    ba_t = ba.T
    beta_c = _sigmoid(ba)
    g_c = -jnp.exp(alog_r_ref[...]) * _softplus(ba + dtb_r_ref[...])
    g_r = -jnp.exp(alog_c_ref[...]) * _softplus(ba_t + dtb_c_ref[...])
    gc_all = _ones_dot(causal01, g_c)
    gr_all = _ones_dot(causal01, g_r, NT)
    if same_t is None:
        gtot_all = jnp.broadcast_to(gc_all[RT - 1:RT, :], (RT, LANES))
    else:
        gtot_all = _ones_dot(jnp.where(same_t, 1.0, 0.0).astype(F32), g_c)

    qk_raw = qkv[:, :2 * GDN_WIDTH]
    qk_n = qk_raw * lax.rsqrt(_lane_group_sums(qk_raw * qk_raw, GDN_DK) + L2_EPS)
    q_all = qk_n[:, :GDN_WIDTH] * (GDN_DK ** -0.5)
    k_all = qk_n[:, GDN_WIDTH:]
    v_all = qkv[:, 2 * GDN_WIDTH:]

    causal2, strict2, eye2, first_half = _pair_masks(R, L)
    first_head = lax.broadcasted_iota(jnp.int32, (R, 2 * GDN_DK), 1) < GDN_DK
    zeros_u = jnp.zeros((R, GDN_DK), F32)
    zeros_rhs = jnp.zeros((R, 2 * GDN_DK), F32)
    chains = [(s, h) for s in groups for h in range(GDN_HEADS)]
    units = [(s, p) for s in groups for p in range(GDN_HEADS // 2)]
    seq_rows = [slice(g * L, (g + 1) * L) for g in seqs]
    qs, ks, vs, betas, gcols, gtots = {}, {}, {}, {}, {}, {}
    for s, h in chains:
        rs = slice(s * R, (s + 1) * R)
        lo = h * GDN_DK
        key = (s, h)
        qs[key] = q_all[rs, lo:lo + GDN_DK]
        ks[key] = k_all[rs, lo:lo + GDN_DK]
        vs[key] = v_all[rs, lo:lo + GDN_DK]
        betas[key] = beta_c[rs, h:h + 1]
        gcols[key] = gc_all[rs, GDN_HEADS + h:GDN_HEADS + h + 1]
        gtots[key] = gtot_all[rs, GDN_HEADS + h:GDN_HEADS + h + 1]
    kq2, decay2, a2 = {}, {}, {}
    for s, p in units:
        rs = slice(s * R, (s + 1) * R)
        cols = slice(2 * p * GDN_DK, (2 * p + 2) * GDN_DK)
        k_pair = k_all[rs, cols]
        kt_bd = jnp.concatenate([jnp.where(first_head, k_pair, 0.0),
                                 jnp.where(first_head, 0.0, k_pair)], axis=0).T
        kq = _dot(jnp.concatenate([k_pair, q_all[rs, cols]], axis=0), kt_bd, mode=P_GRAM)
        h0, h1 = (s, 2 * p), (s, 2 * p + 1)
        gcol2 = jnp.where(first_half, gcols[h0], gcols[h1])
        grow2 = jnp.concatenate([gr_all[GDN_HEADS + 2 * p:GDN_HEADS + 2 * p + 1, rs],
                                 gr_all[GDN_HEADS + 2 * p + 1:GDN_HEADS + 2 * p + 2, rs]], axis=1)
        dec = jnp.where(causal2, jnp.exp(jnp.where(causal2, gcol2 - grow2, 0.0)), 0.0)
        beta2 = jnp.where(first_half, betas[h0], betas[h1])
        kq2[(s, p)] = kq
        decay2[(s, p)] = dec
        a2[(s, p)] = jnp.where(strict2, -(beta2 * kq[:R] * dec), 0.0)
    t2 = _pair_inverses(a2, R, L, eye2, first_half)
    gammas = {key: jnp.exp(gcols[key]) for key in chains}

    def stacked(key, x, zeros):
        return jnp.concatenate([x, zeros] if key[1] % 2 == 0 else [zeros, x], axis=0)

    sols = {key: _dot(t2[(key[0], key[1] // 2)],
                      stacked(key, jnp.concatenate([(betas[key] * gammas[key]) * ks[key],
                                                    betas[key] * vs[key]], axis=1), zeros_rhs),
                      mode=P_SOLVE) for key in chains}
    states = {(s, h): [s_ref[s * G + g, h] for g in seqs] for s, h in chains}
    wss = {key: [_dot(jnp.concatenate([sols[key][rows, :GDN_DK], (qs[key] * gammas[key])[rows]], axis=0),
                      states[key][g], mode=P_STATE) for g, rows in enumerate(seq_rows)]
           for key in chains}
    us = {key: jnp.concatenate([sols[key][rows, GDN_DK:] - wss[key][g][:L]
                                for g, rows in enumerate(seq_rows)], axis=0) for key in chains}
    qk2 = {u: kq2[u][R:] * decay2[u] for u in units}
    outs = {key: jnp.concatenate([wss[key][g][L:] for g in seqs], axis=0)
            + _dot(qk2[(key[0], key[1] // 2)], stacked(key, us[key], zeros_u), mode=P_OUT)
            for key in chains}
    for s, h in chains:
        key = (s, h)
        kt = ks[key] * jnp.exp(gtots[key] - gcols[key])
        for g, rows in enumerate(seq_rows):
            gl = jnp.exp(gtots[key][g * L:g * L + 1, :])
            s_ref[s * G + g, h] = gl * states[key][g] + _dot(kt[rows], us[key][rows], TN, mode=P_STATE)
    ng = jnp.concatenate([ng_ref[...]] * GDN_HEADS, axis=1)
    for s in groups:
        o = jnp.concatenate([outs[(s, h)] for h in range(GDN_HEADS)], axis=1)
        ms = _lane_group_sums(o * o, GDN_DK) * (1.0 / GDN_DK)
        o = o * lax.rsqrt(ms + RMS_EPS) * ng
        o_ref[s] = (o * _silu(z_ref[s])).astype(o_ref.dtype)


def _gdn_mixer(proj2d, cbuf, s0, convw, alog_r, dtb_r, alog_c, dtb_c, ng, seq_len):
    m = proj2d.shape[0]
    nseq = m // seq_len
    R, L, G, S, sps = _mixer_geometry(seq_len, nseq)
    width = 3 * GDN_WIDTH
    proj3d = proj2d.reshape(nseq // G, sps * R, PROJ_WIDTH)
    const2 = lambda i, c: (0, 0)
    rows_map = lambda col: (lambda i, c: (i, c, col))
    per_seq = lambda *dims: pl.BlockSpec((S * G,) + dims, lambda i, c: (i,) + (0,) * len(dims))
    xp_shape = (SUBLANES, LANES) if G == 1 else (S * G, SUBLANES + L, width)
    o, s_new = pl.pallas_call(
        functools.partial(_gdn_kernel, seq_len=seq_len, n_seq=nseq),
        grid=(nseq // (S * G), sps),
        in_specs=[
            pl.BlockSpec((S, R, width), rows_map(COL_QKV // width)),
            pl.BlockSpec((S, R, GDN_WIDTH), rows_map(COL_Z // GDN_WIDTH)),
            pl.BlockSpec((S, R, LANES), rows_map(COL_BA // LANES)),
            per_seq(GDN_CONV - 1, width),
            per_seq(GDN_HEADS, GDN_DK, GDN_DK),
            pl.BlockSpec((GDN_CONV, width), const2),
            pl.BlockSpec((1, LANES), const2),
            pl.BlockSpec((1, LANES), const2),
            pl.BlockSpec((LANES, 1), const2),
            pl.BlockSpec((LANES, 1), const2),
            pl.BlockSpec((1, GDN_DK), const2),
        ],
        out_specs=[
            pl.BlockSpec((S, R, GDN_WIDTH), rows_map(0)),
            per_seq(GDN_HEADS, GDN_DK, GDN_DK),
        ],
        out_shape=[
            jax.ShapeDtypeStruct((nseq // G, sps * R, GDN_WIDTH), BF16),
            jax.ShapeDtypeStruct((nseq, GDN_HEADS, GDN_DK, GDN_DK), F32),
        ],
        scratch_shapes=[pltpu.VMEM(xp_shape, F32)],
        compiler_params=pltpu.CompilerParams(
            dimension_semantics=("arbitrary", "arbitrary"), vmem_limit_bytes=VMEM_LIMIT),
        name="gdn_mixer",
    )(proj3d, proj3d, proj3d, cbuf, s0, convw, alog_r, dtb_r, alog_c, dtb_c, ng)
    return o.reshape(m, GDN_WIDTH), s_new


def _shifted_rows(x_ref, prev_ref, carry_ref, c, R, L, G, S):
    width = x_ref.shape[-1]
    groups = range(S)
    if G == 1:
        @pl.when(c == 0)
        def _():
            for s in groups:
                carry_ref[s, 0:SUBLANES, :] = jnp.zeros((SUBLANES, width), F32)
                carry_ref[s, SUBLANES - 1:SUBLANES, :] = prev_ref[s]

        @pl.when(c > 0)
        def _():
            for s in groups:
                carry_ref[s, 0:SUBLANES, :] = carry_ref[s, R:R + SUBLANES, :]

        xs, prevs = [], []
        for s in groups:
            carry_ref[s, SUBLANES:SUBLANES + R, :] = x_ref[s]
            xs.append(x_ref[s])
            prevs.append(carry_ref[s, SUBLANES - 1:SUBLANES - 1 + R, :])
    else:
        row = lax.broadcasted_iota(jnp.int32, (L, width), 0)
        xs, prevs = [], []
        for s in groups:
            x = x_ref[s]
            xs.append(x)
            for g in range(G):
                xg = x[g * L:(g + 1) * L]
                prevs.append(jnp.where(row == 0, prev_ref[s * G + g], pltpu.roll(xg, 1, 0)))
    cat = lambda parts: parts[0] if len(parts) == 1 else jnp.concatenate(parts, axis=0)
    return cat(xs), cat(prevs)


def _rwkv_kernel(rkv_ref, lora_ref, sh_rkv_ref, sh_lora_ref, s0_ref, mu_rkv_ref, mu_lora_ref,
                 w0_ref, a0_ref, wab_ref, gb_ref, kk_ref, ka_ref, rk_ref, gnw_ref, gnb_ref,
                 o_ref, s_ref, xr_ref, xl_ref, m_ref, *, seq_len, n_seq):
    R, L, G, S, sps = _mixer_geometry(seq_len, n_seq)
    RT = S * R
    NQ = S * G
    c = pl.program_id(1)
    W = RWKV_WIDTH
    HD = RWKV_HEAD
    pairs = range(RWKV_HEADS // 2)
    groups = range(S)
    seqs = range(G)

    @pl.when(c == 0)
    def _():
        zero = jnp.zeros((HD, HD), F32)
        for q in range(NQ):
            for j in pairs:
                vk = jnp.concatenate([jnp.concatenate([s0_ref[q, 2 * j], zero], axis=1),
                                      jnp.concatenate([zero, s0_ref[q, 2 * j + 1]], axis=1)], axis=0)
                m_ref[q, j] = vk.T

    p, p_prev = _shifted_rows(rkv_ref, sh_rkv_ref, xr_ref, c, R, L, G, S)
    xs = p + (p_prev - p) * mu_rkv_ref[...]
    pl_, pl_prev = _shifted_rows(lora_ref, sh_lora_ref, xl_ref, c, R, L, G, S)
    xl = pl_ + (pl_prev - pl_) * mu_lora_ref[...]
    r = xs[:, :W]
    k = xs[:, W:2 * W]
    v = xs[:, 2 * W:]

    wa_in = xl[:, :LANES]
    lane = lax.broadcasted_iota(jnp.int32, (RT, LANES), 1)
    wa_in = jnp.where(lane < RWKV_LORA_W, jnp.tanh(wa_in), wa_in)
    wa = _dot(wa_in, wab_ref[...])
    w = -_softplus(-(w0_ref[...] + wa[:, :W])) - 0.5
    lw = -jnp.exp(w)
    a = _sigmoid(a0_ref[...] + wa[:, W:])
    gate = _dot(_sigmoid(xl[:, LANES:]), gb_ref[...])
    kk_raw = k * kk_ref[...]
    k2 = k * (1.0 + (a - 1.0) * ka_ref[...])

    same_t, causal_t, _ = _seq_masks(RT, L)
    lc = _ones_dot(jnp.where(causal_t, 1.0, 0.0).astype(F32), lw)
    if G == 1:
        ltot = jnp.concatenate([jnp.broadcast_to(lc[(s + 1) * R - 1:(s + 1) * R, :], (R, W))
                                for s in groups], axis=0) if S > 1 else \
            jnp.broadcast_to(lc[R - 1:R, :], (R, W))
    else:
        ltot = _ones_dot(jnp.where(same_t, 1.0, 0.0).astype(F32), lw)
    e_inv = jnp.exp(-lc)
    e_rem = jnp.exp(ltot - lc)

    kk = kk_raw * lax.rsqrt(_lane_group_sums(kk_raw * kk_raw, HD) + L2_EPS)
    kka = kk * a
    ct = kk * jnp.exp(lc - lw)
    rt = r * jnp.exp(lc)
    bh = kka * e_inv
    kh = k2 * e_inv
    bb = kka * e_rem
    kb = k2 * e_rem
    p_rows = [ltot[q * L:q * L + 1] for q in range(NQ)]
    p_rows = p_rows + [p_rows[0]] * (-NQ % SUBLANES)
    pt = jnp.exp(jnp.concatenate(p_rows, axis=0)).T

    causal2, strict2, eye2, even_half = _pair_masks(R, L)
    block_diag = ((lax.broadcasted_iota(jnp.int32, (LANES, LANES), 0) < HD)
                  == (lax.broadcasted_iota(jnp.int32, (LANES, LANES), 1) < HD))
    even_all = jnp.bitwise_and(lax.broadcasted_iota(jnp.int32, (R, W), 1), LANES - 1) < HD
    tile = lambda j: slice(j * LANES, (j + 1) * LANES)
    bd = lambda x: _bd(x, even_half)

    if G > 1:
        row2 = lax.broadcasted_iota(jnp.int32, (2 * R, LANES), 0)
        seq_of_row = jnp.right_shift(jnp.bitwise_and(row2, R - 1), L.bit_length() - 1)

    units = [(s, j) for s in groups for j in pairs]
    xps, vps, bdb, bdk, bkts = {}, {}, {}, {}, {}
    for s in groups:
        rs = slice(s * R, (s + 1) * R)
        x_s = jnp.concatenate([ct[rs], rt[rs]], axis=0)
        bt_s = jnp.concatenate([jnp.where(even_all, bh[rs], 0.0), jnp.where(even_all, 0.0, bh[rs])],
                               axis=0).T
        kt_s = jnp.concatenate([jnp.where(even_all, kh[rs], 0.0), jnp.where(even_all, 0.0, kh[rs])],
                               axis=0).T
        bkt_s = jnp.concatenate([bb[rs], kb[rs]], axis=0).T
        for j in pairs:
            xps[(s, j)] = x_s[:, tile(j)]
            vps[(s, j)] = v[rs, tile(j)]
            bdb[(s, j)] = bt_s[tile(j), :]
            bdk[(s, j)] = kt_s[tile(j), :]
            bkts[(s, j)] = bkt_s[tile(j), :]
    gb = {u: _dot(xps[u], bdb[u], mode=P_GRAM) for u in units}
    gk = {u: _dot(xps[u], bdk[u], mode=P_GRAM) for u in units}
    ms = {(s, j): [m_ref[s * G + g, j] for g in seqs] for s, j in units}
    xm_c, xm_r = {}, {}
    for u in units:
        if G == 1:
            xm = _dot(xps[u], ms[u][0], mode=P_STATE)
            xm_c[u], xm_r[u] = xm[:R], xm[R:]
        else:
            parts = [_dot(jnp.concatenate([xps[u][g * L:(g + 1) * L],
                                           xps[u][R + g * L:R + (g + 1) * L]], axis=0),
                          ms[u][g], mode=P_STATE) for g in seqs]
            xm_c[u] = jnp.concatenate([p_[:L] for p_ in parts], axis=0)
            xm_r[u] = jnp.concatenate([p_[L:] for p_ in parts], axis=0)
    ts = _pair_inverses({u: jnp.where(strict2, -gb[u][:R], 0.0) for u in units}, R, L, eye2, even_half)
    bdv = {u: bd(vps[u]) for u in units}
    akvs = {u: _dot(jnp.where(strict2, gk[u][:R], 0.0), bdv[u], mode=P_OUT) for u in units}
    us = {u: _dot(ts[u], bd(-(xm_c[u] + akvs[u])), mode=P_SOLVE) for u in units}
    uvs = {u: jnp.concatenate([us[u], vps[u]], axis=0) for u in units}
    ys = {u: xm_r[u] + _dot(
        jnp.concatenate([jnp.where(causal2, gb[u][R:], 0.0), jnp.where(causal2, gk[u][R:], 0.0)], axis=1),
        jnp.concatenate([bd(us[u]), bdv[u]], axis=0), mode=P_OUT) for u in units}
    for s, j in units:
        u = (s, j)
        for g in seqs:
            q = s * G + g
            uv_g = uvs[u] if G == 1 else jnp.where(seq_of_row == g, uvs[u], 0.0)
            upd = _dot(bkts[u], uv_g, mode=P_STATE)
            m_ref[q, j] = pt[tile(j), q:q + 1] * ms[u][g] + jnp.where(block_diag, upd, 0.0)

    y_rows = [jnp.concatenate([ys[(s, j)] for j in pairs], axis=1) for s in groups]
    y_all = y_rows[0] if S == 1 else jnp.concatenate(y_rows, axis=0)
    mean = _lane_group_sums(y_all, HD) * (1.0 / HD)
    yc = y_all - mean
    var = _lane_group_sums(yc * yc, HD) * (1.0 / HD)
    yn = yc * lax.rsqrt(var + GN_EPS) * gnw_ref[...] + gnb_ref[...]
    yn = yn + _lane_group_sums(r * k2 * rk_ref[...], HD) * v
    o_ref[...] = (yn * gate).reshape(S, R, W).astype(o_ref.dtype)

    @pl.when(c == sps - 1)
    def _():
        for q in range(NQ):
            for j in pairs:
                vk = m_ref[q, j].T
                s_ref[q, 2 * j] = vk[:HD, :HD]
                s_ref[q, 2 * j + 1] = vk[HD:, HD:]


def _rwkv_mixer(proj2d, sh_rkv, sh_lora, s0, mu_rkv, mu_lora, w0, a0, wab, gb, kk, ka, rk, gnw, gnb,
                seq_len):
    m = proj2d.shape[0]
    nseq = m // seq_len
    R, L, G, S, sps = _mixer_geometry(seq_len, nseq)
    W = RWKV_WIDTH
    proj3d = proj2d.reshape(nseq // G, sps * R, PROJ_WIDTH)
    const2 = lambda i, c: (0, 0)
    row = lambda width: pl.BlockSpec((1, width), const2)
    rows_map = lambda col: (lambda i, c: (i, c, col))
    per_seq = lambda *dims: pl.BlockSpec((S * G,) + dims, lambda i, c: (i,) + (0,) * len(dims))
    carry = lambda width: pltpu.VMEM((S, SUBLANES + R, width) if G == 1 else (SUBLANES, LANES), F32)
    o, s_new = pl.pallas_call(
        functools.partial(_rwkv_kernel, seq_len=seq_len, n_seq=nseq),
        grid=(nseq // (S * G), sps),
        in_specs=[
            pl.BlockSpec((S, R, 3 * W), rows_map(COL_RKV // (3 * W))),
            pl.BlockSpec((S, R, RWKV_LORA), rows_map(COL_LORA // RWKV_LORA)),
            per_seq(1, 3 * W),
            per_seq(1, RWKV_LORA),
            per_seq(RWKV_HEADS, RWKV_HEAD, RWKV_HEAD),
            row(3 * W), row(RWKV_LORA), row(W), row(W),
            pl.BlockSpec((LANES, 2 * W), const2),
            pl.BlockSpec((RWKV_LORA_G, W), const2),
            row(W), row(W), row(W), row(W), row(W),
        ],
        out_specs=[pl.BlockSpec((S, R, W), rows_map(0)), per_seq(RWKV_HEADS, RWKV_HEAD, RWKV_HEAD)],
        out_shape=[
            jax.ShapeDtypeStruct((nseq // G, sps * R, W), BF16),
            jax.ShapeDtypeStruct((nseq, RWKV_HEADS, RWKV_HEAD, RWKV_HEAD), F32),
        ],
        scratch_shapes=[carry(3 * W), carry(RWKV_LORA),
                        pltpu.VMEM((S * G, RWKV_HEADS // 2, LANES, LANES), F32)],
        compiler_params=pltpu.CompilerParams(
            dimension_semantics=("arbitrary", "arbitrary"), vmem_limit_bytes=VMEM_LIMIT),
        name="rwkv_mixer",
    )(proj3d, proj3d, sh_rkv, sh_lora, s0, mu_rkv, mu_lora, w0, a0, wab, gb, kk, ka, rk, gnw, gnb)
    return o.reshape(m, W), s_new


def _outproj_kernel(x_ref, oa_ref, ob_ref, wa_ref, wb_ref, g_ref, x1_ref, hn_ref):
    x1 = (x_ref[...] + jnp.dot(oa_ref[...], wa_ref[...], preferred_element_type=F32)
          + jnp.dot(ob_ref[...], wb_ref[...], preferred_element_type=F32))
    x1_ref[...] = x1
    ms = jnp.mean(x1 * x1, axis=-1, keepdims=True)
    hn_ref[...] = (x1 * lax.rsqrt(ms + RMS_EPS) * g_ref[...]).astype(BF16)


def _outproj(x2d, oa, ob, wo_bf16, g_row, tm):
    m = x2d.shape[0]
    return pl.pallas_call(
        _outproj_kernel,
        grid=(m // tm,),
        in_specs=[
            pl.BlockSpec((tm, D_MODEL), lambda i: (i, 0)),
            pl.BlockSpec((tm, GDN_WIDTH), lambda i: (i, 0)),
            pl.BlockSpec((tm, RWKV_WIDTH), lambda i: (i, 0)),
            pl.BlockSpec((GDN_WIDTH, D_MODEL), lambda i: (0, 0)),
            pl.BlockSpec((RWKV_WIDTH, D_MODEL), lambda i: (1, 0)),
            pl.BlockSpec((1, D_MODEL), lambda i: (0, 0)),
        ],
        out_specs=[pl.BlockSpec((tm, D_MODEL), lambda i: (i, 0)),
                   pl.BlockSpec((tm, D_MODEL), lambda i: (i, 0))],
        out_shape=[jax.ShapeDtypeStruct((m, D_MODEL), F32),
                   jax.ShapeDtypeStruct((m, D_MODEL), BF16)],
        compiler_params=pltpu.CompilerParams(
            dimension_semantics=("arbitrary",), vmem_limit_bytes=VMEM_LIMIT),
        name="outproj",
    )(x2d, oa, ob, wo_bf16, wo_bf16, g_row)


FFN_TF = 512
FFN_NF = D_FF // FFN_TF
FFN_SUB_ROWS = 128


def _ffn_pipeline(f, up_fn, wd_ref, x1_ref, fg_ref, y_ref, acc_ref):
    @pl.when(f == 0)
    def _():
        acc_ref[...] = jnp.zeros(acc_ref.shape, F32)

    acc_ref[...] += jnp.dot(up_fn().astype(BF16), wd_ref[...], preferred_element_type=F32)

    @pl.when(f == FFN_NF - 1)
    def _():
        xo = x1_ref[...] + acc_ref[...]
        ms = jnp.mean(xo * xo, axis=-1, keepdims=True)
        y_ref[...] = xo * lax.rsqrt(ms + RMS_EPS) * fg_ref[...]


def _ffn_long_kernel(hn_ref, x1_ref, wg_ref, wu_ref, cwg_ref, cwu_ref, wd_ref, fg_ref,
                     y_ref, ng_ref, nu_ref, acc_ref, carry_ref, *, tt):
    ti = pl.program_id(1)
    f = pl.program_id(2)
    hist = FFN_CONV - 1
    sub = FFN_SUB_ROWS

    @pl.when(f == 0)
    def _():
        acc_ref[...] = jnp.zeros(acc_ref.shape, F32)

    row = lax.broadcasted_iota(jnp.int32, (SUBLANES, FFN_TF), 0)
    branches = ((wg_ref, cwg_ref, ng_ref), (wu_ref, cwu_ref, nu_ref))
    prevs = []
    for j in range(2):
        prev = carry_ref[f, j]
        prevs.append(jnp.where(ti == 0, jnp.zeros_like(prev), prev))
    nsb = tt // sub
    act = None
    for sb in range(nsb + 1):
        hs = []
        if sb < nsb:
            hn = hn_ref[sb * sub:(sb + 1) * sub, :]
            hs = [jnp.dot(hn, w_ref[...], preferred_element_type=F32) for w_ref, _, _ in branches]
        if sb > 0:
            acc_ref[(sb - 1) * sub:sb * sub, :] += jnp.dot(act, wd_ref[...], preferred_element_type=F32)
        if sb < nsb:
            convs = []
            for j, (_, cw_ref, _) in enumerate(branches):
                h = hs[j]
                cw = cw_ref[...]
                conv = h * cw[hist:hist + 1, :]
                for i in range(hist):
                    d = hist - i
                    sh = pltpu.roll(h, d, 0)
                    head = jnp.where(row < d, pltpu.roll(prevs[j], d, 0), sh[:SUBLANES])
                    conv = conv + jnp.concatenate([head, sh[SUBLANES:]], axis=0) * cw[i:i + 1, :]
                prevs[j] = h[sub - SUBLANES:, :]
                convs.append(conv)
            act = (_silu(convs[0]) * convs[1]).astype(BF16)
    for j, (_, _, n_ref) in enumerate(branches):
        carry_ref[f, j] = prevs[j]
        n_ref[...] = prevs[j][SUBLANES - hist:, :]

    @pl.when(f == FFN_NF - 1)
    def _():
        xo = x1_ref[...] + acc_ref[...]
        ms = jnp.mean(xo * xo, axis=-1, keepdims=True)
        y_ref[...] = xo * lax.rsqrt(ms + RMS_EPS) * fg_ref[...]


def _ffn_long(hn, x1, wup, cw, wdown, fg_row, tt):
    b, t, _ = hn.shape
    tf = FFN_TF
    nf = FFN_NF
    up = lambda f: f
    down = lambda f: f
    return pl.pallas_call(
        functools.partial(_ffn_long_kernel, tt=tt),
        grid=(b, t // tt, nf),
        in_specs=[
            pl.BlockSpec((None, tt, D_MODEL), lambda i, s, f: (i, s, 0)),
            pl.BlockSpec((None, tt, D_MODEL), lambda i, s, f: (i, s, 0)),
            pl.BlockSpec((D_MODEL, tf), lambda i, s, f: (0, up(f))),
            pl.BlockSpec((D_MODEL, tf), lambda i, s, f: (0, nf + up(f))),
            pl.BlockSpec((FFN_CONV, tf), lambda i, s, f: (0, up(f))),
            pl.BlockSpec((FFN_CONV, tf), lambda i, s, f: (0, nf + up(f))),
            pl.BlockSpec((tf, D_MODEL), lambda i, s, f: (down(f), 0)),
            pl.BlockSpec((1, D_MODEL), lambda i, s, f: (0, 0)),
        ],
        out_specs=[
            pl.BlockSpec((None, tt, D_MODEL), lambda i, s, f: (i, s, 0)),
            pl.BlockSpec((None, None, FFN_CONV - 1, tf), lambda i, s, f: (i, s, 0, up(f))),
            pl.BlockSpec((None, None, FFN_CONV - 1, tf), lambda i, s, f: (i, s, 0, up(f))),
        ],
        out_shape=[
            jax.ShapeDtypeStruct((b, t, D_MODEL), F32),
            jax.ShapeDtypeStruct((b, t // tt, FFN_CONV - 1, D_FF), F32),
            jax.ShapeDtypeStruct((b, t // tt, FFN_CONV - 1, D_FF), F32),
        ],
        scratch_shapes=[
            pltpu.VMEM((tt, D_MODEL), F32),
            pltpu.VMEM((nf, 2, SUBLANES, tf), F32),
        ],
        compiler_params=pltpu.CompilerParams(
            dimension_semantics=("arbitrary", "arbitrary", "arbitrary"),
            vmem_limit_bytes=VMEM_LIMIT),
        name="ffn_long",
    )(hn, x1, wup, wup, cw, cw, wdown, fg_row)


def _ffn_short_kernel(hn_ref, x1_ref, wg_ref, wu_ref, cwg_ref, cwu_ref, wd_ref, fg_ref,
                      b0g_ref, b1g_ref, b0u_ref, b1u_ref,
                      y_ref, n0g_ref, n1g_ref, n0u_ref, n1u_ref, acc_ref, z_ref, hb_ref, *, tt, seq):
    f = pl.program_id(1)
    nseq = tt // seq

    def up_fn():
        hn = hn_ref[...]
        t_in_seq = lax.broadcasted_iota(jnp.int32, (tt, FFN_TF), 0) % seq
        convs = []
        groups = ((wg_ref, cwg_ref, b0g_ref, b1g_ref, n0g_ref, n1g_ref),
                  (wu_ref, cwu_ref, b0u_ref, b1u_ref, n0u_ref, n1u_ref))
        for w_ref, cw_ref, b0_ref, b1_ref, n0_ref, n1_ref in groups:
            h = jnp.dot(hn, w_ref[...], preferred_element_type=F32)
            z_ref[...] = jnp.zeros(z_ref.shape, F32)
            for lb in range(FFN_TF // LANES):
                cols = slice(lb * LANES, (lb + 1) * LANES)
                z_ref[lb, pl.ds(0, nseq, stride=seq), :] = b0_ref[:, cols]
                z_ref[lb, pl.ds(1, nseq, stride=seq), :] = b1_ref[:, cols]
                hb_ref[lb] = h[:, cols]
                n0_ref[:, cols] = hb_ref[lb, pl.ds(seq - 2, nseq, stride=seq), :]
                n1_ref[:, cols] = hb_ref[lb, pl.ds(seq - 1, nseq, stride=seq), :]
            z = jnp.concatenate([z_ref[lb] for lb in range(FFN_TF // LANES)], axis=1)
            s1 = jnp.where(t_in_seq == 0, pltpu.roll(z, tt - 1, 0), pltpu.roll(h, 1, 0))
            s2 = jnp.where(t_in_seq < 2, z, pltpu.roll(h, 2, 0))
            cw = cw_ref[...]
            convs.append(h * cw[2:3, :] + s1 * cw[1:2, :] + s2 * cw[0:1, :])
        return _silu(convs[0]) * convs[1]

    _ffn_pipeline(f, up_fn, wd_ref, x1_ref, fg_ref, y_ref, acc_ref)


def _ffn_short(hn, x1, wup, cw, wdown, fg_row, buf0, buf1, tt, seq):
    m = hn.shape[0]
    tf = FFN_TF
    nf = FFN_NF
    nseq = tt // seq
    up = lambda f: f
    down = lambda f: f
    st_g = pl.BlockSpec((nseq, tf), lambda i, f: (i, up(f)))
    st_u = pl.BlockSpec((nseq, tf), lambda i, f: (i, nf + up(f)))
    new = pl.BlockSpec((nseq, tf), lambda i, f: (i, up(f)))
    new_shape = jax.ShapeDtypeStruct((m // seq, D_FF), F32)
    return pl.pallas_call(
        functools.partial(_ffn_short_kernel, tt=tt, seq=seq),
        grid=(m // tt, nf),
        in_specs=[
            pl.BlockSpec((tt, D_MODEL), lambda i, f: (i, 0)),
            pl.BlockSpec((tt, D_MODEL), lambda i, f: (i, 0)),
            pl.BlockSpec((D_MODEL, tf), lambda i, f: (0, up(f))),
            pl.BlockSpec((D_MODEL, tf), lambda i, f: (0, nf + up(f))),
            pl.BlockSpec((FFN_CONV, tf), lambda i, f: (0, up(f))),
            pl.BlockSpec((FFN_CONV, tf), lambda i, f: (0, nf + up(f))),
            pl.BlockSpec((tf, D_MODEL), lambda i, f: (down(f), 0)),
            pl.BlockSpec((1, D_MODEL), lambda i, f: (0, 0)),
            st_g, st_g, st_u, st_u,
        ],
        out_specs=[pl.BlockSpec((tt, D_MODEL), lambda i, f: (i, 0)), new, new, new, new],
        out_shape=[jax.ShapeDtypeStruct((m, D_MODEL), F32), new_shape, new_shape, new_shape, new_shape],
        scratch_shapes=[
            pltpu.VMEM((tt, D_MODEL), F32),
            pltpu.VMEM((tf // LANES, tt, LANES), F32),
            pltpu.VMEM((tf // LANES, tt, LANES), F32),
        ],
        compiler_params=pltpu.CompilerParams(
            dimension_semantics=("arbitrary", "arbitrary"), vmem_limit_bytes=VMEM_LIMIT),
        name="ffn_short",
    )(hn, x1, wup, wup, cw, cw, wdown, fg_row, buf0, buf1, buf0, buf1)


def _pad_lanes(vec, offset):
    out = jnp.zeros((LANES,), F32)
    return out.at[offset:offset + vec.shape[0]].set(vec.astype(F32))


def _trunk(x, s_gdn, s_gconv, s_rwkv, s_shift, s_ffn, prm, *, long_seq):
    b, t, _ = x.shape
    m = b * t
    x2d = x.reshape(m, D_MODEL)
    tm = min(512, m)
    if long_seq:
        tp = min(1024, t)
        proj, qkv_tail = _inproj_conv(x2d, prm["ln1_g"], prm["w_in"], prm["gdn_conv_w"], s_gconv, tp, t)
        gconv_new = qkv_tail.reshape(b, t // tp, SUBLANES, 3 * GDN_WIDTH)[:, -1, SUBLANES - (GDN_CONV - 1):]
    else:
        proj = _inproj(x2d, prm["ln1_g"], prm["w_in"], min(1024, m))
        gconv_new = proj.reshape(b, t, PROJ_WIDTH)[:, t - (GDN_CONV - 1):, COL_QKV:COL_QKV + 3 * GDN_WIDTH]

    o_a, gdn_new = _gdn_mixer(proj, s_gconv, s_gdn, prm["gdn_conv_w"], prm["alog_r"], prm["dtb_r"],
                              prm["alog_c"], prm["dtb_c"], prm["gdn_norm_g"], t)
    sh_rkv = s_shift[:, None, :3 * RWKV_WIDTH]
    sh_lora = s_shift[:, None, 3 * RWKV_WIDTH:]
    o_b, rwkv_new = _rwkv_mixer(proj, sh_rkv, sh_lora, s_rwkv, prm["mu_rkv"], prm["mu_lora"],
                                prm["rwkv_w0"], prm["rwkv_a0"], prm["rwkv_wab"], prm["rwkv_g_b"],
                                prm["rwkv_k_k"], prm["rwkv_k_a"], prm["rwkv_r_k"], prm["rwkv_gn_w"],
                                prm["rwkv_gn_b"], t)

    x1, hn = _outproj(x2d, o_a, o_b, prm["w_o"], prm["ln2_g"], tm)
    if long_seq:
        tt = min(512, t)
        y, n_g, n_u = _ffn_long(hn.reshape(b, t, D_MODEL), x1.reshape(b, t, D_MODEL), prm["ffn_w_up"],
                                prm["ffn_conv_w"], prm["ffn_w_down"], prm["final_g"], tt)
        ffn_new = jnp.concatenate([n_g[:, -1], n_u[:, -1]], axis=-1)
    else:
        tt = min(512, m)
        y, n0g, n1g, n0u, n1u = _ffn_short(hn, x1, prm["ffn_w_up"], prm["ffn_conv_w"],
                                            prm["ffn_w_down"], prm["final_g"],
                                            s_ffn[:, 0], s_ffn[:, 1], tt, t)
        y = y.reshape(b, t, D_MODEL)
        ffn_new = jnp.stack([jnp.concatenate([n0g, n0u], axis=-1),
                             jnp.concatenate([n1g, n1u], axis=-1)], axis=1)

    proj = proj.reshape(b, t, PROJ_WIDTH)
    shift_new = jnp.concatenate([proj[:, t - 1, COL_RKV:COL_RKV + 3 * RWKV_WIDTH],
                                 proj[:, t - 1, COL_LORA:COL_LORA + RWKV_LORA]], axis=-1)
    return y, gdn_new[None], gconv_new[None], rwkv_new[None], shift_new[None], ffn_new[None]


def kernel(x_prompt, x_sample, state_gdn, state_gdn_conv, state_rwkv, state_rwkv_shift, state_ffn_conv, ln1_g, w_in, gdn_conv_w, gdn_a_log, gdn_dt_bias, gdn_norm_g, rwkv_mu, rwkv_w0, rwkv_w_b, rwkv_a0, rwkv_a_b, rwkv_g_b, rwkv_k_k, rwkv_k_a, rwkv_r_k, rwkv_gn_w, rwkv_gn_b, w_o, ln2_g, ffn_w_up, ffn_conv_w, ffn_w_down, final_g):
    assert ln1_g.shape[0] == 1, "single-layer trunk"
    w_perm = _permute_win(w_in[0])
    mu = rwkv_mu[0]
    zeros_w = jnp.zeros((RWKV_LORA_W, RWKV_WIDTH), F32)
    wab = jnp.concatenate([
        jnp.concatenate([rwkv_w_b[0], zeros_w], axis=1),
        jnp.concatenate([zeros_w, rwkv_a_b[0]], axis=1)], axis=0).astype(BF16)
    alog = _pad_lanes(gdn_a_log[0], GDN_HEADS)
    dtb = _pad_lanes(gdn_dt_bias[0], GDN_HEADS)
    prm = {
        "ln1_g": ln1_g[0][None], "w_in": w_perm, "gdn_conv_w": gdn_conv_w[0],
        "alog_r": alog[None], "dtb_r": dtb[None], "alog_c": alog[:, None], "dtb_c": dtb[:, None],
        "gdn_norm_g": gdn_norm_g[0][None],
        "mu_rkv": mu[None, :3 * RWKV_WIDTH], "mu_lora": mu[None, 3 * RWKV_WIDTH:],
        "rwkv_w0": rwkv_w0[0][None], "rwkv_a0": rwkv_a0[0][None], "rwkv_wab": wab,
        "rwkv_g_b": rwkv_g_b[0].astype(BF16), "rwkv_k_k": rwkv_k_k[0][None],
        "rwkv_k_a": rwkv_k_a[0][None], "rwkv_r_k": rwkv_r_k[0].reshape(1, RWKV_WIDTH),
        "rwkv_gn_w": rwkv_gn_w[0][None], "rwkv_gn_b": rwkv_gn_b[0][None],
        "w_o": w_o[0].astype(BF16), "ln2_g": ln2_g[0][None],
        "ffn_w_up": ffn_w_up[0].astype(BF16), "ffn_conv_w": ffn_conv_w[0],
        "ffn_w_down": ffn_w_down[0].astype(BF16), "final_g": final_g[None],
    }

    bp = x_prompt.shape[0]
    zero_states = (
        jnp.zeros((bp,) + state_gdn.shape[2:], F32),
        jnp.zeros((bp,) + state_gdn_conv.shape[2:], F32),
        jnp.zeros((bp,) + state_rwkv.shape[2:], F32),
        jnp.zeros((bp,) + state_rwkv_shift.shape[2:], F32),
        None,
    )
    outs_p = _trunk(x_prompt, *zero_states, prm, long_seq=True)
    outs_s = _trunk(x_sample, state_gdn[0], state_gdn_conv[0], state_rwkv[0], state_rwkv_shift[0],
                    state_ffn_conv[0], prm, long_seq=False)
    return (outs_p[0], outs_s[0]) + tuple(outs_p[1:]) + tuple(outs_s[1:])
```

```python
import functools

import jax
import jax.numpy as jnp
from jax import lax
from jax.experimental import pallas as pl
from jax.experimental.pallas import tpu as pltpu

F32 = jnp.float32
BF16 = jnp.bfloat16

D_MODEL = 2048
GDN_WIDTH = 1024
GDN_HEADS = 8
GDN_DK = 128
GDN_CONV = 4
RWKV_WIDTH = 1024
RWKV_HEAD = 64
RWKV_HEADS = 16
RWKV_LORA_W = 64
RWKV_LORA_A = 64
RWKV_LORA_G = 128
RWKV_LORA = RWKV_LORA_W + RWKV_LORA_A + RWKV_LORA_G
RWKV_PROJ = 3 * RWKV_WIDTH + RWKV_LORA
D_FF = 5632
FFN_CONV = 3
RMS_EPS = 1e-6
L2_EPS = 1e-12
GN_EPS = 64e-5

REF_OFF_Z = 3 * GDN_WIDTH
REF_OFF_B = 4 * GDN_WIDTH
REF_OFF_RWKV = REF_OFF_B + 2 * GDN_HEADS
REF_IN_WIDTH = REF_OFF_RWKV + RWKV_PROJ

LANES = 128
SUBLANES = 8
COL_QKV = 0
COL_RKV = 3 * GDN_WIDTH
COL_Z = COL_RKV + 3 * RWKV_WIDTH
COL_LORA = COL_Z + GDN_WIDTH
COL_BA = COL_LORA + RWKV_LORA
PROJ_WIDTH = 7680
PROJ_TN = 1280

MIX_ROWS = 64
MIX_GROUPS_LONG = 4

NN = (((1,), (0,)), ((), ()))
NT = (((1,), (1,)), ((), ()))
TN = (((0,), (0,)), ((), ()))

VMEM_LIMIT = 56 * 1024 * 1024

P_GRAM = "x1"
P_INV = "x1"
P_SOLVE = "x1"
P_STATE = "x1"
P_OUT = "x1"


def _split(x):
    hi = x.astype(BF16)
    return hi, (x - hi.astype(F32)).astype(BF16)


def _dot(a, b, dims=NN, mode="x1"):
    if mode == "hi":
        return lax.dot_general(a, b, dims, precision=lax.Precision.HIGHEST,
                               preferred_element_type=F32)
    if mode == "x3":
        a_hi, a_lo = _split(a)
        b_hi, b_lo = _split(b)
        d = lambda u, v: lax.dot_general(u, v, dims, preferred_element_type=F32)
        return d(a_hi, b_hi) + (d(a_hi, b_lo) + d(a_lo, b_hi))
    return lax.dot_general(a.astype(BF16), b.astype(BF16), dims, preferred_element_type=F32)


def _ones_dot(ones_mat, x, dims=NN):
    x1 = x.astype(BF16)
    r1 = x - x1.astype(F32)
    x2 = r1.astype(BF16)
    x3 = (r1 - x2.astype(F32)).astype(BF16)
    m = ones_mat.astype(BF16)
    if dims == NN:
        d = lambda v: lax.dot_general(m, v, dims, preferred_element_type=F32)
    else:
        d = lambda v: lax.dot_general(v, m, dims, preferred_element_type=F32)
    return d(x1) + (d(x2) + d(x3))


def _sigmoid(x):
    return 1.0 / (1.0 + jnp.exp(-x))


def _silu(x):
    return x * _sigmoid(x)


def _softplus(x):
    return jnp.maximum(x, 0.0) + jnp.log(1.0 + jnp.exp(-jnp.abs(x)))


def _seq_masks(rows, seq_len):
    r = lax.broadcasted_iota(jnp.int32, (rows, rows), 0)
    c = lax.broadcasted_iota(jnp.int32, (rows, rows), 1)
    if seq_len >= rows:
        return None, r >= c, r > c
    shift = seq_len.bit_length() - 1
    assert 1 << shift == seq_len
    same = jnp.right_shift(r, shift) == jnp.right_shift(c, shift)
    return same, same & (r >= c), same & (r > c)


def _wide_masks(rows, seq_len):
    r = lax.broadcasted_iota(jnp.int32, (rows, 2 * rows), 0)
    c = lax.broadcasted_iota(jnp.int32, (rows, 2 * rows), 1)
    right = c >= rows
    cc = jnp.where(right, c - rows, c)
    if seq_len >= rows:
        return r >= cc, right & (r > cc)
    shift = seq_len.bit_length() - 1
    same = jnp.right_shift(r, shift) == jnp.right_shift(cc, shift)
    return same & (r >= cc), same & right & (r > cc)


def _pair_masks(rows, seq_len):
    half = LANES // 2
    assert rows == half
    lane = lax.broadcasted_iota(jnp.int32, (rows, LANES), 1)
    row = lax.broadcasted_iota(jnp.int32, (rows, LANES), 0)
    col = jnp.bitwise_and(lane, half - 1)
    if seq_len >= rows:
        causal, strict = row >= col, row > col
    else:
        shift = seq_len.bit_length() - 1
        same = jnp.right_shift(row, shift) == jnp.right_shift(col, shift)
        causal, strict = same & (row >= col), same & (row > col)
    return causal, strict, jnp.where(row == col, 1.0, 0.0).astype(F32), lane < half


def _bd(x, first_half):
    return jnp.concatenate([jnp.where(first_half, x, 0.0), jnp.where(first_half, 0.0, x)], axis=0)


def _pair_inverses(neg_a, rows, nilpotency, eye2, first_half):
    qs = dict(neg_a)
    ts = {u: eye2 + q for u, q in qs.items()}
    n = 2
    if n < nilpotency:
        qs = {u: _dot(q, _bd(q, first_half), mode=P_INV) for u, q in qs.items()}
    while n < nilpotency:
        if 2 * n < nilpotency:
            tq = {u: _dot(jnp.concatenate([ts[u], qs[u]], axis=0), _bd(qs[u], first_half), mode=P_INV)
                  for u in qs}
            ts = {u: ts[u] + tq[u][:rows] for u in qs}
            qs = {u: tq[u][rows:] for u in qs}
        else:
            ts = {u: ts[u] + _dot(ts[u], _bd(qs[u], first_half), mode=P_INV) for u in qs}
        n *= 2
    return ts


def _lane_group_sums(x, group):
    tile = 2 * LANES
    shift = group.bit_length() - 1
    li = jnp.right_shift(lax.broadcasted_iota(jnp.int32, (tile, tile), 0), shift)
    lj = jnp.right_shift(lax.broadcasted_iota(jnp.int32, (tile, tile), 1), shift)
    ones = jnp.where(li == lj, 1.0, 0.0).astype(BF16)
    hi, lo = _split(x)
    d = lambda u: lax.dot_general(u, ones, NN, preferred_element_type=F32)
    return jnp.concatenate([d(hi[:, t:t + tile]) + d(lo[:, t:t + tile])
                            for t in range(0, x.shape[1], tile)], axis=1)


def _unit_lower_inverses(mats, rows, nilpotency):
    r = lax.broadcasted_iota(jnp.int32, (rows, rows), 0)
    c = lax.broadcasted_iota(jnp.int32, (rows, rows), 1)
    eye = jnp.where(r == c, 1.0, 0.0).astype(F32)
    qs = [-a for a in mats]
    ts = [eye + q for q in qs]
    n = 2
    if n < nilpotency:
        qs = [_dot(q, q, mode=P_INV) for q in qs]
    while n < nilpotency:
        if 2 * n < nilpotency:
            tq = [_dot(jnp.concatenate([t, q], axis=0), q, mode=P_INV) for t, q in zip(ts, qs)]
            ts = [t + p[:rows] for t, p in zip(ts, tq)]
            qs = [p[rows:] for p in tq]
        else:
            ts = [t + _dot(t, q, mode=P_INV) for t, q in zip(ts, qs)]
        n *= 2
    return ts


def _permute_win_kernel(w_ref, o_ref):
    rw = REF_OFF_RWKV
    rows = w_ref.shape[0]
    cast = lambda lo, hi: w_ref[:, lo:hi].astype(BF16)
    o_ref[:, COL_QKV:COL_RKV] = cast(0, REF_OFF_Z)
    o_ref[:, COL_RKV:COL_Z] = cast(rw, rw + 3 * RWKV_WIDTH)
    o_ref[:, COL_Z:COL_LORA] = cast(REF_OFF_Z, REF_OFF_B)
    o_ref[:, COL_LORA:COL_BA] = cast(rw + 3 * RWKV_WIDTH, REF_IN_WIDTH)
    tail = jnp.concatenate([w_ref[:, REF_OFF_B:REF_OFF_RWKV],
                            jnp.zeros((rows, PROJ_WIDTH - COL_BA - 2 * GDN_HEADS), F32)], axis=1)
    o_ref[:, COL_BA:] = tail.astype(BF16)


def _permute_win(w, tr=256):
    return pl.pallas_call(
        _permute_win_kernel,
        grid=(D_MODEL // tr,),
        in_specs=[pl.BlockSpec((tr, REF_IN_WIDTH), lambda i: (i, 0))],
        out_specs=pl.BlockSpec((tr, PROJ_WIDTH), lambda i: (i, 0)),
        out_shape=jax.ShapeDtypeStruct((D_MODEL, PROJ_WIDTH), BF16),
        compiler_params=pltpu.CompilerParams(
            dimension_semantics=("arbitrary",), vmem_limit_bytes=VMEM_LIMIT),
        name="permute_win",
    )(w)


def _inproj_kernel(x_ref, g_ref, w_ref, o_ref, xn_ref):
    @pl.when(pl.program_id(1) == 0)
    def _():
        x = x_ref[...]
        ms = jnp.mean(x * x, axis=-1, keepdims=True)
        xn_ref[...] = (x * lax.rsqrt(ms + RMS_EPS) * g_ref[...]).astype(BF16)

    o_ref[...] = jnp.dot(xn_ref[...], w_ref[...], preferred_element_type=F32)


def _inproj(x2d, g_row, w_bf16, tm):
    m = x2d.shape[0]
    return pl.pallas_call(
        _inproj_kernel,
        grid=(m // tm, PROJ_WIDTH // PROJ_TN),
        in_specs=[
            pl.BlockSpec((tm, D_MODEL), lambda i, j: (i, 0)),
            pl.BlockSpec((1, D_MODEL), lambda i, j: (0, 0)),
            pl.BlockSpec((D_MODEL, PROJ_TN), lambda i, j: (0, j)),
        ],
        out_specs=pl.BlockSpec((tm, PROJ_TN), lambda i, j: (i, j)),
        out_shape=jax.ShapeDtypeStruct((m, PROJ_WIDTH), F32),
        scratch_shapes=[pltpu.VMEM((tm, D_MODEL), BF16)],
        compiler_params=pltpu.CompilerParams(
            dimension_semantics=("arbitrary", "arbitrary"), vmem_limit_bytes=VMEM_LIMIT),
        name="inproj",
    )(x2d, g_row, w_bf16)


PROJ_TN_CONV = 768
QKV_TILES = 3 * GDN_WIDTH // PROJ_TN_CONV


def _inproj_conv_kernel(x_ref, g_ref, w_ref, cw_ref, cbuf_ref, o_ref, tail_ref, xn_ref, carry_ref,
                        hbuf_ref, *, tm, tiles_per_seq):
    i = pl.program_id(0)
    j = pl.program_id(1)
    hist = GDN_CONV - 1

    @pl.when(j == 0)
    def _():
        x = x_ref[...]
        ms = jnp.mean(x * x, axis=-1, keepdims=True)
        xn_ref[...] = (x * lax.rsqrt(ms + RMS_EPS) * g_ref[...]).astype(BF16)

    @pl.when(j >= QKV_TILES)
    def _():
        o_ref[...] = jnp.dot(xn_ref[...], w_ref[...], preferred_element_type=F32)

    @pl.when(j < QKV_TILES)
    def _():
        jc = jnp.minimum(j, QKV_TILES - 1)
        first = i % tiles_per_seq == 0

        @pl.when(first)
        def _():
            hbuf_ref[0:SUBLANES, :] = jnp.zeros((SUBLANES, PROJ_TN_CONV), F32)
            hbuf_ref[SUBLANES - hist:SUBLANES, :] = cbuf_ref[...]

        @pl.when(jnp.logical_not(first))
        def _():
            hbuf_ref[0:SUBLANES, :] = carry_ref[jc]

        acc = jnp.dot(xn_ref[...], w_ref[...], preferred_element_type=F32)
        hbuf_ref[SUBLANES:SUBLANES + tm, :] = acc
        cw = cw_ref[...]
        conv = acc * cw[hist:hist + 1, :]
        for t in range(hist):
            off = SUBLANES - hist + t
            conv = conv + hbuf_ref[off:off + tm, :] * cw[t:t + 1, :]
        carry_ref[jc] = acc[tm - SUBLANES:, :]
        tail_ref[...] = acc[tm - SUBLANES:, :]
        o_ref[...] = _silu(conv)


def _inproj_conv(x2d, g_row, w_bf16, convw, cbuf, tm, seq_len):
    m = x2d.shape[0]
    tn = PROJ_TN_CONV
    tps = seq_len // tm
    qkv_tile = lambda j: jnp.minimum(j, QKV_TILES - 1)
    return pl.pallas_call(
        functools.partial(_inproj_conv_kernel, tm=tm, tiles_per_seq=tps),
        grid=(m // tm, PROJ_WIDTH // tn),
        in_specs=[
            pl.BlockSpec((tm, D_MODEL), lambda i, j: (i, 0)),
            pl.BlockSpec((1, D_MODEL), lambda i, j: (0, 0)),
            pl.BlockSpec((D_MODEL, tn), lambda i, j: (0, j)),
            pl.BlockSpec((GDN_CONV, tn), lambda i, j: (0, qkv_tile(j))),
            pl.BlockSpec((None, GDN_CONV - 1, tn), lambda i, j: (i // tps, 0, qkv_tile(j))),
        ],
        out_specs=[
            pl.BlockSpec((tm, tn), lambda i, j: (i, j)),
            pl.BlockSpec((None, SUBLANES, tn), lambda i, j: (i, 0, qkv_tile(j))),
        ],
        out_shape=[
            jax.ShapeDtypeStruct((m, PROJ_WIDTH), F32),
            jax.ShapeDtypeStruct((m // tm, SUBLANES, 3 * GDN_WIDTH), F32),
        ],
        scratch_shapes=[
            pltpu.VMEM((tm, D_MODEL), BF16),
            pltpu.VMEM((QKV_TILES, SUBLANES, tn), F32),
            pltpu.VMEM((SUBLANES + tm, tn), F32),
        ],
        compiler_params=pltpu.CompilerParams(
            dimension_semantics=("arbitrary", "arbitrary"), vmem_limit_bytes=VMEM_LIMIT),
        name="inproj_conv",
    )(x2d, g_row, w_bf16, convw, cbuf)


def _mixer_geometry(seq_len, n_seq):
    rows = MIX_ROWS
    length = min(seq_len, rows)
    assert rows % length == 0 and seq_len % length == 0 and length % SUBLANES == 0
    per_group = rows // length
    groups = min(MIX_GROUPS_LONG, n_seq) if per_group == 1 else 1
    assert n_seq % (groups * per_group) == 0
    return rows, length, per_group, groups, seq_len // length


def _gdn_kernel(qkv_ref, z_ref, ba_ref, cbuf_ref, s0_ref, convw_ref, alog_r_ref, dtb_r_ref,
                alog_c_ref, dtb_c_ref, ng_ref, o_ref, s_ref, xp_ref, *, seq_len, n_seq):
    R, L, G, S, _ = _mixer_geometry(seq_len, n_seq)
    RT = S * R
    c = pl.program_id(1)
    width = 3 * GDN_WIDTH
    hist = GDN_CONV - 1
    cw = convw_ref[...]
    groups = range(S)
    seqs = range(G)

    @pl.when(c == 0)
    def _():
        s_ref[...] = s0_ref[...]

    pieces = []
    if G == 1:
        qkv = qkv_ref[...].reshape(RT, width)
    else:
        for s in groups:
            for g in seqs:
                q = s * G + g
                rows = slice(g * L, (g + 1) * L)
                xp_ref[q, SUBLANES - hist:SUBLANES, :] = cbuf_ref[q]
                xp_ref[q, SUBLANES:SUBLANES + L, :] = qkv_ref[s, rows, :]
                piece = qkv_ref[s, rows, :] * cw[hist:hist + 1, :]
                for i in range(hist):
                    off = SUBLANES - hist + i
                    piece = piece + xp_ref[q, off:off + L, :] * cw[i:i + 1, :]
                pieces.append(piece)
        qkv = _silu(pieces[0] if len(pieces) == 1 else jnp.concatenate(pieces, axis=0))

    same_t, causal_t, _ = _seq_masks(RT, L)
    causal01 = jnp.where(causal_t, 1.0, 0.0).astype(F32)

    ba = ba_ref[...].reshape(RT, LANES)
    ba_t = ba.T
    beta_c = _sigmoid(ba)
    g_c = -jnp.exp(alog_r_ref[...]) * _softplus(ba + dtb_r_ref[...])
    g_r = -jnp.exp(alog_c_ref[...]) * _softplus(ba_t + dtb_c_ref[...])
    gc_all = _ones_dot(causal01, g_c)
    gr_all = _ones_dot(causal01, g_r, NT)
    if same_t is None:
        gtot_all = jnp.broadcast_to(gc_all[RT - 1:RT, :], (RT, LANES))
    else:
        gtot_all = _ones_dot(jnp.where(same_t, 1.0, 0.0).astype(F32), g_c)

    qk_raw = qkv[:, :2 * GDN_WIDTH]
    qk_n = qk_raw * lax.rsqrt(_lane_group_sums(qk_raw * qk_raw, GDN_DK) + L2_EPS)
    q_all = qk_n[:, :GDN_WIDTH] * (GDN_DK ** -0.5)
    k_all = qk_n[:, GDN_WIDTH:]
    v_all = qkv[:, 2 * GDN_WIDTH:]

    causal2, strict2, eye2, first_half = _pair_masks(R, L)
    first_head = lax.broadcasted_iota(jnp.int32, (R, 2 * GDN_DK), 1) < GDN_DK
    zeros_u = jnp.zeros((R, GDN_DK), F32)
    zeros_rhs = jnp.zeros((R, 2 * GDN_DK), F32)
    chains = [(s, h) for s in groups for h in range(GDN_HEADS)]
    units = [(s, p) for s in groups for p in range(GDN_HEADS // 2)]
    seq_rows = [slice(g * L, (g + 1) * L) for g in seqs]
    qs, ks, vs, betas, gcols, gtots = {}, {}, {}, {}, {}, {}
    for s, h in chains:
        rs = slice(s * R, (s + 1) * R)
        lo = h * GDN_DK
        key = (s, h)
        qs[key] = q_all[rs, lo:lo + GDN_DK]
        ks[key] = k_all[rs, lo:lo + GDN_DK]
        vs[key] = v_all[rs, lo:lo + GDN_DK]
        betas[key] = beta_c[rs, h:h + 1]
        gcols[key] = gc_all[rs, GDN_HEADS + h:GDN_HEADS + h + 1]
        gtots[key] = gtot_all[rs, GDN_HEADS + h:GDN_HEADS + h + 1]
    kq2, decay2, a2 = {}, {}, {}
    for s, p in units:
        rs = slice(s * R, (s + 1) * R)
        cols = slice(2 * p * GDN_DK, (2 * p + 2) * GDN_DK)
        k_pair = k_all[rs, cols]
        kt_bd = jnp.concatenate([jnp.where(first_head, k_pair, 0.0),
                                 jnp.where(first_head, 0.0, k_pair)], axis=0).T
        kq = _dot(jnp.concatenate([k_pair, q_all[rs, cols]], axis=0), kt_bd, mode=P_GRAM)
        h0, h1 = (s, 2 * p), (s, 2 * p + 1)
        gcol2 = jnp.where(first_half, gcols[h0], gcols[h1])
        grow2 = jnp.concatenate([gr_all[GDN_HEADS + 2 * p:GDN_HEADS + 2 * p + 1, rs],
                                 gr_all[GDN_HEADS + 2 * p + 1:GDN_HEADS + 2 * p + 2, rs]], axis=1)
        dec = jnp.where(causal2, jnp.exp(jnp.where(causal2, gcol2 - grow2, 0.0)), 0.0)
        beta2 = jnp.where(first_half, betas[h0], betas[h1])
        kq2[(s, p)] = kq
        decay2[(s, p)] = dec
        a2[(s, p)] = jnp.where(strict2, -(beta2 * kq[:R] * dec), 0.0)
    t2 = _pair_inverses(a2, R, L, eye2, first_half)
    gammas = {key: jnp.exp(gcols[key]) for key in chains}

    def stacked(key, x, zeros):
        return jnp.concatenate([x, zeros] if key[1] % 2 == 0 else [zeros, x], axis=0)

    sols = {key: _dot(t2[(key[0], key[1] // 2)],
                      stacked(key, jnp.concatenate([(betas[key] * gammas[key]) * ks[key],
                                                    betas[key] * vs[key]], axis=1), zeros_rhs),
                      mode=P_SOLVE) for key in chains}
    states = {(s, h): [s_ref[s * G + g, h] for g in seqs] for s, h in chains}
    wss = {key: [_dot(jnp.concatenate([sols[key][rows, :GDN_DK], (qs[key] * gammas[key])[rows]], axis=0),
                      states[key][g], mode=P_STATE) for g, rows in enumerate(seq_rows)]
           for key in chains}
    us = {key: jnp.concatenate([sols[key][rows, GDN_DK:] - wss[key][g][:L]
                                for g, rows in enumerate(seq_rows)], axis=0) for key in chains}
    qk2 = {u: kq2[u][R:] * decay2[u] for u in units}
    outs = {key: jnp.concatenate([wss[key][g][L:] for g in seqs], axis=0)
            + _dot(qk2[(key[0], key[1] // 2)], stacked(key, us[key], zeros_u), mode=P_OUT)
            for key in chains}
    for s, h in chains:
        key = (s, h)
        kt = ks[key] * jnp.exp(gtots[key] - gcols[key])
        for g, rows in enumerate(seq_rows):
            gl = jnp.exp(gtots[key][g * L:g * L + 1, :])
            s_ref[s * G + g, h] = gl * states[key][g] + _dot(kt[rows], us[key][rows], TN, mode=P_STATE)
    ng = jnp.concatenate([ng_ref[...]] * GDN_HEADS, axis=1)
    for s in groups:
        o = jnp.concatenate([outs[(s, h)] for h in range(GDN_HEADS)], axis=1)
        ms = _lane_group_sums(o * o, GDN_DK) * (1.0 / GDN_DK)
        o = o * lax.rsqrt(ms + RMS_EPS) * ng
        o_ref[s] = (o * _silu(z_ref[s])).astype(o_ref.dtype)


def _gdn_mixer(proj2d, cbuf, s0, convw, alog_r, dtb_r, alog_c, dtb_c, ng, seq_len):
    m = proj2d.shape[0]
    nseq = m // seq_len
    R, L, G, S, sps = _mixer_geometry(seq_len, nseq)
    width = 3 * GDN_WIDTH
    proj3d = proj2d.reshape(nseq // G, sps * R, PROJ_WIDTH)
    const2 = lambda i, c: (0, 0)
    rows_map = lambda col: (lambda i, c: (i, c, col))
    per_seq = lambda *dims: pl.BlockSpec((S * G,) + dims, lambda i, c: (i,) + (0,) * len(dims))
    xp_shape = (SUBLANES, LANES) if G == 1 else (S * G, SUBLANES + L, width)
    o, s_new = pl.pallas_call(
        functools.partial(_gdn_kernel, seq_len=seq_len, n_seq=nseq),
        grid=(nseq // (S * G), sps),
        in_specs=[
            pl.BlockSpec((S, R, width), rows_map(COL_QKV // width)),
            pl.BlockSpec((S, R, GDN_WIDTH), rows_map(COL_Z // GDN_WIDTH)),
            pl.BlockSpec((S, R, LANES), rows_map(COL_BA // LANES)),
            per_seq(GDN_CONV - 1, width),
            per_seq(GDN_HEADS, GDN_DK, GDN_DK),
            pl.BlockSpec((GDN_CONV, width), const2),
            pl.BlockSpec((1, LANES), const2),
            pl.BlockSpec((1, LANES), const2),
            pl.BlockSpec((LANES, 1), const2),
            pl.BlockSpec((LANES, 1), const2),
            pl.BlockSpec((1, GDN_DK), const2),
        ],
        out_specs=[
            pl.BlockSpec((S, R, GDN_WIDTH), rows_map(0)),
            per_seq(GDN_HEADS, GDN_DK, GDN_DK),
        ],
        out_shape=[
            jax.ShapeDtypeStruct((nseq // G, sps * R, GDN_WIDTH), BF16),
            jax.ShapeDtypeStruct((nseq, GDN_HEADS, GDN_DK, GDN_DK), F32),
        ],
        scratch_shapes=[pltpu.VMEM(xp_shape, F32)],
        compiler_params=pltpu.CompilerParams(
            dimension_semantics=("arbitrary", "arbitrary"), vmem_limit_bytes=VMEM_LIMIT),
        name="gdn_mixer",
    )(proj3d, proj3d, proj3d, cbuf, s0, convw, alog_r, dtb_r, alog_c, dtb_c, ng)
    return o.reshape(m, GDN_WIDTH), s_new


def _shifted_rows(x_ref, prev_ref, carry_ref, c, R, L, G, S):
    width = x_ref.shape[-1]
    groups = range(S)
    if G == 1:
        @pl.when(c == 0)
        def _():
            for s in groups:
                carry_ref[s, 0:SUBLANES, :] = jnp.zeros((SUBLANES, width), F32)
                carry_ref[s, SUBLANES - 1:SUBLANES, :] = prev_ref[s]

        @pl.when(c > 0)
        def _():
            for s in groups:
                carry_ref[s, 0:SUBLANES, :] = carry_ref[s, R:R + SUBLANES, :]

        xs, prevs = [], []
        for s in groups:
            carry_ref[s, SUBLANES:SUBLANES + R, :] = x_ref[s]
            xs.append(x_ref[s])
            prevs.append(carry_ref[s, SUBLANES - 1:SUBLANES - 1 + R, :])
    else:
        row = lax.broadcasted_iota(jnp.int32, (L, width), 0)
        xs, prevs = [], []
        for s in groups:
            x = x_ref[s]
            xs.append(x)
            for g in range(G):
                xg = x[g * L:(g + 1) * L]
                prevs.append(jnp.where(row == 0, prev_ref[s * G + g], pltpu.roll(xg, 1, 0)))
    cat = lambda parts: parts[0] if len(parts) == 1 else jnp.concatenate(parts, axis=0)
    return cat(xs), cat(prevs)


def _rwkv_kernel(rkv_ref, lora_ref, sh_rkv_ref, sh_lora_ref, s0_ref, mu_rkv_ref, mu_lora_ref,
                 w0_ref, a0_ref, wab_ref, gb_ref, kk_ref, ka_ref, rk_ref, gnw_ref, gnb_ref,
                 o_ref, s_ref, xr_ref, xl_ref, m_ref, *, seq_len, n_seq):
    R, L, G, S, sps = _mixer_geometry(seq_len, n_seq)
    RT = S * R
    NQ = S * G
    c = pl.program_id(1)
    W = RWKV_WIDTH
    HD = RWKV_HEAD
    pairs = range(RWKV_HEADS // 2)
    groups = range(S)
    seqs = range(G)

    @pl.when(c == 0)
    def _():
        zero = jnp.zeros((HD, HD), F32)
        for q in range(NQ):
            for j in pairs:
                vk = jnp.concatenate([jnp.concatenate([s0_ref[q, 2 * j], zero], axis=1),
                                      jnp.concatenate([zero, s0_ref[q, 2 * j + 1]], axis=1)], axis=0)
                m_ref[q, j] = vk.T

    p, p_prev = _shifted_rows(rkv_ref, sh_rkv_ref, xr_ref, c, R, L, G, S)
    xs = p + (p_prev - p) * mu_rkv_ref[...]
    pl_, pl_prev = _shifted_rows(lora_ref, sh_lora_ref, xl_ref, c, R, L, G, S)
    xl = pl_ + (pl_prev - pl_) * mu_lora_ref[...]
    r = xs[:, :W]
    k = xs[:, W:2 * W]
    v = xs[:, 2 * W:]

    wa_in = xl[:, :LANES]
    lane = lax.broadcasted_iota(jnp.int32, (RT, LANES), 1)
    wa_in = jnp.where(lane < RWKV_LORA_W, jnp.tanh(wa_in), wa_in)
    wa = _dot(wa_in, wab_ref[...])
    w = -_softplus(-(w0_ref[...] + wa[:, :W])) - 0.5
    lw = -jnp.exp(w)
    a = _sigmoid(a0_ref[...] + wa[:, W:])
    gate = _dot(_sigmoid(xl[:, LANES:]), gb_ref[...])
    kk_raw = k * kk_ref[...]
    k2 = k * (1.0 + (a - 1.0) * ka_ref[...])

    same_t, causal_t, _ = _seq_masks(RT, L)
    lc = _ones_dot(jnp.where(causal_t, 1.0, 0.0).astype(F32), lw)
    if G == 1:
        ltot = jnp.concatenate([jnp.broadcast_to(lc[(s + 1) * R - 1:(s + 1) * R, :], (R, W))
                                for s in groups], axis=0) if S > 1 else \
            jnp.broadcast_to(lc[R - 1:R, :], (R, W))
    else:
        ltot = _ones_dot(jnp.where(same_t, 1.0, 0.0).astype(F32), lw)
    e_inv = jnp.exp(-lc)
    e_rem = jnp.exp(ltot - lc)

    kk = kk_raw * lax.rsqrt(_lane_group_sums(kk_raw * kk_raw, HD) + L2_EPS)
    kka = kk * a
    ct = kk * jnp.exp(lc - lw)
    rt = r * jnp.exp(lc)
    bh = kka * e_inv
    kh = k2 * e_inv
    bb = kka * e_rem
    kb = k2 * e_rem
    p_rows = [ltot[q * L:q * L + 1] for q in range(NQ)]
    p_rows = p_rows + [p_rows[0]] * (-NQ % SUBLANES)
    pt = jnp.exp(jnp.concatenate(p_rows, axis=0)).T

    causal2, strict2, eye2, even_half = _pair_masks(R, L)
    block_diag = ((lax.broadcasted_iota(jnp.int32, (LANES, LANES), 0) < HD)
                  == (lax.broadcasted_iota(jnp.int32, (LANES, LANES), 1) < HD))
    even_all = jnp.bitwise_and(lax.broadcasted_iota(jnp.int32, (R, W), 1), LANES - 1) < HD
    tile = lambda j: slice(j * LANES, (j + 1) * LANES)
    bd = lambda x: _bd(x, even_half)

    if G > 1:
        row2 = lax.broadcasted_iota(jnp.int32, (2 * R, LANES), 0)
        seq_of_row = jnp.right_shift(jnp.bitwise_and(row2, R - 1), L.bit_length() - 1)

    units = [(s, j) for s in groups for j in pairs]
    xps, vps, bdb, bdk, bkts = {}, {}, {}, {}, {}
    for s in groups:
        rs = slice(s * R, (s + 1) * R)
        x_s = jnp.concatenate([ct[rs], rt[rs]], axis=0)
        bt_s = jnp.concatenate([jnp.where(even_all, bh[rs], 0.0), jnp.where(even_all, 0.0, bh[rs])],
                               axis=0).T
        kt_s = jnp.concatenate([jnp.where(even_all, kh[rs], 0.0), jnp.where(even_all, 0.0, kh[rs])],
                               axis=0).T
        bkt_s = jnp.concatenate([bb[rs], kb[rs]], axis=0).T
        for j in pairs:
            xps[(s, j)] = x_s[:, tile(j)]
            vps[(s, j)] = v[rs, tile(j)]
            bdb[(s, j)] = bt_s[tile(j), :]
            bdk[(s, j)] = kt_s[tile(j), :]
            bkts[(s, j)] = bkt_s[tile(j), :]
    gb = {u: _dot(xps[u], bdb[u], mode=P_GRAM) for u in units}
    gk = {u: _dot(xps[u], bdk[u], mode=P_GRAM) for u in units}
    ms = {(s, j): [m_ref[s * G + g, j] for g in seqs] for s, j in units}
    xm_c, xm_r = {}, {}
    for u in units:
        if G == 1:
            xm = _dot(xps[u], ms[u][0], mode=P_STATE)
            xm_c[u], xm_r[u] = xm[:R], xm[R:]
        else:
            parts = [_dot(jnp.concatenate([xps[u][g * L:(g + 1) * L],
                                           xps[u][R + g * L:R + (g + 1) * L]], axis=0),
                          ms[u][g], mode=P_STATE) for g in seqs]
            xm_c[u] = jnp.concatenate([p_[:L] for p_ in parts], axis=0)
            xm_r[u] = jnp.concatenate([p_[L:] for p_ in parts], axis=0)
    ts = _pair_inverses({u: jnp.where(strict2, -gb[u][:R], 0.0) for u in units}, R, L, eye2, even_half)
    bdv = {u: bd(vps[u]) for u in units}
    akvs = {u: _dot(jnp.where(strict2, gk[u][:R], 0.0), bdv[u], mode=P_OUT) for u in units}
    us = {u: _dot(ts[u], bd(-(xm_c[u] + akvs[u])), mode=P_SOLVE) for u in units}
    uvs = {u: jnp.concatenate([us[u], vps[u]], axis=0) for u in units}
    ys = {u: xm_r[u] + _dot(
        jnp.concatenate([jnp.where(causal2, gb[u][R:], 0.0), jnp.where(causal2, gk[u][R:], 0.0)], axis=1),
        jnp.concatenate([bd(us[u]), bdv[u]], axis=0), mode=P_OUT) for u in units}
    for s, j in units:
        u = (s, j)
        for g in seqs:
            q = s * G + g
            uv_g = uvs[u] if G == 1 else jnp.where(seq_of_row == g, uvs[u], 0.0)
            upd = _dot(bkts[u], uv_g, mode=P_STATE)
            m_ref[q, j] = pt[tile(j), q:q + 1] * ms[u][g] + jnp.where(block_diag, upd, 0.0)

    y_rows = [jnp.concatenate([ys[(s, j)] for j in pairs], axis=1) for s in groups]
    y_all = y_rows[0] if S == 1 else jnp.concatenate(y_rows, axis=0)
    mean = _lane_group_sums(y_all, HD) * (1.0 / HD)
    yc = y_all - mean
    var = _lane_group_sums(yc * yc, HD) * (1.0 / HD)
    yn = yc * lax.rsqrt(var + GN_EPS) * gnw_ref[...] + gnb_ref[...]
    yn = yn + _lane_group_sums(r * k2 * rk_ref[...], HD) * v
    o_ref[...] = (yn * gate).reshape(S, R, W).astype(o_ref.dtype)

    @pl.when(c == sps - 1)
    def _():
        for q in range(NQ):
            for j in pairs:
                vk = m_ref[q, j].T
                s_ref[q, 2 * j] = vk[:HD, :HD]
                s_ref[q, 2 * j + 1] = vk[HD:, HD:]


def _rwkv_mixer(proj2d, sh_rkv, sh_lora, s0, mu_rkv, mu_lora, w0, a0, wab, gb, kk, ka, rk, gnw, gnb,
                seq_len):
    m = proj2d.shape[0]
    nseq = m // seq_len
    R, L, G, S, sps = _mixer_geometry(seq_len, nseq)
    W = RWKV_WIDTH
    proj3d = proj2d.reshape(nseq // G, sps * R, PROJ_WIDTH)
    const2 = lambda i, c: (0, 0)
    row = lambda width: pl.BlockSpec((1, width), const2)
    rows_map = lambda col: (lambda i, c: (i, c, col))
    per_seq = lambda *dims: pl.BlockSpec((S * G,) + dims, lambda i, c: (i,) + (0,) * len(dims))
    carry = lambda width: pltpu.VMEM((S, SUBLANES + R, width) if G == 1 else (SUBLANES, LANES), F32)
    o, s_new = pl.pallas_call(
        functools.partial(_rwkv_kernel, seq_len=seq_len, n_seq=nseq),
        grid=(nseq // (S * G), sps),
        in_specs=[
            pl.BlockSpec((S, R, 3 * W), rows_map(COL_RKV // (3 * W))),
            pl.BlockSpec((S, R, RWKV_LORA), rows_map(COL_LORA // RWKV_LORA)),
            per_seq(1, 3 * W),
            per_seq(1, RWKV_LORA),
            per_seq(RWKV_HEADS, RWKV_HEAD, RWKV_HEAD),
            row(3 * W), row(RWKV_LORA), row(W), row(W),
            pl.BlockSpec((LANES, 2 * W), const2),
            pl.BlockSpec((RWKV_LORA_G, W), const2),
            row(W), row(W), row(W), row(W), row(W),
        ],
        out_specs=[pl.BlockSpec((S, R, W), rows_map(0)), per_seq(RWKV_HEADS, RWKV_HEAD, RWKV_HEAD)],
        out_shape=[
            jax.ShapeDtypeStruct((nseq // G, sps * R, W), BF16),
            jax.ShapeDtypeStruct((nseq, RWKV_HEADS, RWKV_HEAD, RWKV_HEAD), F32),
        ],
        scratch_shapes=[carry(3 * W), carry(RWKV_LORA),
                        pltpu.VMEM((S * G, RWKV_HEADS // 2, LANES, LANES), F32)],
        compiler_params=pltpu.CompilerParams(
            dimension_semantics=("arbitrary", "arbitrary"), vmem_limit_bytes=VMEM_LIMIT),
        name="rwkv_mixer",
    )(proj3d, proj3d, sh_rkv, sh_lora, s0, mu_rkv, mu_lora, w0, a0, wab, gb, kk, ka, rk, gnw, gnb)
    return o.reshape(m, W), s_new


def _outproj_kernel(x_ref, oa_ref, ob_ref, wa_ref, wb_ref, g_ref, x1_ref, hn_ref):
    x1 = (x_ref[...] + jnp.dot(oa_ref[...], wa_ref[...], preferred_element_type=F32)
          + jnp.dot(ob_ref[...], wb_ref[...], preferred_element_type=F32))
    x1_ref[...] = x1
    ms = jnp.mean(x1 * x1, axis=-1, keepdims=True)
    hn_ref[...] = (x1 * lax.rsqrt(ms + RMS_EPS) * g_ref[...]).astype(BF16)


def _outproj(x2d, oa, ob, wo_bf16, g_row, tm):
    m = x2d.shape[0]
    return pl.pallas_call(
        _outproj_kernel,
        grid=(m // tm,),
        in_specs=[
            pl.BlockSpec((tm, D_MODEL), lambda i: (i, 0)),
            pl.BlockSpec((tm, GDN_WIDTH), lambda i: (i, 0)),
            pl.BlockSpec((tm, RWKV_WIDTH), lambda i: (i, 0)),
            pl.BlockSpec((GDN_WIDTH, D_MODEL), lambda i: (0, 0)),
            pl.BlockSpec((RWKV_WIDTH, D_MODEL), lambda i: (1, 0)),
            pl.BlockSpec((1, D_MODEL), lambda i: (0, 0)),
        ],
        out_specs=[pl.BlockSpec((tm, D_MODEL), lambda i: (i, 0)),
                   pl.BlockSpec((tm, D_MODEL), lambda i: (i, 0))],
        out_shape=[jax.ShapeDtypeStruct((m, D_MODEL), F32),
                   jax.ShapeDtypeStruct((m, D_MODEL), BF16)],
        compiler_params=pltpu.CompilerParams(
            dimension_semantics=("arbitrary",), vmem_limit_bytes=VMEM_LIMIT),
        name="outproj",
    )(x2d, oa, ob, wo_bf16, wo_bf16, g_row)


FFN_TF = 512
FFN_NF = D_FF // FFN_TF


def _ffn_pipeline(f, up_fn, wd_ref, x1_ref, fg_ref, y_ref, acc_ref):
    @pl.when(f == 0)
    def _():
        acc_ref[...] = jnp.zeros(acc_ref.shape, F32)

    acc_ref[...] += jnp.dot(up_fn().astype(BF16), wd_ref[...], preferred_element_type=F32)

    @pl.when(f == FFN_NF - 1)
    def _():
        xo = x1_ref[...] + acc_ref[...]
        ms = jnp.mean(xo * xo, axis=-1, keepdims=True)
        y_ref[...] = xo * lax.rsqrt(ms + RMS_EPS) * fg_ref[...]


def _ffn_long_kernel(hn_ref, x1_ref, wg_ref, wu_ref, cwg_ref, cwu_ref, wd_ref, fg_ref,
                     y_ref, ng_ref, nu_ref, acc_ref, carry_ref, hbuf_ref, *, tt):
    ti = pl.program_id(1)
    f = pl.program_id(2)

    def up_fn():
        hn = hn_ref[...]
        convs = []
        for j, (w_ref, cw_ref, n_ref) in enumerate(((wg_ref, cwg_ref, ng_ref), (wu_ref, cwu_ref, nu_ref))):
            h = jnp.dot(hn, w_ref[...], preferred_element_type=F32)
            hbuf_ref[j, SUBLANES:SUBLANES + tt, :] = h
            prev = carry_ref[f, j]
            hbuf_ref[j, 0:SUBLANES, :] = jnp.where(ti == 0, jnp.zeros_like(prev), prev)
            cw = cw_ref[...]
            conv = h * cw[FFN_CONV - 1:FFN_CONV, :]
            for i in range(FFN_CONV - 1):
                off = SUBLANES - (FFN_CONV - 1) + i
                conv = conv + hbuf_ref[j, off:off + tt, :] * cw[i:i + 1, :]
            carry_ref[f, j] = h[tt - SUBLANES:, :]
            n_ref[...] = h[tt - (FFN_CONV - 1):, :]
            convs.append(conv)
        return _silu(convs[0]) * convs[1]

    _ffn_pipeline(f, up_fn, wd_ref, x1_ref, fg_ref, y_ref, acc_ref)


def _ffn_long(hn, x1, wup, cw, wdown, fg_row, tt):
    b, t, _ = hn.shape
    tf = FFN_TF
    nf = FFN_NF
    up = lambda f: f
    down = lambda f: f
    return pl.pallas_call(
        functools.partial(_ffn_long_kernel, tt=tt),
        grid=(b, t // tt, nf),
        in_specs=[
            pl.BlockSpec((None, tt, D_MODEL), lambda i, s, f: (i, s, 0)),
            pl.BlockSpec((None, tt, D_MODEL), lambda i, s, f: (i, s, 0)),
            pl.BlockSpec((D_MODEL, tf), lambda i, s, f: (0, up(f))),
            pl.BlockSpec((D_MODEL, tf), lambda i, s, f: (0, nf + up(f))),
            pl.BlockSpec((FFN_CONV, tf), lambda i, s, f: (0, up(f))),
            pl.BlockSpec((FFN_CONV, tf), lambda i, s, f: (0, nf + up(f))),
            pl.BlockSpec((tf, D_MODEL), lambda i, s, f: (down(f), 0)),
            pl.BlockSpec((1, D_MODEL), lambda i, s, f: (0, 0)),
        ],
        out_specs=[
            pl.BlockSpec((None, tt, D_MODEL), lambda i, s, f: (i, s, 0)),
            pl.BlockSpec((None, None, FFN_CONV - 1, tf), lambda i, s, f: (i, s, 0, up(f))),
            pl.BlockSpec((None, None, FFN_CONV - 1, tf), lambda i, s, f: (i, s, 0, up(f))),
        ],
        out_shape=[
            jax.ShapeDtypeStruct((b, t, D_MODEL), F32),
            jax.ShapeDtypeStruct((b, t // tt, FFN_CONV - 1, D_FF), F32),
            jax.ShapeDtypeStruct((b, t // tt, FFN_CONV - 1, D_FF), F32),
        ],
        scratch_shapes=[
            pltpu.VMEM((tt, D_MODEL), F32),
            pltpu.VMEM((nf, 2, SUBLANES, tf), F32),
            pltpu.VMEM((2, SUBLANES + tt, tf), F32),
        ],
        compiler_params=pltpu.CompilerParams(
            dimension_semantics=("arbitrary", "arbitrary", "arbitrary"),
            vmem_limit_bytes=VMEM_LIMIT),
        name="ffn_long",
    )(hn, x1, wup, wup, cw, cw, wdown, fg_row)


def _ffn_short_kernel(hn_ref, x1_ref, wg_ref, wu_ref, cwg_ref, cwu_ref, wd_ref, fg_ref,
                      b0g_ref, b1g_ref, b0u_ref, b1u_ref,
                      y_ref, n0g_ref, n1g_ref, n0u_ref, n1u_ref, acc_ref, z_ref, hb_ref, *, tt, seq):
    f = pl.program_id(1)
    nseq = tt // seq

    def up_fn():
        hn = hn_ref[...]
        t_in_seq = lax.broadcasted_iota(jnp.int32, (tt, FFN_TF), 0) % seq
        convs = []
        groups = ((wg_ref, cwg_ref, b0g_ref, b1g_ref, n0g_ref, n1g_ref),
                  (wu_ref, cwu_ref, b0u_ref, b1u_ref, n0u_ref, n1u_ref))
        for w_ref, cw_ref, b0_ref, b1_ref, n0_ref, n1_ref in groups:
            h = jnp.dot(hn, w_ref[...], preferred_element_type=F32)
            z_ref[...] = jnp.zeros(z_ref.shape, F32)
            for lb in range(FFN_TF // LANES):
                cols = slice(lb * LANES, (lb + 1) * LANES)
                z_ref[lb, pl.ds(0, nseq, stride=seq), :] = b0_ref[:, cols]
                z_ref[lb, pl.ds(1, nseq, stride=seq), :] = b1_ref[:, cols]
                hb_ref[lb] = h[:, cols]
                n0_ref[:, cols] = hb_ref[lb, pl.ds(seq - 2, nseq, stride=seq), :]
                n1_ref[:, cols] = hb_ref[lb, pl.ds(seq - 1, nseq, stride=seq), :]
            z = jnp.concatenate([z_ref[lb] for lb in range(FFN_TF // LANES)], axis=1)
            s1 = jnp.where(t_in_seq == 0, pltpu.roll(z, tt - 1, 0), pltpu.roll(h, 1, 0))
            s2 = jnp.where(t_in_seq < 2, z, pltpu.roll(h, 2, 0))
            cw = cw_ref[...]
            convs.append(h * cw[2:3, :] + s1 * cw[1:2, :] + s2 * cw[0:1, :])
        return _silu(convs[0]) * convs[1]

    _ffn_pipeline(f, up_fn, wd_ref, x1_ref, fg_ref, y_ref, acc_ref)


def _ffn_short(hn, x1, wup, cw, wdown, fg_row, buf0, buf1, tt, seq):
    m = hn.shape[0]
    tf = FFN_TF
    nf = FFN_NF
    nseq = tt // seq
    up = lambda f: f
    down = lambda f: f
    st_g = pl.BlockSpec((nseq, tf), lambda i, f: (i, up(f)))
    st_u = pl.BlockSpec((nseq, tf), lambda i, f: (i, nf + up(f)))
    new = pl.BlockSpec((nseq, tf), lambda i, f: (i, up(f)))
    new_shape = jax.ShapeDtypeStruct((m // seq, D_FF), F32)
    return pl.pallas_call(
        functools.partial(_ffn_short_kernel, tt=tt, seq=seq),
        grid=(m // tt, nf),
        in_specs=[
            pl.BlockSpec((tt, D_MODEL), lambda i, f: (i, 0)),
            pl.BlockSpec((tt, D_MODEL), lambda i, f: (i, 0)),
            pl.BlockSpec((D_MODEL, tf), lambda i, f: (0, up(f))),
            pl.BlockSpec((D_MODEL, tf), lambda i, f: (0, nf + up(f))),
            pl.BlockSpec((FFN_CONV, tf), lambda i, f: (0, up(f))),
            pl.BlockSpec((FFN_CONV, tf), lambda i, f: (0, nf + up(f))),
            pl.BlockSpec((tf, D_MODEL), lambda i, f: (down(f), 0)),
            pl.BlockSpec((1, D_MODEL), lambda i, f: (0, 0)),
            st_g, st_g, st_u, st_u,
        ],
        out_specs=[pl.BlockSpec((tt, D_MODEL), lambda i, f: (i, 0)), new, new, new, new],
        out_shape=[jax.ShapeDtypeStruct((m, D_MODEL), F32), new_shape, new_shape, new_shape, new_shape],
        scratch_shapes=[
            pltpu.VMEM((tt, D_MODEL), F32),
            pltpu.VMEM((tf // LANES, tt, LANES), F32),
            pltpu.VMEM((tf // LANES, tt, LANES), F32),
        ],
        compiler_params=pltpu.CompilerParams(
            dimension_semantics=("arbitrary", "arbitrary"), vmem_limit_bytes=VMEM_LIMIT),
        name="ffn_short",
    )(hn, x1, wup, wup, cw, cw, wdown, fg_row, buf0, buf1, buf0, buf1)


def _pad_lanes(vec, offset):
    out = jnp.zeros((LANES,), F32)
    return out.at[offset:offset + vec.shape[0]].set(vec.astype(F32))


def _trunk(x, s_gdn, s_gconv, s_rwkv, s_shift, s_ffn, prm, *, long_seq):
    b, t, _ = x.shape
    m = b * t
    x2d = x.reshape(m, D_MODEL)
    tm = min(512, m)
    if long_seq:
        tp = min(1024, t)
        proj, qkv_tail = _inproj_conv(x2d, prm["ln1_g"], prm["w_in"], prm["gdn_conv_w"], s_gconv, tp, t)
        gconv_new = qkv_tail.reshape(b, t // tp, SUBLANES, 3 * GDN_WIDTH)[:, -1, SUBLANES - (GDN_CONV - 1):]
    else:
        proj = _inproj(x2d, prm["ln1_g"], prm["w_in"], min(1024, m))
        gconv_new = proj.reshape(b, t, PROJ_WIDTH)[:, t - (GDN_CONV - 1):, COL_QKV:COL_QKV + 3 * GDN_WIDTH]

    o_a, gdn_new = _gdn_mixer(proj, s_gconv, s_gdn, prm["gdn_conv_w"], prm["alog_r"], prm["dtb_r"],
                              prm["alog_c"], prm["dtb_c"], prm["gdn_norm_g"], t)
    sh_rkv = s_shift[:, None, :3 * RWKV_WIDTH]
    sh_lora = s_shift[:, None, 3 * RWKV_WIDTH:]
    o_b, rwkv_new = _rwkv_mixer(proj, sh_rkv, sh_lora, s_rwkv, prm["mu_rkv"], prm["mu_lora"],
                                prm["rwkv_w0"], prm["rwkv_a0"], prm["rwkv_wab"], prm["rwkv_g_b"],
                                prm["rwkv_k_k"], prm["rwkv_k_a"], prm["rwkv_r_k"], prm["rwkv_gn_w"],
                                prm["rwkv_gn_b"], t)

    x1, hn = _outproj(x2d, o_a, o_b, prm["w_o"], prm["ln2_g"], tm)
    if long_seq:
        tt = min(512, t)
        y, n_g, n_u = _ffn_long(hn.reshape(b, t, D_MODEL), x1.reshape(b, t, D_MODEL), prm["ffn_w_up"],
                                prm["ffn_conv_w"], prm["ffn_w_down"], prm["final_g"], tt)
        ffn_new = jnp.concatenate([n_g[:, -1], n_u[:, -1]], axis=-1)
    else:
        tt = min(512, m)
        y, n0g, n1g, n0u, n1u = _ffn_short(hn, x1, prm["ffn_w_up"], prm["ffn_conv_w"],
                                            prm["ffn_w_down"], prm["final_g"],
                                            s_ffn[:, 0], s_ffn[:, 1], tt, t)
        y = y.reshape(b, t, D_MODEL)
        ffn_new = jnp.stack([jnp.concatenate([n0g, n0u], axis=-1),
                             jnp.concatenate([n1g, n1u], axis=-1)], axis=1)

    proj = proj.reshape(b, t, PROJ_WIDTH)
    shift_new = jnp.concatenate([proj[:, t - 1, COL_RKV:COL_RKV + 3 * RWKV_WIDTH],
                                 proj[:, t - 1, COL_LORA:COL_LORA + RWKV_LORA]], axis=-1)
    return y, gdn_new[None], gconv_new[None], rwkv_new[None], shift_new[None], ffn_new[None]


def kernel(x_prompt, x_sample, state_gdn, state_gdn_conv, state_rwkv, state_rwkv_shift, state_ffn_conv, ln1_g, w_in, gdn_conv_w, gdn_a_log, gdn_dt_bias, gdn_norm_g, rwkv_mu, rwkv_w0, rwkv_w_b, rwkv_a0, rwkv_a_b, rwkv_g_b, rwkv_k_k, rwkv_k_a, rwkv_r_k, rwkv_gn_w, rwkv_gn_b, w_o, ln2_g, ffn_w_up, ffn_conv_w, ffn_w_down, final_g):
    assert ln1_g.shape[0] == 1, "single-layer trunk"
    w_perm = _permute_win(w_in[0])
    mu = rwkv_mu[0]
    zeros_w = jnp.zeros((RWKV_LORA_W, RWKV_WIDTH), F32)
    wab = jnp.concatenate([
        jnp.concatenate([rwkv_w_b[0], zeros_w], axis=1),
        jnp.concatenate([zeros_w, rwkv_a_b[0]], axis=1)], axis=0).astype(BF16)
    alog = _pad_lanes(gdn_a_log[0], GDN_HEADS)
    dtb = _pad_lanes(gdn_dt_bias[0], GDN_HEADS)
    prm = {
        "ln1_g": ln1_g[0][None], "w_in": w_perm, "gdn_conv_w": gdn_conv_w[0],
        "alog_r": alog[None], "dtb_r": dtb[None], "alog_c": alog[:, None], "dtb_c": dtb[:, None],
        "gdn_norm_g": gdn_norm_g[0][None],
        "mu_rkv": mu[None, :3 * RWKV_WIDTH], "mu_lora": mu[None, 3 * RWKV_WIDTH:],
        "rwkv_w0": rwkv_w0[0][None], "rwkv_a0": rwkv_a0[0][None], "rwkv_wab": wab,
        "rwkv_g_b": rwkv_g_b[0].astype(BF16), "rwkv_k_k": rwkv_k_k[0][None],
        "rwkv_k_a": rwkv_k_a[0][None], "rwkv_r_k": rwkv_r_k[0].reshape(1, RWKV_WIDTH),
        "rwkv_gn_w": rwkv_gn_w[0][None], "rwkv_gn_b": rwkv_gn_b[0][None],
        "w_o": w_o[0].astype(BF16), "ln2_g": ln2_g[0][None],
        "ffn_w_up": ffn_w_up[0].astype(BF16), "ffn_conv_w": ffn_conv_w[0],
        "ffn_w_down": ffn_w_down[0].astype(BF16), "final_g": final_g[None],
    }

    bp = x_prompt.shape[0]
    zero_states = (
        jnp.zeros((bp,) + state_gdn.shape[2:], F32),
        jnp.zeros((bp,) + state_gdn_conv.shape[2:], F32),
        jnp.zeros((bp,) + state_rwkv.shape[2:], F32),
        jnp.zeros((bp,) + state_rwkv_shift.shape[2:], F32),
        None,
    )
    outs_p = _trunk(x_prompt, *zero_states, prm, long_seq=True)
    outs_s = _trunk(x_sample, state_gdn[0], state_gdn_conv[0], state_rwkv[0], state_rwkv_shift[0],
                    state_ffn_conv[0], prm, long_seq=False)
    return (outs_p[0], outs_s[0]) + tuple(outs_p[1:]) + tuple(outs_s[1:])
```

```python
import functools

import jax
import jax.numpy as jnp
from jax import lax
from jax.experimental import pallas as pl
from jax.experimental.pallas import tpu as pltpu

F32 = jnp.float32
BF16 = jnp.bfloat16

D_MODEL = 2048
GDN_WIDTH = 1024
GDN_HEADS = 8
GDN_DK = 128
GDN_CONV = 4
RWKV_WIDTH = 1024
RWKV_HEAD = 64
RWKV_HEADS = 16
RWKV_LORA_W = 64
RWKV_LORA_A = 64
RWKV_LORA_G = 128
RWKV_LORA = RWKV_LORA_W + RWKV_LORA_A + RWKV_LORA_G
RWKV_PROJ = 3 * RWKV_WIDTH + RWKV_LORA
D_FF = 5632
FFN_CONV = 3
RMS_EPS = 1e-6
L2_EPS = 1e-12
GN_EPS = 64e-5

REF_OFF_Z = 3 * GDN_WIDTH
REF_OFF_B = 4 * GDN_WIDTH
REF_OFF_RWKV = REF_OFF_B + 2 * GDN_HEADS
REF_IN_WIDTH = REF_OFF_RWKV + RWKV_PROJ

LANES = 128
SUBLANES = 8
COL_QKV = 0
COL_RKV = 3 * GDN_WIDTH
COL_Z = COL_RKV + 3 * RWKV_WIDTH
COL_LORA = COL_Z + GDN_WIDTH
COL_BA = COL_LORA + RWKV_LORA
PROJ_WIDTH = 7680
PROJ_TN = 1280

MIX_ROWS = 64
MIX_GROUPS_LONG = 4
MIX_GROUPS_SHORT = 2

NN = (((1,), (0,)), ((), ()))
NT = (((1,), (1,)), ((), ()))
TN = (((0,), (0,)), ((), ()))

VMEM_LIMIT = 56 * 1024 * 1024

P_GRAM = "x1"
P_INV = "x1"
P_SOLVE = "x1"
P_STATE = "x1"
P_OUT = "x1"


def _split(x):
    hi = x.astype(BF16)
    return hi, (x - hi.astype(F32)).astype(BF16)


def _dot(a, b, dims=NN, mode="x1"):
    if mode == "hi":
        return lax.dot_general(a, b, dims, precision=lax.Precision.HIGHEST,
                               preferred_element_type=F32)
    if mode == "x3":
        a_hi, a_lo = _split(a)
        b_hi, b_lo = _split(b)
        d = lambda u, v: lax.dot_general(u, v, dims, preferred_element_type=F32)
        return d(a_hi, b_hi) + (d(a_hi, b_lo) + d(a_lo, b_hi))
    return lax.dot_general(a.astype(BF16), b.astype(BF16), dims, preferred_element_type=F32)


def _ones_dot(ones_mat, x, dims=NN):
    x1 = x.astype(BF16)
    r1 = x - x1.astype(F32)
    x2 = r1.astype(BF16)
    x3 = (r1 - x2.astype(F32)).astype(BF16)
    m = ones_mat.astype(BF16)
    if dims == NN:
        d = lambda v: lax.dot_general(m, v, dims, preferred_element_type=F32)
    else:
        d = lambda v: lax.dot_general(v, m, dims, preferred_element_type=F32)
    return d(x1) + (d(x2) + d(x3))


def _sigmoid(x):
    return 1.0 / (1.0 + jnp.exp(-x))


def _silu(x):
    return x * _sigmoid(x)


def _softplus(x):
    return jnp.maximum(x, 0.0) + jnp.log(1.0 + jnp.exp(-jnp.abs(x)))


def _seq_masks(rows, seq_len):
    r = lax.broadcasted_iota(jnp.int32, (rows, rows), 0)
    c = lax.broadcasted_iota(jnp.int32, (rows, rows), 1)
    if seq_len >= rows:
        return None, r >= c, r > c
    shift = seq_len.bit_length() - 1
    assert 1 << shift == seq_len
    same = jnp.right_shift(r, shift) == jnp.right_shift(c, shift)
    return same, same & (r >= c), same & (r > c)


def _wide_masks(rows, seq_len):
    r = lax.broadcasted_iota(jnp.int32, (rows, 2 * rows), 0)
    c = lax.broadcasted_iota(jnp.int32, (rows, 2 * rows), 1)
    right = c >= rows
    cc = jnp.where(right, c - rows, c)
    if seq_len >= rows:
        return r >= cc, right & (r > cc)
    shift = seq_len.bit_length() - 1
    same = jnp.right_shift(r, shift) == jnp.right_shift(cc, shift)
    return same & (r >= cc), same & right & (r > cc)


def _pair_masks(rows, seq_len):
    half = LANES // 2
    assert rows == half
    lane = lax.broadcasted_iota(jnp.int32, (rows, LANES), 1)
    row = lax.broadcasted_iota(jnp.int32, (rows, LANES), 0)
    col = jnp.bitwise_and(lane, half - 1)
    if seq_len >= rows:
        causal, strict = row >= col, row > col
    else:
        shift = seq_len.bit_length() - 1
        same = jnp.right_shift(row, shift) == jnp.right_shift(col, shift)
        causal, strict = same & (row >= col), same & (row > col)
    return causal, strict, jnp.where(row == col, 1.0, 0.0).astype(F32), lane < half


def _bd(x, first_half):
    return jnp.concatenate([jnp.where(first_half, x, 0.0), jnp.where(first_half, 0.0, x)], axis=0)


def _pair_inverses(neg_a, rows, nilpotency, eye2, first_half):
    qs = dict(neg_a)
    ts = {u: eye2 + q for u, q in qs.items()}
    n = 2
    if n < nilpotency:
        qs = {u: _dot(q, _bd(q, first_half), mode=P_INV) for u, q in qs.items()}
    while n < nilpotency:
        if 2 * n < nilpotency:
            tq = {u: _dot(jnp.concatenate([ts[u], qs[u]], axis=0), _bd(qs[u], first_half), mode=P_INV)
                  for u in qs}
            ts = {u: ts[u] + tq[u][:rows] for u in qs}
            qs = {u: tq[u][rows:] for u in qs}
        else:
            ts = {u: ts[u] + _dot(ts[u], _bd(qs[u], first_half), mode=P_INV) for u in qs}
        n *= 2
    return ts


def _lane_group_sums(x, group):
    tile = 2 * LANES
    shift = group.bit_length() - 1
    li = jnp.right_shift(lax.broadcasted_iota(jnp.int32, (tile, tile), 0), shift)
    lj = jnp.right_shift(lax.broadcasted_iota(jnp.int32, (tile, tile), 1), shift)
    ones = jnp.where(li == lj, 1.0, 0.0).astype(BF16)
    hi, lo = _split(x)
    d = lambda u: lax.dot_general(u, ones, NN, preferred_element_type=F32)
    return jnp.concatenate([d(hi[:, t:t + tile]) + d(lo[:, t:t + tile])
                            for t in range(0, x.shape[1], tile)], axis=1)


def _unit_lower_inverses(mats, rows, nilpotency):
    r = lax.broadcasted_iota(jnp.int32, (rows, rows), 0)
    c = lax.broadcasted_iota(jnp.int32, (rows, rows), 1)
    eye = jnp.where(r == c, 1.0, 0.0).astype(F32)
    qs = [-a for a in mats]
    ts = [eye + q for q in qs]
    n = 2
    if n < nilpotency:
        qs = [_dot(q, q, mode=P_INV) for q in qs]
    while n < nilpotency:
        if 2 * n < nilpotency:
            tq = [_dot(jnp.concatenate([t, q], axis=0), q, mode=P_INV) for t, q in zip(ts, qs)]
            ts = [t + p[:rows] for t, p in zip(ts, tq)]
            qs = [p[rows:] for p in tq]
        else:
            ts = [t + _dot(t, q, mode=P_INV) for t, q in zip(ts, qs)]
        n *= 2
    return ts


def _permute_win_kernel(w_ref, o_ref):
    rw = REF_OFF_RWKV
    rows = w_ref.shape[0]
    cast = lambda lo, hi: w_ref[:, lo:hi].astype(BF16)
    o_ref[:, COL_QKV:COL_RKV] = cast(0, REF_OFF_Z)
    o_ref[:, COL_RKV:COL_Z] = cast(rw, rw + 3 * RWKV_WIDTH)
    o_ref[:, COL_Z:COL_LORA] = cast(REF_OFF_Z, REF_OFF_B)
    o_ref[:, COL_LORA:COL_BA] = cast(rw + 3 * RWKV_WIDTH, REF_IN_WIDTH)
    tail = jnp.concatenate([w_ref[:, REF_OFF_B:REF_OFF_RWKV],
                            jnp.zeros((rows, PROJ_WIDTH - COL_BA - 2 * GDN_HEADS), F32)], axis=1)
    o_ref[:, COL_BA:] = tail.astype(BF16)


def _permute_win(w, tr=256):
    return pl.pallas_call(
        _permute_win_kernel,
        grid=(D_MODEL // tr,),
        in_specs=[pl.BlockSpec((tr, REF_IN_WIDTH), lambda i: (i, 0))],
        out_specs=pl.BlockSpec((tr, PROJ_WIDTH), lambda i: (i, 0)),
        out_shape=jax.ShapeDtypeStruct((D_MODEL, PROJ_WIDTH), BF16),
        compiler_params=pltpu.CompilerParams(
            dimension_semantics=("arbitrary",), vmem_limit_bytes=VMEM_LIMIT),
        name="permute_win",
    )(w)


def _inproj_kernel(x_ref, g_ref, w_ref, o_ref, xn_ref):
    @pl.when(pl.program_id(1) == 0)
    def _():
        x = x_ref[...]
        ms = jnp.mean(x * x, axis=-1, keepdims=True)
        xn_ref[...] = (x * lax.rsqrt(ms + RMS_EPS) * g_ref[...]).astype(BF16)

    o_ref[...] = jnp.dot(xn_ref[...], w_ref[...], preferred_element_type=F32)


def _inproj(x2d, g_row, w_bf16, tm):
    m = x2d.shape[0]
    return pl.pallas_call(
        _inproj_kernel,
        grid=(m // tm, PROJ_WIDTH // PROJ_TN),
        in_specs=[
            pl.BlockSpec((tm, D_MODEL), lambda i, j: (i, 0)),
            pl.BlockSpec((1, D_MODEL), lambda i, j: (0, 0)),
            pl.BlockSpec((D_MODEL, PROJ_TN), lambda i, j: (0, j)),
        ],
        out_specs=pl.BlockSpec((tm, PROJ_TN), lambda i, j: (i, j)),
        out_shape=jax.ShapeDtypeStruct((m, PROJ_WIDTH), F32),
        scratch_shapes=[pltpu.VMEM((tm, D_MODEL), BF16)],
        compiler_params=pltpu.CompilerParams(
            dimension_semantics=("arbitrary", "arbitrary"), vmem_limit_bytes=VMEM_LIMIT),
        name="inproj",
    )(x2d, g_row, w_bf16)


def _mixer_geometry(seq_len, n_seq, short_groups=1):
    rows = MIX_ROWS
    length = min(seq_len, rows)
    assert rows % length == 0 and seq_len % length == 0 and length % SUBLANES == 0
    per_group = rows // length
    groups = MIX_GROUPS_LONG if per_group == 1 else short_groups
    while n_seq % (groups * per_group):
        groups //= 2
    assert groups >= 1
    return rows, length, per_group, groups, seq_len // length


def _gdn_kernel(qkv_ref, z_ref, ba_ref, cbuf_ref, s0_ref, convw_ref, alog_r_ref, dtb_r_ref,
                alog_c_ref, dtb_c_ref, ng_ref, o_ref, s_ref, xp_ref, *, seq_len, n_seq):
    R, L, G, S, _ = _mixer_geometry(seq_len, n_seq, MIX_GROUPS_SHORT)
    RT = S * R
    c = pl.program_id(1)
    width = 3 * GDN_WIDTH
    hist = GDN_CONV - 1
    cw = convw_ref[...]
    groups = range(S)
    seqs = range(G)

    @pl.when(c == 0)
    def _():
        s_ref[...] = s0_ref[...]

    pieces = []
    if G == 1:
        @pl.when(c == 0)
        def _():
            for s in groups:
                xp_ref[s, 0:SUBLANES, :] = jnp.zeros((SUBLANES, width), F32)
                xp_ref[s, SUBLANES - hist:SUBLANES, :] = cbuf_ref[s]

        @pl.when(c > 0)
        def _():
            for s in groups:
                xp_ref[s, 0:SUBLANES, :] = xp_ref[s, R:R + SUBLANES, :]

        for s in groups:
            xp_ref[s, SUBLANES:SUBLANES + R, :] = qkv_ref[s]
            piece = qkv_ref[s] * cw[hist:hist + 1, :]
            for i in range(hist):
                off = SUBLANES - hist + i
                piece = piece + xp_ref[s, off:off + R, :] * cw[i:i + 1, :]
            pieces.append(piece)
    else:
        for s in groups:
            for g in seqs:
                q = s * G + g
                rows = slice(g * L, (g + 1) * L)
                xp_ref[q, SUBLANES - hist:SUBLANES, :] = cbuf_ref[q]
                xp_ref[q, SUBLANES:SUBLANES + L, :] = qkv_ref[s, rows, :]
                piece = qkv_ref[s, rows, :] * cw[hist:hist + 1, :]
                for i in range(hist):
                    off = SUBLANES - hist + i
                    piece = piece + xp_ref[q, off:off + L, :] * cw[i:i + 1, :]
                pieces.append(piece)
    qkv = _silu(pieces[0] if len(pieces) == 1 else jnp.concatenate(pieces, axis=0))

    same_t, causal_t, _ = _seq_masks(RT, L)
    causal01 = jnp.where(causal_t, 1.0, 0.0).astype(F32)

    ba = ba_ref[...].reshape(RT, LANES)
    ba_t = ba.T
    beta_c = _sigmoid(ba)
    g_c = -jnp.exp(alog_r_ref[...]) * _softplus(ba + dtb_r_ref[...])
    g_r = -jnp.exp(alog_c_ref[...]) * _softplus(ba_t + dtb_c_ref[...])
    gc_all = _ones_dot(causal01, g_c)
    gr_all = _ones_dot(causal01, g_r, NT)
    if same_t is None:
        gtot_all = jnp.broadcast_to(gc_all[RT - 1:RT, :], (RT, LANES))
    else:
        gtot_all = _ones_dot(jnp.where(same_t, 1.0, 0.0).astype(F32), g_c)

    qk_raw = qkv[:, :2 * GDN_WIDTH]
    qk_n = qk_raw * lax.rsqrt(_lane_group_sums(qk_raw * qk_raw, GDN_DK) + L2_EPS)
    q_all = qk_n[:, :GDN_WIDTH] * (GDN_DK ** -0.5)
    k_all = qk_n[:, GDN_WIDTH:]
    v_all = qkv[:, 2 * GDN_WIDTH:]

    causal2, strict2, eye2, first_half = _pair_masks(R, L)
    first_head = lax.broadcasted_iota(jnp.int32, (R, 2 * GDN_DK), 1) < GDN_DK
    zeros_u = jnp.zeros((R, GDN_DK), F32)
    zeros_rhs = jnp.zeros((R, 2 * GDN_DK), F32)
    chains = [(s, h) for s in groups for h in range(GDN_HEADS)]
    units = [(s, p) for s in groups for p in range(GDN_HEADS // 2)]
    seq_rows = [slice(g * L, (g + 1) * L) for g in seqs]
    qs, ks, vs, betas, gcols, gtots = {}, {}, {}, {}, {}, {}
    for s, h in chains:
        rs = slice(s * R, (s + 1) * R)
        lo = h * GDN_DK
        key = (s, h)
        qs[key] = q_all[rs, lo:lo + GDN_DK]
        ks[key] = k_all[rs, lo:lo + GDN_DK]
        vs[key] = v_all[rs, lo:lo + GDN_DK]
        betas[key] = beta_c[rs, h:h + 1]
        gcols[key] = gc_all[rs, GDN_HEADS + h:GDN_HEADS + h + 1]
        gtots[key] = gtot_all[rs, GDN_HEADS + h:GDN_HEADS + h + 1]
    kq2, decay2, a2 = {}, {}, {}
    for s, p in units:
        rs = slice(s * R, (s + 1) * R)
        cols = slice(2 * p * GDN_DK, (2 * p + 2) * GDN_DK)
        k_pair = k_all[rs, cols]
        kt_bd = jnp.concatenate([jnp.where(first_head, k_pair, 0.0),
                                 jnp.where(first_head, 0.0, k_pair)], axis=0).T
        kq = _dot(jnp.concatenate([k_pair, q_all[rs, cols]], axis=0), kt_bd, mode=P_GRAM)
        h0, h1 = (s, 2 * p), (s, 2 * p + 1)
        gcol2 = jnp.where(first_half, gcols[h0], gcols[h1])
        grow2 = jnp.concatenate([gr_all[GDN_HEADS + 2 * p:GDN_HEADS + 2 * p + 1, rs],
                                 gr_all[GDN_HEADS + 2 * p + 1:GDN_HEADS + 2 * p + 2, rs]], axis=1)
        dec = jnp.where(causal2, jnp.exp(jnp.where(causal2, gcol2 - grow2, 0.0)), 0.0)
        beta2 = jnp.where(first_half, betas[h0], betas[h1])
        kq2[(s, p)] = kq
        decay2[(s, p)] = dec
        a2[(s, p)] = jnp.where(strict2, -(beta2 * kq[:R] * dec), 0.0)
    t2 = _pair_inverses(a2, R, L, eye2, first_half)
    gammas = {key: jnp.exp(gcols[key]) for key in chains}

    def stacked(key, x, zeros):
        return jnp.concatenate([x, zeros] if key[1] % 2 == 0 else [zeros, x], axis=0)

    sols = {key: _dot(t2[(key[0], key[1] // 2)],
                      stacked(key, jnp.concatenate([(betas[key] * gammas[key]) * ks[key],
                                                    betas[key] * vs[key]], axis=1), zeros_rhs),
                      mode=P_SOLVE) for key in chains}
    states = {(s, h): [s_ref[s * G + g, h] for g in seqs] for s, h in chains}
    wss = {key: [_dot(jnp.concatenate([sols[key][rows, :GDN_DK], (qs[key] * gammas[key])[rows]], axis=0),
                      states[key][g], mode=P_STATE) for g, rows in enumerate(seq_rows)]
           for key in chains}
    us = {key: jnp.concatenate([sols[key][rows, GDN_DK:] - wss[key][g][:L]
                                for g, rows in enumerate(seq_rows)], axis=0) for key in chains}
    qk2 = {u: kq2[u][R:] * decay2[u] for u in units}
    outs = {key: jnp.concatenate([wss[key][g][L:] for g in seqs], axis=0)
            + _dot(qk2[(key[0], key[1] // 2)], stacked(key, us[key], zeros_u), mode=P_OUT)
            for key in chains}
    for s, h in chains:
        key = (s, h)
        kt = ks[key] * jnp.exp(gtots[key] - gcols[key])
        for g, rows in enumerate(seq_rows):
            gl = jnp.exp(gtots[key][g * L:g * L + 1, :])
            s_ref[s * G + g, h] = gl * states[key][g] + _dot(kt[rows], us[key][rows], TN, mode=P_STATE)
    ng = jnp.concatenate([ng_ref[...]] * GDN_HEADS, axis=1)
    for s in groups:
        o = jnp.concatenate([outs[(s, h)] for h in range(GDN_HEADS)], axis=1)
        ms = _lane_group_sums(o * o, GDN_DK) * (1.0 / GDN_DK)
        o = o * lax.rsqrt(ms + RMS_EPS) * ng
        o_ref[s] = (o * _silu(z_ref[s])).astype(o_ref.dtype)


def _gdn_mixer(proj2d, cbuf, s0, convw, alog_r, dtb_r, alog_c, dtb_c, ng, seq_len):
    m = proj2d.shape[0]
    nseq = m // seq_len
    R, L, G, S, sps = _mixer_geometry(seq_len, nseq, MIX_GROUPS_SHORT)
    width = 3 * GDN_WIDTH
    proj3d = proj2d.reshape(nseq // G, sps * R, PROJ_WIDTH)
    const2 = lambda i, c: (0, 0)
    rows_map = lambda col: (lambda i, c: (i, c, col))
    per_seq = lambda *dims: pl.BlockSpec((S * G,) + dims, lambda i, c: (i,) + (0,) * len(dims))
    xp_shape = (S, SUBLANES + R, width) if G == 1 else (S * G, SUBLANES + L, width)
    o, s_new = pl.pallas_call(
        functools.partial(_gdn_kernel, seq_len=seq_len, n_seq=nseq),
        grid=(nseq // (S * G), sps),
        in_specs=[
            pl.BlockSpec((S, R, width), rows_map(COL_QKV // width)),
            pl.BlockSpec((S, R, GDN_WIDTH), rows_map(COL_Z // GDN_WIDTH)),
            pl.BlockSpec((S, R, LANES), rows_map(COL_BA // LANES)),
            per_seq(GDN_CONV - 1, width),
            per_seq(GDN_HEADS, GDN_DK, GDN_DK),
            pl.BlockSpec((GDN_CONV, width), const2),
            pl.BlockSpec((1, LANES), const2),
            pl.BlockSpec((1, LANES), const2),
            pl.BlockSpec((LANES, 1), const2),
            pl.BlockSpec((LANES, 1), const2),
            pl.BlockSpec((1, GDN_DK), const2),
        ],
        out_specs=[
            pl.BlockSpec((S, R, GDN_WIDTH), rows_map(0)),
            per_seq(GDN_HEADS, GDN_DK, GDN_DK),
        ],
        out_shape=[
            jax.ShapeDtypeStruct((nseq // G, sps * R, GDN_WIDTH), BF16),
            jax.ShapeDtypeStruct((nseq, GDN_HEADS, GDN_DK, GDN_DK), F32),
        ],
        scratch_shapes=[pltpu.VMEM(xp_shape, F32)],
        compiler_params=pltpu.CompilerParams(
            dimension_semantics=("arbitrary", "arbitrary"), vmem_limit_bytes=VMEM_LIMIT),
        name="gdn_mixer",
    )(proj3d, proj3d, proj3d, cbuf, s0, convw, alog_r, dtb_r, alog_c, dtb_c, ng)
    return o.reshape(m, GDN_WIDTH), s_new


def _shifted_rows(x_ref, prev_ref, carry_ref, c, R, L, G, S):
    width = x_ref.shape[-1]
    groups = range(S)
    if G == 1:
        @pl.when(c == 0)
        def _():
            for s in groups:
                carry_ref[s, 0:SUBLANES, :] = jnp.zeros((SUBLANES, width), F32)
                carry_ref[s, SUBLANES - 1:SUBLANES, :] = prev_ref[s]

        @pl.when(c > 0)
        def _():
            for s in groups:
                carry_ref[s, 0:SUBLANES, :] = carry_ref[s, R:R + SUBLANES, :]

        xs, prevs = [], []
        for s in groups:
            carry_ref[s, SUBLANES:SUBLANES + R, :] = x_ref[s]
            xs.append(x_ref[s])
            prevs.append(carry_ref[s, SUBLANES - 1:SUBLANES - 1 + R, :])
    else:
        row = lax.broadcasted_iota(jnp.int32, (L, width), 0)
        xs, prevs = [], []
        for s in groups:
            x = x_ref[s]
            xs.append(x)
            for g in range(G):
                xg = x[g * L:(g + 1) * L]
                prevs.append(jnp.where(row == 0, prev_ref[s * G + g], pltpu.roll(xg, 1, 0)))
    cat = lambda parts: parts[0] if len(parts) == 1 else jnp.concatenate(parts, axis=0)
    return cat(xs), cat(prevs)


def _rwkv_kernel(rkv_ref, lora_ref, sh_rkv_ref, sh_lora_ref, s0_ref, mu_rkv_ref, mu_lora_ref,
                 w0_ref, a0_ref, wab_ref, gb_ref, kk_ref, ka_ref, rk_ref, gnw_ref, gnb_ref,
                 o_ref, s_ref, xr_ref, xl_ref, m_ref, *, seq_len, n_seq):
    R, L, G, S, sps = _mixer_geometry(seq_len, n_seq)
    RT = S * R
    NQ = S * G
    c = pl.program_id(1)
    W = RWKV_WIDTH
    HD = RWKV_HEAD
    pairs = range(RWKV_HEADS // 2)
    groups = range(S)
    seqs = range(G)

    @pl.when(c == 0)
    def _():
        zero = jnp.zeros((HD, HD), F32)
        for q in range(NQ):
            for j in pairs:
                vk = jnp.concatenate([jnp.concatenate([s0_ref[q, 2 * j], zero], axis=1),
                                      jnp.concatenate([zero, s0_ref[q, 2 * j + 1]], axis=1)], axis=0)
                m_ref[q, j] = vk.T

    p, p_prev = _shifted_rows(rkv_ref, sh_rkv_ref, xr_ref, c, R, L, G, S)
    xs = p + (p_prev - p) * mu_rkv_ref[...]
    pl_, pl_prev = _shifted_rows(lora_ref, sh_lora_ref, xl_ref, c, R, L, G, S)
    xl = pl_ + (pl_prev - pl_) * mu_lora_ref[...]
    r = xs[:, :W]
    k = xs[:, W:2 * W]
    v = xs[:, 2 * W:]

    wa_in = xl[:, :LANES]
    lane = lax.broadcasted_iota(jnp.int32, (RT, LANES), 1)
    wa_in = jnp.where(lane < RWKV_LORA_W, jnp.tanh(wa_in), wa_in)
    wa = _dot(wa_in, wab_ref[...])
    w = -_softplus(-(w0_ref[...] + wa[:, :W])) - 0.5
    lw = -jnp.exp(w)
    a = _sigmoid(a0_ref[...] + wa[:, W:])
    gate = _dot(_sigmoid(xl[:, LANES:]), gb_ref[...])
    kk_raw = k * kk_ref[...]
    k2 = k * (1.0 + (a - 1.0) * ka_ref[...])

    same_t, causal_t, _ = _seq_masks(RT, L)
    lc = _ones_dot(jnp.where(causal_t, 1.0, 0.0).astype(F32), lw)
    if G == 1:
        ltot = jnp.concatenate([jnp.broadcast_to(lc[(s + 1) * R - 1:(s + 1) * R, :], (R, W))
                                for s in groups], axis=0) if S > 1 else \
            jnp.broadcast_to(lc[R - 1:R, :], (R, W))
    else:
        ltot = _ones_dot(jnp.where(same_t, 1.0, 0.0).astype(F32), lw)
    e_inv = jnp.exp(-lc)
    e_rem = jnp.exp(ltot - lc)

    kk = kk_raw * lax.rsqrt(_lane_group_sums(kk_raw * kk_raw, HD) + L2_EPS)
    kka = kk * a
    ct = kk * jnp.exp(lc - lw)
    rt = r * jnp.exp(lc)
    bh = kka * e_inv
    kh = k2 * e_inv
    bb = kka * e_rem
    kb = k2 * e_rem
    p_rows = [ltot[q * L:q * L + 1] for q in range(NQ)]
    p_rows = p_rows + [p_rows[0]] * (-NQ % SUBLANES)
    pt = jnp.exp(jnp.concatenate(p_rows, axis=0)).T

    causal2, strict2, eye2, even_half = _pair_masks(R, L)
    block_diag = ((lax.broadcasted_iota(jnp.int32, (LANES, LANES), 0) < HD)
                  == (lax.broadcasted_iota(jnp.int32, (LANES, LANES), 1) < HD))
    even_all = jnp.bitwise_and(lax.broadcasted_iota(jnp.int32, (R, W), 1), LANES - 1) < HD
    tile = lambda j: slice(j * LANES, (j + 1) * LANES)
    bd = lambda x: _bd(x, even_half)

    if G > 1:
        row2 = lax.broadcasted_iota(jnp.int32, (2 * R, LANES), 0)
        seq_of_row = jnp.right_shift(jnp.bitwise_and(row2, R - 1), L.bit_length() - 1)

    units = [(s, j) for s in groups for j in pairs]
    xps, vps, bdb, bdk, bkts = {}, {}, {}, {}, {}
    for s in groups:
        rs = slice(s * R, (s + 1) * R)
        x_s = jnp.concatenate([ct[rs], rt[rs]], axis=0)
        bt_s = jnp.concatenate([jnp.where(even_all, bh[rs], 0.0), jnp.where(even_all, 0.0, bh[rs])],
                               axis=0).T
        kt_s = jnp.concatenate([jnp.where(even_all, kh[rs], 0.0), jnp.where(even_all, 0.0, kh[rs])],
                               axis=0).T
        bkt_s = jnp.concatenate([bb[rs], kb[rs]], axis=0).T
        for j in pairs:
            xps[(s, j)] = x_s[:, tile(j)]
            vps[(s, j)] = v[rs, tile(j)]
            bdb[(s, j)] = bt_s[tile(j), :]
            bdk[(s, j)] = kt_s[tile(j), :]
            bkts[(s, j)] = bkt_s[tile(j), :]
    gb = {u: _dot(xps[u], bdb[u], mode=P_GRAM) for u in units}
    gk = {u: _dot(xps[u], bdk[u], mode=P_GRAM) for u in units}
    ms = {(s, j): [m_ref[s * G + g, j] for g in seqs] for s, j in units}
    xm_c, xm_r = {}, {}
    for u in units:
        if G == 1:
            xm = _dot(xps[u], ms[u][0], mode=P_STATE)
            xm_c[u], xm_r[u] = xm[:R], xm[R:]
        else:
            parts = [_dot(jnp.concatenate([xps[u][g * L:(g + 1) * L],
                                           xps[u][R + g * L:R + (g + 1) * L]], axis=0),
                          ms[u][g], mode=P_STATE) for g in seqs]
            xm_c[u] = jnp.concatenate([p_[:L] for p_ in parts], axis=0)
            xm_r[u] = jnp.concatenate([p_[L:] for p_ in parts], axis=0)
    ts = _pair_inverses({u: jnp.where(strict2, -gb[u][:R], 0.0) for u in units}, R, L, eye2, even_half)
    bdv = {u: bd(vps[u]) for u in units}
    akvs = {u: _dot(jnp.where(strict2, gk[u][:R], 0.0), bdv[u], mode=P_OUT) for u in units}
    us = {u: _dot(ts[u], bd(-(xm_c[u] + akvs[u])), mode=P_SOLVE) for u in units}
    uvs = {u: jnp.concatenate([us[u], vps[u]], axis=0) for u in units}
    ys = {u: xm_r[u] + _dot(
        jnp.concatenate([jnp.where(causal2, gb[u][R:], 0.0), jnp.where(causal2, gk[u][R:], 0.0)], axis=1),
        jnp.concatenate([bd(us[u]), bdv[u]], axis=0), mode=P_OUT) for u in units}
    for s, j in units:
        u = (s, j)
        for g in seqs:
            q = s * G + g
            uv_g = uvs[u] if G == 1 else jnp.where(seq_of_row == g, uvs[u], 0.0)
            upd = _dot(bkts[u], uv_g, mode=P_STATE)
            m_ref[q, j] = pt[tile(j), q:q + 1] * ms[u][g] + jnp.where(block_diag, upd, 0.0)

    y_rows = [jnp.concatenate([ys[(s, j)] for j in pairs], axis=1) for s in groups]
    y_all = y_rows[0] if S == 1 else jnp.concatenate(y_rows, axis=0)
    mean = _lane_group_sums(y_all, HD) * (1.0 / HD)
    yc = y_all - mean
    var = _lane_group_sums(yc * yc, HD) * (1.0 / HD)
    yn = yc * lax.rsqrt(var + GN_EPS) * gnw_ref[...] + gnb_ref[...]
    yn = yn + _lane_group_sums(r * k2 * rk_ref[...], HD) * v
    o_ref[...] = (yn * gate).reshape(S, R, W).astype(o_ref.dtype)

    @pl.when(c == sps - 1)
    def _():
        for q in range(NQ):
            for j in pairs:
                vk = m_ref[q, j].T
                s_ref[q, 2 * j] = vk[:HD, :HD]
                s_ref[q, 2 * j + 1] = vk[HD:, HD:]


def _rwkv_mixer(proj2d, sh_rkv, sh_lora, s0, mu_rkv, mu_lora, w0, a0, wab, gb, kk, ka, rk, gnw, gnb,
                seq_len):
    m = proj2d.shape[0]
    nseq = m // seq_len
    R, L, G, S, sps = _mixer_geometry(seq_len, nseq)
    W = RWKV_WIDTH
    proj3d = proj2d.reshape(nseq // G, sps * R, PROJ_WIDTH)
    const2 = lambda i, c: (0, 0)
    row = lambda width: pl.BlockSpec((1, width), const2)
    rows_map = lambda col: (lambda i, c: (i, c, col))
    per_seq = lambda *dims: pl.BlockSpec((S * G,) + dims, lambda i, c: (i,) + (0,) * len(dims))
    carry = lambda width: pltpu.VMEM((S, SUBLANES + R, width) if G == 1 else (SUBLANES, LANES), F32)
    o, s_new = pl.pallas_call(
        functools.partial(_rwkv_kernel, seq_len=seq_len, n_seq=nseq),
        grid=(nseq // (S * G), sps),
        in_specs=[
            pl.BlockSpec((S, R, 3 * W), rows_map(COL_RKV // (3 * W))),
            pl.BlockSpec((S, R, RWKV_LORA), rows_map(COL_LORA // RWKV_LORA)),
            per_seq(1, 3 * W),
            per_seq(1, RWKV_LORA),
            per_seq(RWKV_HEADS, RWKV_HEAD, RWKV_HEAD),
            row(3 * W), row(RWKV_LORA), row(W), row(W),
            pl.BlockSpec((LANES, 2 * W), const2),
            pl.BlockSpec((RWKV_LORA_G, W), const2),
            row(W), row(W), row(W), row(W), row(W),
        ],
        out_specs=[pl.BlockSpec((S, R, W), rows_map(0)), per_seq(RWKV_HEADS, RWKV_HEAD, RWKV_HEAD)],
        out_shape=[
            jax.ShapeDtypeStruct((nseq // G, sps * R, W), BF16),
            jax.ShapeDtypeStruct((nseq, RWKV_HEADS, RWKV_HEAD, RWKV_HEAD), F32),
        ],
        scratch_shapes=[carry(3 * W), carry(RWKV_LORA),
                        pltpu.VMEM((S * G, RWKV_HEADS // 2, LANES, LANES), F32)],
        compiler_params=pltpu.CompilerParams(
            dimension_semantics=("arbitrary", "arbitrary"), vmem_limit_bytes=VMEM_LIMIT),
        name="rwkv_mixer",
    )(proj3d, proj3d, sh_rkv, sh_lora, s0, mu_rkv, mu_lora, w0, a0, wab, gb, kk, ka, rk, gnw, gnb)
    return o.reshape(m, W), s_new


def _outproj_kernel(x_ref, oa_ref, ob_ref, wa_ref, wb_ref, g_ref, x1_ref, hn_ref):
    x1 = (x_ref[...] + jnp.dot(oa_ref[...], wa_ref[...], preferred_element_type=F32)
          + jnp.dot(ob_ref[...], wb_ref[...], preferred_element_type=F32))
    x1_ref[...] = x1
    ms = jnp.mean(x1 * x1, axis=-1, keepdims=True)
    hn_ref[...] = (x1 * lax.rsqrt(ms + RMS_EPS) * g_ref[...]).astype(BF16)


def _outproj(x2d, oa, ob, wo_bf16, g_row, tm):
    m = x2d.shape[0]
    return pl.pallas_call(
        _outproj_kernel,
        grid=(m // tm,),
        in_specs=[
            pl.BlockSpec((tm, D_MODEL), lambda i: (i, 0)),
            pl.BlockSpec((tm, GDN_WIDTH), lambda i: (i, 0)),
            pl.BlockSpec((tm, RWKV_WIDTH), lambda i: (i, 0)),
            pl.BlockSpec((GDN_WIDTH, D_MODEL), lambda i: (0, 0)),
            pl.BlockSpec((RWKV_WIDTH, D_MODEL), lambda i: (1, 0)),
            pl.BlockSpec((1, D_MODEL), lambda i: (0, 0)),
        ],
        out_specs=[pl.BlockSpec((tm, D_MODEL), lambda i: (i, 0)),
                   pl.BlockSpec((tm, D_MODEL), lambda i: (i, 0))],
        out_shape=[jax.ShapeDtypeStruct((m, D_MODEL), F32),
                   jax.ShapeDtypeStruct((m, D_MODEL), BF16)],
        compiler_params=pltpu.CompilerParams(
            dimension_semantics=("arbitrary",), vmem_limit_bytes=VMEM_LIMIT),
        name="outproj",
    )(x2d, oa, ob, wo_bf16, wo_bf16, g_row)


FFN_TF = 512
FFN_NF = D_FF // FFN_TF


def _ffn_pipeline(f, up_fn, wd_ref, x1_ref, fg_ref, y_ref, acc_ref):
    @pl.when(f == 0)
    def _():
        acc_ref[...] = jnp.zeros(acc_ref.shape, F32)

    acc_ref[...] += jnp.dot(up_fn().astype(BF16), wd_ref[...], preferred_element_type=F32)

    @pl.when(f == FFN_NF - 1)
    def _():
        xo = x1_ref[...] + acc_ref[...]
        ms = jnp.mean(xo * xo, axis=-1, keepdims=True)
        y_ref[...] = xo * lax.rsqrt(ms + RMS_EPS) * fg_ref[...]


def _ffn_long_kernel(hn_ref, x1_ref, wg_ref, wu_ref, cwg_ref, cwu_ref, wd_ref, fg_ref,
                     y_ref, ng_ref, nu_ref, acc_ref, carry_ref, hbuf_ref, *, tt):
    ti = pl.program_id(1)
    f = pl.program_id(2)

    def up_fn():
        hn = hn_ref[...]
        convs = []
        for j, (w_ref, cw_ref, n_ref) in enumerate(((wg_ref, cwg_ref, ng_ref), (wu_ref, cwu_ref, nu_ref))):
            h = jnp.dot(hn, w_ref[...], preferred_element_type=F32)
            hbuf_ref[j, SUBLANES:SUBLANES + tt, :] = h
            prev = carry_ref[f, j]
            hbuf_ref[j, 0:SUBLANES, :] = jnp.where(ti == 0, jnp.zeros_like(prev), prev)
            cw = cw_ref[...]
            conv = h * cw[FFN_CONV - 1:FFN_CONV, :]
            for i in range(FFN_CONV - 1):
                off = SUBLANES - (FFN_CONV - 1) + i
                conv = conv + hbuf_ref[j, off:off + tt, :] * cw[i:i + 1, :]
            carry_ref[f, j] = h[tt - SUBLANES:, :]
            n_ref[...] = h[tt - (FFN_CONV - 1):, :]
            convs.append(conv)
        return _silu(convs[0]) * convs[1]

    _ffn_pipeline(f, up_fn, wd_ref, x1_ref, fg_ref, y_ref, acc_ref)


def _ffn_long(hn, x1, wup, cw, wdown, fg_row, tt):
    b, t, _ = hn.shape
    tf = FFN_TF
    nf = FFN_NF
    up = lambda f: f
    down = lambda f: f
    return pl.pallas_call(
        functools.partial(_ffn_long_kernel, tt=tt),
        grid=(b, t // tt, nf),
        in_specs=[
            pl.BlockSpec((None, tt, D_MODEL), lambda i, s, f: (i, s, 0)),
            pl.BlockSpec((None, tt, D_MODEL), lambda i, s, f: (i, s, 0)),
            pl.BlockSpec((D_MODEL, tf), lambda i, s, f: (0, up(f))),
            pl.BlockSpec((D_MODEL, tf), lambda i, s, f: (0, nf + up(f))),
            pl.BlockSpec((FFN_CONV, tf), lambda i, s, f: (0, up(f))),
            pl.BlockSpec((FFN_CONV, tf), lambda i, s, f: (0, nf + up(f))),
            pl.BlockSpec((tf, D_MODEL), lambda i, s, f: (down(f), 0)),
            pl.BlockSpec((1, D_MODEL), lambda i, s, f: (0, 0)),
        ],
        out_specs=[
            pl.BlockSpec((None, tt, D_MODEL), lambda i, s, f: (i, s, 0)),
            pl.BlockSpec((None, None, FFN_CONV - 1, tf), lambda i, s, f: (i, s, 0, up(f))),
            pl.BlockSpec((None, None, FFN_CONV - 1, tf), lambda i, s, f: (i, s, 0, up(f))),
        ],
        out_shape=[
            jax.ShapeDtypeStruct((b, t, D_MODEL), F32),
            jax.ShapeDtypeStruct((b, t // tt, FFN_CONV - 1, D_FF), F32),
            jax.ShapeDtypeStruct((b, t // tt, FFN_CONV - 1, D_FF), F32),
        ],
        scratch_shapes=[
            pltpu.VMEM((tt, D_MODEL), F32),
            pltpu.VMEM((nf, 2, SUBLANES, tf), F32),
            pltpu.VMEM((2, SUBLANES + tt, tf), F32),
        ],
        compiler_params=pltpu.CompilerParams(
            dimension_semantics=("arbitrary", "arbitrary", "arbitrary"),
            vmem_limit_bytes=VMEM_LIMIT),
        name="ffn_long",
    )(hn, x1, wup, wup, cw, cw, wdown, fg_row)


def _ffn_short_kernel(hn_ref, x1_ref, wg_ref, wu_ref, cwg_ref, cwu_ref, wd_ref, fg_ref,
                      b0g_ref, b1g_ref, b0u_ref, b1u_ref,
                      y_ref, n0g_ref, n1g_ref, n0u_ref, n1u_ref, acc_ref, z_ref, hb_ref, *, tt, seq):
    f = pl.program_id(1)
    nseq = tt // seq

    def up_fn():
        hn = hn_ref[...]
        t_in_seq = lax.broadcasted_iota(jnp.int32, (tt, FFN_TF), 0) % seq
        convs = []
        groups = ((wg_ref, cwg_ref, b0g_ref, b1g_ref, n0g_ref, n1g_ref),
                  (wu_ref, cwu_ref, b0u_ref, b1u_ref, n0u_ref, n1u_ref))
        for w_ref, cw_ref, b0_ref, b1_ref, n0_ref, n1_ref in groups:
            h = jnp.dot(hn, w_ref[...], preferred_element_type=F32)
            z_ref[...] = jnp.zeros(z_ref.shape, F32)
            for lb in range(FFN_TF // LANES):
                cols = slice(lb * LANES, (lb + 1) * LANES)
                z_ref[lb, pl.ds(0, nseq, stride=seq), :] = b0_ref[:, cols]
                z_ref[lb, pl.ds(1, nseq, stride=seq), :] = b1_ref[:, cols]
                hb_ref[lb] = h[:, cols]
                n0_ref[:, cols] = hb_ref[lb, pl.ds(seq - 2, nseq, stride=seq), :]
                n1_ref[:, cols] = hb_ref[lb, pl.ds(seq - 1, nseq, stride=seq), :]
            z = jnp.concatenate([z_ref[lb] for lb in range(FFN_TF // LANES)], axis=1)
            s1 = jnp.where(t_in_seq == 0, pltpu.roll(z, tt - 1, 0), pltpu.roll(h, 1, 0))
            s2 = jnp.where(t_in_seq < 2, z, pltpu.roll(h, 2, 0))
            cw = cw_ref[...]
            convs.append(h * cw[2:3, :] + s1 * cw[1:2, :] + s2 * cw[0:1, :])
        return _silu(convs[0]) * convs[1]

    _ffn_pipeline(f, up_fn, wd_ref, x1_ref, fg_ref, y_ref, acc_ref)


def _ffn_short(hn, x1, wup, cw, wdown, fg_row, buf0, buf1, tt, seq):
    m = hn.shape[0]
    tf = FFN_TF
    nf = FFN_NF
    nseq = tt // seq
    up = lambda f: f
    down = lambda f: f
    st_g = pl.BlockSpec((nseq, tf), lambda i, f: (i, up(f)))
    st_u = pl.BlockSpec((nseq, tf), lambda i, f: (i, nf + up(f)))
    new = pl.BlockSpec((nseq, tf), lambda i, f: (i, up(f)))
    new_shape = jax.ShapeDtypeStruct((m // seq, D_FF), F32)
    return pl.pallas_call(
        functools.partial(_ffn_short_kernel, tt=tt, seq=seq),
        grid=(m // tt, nf),
        in_specs=[
            pl.BlockSpec((tt, D_MODEL), lambda i, f: (i, 0)),
            pl.BlockSpec((tt, D_MODEL), lambda i, f: (i, 0)),
            pl.BlockSpec((D_MODEL, tf), lambda i, f: (0, up(f))),
            pl.BlockSpec((D_MODEL, tf), lambda i, f: (0, nf + up(f))),
            pl.BlockSpec((FFN_CONV, tf), lambda i, f: (0, up(f))),
            pl.BlockSpec((FFN_CONV, tf), lambda i, f: (0, nf + up(f))),
            pl.BlockSpec((tf, D_MODEL), lambda i, f: (down(f), 0)),
            pl.BlockSpec((1, D_MODEL), lambda i, f: (0, 0)),
            st_g, st_g, st_u, st_u,
        ],
        out_specs=[pl.BlockSpec((tt, D_MODEL), lambda i, f: (i, 0)), new, new, new, new],
        out_shape=[jax.ShapeDtypeStruct((m, D_MODEL), F32), new_shape, new_shape, new_shape, new_shape],
        scratch_shapes=[
            pltpu.VMEM((tt, D_MODEL), F32),
            pltpu.VMEM((tf // LANES, tt, LANES), F32),
            pltpu.VMEM((tf // LANES, tt, LANES), F32),
        ],
        compiler_params=pltpu.CompilerParams(
            dimension_semantics=("arbitrary", "arbitrary"), vmem_limit_bytes=VMEM_LIMIT),
        name="ffn_short",
    )(hn, x1, wup, wup, cw, cw, wdown, fg_row, buf0, buf1, buf0, buf1)


def _pad_lanes(vec, offset):
    out = jnp.zeros((LANES,), F32)
    return out.at[offset:offset + vec.shape[0]].set(vec.astype(F32))


def _trunk(x, s_gdn, s_gconv, s_rwkv, s_shift, s_ffn, prm, *, long_seq):
    b, t, _ = x.shape
    m = b * t
    x2d = x.reshape(m, D_MODEL)
    tm = min(512, m)
    proj = _inproj(x2d, prm["ln1_g"], prm["w_in"], min(1024, m))
    gconv_new = proj.reshape(b, t, PROJ_WIDTH)[:, t - (GDN_CONV - 1):, COL_QKV:COL_QKV + 3 * GDN_WIDTH]

    o_a, gdn_new = _gdn_mixer(proj, s_gconv, s_gdn, prm["gdn_conv_w"], prm["alog_r"], prm["dtb_r"],
                              prm["alog_c"], prm["dtb_c"], prm["gdn_norm_g"], t)
    sh_rkv = s_shift[:, None, :3 * RWKV_WIDTH]
    sh_lora = s_shift[:, None, 3 * RWKV_WIDTH:]
    o_b, rwkv_new = _rwkv_mixer(proj, sh_rkv, sh_lora, s_rwkv, prm["mu_rkv"], prm["mu_lora"],
                                prm["rwkv_w0"], prm["rwkv_a0"], prm["rwkv_wab"], prm["rwkv_g_b"],
                                prm["rwkv_k_k"], prm["rwkv_k_a"], prm["rwkv_r_k"], prm["rwkv_gn_w"],
                                prm["rwkv_gn_b"], t)

    x1, hn = _outproj(x2d, o_a, o_b, prm["w_o"], prm["ln2_g"], tm)
    if long_seq:
        tt = min(512, t)
        y, n_g, n_u = _ffn_long(hn.reshape(b, t, D_MODEL), x1.reshape(b, t, D_MODEL), prm["ffn_w_up"],
                                prm["ffn_conv_w"], prm["ffn_w_down"], prm["final_g"], tt)
        ffn_new = jnp.concatenate([n_g[:, -1], n_u[:, -1]], axis=-1)
    else:
        tt = min(512, m)
        y, n0g, n1g, n0u, n1u = _ffn_short(hn, x1, prm["ffn_w_up"], prm["ffn_conv_w"],
                                            prm["ffn_w_down"], prm["final_g"],
                                            s_ffn[:, 0], s_ffn[:, 1], tt, t)
        y = y.reshape(b, t, D_MODEL)
        ffn_new = jnp.stack([jnp.concatenate([n0g, n0u], axis=-1),
                             jnp.concatenate([n1g, n1u], axis=-1)], axis=1)

    proj = proj.reshape(b, t, PROJ_WIDTH)
    shift_new = jnp.concatenate([proj[:, t - 1, COL_RKV:COL_RKV + 3 * RWKV_WIDTH],
                                 proj[:, t - 1, COL_LORA:COL_LORA + RWKV_LORA]], axis=-1)
    return y, gdn_new[None], gconv_new[None], rwkv_new[None], shift_new[None], ffn_new[None]


def kernel(x_prompt, x_sample, state_gdn, state_gdn_conv, state_rwkv, state_rwkv_shift, state_ffn_conv, ln1_g, w_in, gdn_conv_w, gdn_a_log, gdn_dt_bias, gdn_norm_g, rwkv_mu, rwkv_w0, rwkv_w_b, rwkv_a0, rwkv_a_b, rwkv_g_b, rwkv_k_k, rwkv_k_a, rwkv_r_k, rwkv_gn_w, rwkv_gn_b, w_o, ln2_g, ffn_w_up, ffn_conv_w, ffn_w_down, final_g):
    assert ln1_g.shape[0] == 1, "single-layer trunk"
    w_perm = _permute_win(w_in[0])
    mu = rwkv_mu[0]
    zeros_w = jnp.zeros((RWKV_LORA_W, RWKV_WIDTH), F32)
    wab = jnp.concatenate([
        jnp.concatenate([rwkv_w_b[0], zeros_w], axis=1),
        jnp.concatenate([zeros_w, rwkv_a_b[0]], axis=1)], axis=0).astype(BF16)
    alog = _pad_lanes(gdn_a_log[0], GDN_HEADS)
    dtb = _pad_lanes(gdn_dt_bias[0], GDN_HEADS)
    prm = {
        "ln1_g": ln1_g[0][None], "w_in": w_perm, "gdn_conv_w": gdn_conv_w[0],
        "alog_r": alog[None], "dtb_r": dtb[None], "alog_c": alog[:, None], "dtb_c": dtb[:, None],
        "gdn_norm_g": gdn_norm_g[0][None],
        "mu_rkv": mu[None, :3 * RWKV_WIDTH], "mu_lora": mu[None, 3 * RWKV_WIDTH:],
        "rwkv_w0": rwkv_w0[0][None], "rwkv_a0": rwkv_a0[0][None], "rwkv_wab": wab,
        "rwkv_g_b": rwkv_g_b[0].astype(BF16), "rwkv_k_k": rwkv_k_k[0][None],
        "rwkv_k_a": rwkv_k_a[0][None], "rwkv_r_k": rwkv_r_k[0].reshape(1, RWKV_WIDTH),
        "rwkv_gn_w": rwkv_gn_w[0][None], "rwkv_gn_b": rwkv_gn_b[0][None],
        "w_o": w_o[0].astype(BF16), "ln2_g": ln2_g[0][None],
        "ffn_w_up": ffn_w_up[0].astype(BF16), "ffn_conv_w": ffn_conv_w[0],
        "ffn_w_down": ffn_w_down[0].astype(BF16), "final_g": final_g[None],
    }

    bp = x_prompt.shape[0]
    zero_states = (
        jnp.zeros((bp,) + state_gdn.shape[2:], F32),
        jnp.zeros((bp,) + state_gdn_conv.shape[2:], F32),
        jnp.zeros((bp,) + state_rwkv.shape[2:], F32),
        jnp.zeros((bp,) + state_rwkv_shift.shape[2:], F32),
        None,
    )
    outs_p = _trunk(x_prompt, *zero_states, prm, long_seq=True)
    outs_s = _trunk(x_sample, state_gdn[0], state_gdn_conv[0], state_rwkv[0], state_rwkv_shift[0],
                    state_ffn_conv[0], prm, long_seq=False)
    return (outs_p[0], outs_s[0]) + tuple(outs_p[1:]) + tuple(outs_s[1:])
```

```python
import functools

import jax
import jax.numpy as jnp
from jax import lax
from jax.experimental import pallas as pl
from jax.experimental.pallas import tpu as pltpu

F32 = jnp.float32
BF16 = jnp.bfloat16

D_MODEL = 2048
GDN_WIDTH = 1024
GDN_HEADS = 8
GDN_DK = 128
GDN_CONV = 4
RWKV_WIDTH = 1024
RWKV_HEAD = 64
RWKV_HEADS = 16
RWKV_LORA_W = 64
RWKV_LORA_A = 64
RWKV_LORA_G = 128
RWKV_LORA = RWKV_LORA_W + RWKV_LORA_A + RWKV_LORA_G
RWKV_PROJ = 3 * RWKV_WIDTH + RWKV_LORA
D_FF = 5632
FFN_CONV = 3
RMS_EPS = 1e-6
L2_EPS = 1e-12
GN_EPS = 64e-5

REF_OFF_Z = 3 * GDN_WIDTH
REF_OFF_B = 4 * GDN_WIDTH
REF_OFF_RWKV = REF_OFF_B + 2 * GDN_HEADS
REF_IN_WIDTH = REF_OFF_RWKV + RWKV_PROJ

LANES = 128
SUBLANES = 8
COL_QKV = 0
COL_RKV = 3 * GDN_WIDTH
COL_Z = COL_RKV + 3 * RWKV_WIDTH
COL_LORA = COL_Z + GDN_WIDTH
COL_BA = COL_LORA + RWKV_LORA
PROJ_WIDTH = 7680
PROJ_TN = 1280

MIX_ROWS = 64
MIX_GROUPS_LONG = 4
MIX_GROUPS_SHORT = 2

NN = (((1,), (0,)), ((), ()))
NT = (((1,), (1,)), ((), ()))
TN = (((0,), (0,)), ((), ()))

VMEM_LIMIT = 56 * 1024 * 1024
VMEM_LIMIT_FFN = 62 * 1024 * 1024

P_GRAM = "x1"
P_INV = "x1"
P_SOLVE = "x1"
P_STATE = "x1"
P_OUT = "x1"


def _split(x):
    hi = x.astype(BF16)
    return hi, (x - hi.astype(F32)).astype(BF16)


def _dot(a, b, dims=NN, mode="x1"):
    if mode == "hi":
        return lax.dot_general(a, b, dims, precision=lax.Precision.HIGHEST,
                               preferred_element_type=F32)
    if mode == "x3":
        a_hi, a_lo = _split(a)
        b_hi, b_lo = _split(b)
        d = lambda u, v: lax.dot_general(u, v, dims, preferred_element_type=F32)
        return d(a_hi, b_hi) + (d(a_hi, b_lo) + d(a_lo, b_hi))
    return lax.dot_general(a.astype(BF16), b.astype(BF16), dims, preferred_element_type=F32)


def _ones_dot(ones_mat, x, dims=NN):
    x1 = x.astype(BF16)
    r1 = x - x1.astype(F32)
    x2 = r1.astype(BF16)
    x3 = (r1 - x2.astype(F32)).astype(BF16)
    m = ones_mat.astype(BF16)
    if dims == NN:
        d = lambda v: lax.dot_general(m, v, dims, preferred_element_type=F32)
    else:
        d = lambda v: lax.dot_general(v, m, dims, preferred_element_type=F32)
    return d(x1) + (d(x2) + d(x3))


def _sigmoid(x):
    return 1.0 / (1.0 + jnp.exp(-x))


def _silu(x):
    return x * _sigmoid(x)


def _softplus(x):
    return jnp.maximum(x, 0.0) + jnp.log(1.0 + jnp.exp(-jnp.abs(x)))


def _seq_masks(rows, seq_len):
    r = lax.broadcasted_iota(jnp.int32, (rows, rows), 0)
    c = lax.broadcasted_iota(jnp.int32, (rows, rows), 1)
    if seq_len >= rows:
        return None, r >= c, r > c
    shift = seq_len.bit_length() - 1
    assert 1 << shift == seq_len
    same = jnp.right_shift(r, shift) == jnp.right_shift(c, shift)
    return same, same & (r >= c), same & (r > c)


def _wide_masks(rows, seq_len):
    r = lax.broadcasted_iota(jnp.int32, (rows, 2 * rows), 0)
    c = lax.broadcasted_iota(jnp.int32, (rows, 2 * rows), 1)
    right = c >= rows
    cc = jnp.where(right, c - rows, c)
    if seq_len >= rows:
        return r >= cc, right & (r > cc)
    shift = seq_len.bit_length() - 1
    same = jnp.right_shift(r, shift) == jnp.right_shift(cc, shift)
    return same & (r >= cc), same & right & (r > cc)


def _pair_masks(rows, seq_len):
    half = LANES // 2
    assert rows == half
    lane = lax.broadcasted_iota(jnp.int32, (rows, LANES), 1)
    row = lax.broadcasted_iota(jnp.int32, (rows, LANES), 0)
    col = jnp.bitwise_and(lane, half - 1)
    if seq_len >= rows:
        causal, strict = row >= col, row > col
    else:
        shift = seq_len.bit_length() - 1
        same = jnp.right_shift(row, shift) == jnp.right_shift(col, shift)
        causal, strict = same & (row >= col), same & (row > col)
    return causal, strict, jnp.where(row == col, 1.0, 0.0).astype(F32), lane < half


def _bd(x, first_half):
    return jnp.concatenate([jnp.where(first_half, x, 0.0), jnp.where(first_half, 0.0, x)], axis=0)


def _pair_inverses(neg_a, rows, nilpotency, eye2, first_half):
    qs = dict(neg_a)
    ts = {u: eye2 + q for u, q in qs.items()}
    n = 2
    if n < nilpotency:
        qs = {u: _dot(q, _bd(q, first_half), mode=P_INV) for u, q in qs.items()}
    while n < nilpotency:
        if 2 * n < nilpotency:
            tq = {u: _dot(jnp.concatenate([ts[u], qs[u]], axis=0), _bd(qs[u], first_half), mode=P_INV)
                  for u in qs}
            ts = {u: ts[u] + tq[u][:rows] for u in qs}
            qs = {u: tq[u][rows:] for u in qs}
        else:
            ts = {u: ts[u] + _dot(ts[u], _bd(qs[u], first_half), mode=P_INV) for u in qs}
        n *= 2
    return ts


def _lane_group_sums(x, group):
    tile = 2 * LANES
    shift = group.bit_length() - 1
    li = jnp.right_shift(lax.broadcasted_iota(jnp.int32, (tile, tile), 0), shift)
    lj = jnp.right_shift(lax.broadcasted_iota(jnp.int32, (tile, tile), 1), shift)
    ones = jnp.where(li == lj, 1.0, 0.0).astype(BF16)
    hi, lo = _split(x)
    d = lambda u: lax.dot_general(u, ones, NN, preferred_element_type=F32)
    return jnp.concatenate([d(hi[:, t:t + tile]) + d(lo[:, t:t + tile])
                            for t in range(0, x.shape[1], tile)], axis=1)


def _unit_lower_inverses(mats, rows, nilpotency):
    r = lax.broadcasted_iota(jnp.int32, (rows, rows), 0)
    c = lax.broadcasted_iota(jnp.int32, (rows, rows), 1)
    eye = jnp.where(r == c, 1.0, 0.0).astype(F32)
    qs = [-a for a in mats]
    ts = [eye + q for q in qs]
    n = 2
    if n < nilpotency:
        qs = [_dot(q, q, mode=P_INV) for q in qs]
    while n < nilpotency:
        if 2 * n < nilpotency:
            tq = [_dot(jnp.concatenate([t, q], axis=0), q, mode=P_INV) for t, q in zip(ts, qs)]
            ts = [t + p[:rows] for t, p in zip(ts, tq)]
            qs = [p[rows:] for p in tq]
        else:
            ts = [t + _dot(t, q, mode=P_INV) for t, q in zip(ts, qs)]
        n *= 2
    return ts


def _permute_win_kernel(w_ref, o_ref):
    rw = REF_OFF_RWKV
    rows = w_ref.shape[0]
    cast = lambda lo, hi: w_ref[:, lo:hi].astype(BF16)
    o_ref[:, COL_QKV:COL_RKV] = cast(0, REF_OFF_Z)
    o_ref[:, COL_RKV:COL_Z] = cast(rw, rw + 3 * RWKV_WIDTH)
    o_ref[:, COL_Z:COL_LORA] = cast(REF_OFF_Z, REF_OFF_B)
    o_ref[:, COL_LORA:COL_BA] = cast(rw + 3 * RWKV_WIDTH, REF_IN_WIDTH)
    tail = jnp.concatenate([w_ref[:, REF_OFF_B:REF_OFF_RWKV],
                            jnp.zeros((rows, PROJ_WIDTH - COL_BA - 2 * GDN_HEADS), F32)], axis=1)
    o_ref[:, COL_BA:] = tail.astype(BF16)


def _permute_win(w, tr=256):
    return pl.pallas_call(
        _permute_win_kernel,
        grid=(D_MODEL // tr,),
        in_specs=[pl.BlockSpec((tr, REF_IN_WIDTH), lambda i: (i, 0))],
        out_specs=pl.BlockSpec((tr, PROJ_WIDTH), lambda i: (i, 0)),
        out_shape=jax.ShapeDtypeStruct((D_MODEL, PROJ_WIDTH), BF16),
        compiler_params=pltpu.CompilerParams(
            dimension_semantics=("arbitrary",), vmem_limit_bytes=VMEM_LIMIT),
        name="permute_win",
    )(w)


def _inproj_kernel(x_ref, g_ref, w_ref, o_ref, xn_ref):
    @pl.when(pl.program_id(1) == 0)
    def _():
        x = x_ref[...]
        ms = jnp.mean(x * x, axis=-1, keepdims=True)
        xn_ref[...] = (x * lax.rsqrt(ms + RMS_EPS) * g_ref[...]).astype(BF16)

    o_ref[...] = jnp.dot(xn_ref[...], w_ref[...], preferred_element_type=F32)


def _inproj(x2d, g_row, w_bf16, tm):
    m = x2d.shape[0]
    return pl.pallas_call(
        _inproj_kernel,
        grid=(m // tm, PROJ_WIDTH // PROJ_TN),
        in_specs=[
            pl.BlockSpec((tm, D_MODEL), lambda i, j: (i, 0)),
            pl.BlockSpec((1, D_MODEL), lambda i, j: (0, 0)),
            pl.BlockSpec((D_MODEL, PROJ_TN), lambda i, j: (0, j)),
        ],
        out_specs=pl.BlockSpec((tm, PROJ_TN), lambda i, j: (i, j)),
        out_shape=jax.ShapeDtypeStruct((m, PROJ_WIDTH), F32),
        scratch_shapes=[pltpu.VMEM((tm, D_MODEL), BF16)],
        compiler_params=pltpu.CompilerParams(
            dimension_semantics=("arbitrary", "arbitrary"), vmem_limit_bytes=VMEM_LIMIT),
        name="inproj",
    )(x2d, g_row, w_bf16)


def _mixer_geometry(seq_len, n_seq, short_groups=1):
    rows = MIX_ROWS
    length = min(seq_len, rows)
    assert rows % length == 0 and seq_len % length == 0 and length % SUBLANES == 0
    per_group = rows // length
    groups = MIX_GROUPS_LONG if per_group == 1 else short_groups
    while n_seq % (groups * per_group):
        groups //= 2
    assert groups >= 1
    return rows, length, per_group, groups, seq_len // length


def _gdn_kernel(qkv_ref, z_ref, ba_ref, cbuf_ref, s0_ref, convw_ref, alog_r_ref, dtb_r_ref,
                alog_c_ref, dtb_c_ref, ng_ref, o_ref, s_ref, xp_ref, *, seq_len, n_seq):
    R, L, G, S, _ = _mixer_geometry(seq_len, n_seq, MIX_GROUPS_SHORT)
    RT = S * R
    c = pl.program_id(1)
    width = 3 * GDN_WIDTH
    hist = GDN_CONV - 1
    cw = convw_ref[...]
    groups = range(S)
    seqs = range(G)

    @pl.when(c == 0)
    def _():
        s_ref[...] = s0_ref[...]

    pieces = []
    if G == 1:
        @pl.when(c == 0)
        def _():
            for s in groups:
                xp_ref[s, 0:SUBLANES, :] = jnp.zeros((SUBLANES, width), F32)
                xp_ref[s, SUBLANES - hist:SUBLANES, :] = cbuf_ref[s]

        @pl.when(c > 0)
        def _():
            for s in groups:
                xp_ref[s, 0:SUBLANES, :] = xp_ref[s, R:R + SUBLANES, :]

        for s in groups:
            xp_ref[s, SUBLANES:SUBLANES + R, :] = qkv_ref[s]
            piece = qkv_ref[s] * cw[hist:hist + 1, :]
            for i in range(hist):
                off = SUBLANES - hist + i
                piece = piece + xp_ref[s, off:off + R, :] * cw[i:i + 1, :]
            pieces.append(piece)
    else:
        for s in groups:
            for g in seqs:
                q = s * G + g
                rows = slice(g * L, (g + 1) * L)
                xp_ref[q, SUBLANES - hist:SUBLANES, :] = cbuf_ref[q]
                xp_ref[q, SUBLANES:SUBLANES + L, :] = qkv_ref[s, rows, :]
                piece = qkv_ref[s, rows, :] * cw[hist:hist + 1, :]
                for i in range(hist):
                    off = SUBLANES - hist + i
                    piece = piece + xp_ref[q, off:off + L, :] * cw[i:i + 1, :]
                pieces.append(piece)
    qkv = _silu(pieces[0] if len(pieces) == 1 else jnp.concatenate(pieces, axis=0))

    same_t, causal_t, _ = _seq_masks(RT, L)
    causal01 = jnp.where(causal_t, 1.0, 0.0).astype(F32)

    ba = ba_ref[...].reshape(RT, LANES)
    ba_t = ba.T
    beta_c = _sigmoid(ba)
    g_c = -jnp.exp(alog_r_ref[...]) * _softplus(ba + dtb_r_ref[...])
    g_r = -jnp.exp(alog_c_ref[...]) * _softplus(ba_t + dtb_c_ref[...])
    gc_all = _ones_dot(causal01, g_c)
    gr_all = _ones_dot(causal01, g_r, NT)
    if same_t is None:
        gtot_all = jnp.broadcast_to(gc_all[RT - 1:RT, :], (RT, LANES))
    else:
        gtot_all = _ones_dot(jnp.where(same_t, 1.0, 0.0).astype(F32), g_c)

    qk_raw = qkv[:, :2 * GDN_WIDTH]
    qk_n = qk_raw * lax.rsqrt(_lane_group_sums(qk_raw * qk_raw, GDN_DK) + L2_EPS)
    q_all = qk_n[:, :GDN_WIDTH] * (GDN_DK ** -0.5)
    k_all = qk_n[:, GDN_WIDTH:]
    v_all = qkv[:, 2 * GDN_WIDTH:]

    causal2, strict2, eye2, first_half = _pair_masks(R, L)
    first_head = lax.broadcasted_iota(jnp.int32, (R, 2 * GDN_DK), 1) < GDN_DK
    zeros_u = jnp.zeros((R, GDN_DK), F32)
    zeros_rhs = jnp.zeros((R, 2 * GDN_DK), F32)
    chains = [(s, h) for s in groups for h in range(GDN_HEADS)]
    units = [(s, p) for s in groups for p in range(GDN_HEADS // 2)]
    seq_rows = [slice(g * L, (g + 1) * L) for g in seqs]
    qs, ks, vs, betas, gcols, gtots = {}, {}, {}, {}, {}, {}
    for s, h in chains:
        rs = slice(s * R, (s + 1) * R)
        lo = h * GDN_DK
        key = (s, h)
        qs[key] = q_all[rs, lo:lo + GDN_DK]
        ks[key] = k_all[rs, lo:lo + GDN_DK]
        vs[key] = v_all[rs, lo:lo + GDN_DK]
        betas[key] = beta_c[rs, h:h + 1]
        gcols[key] = gc_all[rs, GDN_HEADS + h:GDN_HEADS + h + 1]
        gtots[key] = gtot_all[rs, GDN_HEADS + h:GDN_HEADS + h + 1]
    kq2, decay2, a2 = {}, {}, {}
    for s, p in units:
        rs = slice(s * R, (s + 1) * R)
        cols = slice(2 * p * GDN_DK, (2 * p + 2) * GDN_DK)
        k_pair = k_all[rs, cols]
        kt_bd = jnp.concatenate([jnp.where(first_head, k_pair, 0.0),
                                 jnp.where(first_head, 0.0, k_pair)], axis=0).T
        kq = _dot(jnp.concatenate([k_pair, q_all[rs, cols]], axis=0), kt_bd, mode=P_GRAM)
        h0, h1 = (s, 2 * p), (s, 2 * p + 1)
        gcol2 = jnp.where(first_half, gcols[h0], gcols[h1])
        grow2 = jnp.concatenate([gr_all[GDN_HEADS + 2 * p:GDN_HEADS + 2 * p + 1, rs],
                                 gr_all[GDN_HEADS + 2 * p + 1:GDN_HEADS + 2 * p + 2, rs]], axis=1)
        dec = jnp.where(causal2, jnp.exp(jnp.where(causal2, gcol2 - grow2, 0.0)), 0.0)
        beta2 = jnp.where(first_half, betas[h0], betas[h1])
        kq2[(s, p)] = kq
        decay2[(s, p)] = dec
        a2[(s, p)] = jnp.where(strict2, -(beta2 * kq[:R] * dec), 0.0)
    t2 = _pair_inverses(a2, R, L, eye2, first_half)
    gammas = {key: jnp.exp(gcols[key]) for key in chains}

    def stacked(key, x, zeros):
        return jnp.concatenate([x, zeros] if key[1] % 2 == 0 else [zeros, x], axis=0)

    sols = {key: _dot(t2[(key[0], key[1] // 2)],
                      stacked(key, jnp.concatenate([(betas[key] * gammas[key]) * ks[key],
                                                    betas[key] * vs[key]], axis=1), zeros_rhs),
                      mode=P_SOLVE) for key in chains}
    states = {(s, h): [s_ref[s * G + g, h] for g in seqs] for s, h in chains}
    wss = {key: [_dot(jnp.concatenate([sols[key][rows, :GDN_DK], (qs[key] * gammas[key])[rows]], axis=0),
                      states[key][g], mode=P_STATE) for g, rows in enumerate(seq_rows)]
           for key in chains}
    us = {key: jnp.concatenate([sols[key][rows, GDN_DK:] - wss[key][g][:L]
                                for g, rows in enumerate(seq_rows)], axis=0) for key in chains}
    qk2 = {u: kq2[u][R:] * decay2[u] for u in units}
    outs = {key: jnp.concatenate([wss[key][g][L:] for g in seqs], axis=0)
            + _dot(qk2[(key[0], key[1] // 2)], stacked(key, us[key], zeros_u), mode=P_OUT)
            for key in chains}
    for s, h in chains:
        key = (s, h)
        kt = ks[key] * jnp.exp(gtots[key] - gcols[key])
        for g, rows in enumerate(seq_rows):
            gl = jnp.exp(gtots[key][g * L:g * L + 1, :])
            s_ref[s * G + g, h] = gl * states[key][g] + _dot(kt[rows], us[key][rows], TN, mode=P_STATE)
    ng = jnp.concatenate([ng_ref[...]] * GDN_HEADS, axis=1)
    for s in groups:
        o = jnp.concatenate([outs[(s, h)] for h in range(GDN_HEADS)], axis=1)
        ms = _lane_group_sums(o * o, GDN_DK) * (1.0 / GDN_DK)
        o = o * lax.rsqrt(ms + RMS_EPS) * ng
        o_ref[s] = (o * _silu(z_ref[s])).astype(o_ref.dtype)


def _gdn_mixer(proj2d, cbuf, s0, convw, alog_r, dtb_r, alog_c, dtb_c, ng, seq_len):
    m = proj2d.shape[0]
    nseq = m // seq_len
    R, L, G, S, sps = _mixer_geometry(seq_len, nseq, MIX_GROUPS_SHORT)
    width = 3 * GDN_WIDTH
    proj3d = proj2d.reshape(nseq // G, sps * R, PROJ_WIDTH)
    const2 = lambda i, c: (0, 0)
    rows_map = lambda col: (lambda i, c: (i, c, col))
    per_seq = lambda *dims: pl.BlockSpec((S * G,) + dims, lambda i, c: (i,) + (0,) * len(dims))
    xp_shape = (S, SUBLANES + R, width) if G == 1 else (S * G, SUBLANES + L, width)
    o, s_new = pl.pallas_call(
        functools.partial(_gdn_kernel, seq_len=seq_len, n_seq=nseq),
        grid=(nseq // (S * G), sps),
        in_specs=[
            pl.BlockSpec((S, R, width), rows_map(COL_QKV // width)),
            pl.BlockSpec((S, R, GDN_WIDTH), rows_map(COL_Z // GDN_WIDTH)),
            pl.BlockSpec((S, R, LANES), rows_map(COL_BA // LANES)),
            per_seq(GDN_CONV - 1, width),
            per_seq(GDN_HEADS, GDN_DK, GDN_DK),
            pl.BlockSpec((GDN_CONV, width), const2),
            pl.BlockSpec((1, LANES), const2),
            pl.BlockSpec((1, LANES), const2),
            pl.BlockSpec((LANES, 1), const2),
            pl.BlockSpec((LANES, 1), const2),
            pl.BlockSpec((1, GDN_DK), const2),
        ],
        out_specs=[
            pl.BlockSpec((S, R, GDN_WIDTH), rows_map(0)),
            per_seq(GDN_HEADS, GDN_DK, GDN_DK),
        ],
        out_shape=[
            jax.ShapeDtypeStruct((nseq // G, sps * R, GDN_WIDTH), BF16),
            jax.ShapeDtypeStruct((nseq, GDN_HEADS, GDN_DK, GDN_DK), F32),
        ],
        scratch_shapes=[pltpu.VMEM(xp_shape, F32)],
        compiler_params=pltpu.CompilerParams(
            dimension_semantics=("arbitrary", "arbitrary"), vmem_limit_bytes=VMEM_LIMIT),
        name="gdn_mixer",
    )(proj3d, proj3d, proj3d, cbuf, s0, convw, alog_r, dtb_r, alog_c, dtb_c, ng)
    return o.reshape(m, GDN_WIDTH), s_new


def _shifted_rows(x_ref, prev_ref, carry_ref, c, R, L, G, S):
    width = x_ref.shape[-1]
    groups = range(S)
    if G == 1:
        @pl.when(c == 0)
        def _():
            for s in groups:
                carry_ref[s, 0:SUBLANES, :] = jnp.zeros((SUBLANES, width), F32)
                carry_ref[s, SUBLANES - 1:SUBLANES, :] = prev_ref[s]

        @pl.when(c > 0)
        def _():
            for s in groups:
                carry_ref[s, 0:SUBLANES, :] = carry_ref[s, R:R + SUBLANES, :]

        xs, prevs = [], []
        for s in groups:
            carry_ref[s, SUBLANES:SUBLANES + R, :] = x_ref[s]
            xs.append(x_ref[s])
            prevs.append(carry_ref[s, SUBLANES - 1:SUBLANES - 1 + R, :])
    else:
        row = lax.broadcasted_iota(jnp.int32, (L, width), 0)
        xs, prevs = [], []
        for s in groups:
            x = x_ref[s]
            xs.append(x)
            for g in range(G):
                xg = x[g * L:(g + 1) * L]
                prevs.append(jnp.where(row == 0, prev_ref[s * G + g], pltpu.roll(xg, 1, 0)))
    cat = lambda parts: parts[0] if len(parts) == 1 else jnp.concatenate(parts, axis=0)
    return cat(xs), cat(prevs)


def _rwkv_kernel(rkv_ref, lora_ref, sh_rkv_ref, sh_lora_ref, s0_ref, mu_rkv_ref, mu_lora_ref,
                 w0_ref, a0_ref, wab_ref, gb_ref, kk_ref, ka_ref, rk_ref, gnw_ref, gnb_ref,
                 o_ref, s_ref, xr_ref, xl_ref, m_ref, *, seq_len, n_seq):
    R, L, G, S, sps = _mixer_geometry(seq_len, n_seq)
    RT = S * R
    NQ = S * G
    c = pl.program_id(1)
    W = RWKV_WIDTH
    HD = RWKV_HEAD
    pairs = range(RWKV_HEADS // 2)
    groups = range(S)
    seqs = range(G)

    @pl.when(c == 0)
    def _():
        zero = jnp.zeros((HD, HD), F32)
        for q in range(NQ):
            for j in pairs:
                vk = jnp.concatenate([jnp.concatenate([s0_ref[q, 2 * j], zero], axis=1),
                                      jnp.concatenate([zero, s0_ref[q, 2 * j + 1]], axis=1)], axis=0)
                m_ref[q, j] = vk.T

    p, p_prev = _shifted_rows(rkv_ref, sh_rkv_ref, xr_ref, c, R, L, G, S)
    xs = p + (p_prev - p) * mu_rkv_ref[...]
    pl_, pl_prev = _shifted_rows(lora_ref, sh_lora_ref, xl_ref, c, R, L, G, S)
    xl = pl_ + (pl_prev - pl_) * mu_lora_ref[...]
    r = xs[:, :W]
    k = xs[:, W:2 * W]
    v = xs[:, 2 * W:]

    wa_in = xl[:, :LANES]
    lane = lax.broadcasted_iota(jnp.int32, (RT, LANES), 1)
    wa_in = jnp.where(lane < RWKV_LORA_W, jnp.tanh(wa_in), wa_in)
    wa = _dot(wa_in, wab_ref[...])
    w = -_softplus(-(w0_ref[...] + wa[:, :W])) - 0.5
    lw = -jnp.exp(w)
    a = _sigmoid(a0_ref[...] + wa[:, W:])
    gate = _dot(_sigmoid(xl[:, LANES:]), gb_ref[...])
    kk_raw = k * kk_ref[...]
    k2 = k * (1.0 + (a - 1.0) * ka_ref[...])

    same_t, causal_t, _ = _seq_masks(RT, L)
    lc = _ones_dot(jnp.where(causal_t, 1.0, 0.0).astype(F32), lw)
    if G == 1:
        ltot = jnp.concatenate([jnp.broadcast_to(lc[(s + 1) * R - 1:(s + 1) * R, :], (R, W))
                                for s in groups], axis=0) if S > 1 else \
            jnp.broadcast_to(lc[R - 1:R, :], (R, W))
    else:
        ltot = _ones_dot(jnp.where(same_t, 1.0, 0.0).astype(F32), lw)
    e_inv = jnp.exp(-lc)
    e_rem = jnp.exp(ltot - lc)

    kk = kk_raw * lax.rsqrt(_lane_group_sums(kk_raw * kk_raw, HD) + L2_EPS)
    kka = kk * a
    ct = kk * jnp.exp(lc - lw)
    rt = r * jnp.exp(lc)
    bh = kka * e_inv
    kh = k2 * e_inv
    bb = kka * e_rem
    kb = k2 * e_rem
    p_rows = [ltot[q * L:q * L + 1] for q in range(NQ)]
    p_rows = p_rows + [p_rows[0]] * (-NQ % SUBLANES)
    pt = jnp.exp(jnp.concatenate(p_rows, axis=0)).T

    causal2, strict2, eye2, even_half = _pair_masks(R, L)
    block_diag = ((lax.broadcasted_iota(jnp.int32, (LANES, LANES), 0) < HD)
                  == (lax.broadcasted_iota(jnp.int32, (LANES, LANES), 1) < HD))
    even_all = jnp.bitwise_and(lax.broadcasted_iota(jnp.int32, (R, W), 1), LANES - 1) < HD
    tile = lambda j: slice(j * LANES, (j + 1) * LANES)
    bd = lambda x: _bd(x, even_half)

    if G > 1:
        row2 = lax.broadcasted_iota(jnp.int32, (2 * R, LANES), 0)
        seq_of_row = jnp.right_shift(jnp.bitwise_and(row2, R - 1), L.bit_length() - 1)

    units = [(s, j) for s in groups for j in pairs]
    xps, vps, bdb, bdk, bkts = {}, {}, {}, {}, {}
    for s in groups:
        rs = slice(s * R, (s + 1) * R)
        x_s = jnp.concatenate([ct[rs], rt[rs]], axis=0)
        bt_s = jnp.concatenate([jnp.where(even_all, bh[rs], 0.0), jnp.where(even_all, 0.0, bh[rs])],
                               axis=0).T
        kt_s = jnp.concatenate([jnp.where(even_all, kh[rs], 0.0), jnp.where(even_all, 0.0, kh[rs])],
                               axis=0).T
        bkt_s = jnp.concatenate([bb[rs], kb[rs]], axis=0).T
        for j in pairs:
            xps[(s, j)] = x_s[:, tile(j)]
            vps[(s, j)] = v[rs, tile(j)]
            bdb[(s, j)] = bt_s[tile(j), :]
            bdk[(s, j)] = kt_s[tile(j), :]
            bkts[(s, j)] = bkt_s[tile(j), :]
    gb = {u: _dot(xps[u], bdb[u], mode=P_GRAM) for u in units}
    gk = {u: _dot(xps[u], bdk[u], mode=P_GRAM) for u in units}
    ms = {(s, j): [m_ref[s * G + g, j] for g in seqs] for s, j in units}
    xm_c, xm_r = {}, {}
    for u in units:
        if G == 1:
            xm = _dot(xps[u], ms[u][0], mode=P_STATE)
            xm_c[u], xm_r[u] = xm[:R], xm[R:]
        else:
            parts = [_dot(jnp.concatenate([xps[u][g * L:(g + 1) * L],
                                           xps[u][R + g * L:R + (g + 1) * L]], axis=0),
                          ms[u][g], mode=P_STATE) for g in seqs]
            xm_c[u] = jnp.concatenate([p_[:L] for p_ in parts], axis=0)
            xm_r[u] = jnp.concatenate([p_[L:] for p_ in parts], axis=0)
    ts = _pair_inverses({u: jnp.where(strict2, -gb[u][:R], 0.0) for u in units}, R, L, eye2, even_half)
    bdv = {u: bd(vps[u]) for u in units}
    akvs = {u: _dot(jnp.where(strict2, gk[u][:R], 0.0), bdv[u], mode=P_OUT) for u in units}
    us = {u: _dot(ts[u], bd(-(xm_c[u] + akvs[u])), mode=P_SOLVE) for u in units}
    uvs = {u: jnp.concatenate([us[u], vps[u]], axis=0) for u in units}
    ys = {u: xm_r[u] + _dot(
        jnp.concatenate([jnp.where(causal2, gb[u][R:], 0.0), jnp.where(causal2, gk[u][R:], 0.0)], axis=1),
        jnp.concatenate([bd(us[u]), bdv[u]], axis=0), mode=P_OUT) for u in units}
    for s, j in units:
        u = (s, j)
        for g in seqs:
            q = s * G + g
            uv_g = uvs[u] if G == 1 else jnp.where(seq_of_row == g, uvs[u], 0.0)
            upd = _dot(bkts[u], uv_g, mode=P_STATE)
            m_ref[q, j] = pt[tile(j), q:q + 1] * ms[u][g] + jnp.where(block_diag, upd, 0.0)

    y_rows = [jnp.concatenate([ys[(s, j)] for j in pairs], axis=1) for s in groups]
    y_all = y_rows[0] if S == 1 else jnp.concatenate(y_rows, axis=0)
    mean = _lane_group_sums(y_all, HD) * (1.0 / HD)
    yc = y_all - mean
    var = _lane_group_sums(yc * yc, HD) * (1.0 / HD)
    yn = yc * lax.rsqrt(var + GN_EPS) * gnw_ref[...] + gnb_ref[...]
    yn = yn + _lane_group_sums(r * k2 * rk_ref[...], HD) * v
    o_ref[...] = (yn * gate).reshape(S, R, W).astype(o_ref.dtype)

    @pl.when(c == sps - 1)
    def _():
        for q in range(NQ):
            for j in pairs:
                vk = m_ref[q, j].T
                s_ref[q, 2 * j] = vk[:HD, :HD]
                s_ref[q, 2 * j + 1] = vk[HD:, HD:]


def _rwkv_mixer(proj2d, sh_rkv, sh_lora, s0, mu_rkv, mu_lora, w0, a0, wab, gb, kk, ka, rk, gnw, gnb,
                seq_len):
    m = proj2d.shape[0]
    nseq = m // seq_len
    R, L, G, S, sps = _mixer_geometry(seq_len, nseq)
    W = RWKV_WIDTH
    proj3d = proj2d.reshape(nseq // G, sps * R, PROJ_WIDTH)
    const2 = lambda i, c: (0, 0)
    row = lambda width: pl.BlockSpec((1, width), const2)
    rows_map = lambda col: (lambda i, c: (i, c, col))
    per_seq = lambda *dims: pl.BlockSpec((S * G,) + dims, lambda i, c: (i,) + (0,) * len(dims))
    carry = lambda width: pltpu.VMEM((S, SUBLANES + R, width) if G == 1 else (SUBLANES, LANES), F32)
    o, s_new = pl.pallas_call(
        functools.partial(_rwkv_kernel, seq_len=seq_len, n_seq=nseq),
        grid=(nseq // (S * G), sps),
        in_specs=[
            pl.BlockSpec((S, R, 3 * W), rows_map(COL_RKV // (3 * W))),
            pl.BlockSpec((S, R, RWKV_LORA), rows_map(COL_LORA // RWKV_LORA)),
            per_seq(1, 3 * W),
            per_seq(1, RWKV_LORA),
            per_seq(RWKV_HEADS, RWKV_HEAD, RWKV_HEAD),
            row(3 * W), row(RWKV_LORA), row(W), row(W),
            pl.BlockSpec((LANES, 2 * W), const2),
            pl.BlockSpec((RWKV_LORA_G, W), const2),
            row(W), row(W), row(W), row(W), row(W),
        ],
        out_specs=[pl.BlockSpec((S, R, W), rows_map(0)), per_seq(RWKV_HEADS, RWKV_HEAD, RWKV_HEAD)],
        out_shape=[
            jax.ShapeDtypeStruct((nseq // G, sps * R, W), BF16),
            jax.ShapeDtypeStruct((nseq, RWKV_HEADS, RWKV_HEAD, RWKV_HEAD), F32),
        ],
        scratch_shapes=[carry(3 * W), carry(RWKV_LORA),
                        pltpu.VMEM((S * G, RWKV_HEADS // 2, LANES, LANES), F32)],
        compiler_params=pltpu.CompilerParams(
            dimension_semantics=("arbitrary", "arbitrary"), vmem_limit_bytes=VMEM_LIMIT),
        name="rwkv_mixer",
    )(proj3d, proj3d, sh_rkv, sh_lora, s0, mu_rkv, mu_lora, w0, a0, wab, gb, kk, ka, rk, gnw, gnb)
    return o.reshape(m, W), s_new


def _outproj_kernel(x_ref, oa_ref, ob_ref, wa_ref, wb_ref, g_ref, x1_ref, hn_ref):
    x1 = (x_ref[...] + jnp.dot(oa_ref[...], wa_ref[...], preferred_element_type=F32)
          + jnp.dot(ob_ref[...], wb_ref[...], preferred_element_type=F32))
    x1_ref[...] = x1
    ms = jnp.mean(x1 * x1, axis=-1, keepdims=True)
    hn_ref[...] = (x1 * lax.rsqrt(ms + RMS_EPS) * g_ref[...]).astype(BF16)


def _outproj(x2d, oa, ob, wo_bf16, g_row, tm):
    m = x2d.shape[0]
    return pl.pallas_call(
        _outproj_kernel,
        grid=(m // tm,),
        in_specs=[
            pl.BlockSpec((tm, D_MODEL), lambda i: (i, 0)),
            pl.BlockSpec((tm, GDN_WIDTH), lambda i: (i, 0)),
            pl.BlockSpec((tm, RWKV_WIDTH), lambda i: (i, 0)),
            pl.BlockSpec((GDN_WIDTH, D_MODEL), lambda i: (0, 0)),
            pl.BlockSpec((RWKV_WIDTH, D_MODEL), lambda i: (1, 0)),
            pl.BlockSpec((1, D_MODEL), lambda i: (0, 0)),
        ],
        out_specs=[pl.BlockSpec((tm, D_MODEL), lambda i: (i, 0)),
                   pl.BlockSpec((tm, D_MODEL), lambda i: (i, 0))],
        out_shape=[jax.ShapeDtypeStruct((m, D_MODEL), F32),
                   jax.ShapeDtypeStruct((m, D_MODEL), BF16)],
        compiler_params=pltpu.CompilerParams(
            dimension_semantics=("arbitrary",), vmem_limit_bytes=VMEM_LIMIT),
        name="outproj",
    )(x2d, oa, ob, wo_bf16, wo_bf16, g_row)


FFN_TF = 512
FFN_NF = D_FF // FFN_TF


def _ffn_pipeline(f, up_fn, wd_ref, x1_ref, fg_ref, y_ref, acc_ref):
    @pl.when(f == 0)
    def _():
        acc_ref[...] = jnp.zeros(acc_ref.shape, F32)

    acc_ref[...] += jnp.dot(up_fn().astype(BF16), wd_ref[...], preferred_element_type=F32)

    @pl.when(f == FFN_NF - 1)
    def _():
        xo = x1_ref[...] + acc_ref[...]
        ms = jnp.mean(xo * xo, axis=-1, keepdims=True)
        y_ref[...] = xo * lax.rsqrt(ms + RMS_EPS) * fg_ref[...]


def _ffn_long_kernel(hn_ref, x1_ref, wg0_ref, wu0_ref, cwg0_ref, cwu0_ref, wd0_ref,
                     wg1_ref, wu1_ref, cwg1_ref, cwu1_ref, wd1_ref, fg_ref,
                     y_ref, n_ref, acc_ref, carry_ref, hbuf_ref, *, tt):
    ti = pl.program_id(1)
    f = pl.program_id(2)
    steps = pl.num_programs(2)
    hn = hn_ref[...]

    def act(slot, chunk, wg_ref, wu_ref, cwg_ref, cwu_ref):
        convs = []
        for j, (w_ref, cw_ref) in enumerate(((wg_ref, cwg_ref), (wu_ref, cwu_ref))):
            h = jnp.dot(hn, w_ref[...], preferred_element_type=F32)
            hbuf_ref[slot, j, SUBLANES:SUBLANES + tt, :] = h
            prev = carry_ref[chunk, j]
            hbuf_ref[slot, j, 0:SUBLANES, :] = jnp.where(ti == 0, jnp.zeros_like(prev), prev)
            cw = cw_ref[...]
            conv = h * cw[FFN_CONV - 1:FFN_CONV, :]
            for i in range(FFN_CONV - 1):
                off = SUBLANES - (FFN_CONV - 1) + i
                conv = conv + hbuf_ref[slot, j, off:off + tt, :] * cw[i:i + 1, :]
            carry_ref[chunk, j] = h[tt - SUBLANES:, :]
            n_ref[j, :, pl.ds(pl.multiple_of(chunk * FFN_TF, FFN_TF), FFN_TF)] = h[tt - (FFN_CONV - 1):, :]
            convs.append(conv)
        return (_silu(convs[0]) * convs[1]).astype(BF16)

    @pl.when(f == 0)
    def _():
        acc_ref[...] = jnp.zeros(acc_ref.shape, F32)

    def both():
        a0 = act(0, 2 * f, wg0_ref, wu0_ref, cwg0_ref, cwu0_ref)
        a1 = act(1, 2 * f + 1, wg1_ref, wu1_ref, cwg1_ref, cwu1_ref)
        acc_ref[...] += (jnp.dot(a0, wd0_ref[...], preferred_element_type=F32)
                         + jnp.dot(a1, wd1_ref[...], preferred_element_type=F32))

    if FFN_NF % 2 == 0:
        both()
    else:
        pl.when(f < steps - 1)(both)

        @pl.when(f == steps - 1)
        def _():
            a0 = act(0, 2 * f, wg0_ref, wu0_ref, cwg0_ref, cwu0_ref)
            acc_ref[...] += jnp.dot(a0, wd0_ref[...], preferred_element_type=F32)

    @pl.when(f == steps - 1)
    def _():
        xo = x1_ref[...] + acc_ref[...]
        ms = jnp.mean(xo * xo, axis=-1, keepdims=True)
        y_ref[...] = xo * lax.rsqrt(ms + RMS_EPS) * fg_ref[...]


def _ffn_long(hn, x1, wup, cw, wdown, fg_row, tt):
    b, t, _ = hn.shape
    tf = FFN_TF
    nf = FFN_NF
    steps = -(-nf // 2)
    chunk = (lambda f: 2 * f, lambda f: jnp.minimum(2 * f + 1, nf - 1))
    weights = []
    for k in range(2):
        weights += [
            pl.BlockSpec((D_MODEL, tf), lambda i, s, f, k=k: (0, chunk[k](f))),
            pl.BlockSpec((D_MODEL, tf), lambda i, s, f, k=k: (0, nf + chunk[k](f))),
            pl.BlockSpec((FFN_CONV, tf), lambda i, s, f, k=k: (0, chunk[k](f))),
            pl.BlockSpec((FFN_CONV, tf), lambda i, s, f, k=k: (0, nf + chunk[k](f))),
            pl.BlockSpec((tf, D_MODEL), lambda i, s, f, k=k: (chunk[k](f), 0)),
        ]
    return pl.pallas_call(
        functools.partial(_ffn_long_kernel, tt=tt),
        grid=(b, t // tt, steps),
        in_specs=[
            pl.BlockSpec((None, tt, D_MODEL), lambda i, s, f: (i, s, 0)),
            pl.BlockSpec((None, tt, D_MODEL), lambda i, s, f: (i, s, 0)),
        ] + weights + [pl.BlockSpec((1, D_MODEL), lambda i, s, f: (0, 0))],
        out_specs=[
            pl.BlockSpec((None, tt, D_MODEL), lambda i, s, f: (i, s, 0)),
            pl.BlockSpec((None, None, 2, FFN_CONV - 1, D_FF), lambda i, s, f: (i, s, 0, 0, 0)),
        ],
        out_shape=[
            jax.ShapeDtypeStruct((b, t, D_MODEL), F32),
            jax.ShapeDtypeStruct((b, t // tt, 2, FFN_CONV - 1, D_FF), F32),
        ],
        scratch_shapes=[
            pltpu.VMEM((tt, D_MODEL), F32),
            pltpu.VMEM((nf, 2, SUBLANES, tf), F32),
            pltpu.VMEM((2, 2, SUBLANES + tt, tf), F32),
        ],
        compiler_params=pltpu.CompilerParams(
            dimension_semantics=("arbitrary", "arbitrary", "arbitrary"),
            vmem_limit_bytes=VMEM_LIMIT_FFN),
        name="ffn_long",
    )(hn, x1, wup, wup, cw, cw, wdown, wup, wup, cw, cw, wdown, fg_row)


def _ffn_short_kernel(hn_ref, x1_ref, wg_ref, wu_ref, cwg_ref, cwu_ref, wd_ref, fg_ref,
                      b0g_ref, b1g_ref, b0u_ref, b1u_ref,
                      y_ref, n0g_ref, n1g_ref, n0u_ref, n1u_ref, acc_ref, z_ref, hb_ref, *, tt, seq):
    f = pl.program_id(1)
    nseq = tt // seq

    def up_fn():
        hn = hn_ref[...]
        t_in_seq = lax.broadcasted_iota(jnp.int32, (tt, FFN_TF), 0) % seq
        convs = []
        groups = ((wg_ref, cwg_ref, b0g_ref, b1g_ref, n0g_ref, n1g_ref),
                  (wu_ref, cwu_ref, b0u_ref, b1u_ref, n0u_ref, n1u_ref))
        for w_ref, cw_ref, b0_ref, b1_ref, n0_ref, n1_ref in groups:
            h = jnp.dot(hn, w_ref[...], preferred_element_type=F32)
            z_ref[...] = jnp.zeros(z_ref.shape, F32)
            for lb in range(FFN_TF // LANES):
                cols = slice(lb * LANES, (lb + 1) * LANES)
                z_ref[lb, pl.ds(0, nseq, stride=seq), :] = b0_ref[:, cols]
                z_ref[lb, pl.ds(1, nseq, stride=seq), :] = b1_ref[:, cols]
                hb_ref[lb] = h[:, cols]
                n0_ref[:, cols] = hb_ref[lb, pl.ds(seq - 2, nseq, stride=seq), :]
                n1_ref[:, cols] = hb_ref[lb, pl.ds(seq - 1, nseq, stride=seq), :]
            z = jnp.concatenate([z_ref[lb] for lb in range(FFN_TF // LANES)], axis=1)
            s1 = jnp.where(t_in_seq == 0, pltpu.roll(z, tt - 1, 0), pltpu.roll(h, 1, 0))
            s2 = jnp.where(t_in_seq < 2, z, pltpu.roll(h, 2, 0))
            cw = cw_ref[...]
            convs.append(h * cw[2:3, :] + s1 * cw[1:2, :] + s2 * cw[0:1, :])
        return _silu(convs[0]) * convs[1]

    _ffn_pipeline(f, up_fn, wd_ref, x1_ref, fg_ref, y_ref, acc_ref)


def _ffn_short(hn, x1, wup, cw, wdown, fg_row, buf0, buf1, tt, seq):
    m = hn.shape[0]
    tf = FFN_TF
    nf = FFN_NF
    nseq = tt // seq
    up = lambda f: f
    down = lambda f: f
    st_g = pl.BlockSpec((nseq, tf), lambda i, f: (i, up(f)))
    st_u = pl.BlockSpec((nseq, tf), lambda i, f: (i, nf + up(f)))
    new = pl.BlockSpec((nseq, tf), lambda i, f: (i, up(f)))
    new_shape = jax.ShapeDtypeStruct((m // seq, D_FF), F32)
    return pl.pallas_call(
        functools.partial(_ffn_short_kernel, tt=tt, seq=seq),
        grid=(m // tt, nf),
        in_specs=[
            pl.BlockSpec((tt, D_MODEL), lambda i, f: (i, 0)),
            pl.BlockSpec((tt, D_MODEL), lambda i, f: (i, 0)),
            pl.BlockSpec((D_MODEL, tf), lambda i, f: (0, up(f))),
            pl.BlockSpec((D_MODEL, tf), lambda i, f: (0, nf + up(f))),
            pl.BlockSpec((FFN_CONV, tf), lambda i, f: (0, up(f))),
            pl.BlockSpec((FFN_CONV, tf), lambda i, f: (0, nf + up(f))),
            pl.BlockSpec((tf, D_MODEL), lambda i, f: (down(f), 0)),
            pl.BlockSpec((1, D_MODEL), lambda i, f: (0, 0)),
            st_g, st_g, st_u, st_u,
        ],
        out_specs=[pl.BlockSpec((tt, D_MODEL), lambda i, f: (i, 0)), new, new, new, new],
        out_shape=[jax.ShapeDtypeStruct((m, D_MODEL), F32), new_shape, new_shape, new_shape, new_shape],
        scratch_shapes=[
            pltpu.VMEM((tt, D_MODEL), F32),
            pltpu.VMEM((tf // LANES, tt, LANES), F32),
            pltpu.VMEM((tf // LANES, tt, LANES), F32),
        ],
        compiler_params=pltpu.CompilerParams(
            dimension_semantics=("arbitrary", "arbitrary"), vmem_limit_bytes=VMEM_LIMIT),
        name="ffn_short",
    )(hn, x1, wup, wup, cw, cw, wdown, fg_row, buf0, buf1, buf0, buf1)


def _pad_lanes(vec, offset):
    out = jnp.zeros((LANES,), F32)
    return out.at[offset:offset + vec.shape[0]].set(vec.astype(F32))


def _trunk(x, s_gdn, s_gconv, s_rwkv, s_shift, s_ffn, prm, *, long_seq):
    b, t, _ = x.shape
    m = b * t
    x2d = x.reshape(m, D_MODEL)
    tm = min(512, m)
    proj = _inproj(x2d, prm["ln1_g"], prm["w_in"], min(1024, m))
    gconv_new = proj.reshape(b, t, PROJ_WIDTH)[:, t - (GDN_CONV - 1):, COL_QKV:COL_QKV + 3 * GDN_WIDTH]

    o_a, gdn_new = _gdn_mixer(proj, s_gconv, s_gdn, prm["gdn_conv_w"], prm["alog_r"], prm["dtb_r"],
                              prm["alog_c"], prm["dtb_c"], prm["gdn_norm_g"], t)
    sh_rkv = s_shift[:, None, :3 * RWKV_WIDTH]
    sh_lora = s_shift[:, None, 3 * RWKV_WIDTH:]
    o_b, rwkv_new = _rwkv_mixer(proj, sh_rkv, sh_lora, s_rwkv, prm["mu_rkv"], prm["mu_lora"],
                                prm["rwkv_w0"], prm["rwkv_a0"], prm["rwkv_wab"], prm["rwkv_g_b"],
                                prm["rwkv_k_k"], prm["rwkv_k_a"], prm["rwkv_r_k"], prm["rwkv_gn_w"],
                                prm["rwkv_gn_b"], t)

    x1, hn = _outproj(x2d, o_a, o_b, prm["w_o"], prm["ln2_g"], tm)
    if long_seq:
        tt = min(512, t)
        y, n_gu = _ffn_long(hn.reshape(b, t, D_MODEL), x1.reshape(b, t, D_MODEL), prm["ffn_w_up"],
                            prm["ffn_conv_w"], prm["ffn_w_down"], prm["final_g"], tt)
        ffn_new = jnp.concatenate([n_gu[:, -1, 0], n_gu[:, -1, 1]], axis=-1)
    else:
        tt = min(512, m)
        y, n0g, n1g, n0u, n1u = _ffn_short(hn, x1, prm["ffn_w_up"], prm["ffn_conv_w"],
                                            prm["ffn_w_down"], prm["final_g"],
                                            s_ffn[:, 0], s_ffn[:, 1], tt, t)
        y = y.reshape(b, t, D_MODEL)
        ffn_new = jnp.stack([jnp.concatenate([n0g, n0u], axis=-1),
                             jnp.concatenate([n1g, n1u], axis=-1)], axis=1)

    proj = proj.reshape(b, t, PROJ_WIDTH)
    shift_new = jnp.concatenate([proj[:, t - 1, COL_RKV:COL_RKV + 3 * RWKV_WIDTH],
                                 proj[:, t - 1, COL_LORA:COL_LORA + RWKV_LORA]], axis=-1)
    return y, gdn_new[None], gconv_new[None], rwkv_new[None], shift_new[None], ffn_new[None]


def kernel(x_prompt, x_sample, state_gdn, state_gdn_conv, state_rwkv, state_rwkv_shift, state_ffn_conv, ln1_g, w_in, gdn_conv_w, gdn_a_log, gdn_dt_bias, gdn_norm_g, rwkv_mu, rwkv_w0, rwkv_w_b, rwkv_a0, rwkv_a_b, rwkv_g_b, rwkv_k_k, rwkv_k_a, rwkv_r_k, rwkv_gn_w, rwkv_gn_b, w_o, ln2_g, ffn_w_up, ffn_conv_w, ffn_w_down, final_g):
    assert ln1_g.shape[0] == 1, "single-layer trunk"
    w_perm = _permute_win(w_in[0])
    mu = rwkv_mu[0]
    zeros_w = jnp.zeros((RWKV_LORA_W, RWKV_WIDTH), F32)
    wab = jnp.concatenate([
        jnp.concatenate([rwkv_w_b[0], zeros_w], axis=1),
        jnp.concatenate([zeros_w, rwkv_a_b[0]], axis=1)], axis=0).astype(BF16)
    alog = _pad_lanes(gdn_a_log[0], GDN_HEADS)
    dtb = _pad_lanes(gdn_dt_bias[0], GDN_HEADS)
    prm = {
        "ln1_g": ln1_g[0][None], "w_in": w_perm, "gdn_conv_w": gdn_conv_w[0],
        "alog_r": alog[None], "dtb_r": dtb[None], "alog_c": alog[:, None], "dtb_c": dtb[:, None],
        "gdn_norm_g": gdn_norm_g[0][None],
        "mu_rkv": mu[None, :3 * RWKV_WIDTH], "mu_lora": mu[None, 3 * RWKV_WIDTH:],
        "rwkv_w0": rwkv_w0[0][None], "rwkv_a0": rwkv_a0[0][None], "rwkv_wab": wab,
        "rwkv_g_b": rwkv_g_b[0].astype(BF16), "rwkv_k_k": rwkv_k_k[0][None],
        "rwkv_k_a": rwkv_k_a[0][None], "rwkv_r_k": rwkv_r_k[0].reshape(1, RWKV_WIDTH),
        "rwkv_gn_w": rwkv_gn_w[0][None], "rwkv_gn_b": rwkv_gn_b[0][None],
        "w_o": w_o[0].astype(BF16), "ln2_g": ln2_g[0][None],
        "ffn_w_up": ffn_w_up[0].astype(BF16), "ffn_conv_w": ffn_conv_w[0],
        "ffn_w_down": ffn_w_down[0].astype(BF16), "final_g": final_g[None],
    }

    bp = x_prompt.shape[0]
    zero_states = (
        jnp.zeros((bp,) + state_gdn.shape[2:], F32),
        jnp.zeros((bp,) + state_gdn_conv.shape[2:], F32),
        jnp.zeros((bp,) + state_rwkv.shape[2:], F32),
        jnp.zeros((bp,) + state_rwkv_shift.shape[2:], F32),
        None,
    )
    outs_p = _trunk(x_prompt, *zero_states, prm, long_seq=True)
    outs_s = _trunk(x_sample, state_gdn[0], state_gdn_conv[0], state_rwkv[0], state_rwkv_shift[0],
                    state_ffn_conv[0], prm, long_seq=False)
    return (outs_p[0], outs_s[0]) + tuple(outs_p[1:]) + tuple(outs_s[1:])
```

```python
import functools

import jax
import jax.numpy as jnp
from jax import lax
from jax.experimental import pallas as pl
from jax.experimental.pallas import tpu as pltpu

F32 = jnp.float32
BF16 = jnp.bfloat16

D_MODEL = 2048
GDN_WIDTH = 1024
GDN_HEADS = 8
GDN_DK = 128
GDN_CONV = 4
RWKV_WIDTH = 1024
RWKV_HEAD = 64
RWKV_HEADS = 16
RWKV_LORA_W = 64
RWKV_LORA_A = 64
RWKV_LORA_G = 128
RWKV_LORA = RWKV_LORA_W + RWKV_LORA_A + RWKV_LORA_G
RWKV_PROJ = 3 * RWKV_WIDTH + RWKV_LORA
D_FF = 5632
FFN_CONV = 3
RMS_EPS = 1e-6
L2_EPS = 1e-12
GN_EPS = 64e-5

REF_OFF_Z = 3 * GDN_WIDTH
REF_OFF_B = 4 * GDN_WIDTH
REF_OFF_RWKV = REF_OFF_B + 2 * GDN_HEADS
REF_IN_WIDTH = REF_OFF_RWKV + RWKV_PROJ

LANES = 128
SUBLANES = 8
COL_QKV = 0
COL_RKV = 3 * GDN_WIDTH
COL_Z = COL_RKV + 3 * RWKV_WIDTH
COL_LORA = COL_Z + GDN_WIDTH
COL_BA = COL_LORA + RWKV_LORA
PROJ_WIDTH = 7680
PROJ_TN = 1280

MIX_ROWS = 64
MIX_GROUPS_LONG = 4
MIX_GROUPS_SHORT = 2

NN = (((1,), (0,)), ((), ()))
NT = (((1,), (1,)), ((), ()))
TN = (((0,), (0,)), ((), ()))

VMEM_LIMIT = 56 * 1024 * 1024
VMEM_LIMIT_FFN = 62 * 1024 * 1024

P_GRAM = "x1"
P_INV = "x1"
P_SOLVE = "x1"
P_STATE = "x1"
P_OUT = "x1"


def _split(x):
    hi = x.astype(BF16)
    return hi, (x - hi.astype(F32)).astype(BF16)


def _dot(a, b, dims=NN, mode="x1"):
    if mode == "hi":
        return lax.dot_general(a, b, dims, precision=lax.Precision.HIGHEST,
                               preferred_element_type=F32)
    if mode == "x3":
        a_hi, a_lo = _split(a)
        b_hi, b_lo = _split(b)
        d = lambda u, v: lax.dot_general(u, v, dims, preferred_element_type=F32)
        return d(a_hi, b_hi) + (d(a_hi, b_lo) + d(a_lo, b_hi))
    return lax.dot_general(a.astype(BF16), b.astype(BF16), dims, preferred_element_type=F32)


def _ones_dot(ones_mat, x, dims=NN):
    x1 = x.astype(BF16)
    r1 = x - x1.astype(F32)
    x2 = r1.astype(BF16)
    x3 = (r1 - x2.astype(F32)).astype(BF16)
    m = ones_mat.astype(BF16)
    if dims == NN:
        d = lambda v: lax.dot_general(m, v, dims, preferred_element_type=F32)
    else:
        d = lambda v: lax.dot_general(v, m, dims, preferred_element_type=F32)
    return d(x1) + (d(x2) + d(x3))


def _sigmoid(x):
    return 1.0 / (1.0 + jnp.exp(-x))


def _silu(x):
    return x * _sigmoid(x)


def _softplus(x):
    return jnp.maximum(x, 0.0) + jnp.log(1.0 + jnp.exp(-jnp.abs(x)))


def _seq_masks(rows, seq_len):
    r = lax.broadcasted_iota(jnp.int32, (rows, rows), 0)
    c = lax.broadcasted_iota(jnp.int32, (rows, rows), 1)
    if seq_len >= rows:
        return None, r >= c, r > c
    shift = seq_len.bit_length() - 1
    assert 1 << shift == seq_len
    same = jnp.right_shift(r, shift) == jnp.right_shift(c, shift)
    return same, same & (r >= c), same & (r > c)


def _wide_masks(rows, seq_len):
    r = lax.broadcasted_iota(jnp.int32, (rows, 2 * rows), 0)
    c = lax.broadcasted_iota(jnp.int32, (rows, 2 * rows), 1)
    right = c >= rows
    cc = jnp.where(right, c - rows, c)
    if seq_len >= rows:
        return r >= cc, right & (r > cc)
    shift = seq_len.bit_length() - 1
    same = jnp.right_shift(r, shift) == jnp.right_shift(cc, shift)
    return same & (r >= cc), same & right & (r > cc)


def _pair_masks(rows, seq_len):
    half = LANES // 2
    assert rows == half
    lane = lax.broadcasted_iota(jnp.int32, (rows, LANES), 1)
    row = lax.broadcasted_iota(jnp.int32, (rows, LANES), 0)
    col = jnp.bitwise_and(lane, half - 1)
    if seq_len >= rows:
        causal, strict = row >= col, row > col
    else:
        shift = seq_len.bit_length() - 1
        same = jnp.right_shift(row, shift) == jnp.right_shift(col, shift)
        causal, strict = same & (row >= col), same & (row > col)
    return causal, strict, jnp.where(row == col, 1.0, 0.0).astype(F32), lane < half


def _bd(x, first_half):
    return jnp.concatenate([jnp.where(first_half, x, 0.0), jnp.where(first_half, 0.0, x)], axis=0)


def _pair_inverses(neg_a, rows, nilpotency, eye2, first_half):
    qs = dict(neg_a)
    ts = {u: eye2 + q for u, q in qs.items()}
    n = 2
    if n < nilpotency:
        qs = {u: _dot(q, _bd(q, first_half), mode=P_INV) for u, q in qs.items()}
    while n < nilpotency:
        if 2 * n < nilpotency:
            tq = {u: _dot(jnp.concatenate([ts[u], qs[u]], axis=0), _bd(qs[u], first_half), mode=P_INV)
                  for u in qs}
            ts = {u: ts[u] + tq[u][:rows] for u in qs}
            qs = {u: tq[u][rows:] for u in qs}
        else:
            ts = {u: ts[u] + _dot(ts[u], _bd(qs[u], first_half), mode=P_INV) for u in qs}
        n *= 2
    return ts


def _lane_group_sums(x, group):
    tile = 2 * LANES
    shift = group.bit_length() - 1
    li = jnp.right_shift(lax.broadcasted_iota(jnp.int32, (tile, tile), 0), shift)
    lj = jnp.right_shift(lax.broadcasted_iota(jnp.int32, (tile, tile), 1), shift)
    ones = jnp.where(li == lj, 1.0, 0.0).astype(BF16)
    hi, lo = _split(x)
    d = lambda u: lax.dot_general(u, ones, NN, preferred_element_type=F32)
    return jnp.concatenate([d(hi[:, t:t + tile]) + d(lo[:, t:t + tile])
                            for t in range(0, x.shape[1], tile)], axis=1)


def _unit_lower_inverses(mats, rows, nilpotency):
    r = lax.broadcasted_iota(jnp.int32, (rows, rows), 0)
    c = lax.broadcasted_iota(jnp.int32, (rows, rows), 1)
    eye = jnp.where(r == c, 1.0, 0.0).astype(F32)
    qs = [-a for a in mats]
    ts = [eye + q for q in qs]
    n = 2
    if n < nilpotency:
        qs = [_dot(q, q, mode=P_INV) for q in qs]
    while n < nilpotency:
        if 2 * n < nilpotency:
            tq = [_dot(jnp.concatenate([t, q], axis=0), q, mode=P_INV) for t, q in zip(ts, qs)]
            ts = [t + p[:rows] for t, p in zip(ts, tq)]
            qs = [p[rows:] for p in tq]
        else:
            ts = [t + _dot(t, q, mode=P_INV) for t, q in zip(ts, qs)]
        n *= 2
    return ts


def _permute_win_kernel(w_ref, o_ref):
    rw = REF_OFF_RWKV
    rows = w_ref.shape[0]
    cast = lambda lo, hi: w_ref[:, lo:hi].astype(BF16)
    o_ref[:, COL_QKV:COL_RKV] = cast(0, REF_OFF_Z)
    o_ref[:, COL_RKV:COL_Z] = cast(rw, rw + 3 * RWKV_WIDTH)
    o_ref[:, COL_Z:COL_LORA] = cast(REF_OFF_Z, REF_OFF_B)
    o_ref[:, COL_LORA:COL_BA] = cast(rw + 3 * RWKV_WIDTH, REF_IN_WIDTH)
    tail = jnp.concatenate([w_ref[:, REF_OFF_B:REF_OFF_RWKV],
                            jnp.zeros((rows, PROJ_WIDTH - COL_BA - 2 * GDN_HEADS), F32)], axis=1)
    o_ref[:, COL_BA:] = tail.astype(BF16)


def _permute_win(w, tr=256):
    return pl.pallas_call(
        _permute_win_kernel,
        grid=(D_MODEL // tr,),
        in_specs=[pl.BlockSpec((tr, REF_IN_WIDTH), lambda i: (i, 0))],
        out_specs=pl.BlockSpec((tr, PROJ_WIDTH), lambda i: (i, 0)),
        out_shape=jax.ShapeDtypeStruct((D_MODEL, PROJ_WIDTH), BF16),
        compiler_params=pltpu.CompilerParams(
            dimension_semantics=("arbitrary",), vmem_limit_bytes=VMEM_LIMIT),
        name="permute_win",
    )(w)


def _inproj_kernel(x_ref, g_ref, w_ref, o_ref, xn_ref):
    @pl.when(pl.program_id(1) == 0)
    def _():
        x = x_ref[...]
        ms = jnp.mean(x * x, axis=-1, keepdims=True)
        xn_ref[...] = (x * lax.rsqrt(ms + RMS_EPS) * g_ref[...]).astype(BF16)

    o_ref[...] = jnp.dot(xn_ref[...], w_ref[...], preferred_element_type=F32)


def _inproj(x2d, g_row, w_bf16, tm):
    m = x2d.shape[0]
    return pl.pallas_call(
        _inproj_kernel,
        grid=(m // tm, PROJ_WIDTH // PROJ_TN),
        in_specs=[
            pl.BlockSpec((tm, D_MODEL), lambda i, j: (i, 0)),
            pl.BlockSpec((1, D_MODEL), lambda i, j: (0, 0)),
            pl.BlockSpec((D_MODEL, PROJ_TN), lambda i, j: (0, j)),
        ],
        out_specs=pl.BlockSpec((tm, PROJ_TN), lambda i, j: (i, j)),
        out_shape=jax.ShapeDtypeStruct((m, PROJ_WIDTH), F32),
        scratch_shapes=[pltpu.VMEM((tm, D_MODEL), BF16)],
        compiler_params=pltpu.CompilerParams(
            dimension_semantics=("arbitrary", "arbitrary"), vmem_limit_bytes=VMEM_LIMIT),
        name="inproj",
    )(x2d, g_row, w_bf16)


def _mixer_geometry(seq_len, n_seq, short_groups=1):
    rows = MIX_ROWS
    length = min(seq_len, rows)
    assert rows % length == 0 and seq_len % length == 0 and length % SUBLANES == 0
    per_group = rows // length
    groups = MIX_GROUPS_LONG if per_group == 1 else short_groups
    while n_seq % (groups * per_group):
        groups //= 2
    assert groups >= 1
    return rows, length, per_group, groups, seq_len // length


def _gdn_kernel(qkv_ref, z_ref, ba_ref, cbuf_ref, s0_ref, convw_ref, alog_r_ref, dtb_r_ref,
                alog_c_ref, dtb_c_ref, ng_ref, o_ref, s_ref, xp_ref, *, seq_len, n_seq):
    R, L, G, S, _ = _mixer_geometry(seq_len, n_seq, MIX_GROUPS_SHORT)
    RT = S * R
    c = pl.program_id(1)
    width = 3 * GDN_WIDTH
    hist = GDN_CONV - 1
    cw = convw_ref[...]
    groups = range(S)
    seqs = range(G)

    @pl.when(c == 0)
    def _():
        s_ref[...] = s0_ref[...]

    pieces = []
    if G == 1:
        @pl.when(c == 0)
        def _():
            for s in groups:
                xp_ref[s, 0:SUBLANES, :] = jnp.zeros((SUBLANES, width), F32)
                xp_ref[s, SUBLANES - hist:SUBLANES, :] = cbuf_ref[s]

        @pl.when(c > 0)
        def _():
            for s in groups:
                xp_ref[s, 0:SUBLANES, :] = xp_ref[s, R:R + SUBLANES, :]

        for s in groups:
            xp_ref[s, SUBLANES:SUBLANES + R, :] = qkv_ref[s]
            piece = qkv_ref[s] * cw[hist:hist + 1, :]
            for i in range(hist):
                off = SUBLANES - hist + i
                piece = piece + xp_ref[s, off:off + R, :] * cw[i:i + 1, :]
            pieces.append(piece)
    else:
        for s in groups:
            for g in seqs:
                q = s * G + g
                rows = slice(g * L, (g + 1) * L)
                xp_ref[q, SUBLANES - hist:SUBLANES, :] = cbuf_ref[q]
                xp_ref[q, SUBLANES:SUBLANES + L, :] = qkv_ref[s, rows, :]
                piece = qkv_ref[s, rows, :] * cw[hist:hist + 1, :]
                for i in range(hist):
                    off = SUBLANES - hist + i
                    piece = piece + xp_ref[q, off:off + L, :] * cw[i:i + 1, :]
                pieces.append(piece)
    qkv = _silu(pieces[0] if len(pieces) == 1 else jnp.concatenate(pieces, axis=0))

    same_t, causal_t, _ = _seq_masks(RT, L)
    causal01 = jnp.where(causal_t, 1.0, 0.0).astype(F32)

    ba = ba_ref[...].reshape(RT, LANES)
    ba_t = ba.T
    beta_c = _sigmoid(ba)
    g_c = -jnp.exp(alog_r_ref[...]) * _softplus(ba + dtb_r_ref[...])
    g_r = -jnp.exp(alog_c_ref[...]) * _softplus(ba_t + dtb_c_ref[...])
    gc_all = _ones_dot(causal01, g_c)
    gr_all = _ones_dot(causal01, g_r, NT)
    if same_t is None:
        gtot_all = jnp.broadcast_to(gc_all[RT - 1:RT, :], (RT, LANES))
    else:
        gtot_all = _ones_dot(jnp.where(same_t, 1.0, 0.0).astype(F32), g_c)

    qk_raw = qkv[:, :2 * GDN_WIDTH]
    qk_n = qk_raw * lax.rsqrt(_lane_group_sums(qk_raw * qk_raw, GDN_DK) + L2_EPS)
    q_all = qk_n[:, :GDN_WIDTH] * (GDN_DK ** -0.5)
    k_all = qk_n[:, GDN_WIDTH:]
    v_all = qkv[:, 2 * GDN_WIDTH:]

    causal2, strict2, eye2, first_half = _pair_masks(R, L)
    first_head = lax.broadcasted_iota(jnp.int32, (R, 2 * GDN_DK), 1) < GDN_DK
    zeros_u = jnp.zeros((R, GDN_DK), F32)
    zeros_rhs = jnp.zeros((R, 2 * GDN_DK), F32)
    chains = [(s, h) for s in groups for h in range(GDN_HEADS)]
    units = [(s, p) for s in groups for p in range(GDN_HEADS // 2)]
    seq_rows = [slice(g * L, (g + 1) * L) for g in seqs]
    qs, ks, vs, betas, gcols, gtots = {}, {}, {}, {}, {}, {}
    for s, h in chains:
        rs = slice(s * R, (s + 1) * R)
        lo = h * GDN_DK
        key = (s, h)
        qs[key] = q_all[rs, lo:lo + GDN_DK]
        ks[key] = k_all[rs, lo:lo + GDN_DK]
        vs[key] = v_all[rs, lo:lo + GDN_DK]
        betas[key] = beta_c[rs, h:h + 1]
        gcols[key] = gc_all[rs, GDN_HEADS + h:GDN_HEADS + h + 1]
        gtots[key] = gtot_all[rs, GDN_HEADS + h:GDN_HEADS + h + 1]
    kq2, decay2, a2 = {}, {}, {}
    for s, p in units:
        rs = slice(s * R, (s + 1) * R)
        cols = slice(2 * p * GDN_DK, (2 * p + 2) * GDN_DK)
        k_pair = k_all[rs, cols]
        kt_bd = jnp.concatenate([jnp.where(first_head, k_pair, 0.0),
                                 jnp.where(first_head, 0.0, k_pair)], axis=0).T
        kq = _dot(jnp.concatenate([k_pair, q_all[rs, cols]], axis=0), kt_bd, mode=P_GRAM)
        h0, h1 = (s, 2 * p), (s, 2 * p + 1)
        gcol2 = jnp.where(first_half, gcols[h0], gcols[h1])
        grow2 = jnp.concatenate([gr_all[GDN_HEADS + 2 * p:GDN_HEADS + 2 * p + 1, rs],
                                 gr_all[GDN_HEADS + 2 * p + 1:GDN_HEADS + 2 * p + 2, rs]], axis=1)
        dec = jnp.where(causal2, jnp.exp(jnp.where(causal2, gcol2 - grow2, 0.0)), 0.0)
        beta2 = jnp.where(first_half, betas[h0], betas[h1])
        kq2[(s, p)] = kq
        decay2[(s, p)] = dec
        a2[(s, p)] = jnp.where(strict2, -(beta2 * kq[:R] * dec), 0.0)
    t2 = _pair_inverses(a2, R, L, eye2, first_half)
    gammas = {key: jnp.exp(gcols[key]) for key in chains}

    def stacked(key, x, zeros):
        return jnp.concatenate([x, zeros] if key[1] % 2 == 0 else [zeros, x], axis=0)

    sols = {key: _dot(t2[(key[0], key[1] // 2)],
                      stacked(key, jnp.concatenate([(betas[key] * gammas[key]) * ks[key],
                                                    betas[key] * vs[key]], axis=1), zeros_rhs),
                      mode=P_SOLVE) for key in chains}
    states = {(s, h): [s_ref[s * G + g, h] for g in seqs] for s, h in chains}
    wss = {key: [_dot(jnp.concatenate([sols[key][rows, :GDN_DK], (qs[key] * gammas[key])[rows]], axis=0),
                      states[key][g], mode=P_STATE) for g, rows in enumerate(seq_rows)]
           for key in chains}
    us = {key: jnp.concatenate([sols[key][rows, GDN_DK:] - wss[key][g][:L]
                                for g, rows in enumerate(seq_rows)], axis=0) for key in chains}
    qk2 = {u: kq2[u][R:] * decay2[u] for u in units}
    outs = {key: jnp.concatenate([wss[key][g][L:] for g in seqs], axis=0)
            + _dot(qk2[(key[0], key[1] // 2)], stacked(key, us[key], zeros_u), mode=P_OUT)
            for key in chains}
    for s, h in chains:
        key = (s, h)
        kt = ks[key] * jnp.exp(gtots[key] - gcols[key])
        for g, rows in enumerate(seq_rows):
            gl = jnp.exp(gtots[key][g * L:g * L + 1, :])
            s_ref[s * G + g, h] = gl * states[key][g] + _dot(kt[rows], us[key][rows], TN, mode=P_STATE)
    ng = jnp.concatenate([ng_ref[...]] * GDN_HEADS, axis=1)
    for s in groups:
        o = jnp.concatenate([outs[(s, h)] for h in range(GDN_HEADS)], axis=1)
        ms = _lane_group_sums(o * o, GDN_DK) * (1.0 / GDN_DK)
        o = o * lax.rsqrt(ms + RMS_EPS) * ng
        o_ref[s] = (o * _silu(z_ref[s])).astype(o_ref.dtype)


def _gdn_mixer(proj2d, cbuf, s0, convw, alog_r, dtb_r, alog_c, dtb_c, ng, seq_len):
    m = proj2d.shape[0]
    nseq = m // seq_len
    R, L, G, S, sps = _mixer_geometry(seq_len, nseq, MIX_GROUPS_SHORT)
    width = 3 * GDN_WIDTH
    proj3d = proj2d.reshape(nseq // G, sps * R, PROJ_WIDTH)
    const2 = lambda i, c: (0, 0)
    rows_map = lambda col: (lambda i, c: (i, c, col))
    per_seq = lambda *dims: pl.BlockSpec((S * G,) + dims, lambda i, c: (i,) + (0,) * len(dims))
    xp_shape = (S, SUBLANES + R, width) if G == 1 else (S * G, SUBLANES + L, width)
    o, s_new = pl.pallas_call(
        functools.partial(_gdn_kernel, seq_len=seq_len, n_seq=nseq),
        grid=(nseq // (S * G), sps),
        in_specs=[
            pl.BlockSpec((S, R, width), rows_map(COL_QKV // width)),
            pl.BlockSpec((S, R, GDN_WIDTH), rows_map(COL_Z // GDN_WIDTH)),
            pl.BlockSpec((S, R, LANES), rows_map(COL_BA // LANES)),
            per_seq(GDN_CONV - 1, width),
            per_seq(GDN_HEADS, GDN_DK, GDN_DK),
            pl.BlockSpec((GDN_CONV, width), const2),
            pl.BlockSpec((1, LANES), const2),
            pl.BlockSpec((1, LANES), const2),
            pl.BlockSpec((LANES, 1), const2),
            pl.BlockSpec((LANES, 1), const2),
            pl.BlockSpec((1, GDN_DK), const2),
        ],
        out_specs=[
            pl.BlockSpec((S, R, GDN_WIDTH), rows_map(0)),
            per_seq(GDN_HEADS, GDN_DK, GDN_DK),
        ],
        out_shape=[
            jax.ShapeDtypeStruct((nseq // G, sps * R, GDN_WIDTH), BF16),
            jax.ShapeDtypeStruct((nseq, GDN_HEADS, GDN_DK, GDN_DK), F32),
        ],
        scratch_shapes=[pltpu.VMEM(xp_shape, F32)],
        compiler_params=pltpu.CompilerParams(
            dimension_semantics=("arbitrary", "arbitrary"), vmem_limit_bytes=VMEM_LIMIT),
        name="gdn_mixer",
    )(proj3d, proj3d, proj3d, cbuf, s0, convw, alog_r, dtb_r, alog_c, dtb_c, ng)
    return o.reshape(m, GDN_WIDTH), s_new


def _shifted_rows(x_ref, prev_ref, carry_ref, c, R, L, G, S):
    width = x_ref.shape[-1]
    groups = range(S)
    if G == 1:
        @pl.when(c == 0)
        def _():
            for s in groups:
                carry_ref[s, 0:SUBLANES, :] = jnp.zeros((SUBLANES, width), F32)
                carry_ref[s, SUBLANES - 1:SUBLANES, :] = prev_ref[s:s + 1, :]

        @pl.when(c > 0)
        def _():
            for s in groups:
                carry_ref[s, 0:SUBLANES, :] = carry_ref[s, R:R + SUBLANES, :]

        xs, prevs = [], []
        for s in groups:
            carry_ref[s, SUBLANES:SUBLANES + R, :] = x_ref[s]
            xs.append(x_ref[s])
            prevs.append(carry_ref[s, SUBLANES - 1:SUBLANES - 1 + R, :])
    else:
        row = lax.broadcasted_iota(jnp.int32, (L, width), 0)
        xs, prevs = [], []
        for s in groups:
            x = x_ref[s]
            xs.append(x)
            for g in range(G):
                xg = x[g * L:(g + 1) * L]
                q = s * G + g
                prevs.append(jnp.where(row == 0, prev_ref[q:q + 1, :], pltpu.roll(xg, 1, 0)))
    cat = lambda parts: parts[0] if len(parts) == 1 else jnp.concatenate(parts, axis=0)
    return cat(xs), cat(prevs)


def _rwkv_kernel(rkv_ref, lora_ref, sh_rkv_ref, sh_lora_ref, s0_ref, mu_rkv_ref, mu_lora_ref,
                 w0_ref, a0_ref, wab_ref, gb_ref, kk_ref, ka_ref, rk_ref, gnw_ref, gnb_ref,
                 o_ref, s_ref, xr_ref, xl_ref, m_ref, *, seq_len, n_seq):
    R, L, G, S, sps = _mixer_geometry(seq_len, n_seq)
    RT = S * R
    NQ = S * G
    c = pl.program_id(1)
    W = RWKV_WIDTH
    HD = RWKV_HEAD
    pairs = range(RWKV_HEADS // 2)
    groups = range(S)
    seqs = range(G)

    @pl.when(c == 0)
    def _():
        zero = jnp.zeros((HD, HD), F32)
        for q in range(NQ):
            for j in pairs:
                vk = jnp.concatenate([jnp.concatenate([s0_ref[q, 2 * j], zero], axis=1),
                                      jnp.concatenate([zero, s0_ref[q, 2 * j + 1]], axis=1)], axis=0)
                m_ref[q, j] = vk.T

    p, p_prev = _shifted_rows(rkv_ref, sh_rkv_ref, xr_ref, c, R, L, G, S)
    xs = p + (p_prev - p) * mu_rkv_ref[...]
    pl_, pl_prev = _shifted_rows(lora_ref, sh_lora_ref, xl_ref, c, R, L, G, S)
    xl = pl_ + (pl_prev - pl_) * mu_lora_ref[...]
    r = xs[:, :W]
    k = xs[:, W:2 * W]
    v = xs[:, 2 * W:]

    wa_in = xl[:, :LANES]
    lane = lax.broadcasted_iota(jnp.int32, (RT, LANES), 1)
    wa_in = jnp.where(lane < RWKV_LORA_W, jnp.tanh(wa_in), wa_in)
    wa = _dot(wa_in, wab_ref[...])
    w = -_softplus(-(w0_ref[...] + wa[:, :W])) - 0.5
    lw = -jnp.exp(w)
    a = _sigmoid(a0_ref[...] + wa[:, W:])
    gate = _dot(_sigmoid(xl[:, LANES:]), gb_ref[...])
    kk_raw = k * kk_ref[...]
    k2 = k * (1.0 + (a - 1.0) * ka_ref[...])

    same_t, causal_t, _ = _seq_masks(RT, L)
    lc = _ones_dot(jnp.where(causal_t, 1.0, 0.0).astype(F32), lw)
    if G == 1:
        ltot = jnp.concatenate([jnp.broadcast_to(lc[(s + 1) * R - 1:(s + 1) * R, :], (R, W))
                                for s in groups], axis=0) if S > 1 else \
            jnp.broadcast_to(lc[R - 1:R, :], (R, W))
    else:
        ltot = _ones_dot(jnp.where(same_t, 1.0, 0.0).astype(F32), lw)
    e_inv = jnp.exp(-lc)
    e_rem = jnp.exp(ltot - lc)

    kk = kk_raw * lax.rsqrt(_lane_group_sums(kk_raw * kk_raw, HD) + L2_EPS)
    kka = kk * a
    ct = kk * jnp.exp(lc - lw)
    rt = r * jnp.exp(lc)
    bh = kka * e_inv
    kh = k2 * e_inv
    bb = kka * e_rem
    kb = k2 * e_rem
    p_rows = [ltot[q * L:q * L + 1] for q in range(NQ)]
    p_rows = p_rows + [p_rows[0]] * (-NQ % SUBLANES)
    pt = jnp.exp(jnp.concatenate(p_rows, axis=0)).T

    causal2, strict2, eye2, even_half = _pair_masks(R, L)
    block_diag = ((lax.broadcasted_iota(jnp.int32, (LANES, LANES), 0) < HD)
                  == (lax.broadcasted_iota(jnp.int32, (LANES, LANES), 1) < HD))
    even_all = jnp.bitwise_and(lax.broadcasted_iota(jnp.int32, (R, W), 1), LANES - 1) < HD
    tile = lambda j: slice(j * LANES, (j + 1) * LANES)
    bd = lambda x: _bd(x, even_half)

    if G > 1:
        row2 = lax.broadcasted_iota(jnp.int32, (2 * R, LANES), 0)
        seq_of_row = jnp.right_shift(jnp.bitwise_and(row2, R - 1), L.bit_length() - 1)

    units = [(s, j) for s in groups for j in pairs]
    xps, vps, bdb, bdk, bkts = {}, {}, {}, {}, {}
    for s in groups:
        rs = slice(s * R, (s + 1) * R)
        x_s = jnp.concatenate([ct[rs], rt[rs]], axis=0)
        bt_s = jnp.concatenate([jnp.where(even_all, bh[rs], 0.0), jnp.where(even_all, 0.0, bh[rs])],
                               axis=0).T
        kt_s = jnp.concatenate([jnp.where(even_all, kh[rs], 0.0), jnp.where(even_all, 0.0, kh[rs])],
                               axis=0).T
        bkt_s = jnp.concatenate([bb[rs], kb[rs]], axis=0).T
        for j in pairs:
            xps[(s, j)] = x_s[:, tile(j)]
            vps[(s, j)] = v[rs, tile(j)]
            bdb[(s, j)] = bt_s[tile(j), :]
            bdk[(s, j)] = kt_s[tile(j), :]
            bkts[(s, j)] = bkt_s[tile(j), :]
    gb = {u: _dot(xps[u], bdb[u], mode=P_GRAM) for u in units}
    gk = {u: _dot(xps[u], bdk[u], mode=P_GRAM) for u in units}
    ms = {(s, j): [m_ref[s * G + g, j] for g in seqs] for s, j in units}
    xm_c, xm_r = {}, {}
    for u in units:
        if G == 1:
            xm = _dot(xps[u], ms[u][0], mode=P_STATE)
            xm_c[u], xm_r[u] = xm[:R], xm[R:]
        else:
            parts = [_dot(jnp.concatenate([xps[u][g * L:(g + 1) * L],
                                           xps[u][R + g * L:R + (g + 1) * L]], axis=0),
                          ms[u][g], mode=P_STATE) for g in seqs]
            xm_c[u] = jnp.concatenate([p_[:L] for p_ in parts], axis=0)
            xm_r[u] = jnp.concatenate([p_[L:] for p_ in parts], axis=0)
    ts = _pair_inverses({u: jnp.where(strict2, -gb[u][:R], 0.0) for u in units}, R, L, eye2, even_half)
    bdv = {u: bd(vps[u]) for u in units}
    akvs = {u: _dot(jnp.where(strict2, gk[u][:R], 0.0), bdv[u], mode=P_OUT) for u in units}
    us = {u: _dot(ts[u], bd(-(xm_c[u] + akvs[u])), mode=P_SOLVE) for u in units}
    uvs = {u: jnp.concatenate([us[u], vps[u]], axis=0) for u in units}
    ys = {u: xm_r[u] + _dot(
        jnp.concatenate([jnp.where(causal2, gb[u][R:], 0.0), jnp.where(causal2, gk[u][R:], 0.0)], axis=1),
        jnp.concatenate([bd(us[u]), bdv[u]], axis=0), mode=P_OUT) for u in units}
    for s, j in units:
        u = (s, j)
        for g in seqs:
            q = s * G + g
            uv_g = uvs[u] if G == 1 else jnp.where(seq_of_row == g, uvs[u], 0.0)
            upd = _dot(bkts[u], uv_g, mode=P_STATE)
            m_ref[q, j] = pt[tile(j), q:q + 1] * ms[u][g] + jnp.where(block_diag, upd, 0.0)

    y_rows = [jnp.concatenate([ys[(s, j)] for j in pairs], axis=1) for s in groups]
    y_all = y_rows[0] if S == 1 else jnp.concatenate(y_rows, axis=0)
    mean = _lane_group_sums(y_all, HD) * (1.0 / HD)
    yc = y_all - mean
    var = _lane_group_sums(yc * yc, HD) * (1.0 / HD)
    yn = yc * lax.rsqrt(var + GN_EPS) * gnw_ref[...] + gnb_ref[...]
    yn = yn + _lane_group_sums(r * k2 * rk_ref[...], HD) * v
    o_ref[...] = (yn * gate).reshape(S, R, W).astype(o_ref.dtype)

    @pl.when(c == sps - 1)
    def _():
        for q in range(NQ):
            for j in pairs:
                vk = m_ref[q, j].T
                s_ref[q, 2 * j] = vk[:HD, :HD]
                s_ref[q, 2 * j + 1] = vk[HD:, HD:]


def _rwkv_mixer(proj2d, sh_rkv, sh_lora, s0, mu_rkv, mu_lora, w0, a0, wab, gb, kk, ka, rk, gnw, gnb,
                seq_len):
    m = proj2d.shape[0]
    nseq = m // seq_len
    R, L, G, S, sps = _mixer_geometry(seq_len, nseq)
    W = RWKV_WIDTH
    proj3d = proj2d.reshape(nseq // G, sps * R, PROJ_WIDTH)
    const2 = lambda i, c: (0, 0)
    row = lambda width: pl.BlockSpec((1, width), const2)
    rows_map = lambda col: (lambda i, c: (i, c, col))
    per_seq = lambda *dims: pl.BlockSpec((S * G,) + dims, lambda i, c: (i,) + (0,) * len(dims))
    carry = lambda width: pltpu.VMEM((S, SUBLANES + R, width) if G == 1 else (SUBLANES, LANES), F32)
    o, s_new = pl.pallas_call(
        functools.partial(_rwkv_kernel, seq_len=seq_len, n_seq=nseq),
        grid=(nseq // (S * G), sps),
        in_specs=[
            pl.BlockSpec((S, R, 3 * W), rows_map(COL_RKV // (3 * W))),
            pl.BlockSpec((S, R, RWKV_LORA), rows_map(COL_LORA // RWKV_LORA)),
            pl.BlockSpec((S * G, 3 * W), lambda i, c: (i, 0)),
            pl.BlockSpec((S * G, RWKV_LORA), lambda i, c: (i, 3 * W // RWKV_LORA)),
            per_seq(RWKV_HEADS, RWKV_HEAD, RWKV_HEAD),
            row(3 * W), row(RWKV_LORA), row(W), row(W),
            pl.BlockSpec((LANES, 2 * W), const2),
            pl.BlockSpec((RWKV_LORA_G, W), const2),
            row(W), row(W), row(W), row(W), row(W),
        ],
        out_specs=[pl.BlockSpec((S, R, W), rows_map(0)), per_seq(RWKV_HEADS, RWKV_HEAD, RWKV_HEAD)],
        out_shape=[
            jax.ShapeDtypeStruct((nseq // G, sps * R, W), BF16),
            jax.ShapeDtypeStruct((nseq, RWKV_HEADS, RWKV_HEAD, RWKV_HEAD), F32),
        ],
        scratch_shapes=[carry(3 * W), carry(RWKV_LORA),
                        pltpu.VMEM((S * G, RWKV_HEADS // 2, LANES, LANES), F32)],
        compiler_params=pltpu.CompilerParams(
            dimension_semantics=("arbitrary", "arbitrary"), vmem_limit_bytes=VMEM_LIMIT),
        name="rwkv_mixer",
    )(proj3d, proj3d, sh_rkv, sh_lora, s0, mu_rkv, mu_lora, w0, a0, wab, gb, kk, ka, rk, gnw, gnb)
    return o.reshape(m, W), s_new


def _outproj_kernel(x_ref, oa_ref, ob_ref, wa_ref, wb_ref, g_ref, x1_ref, hn_ref):
    x1 = (x_ref[...] + jnp.dot(oa_ref[...], wa_ref[...], preferred_element_type=F32)
          + jnp.dot(ob_ref[...], wb_ref[...], preferred_element_type=F32))
    x1_ref[...] = x1
    ms = jnp.mean(x1 * x1, axis=-1, keepdims=True)
    hn_ref[...] = (x1 * lax.rsqrt(ms + RMS_EPS) * g_ref[...]).astype(BF16)


def _outproj(x2d, oa, ob, wo_bf16, g_row, tm):
    m = x2d.shape[0]
    return pl.pallas_call(
        _outproj_kernel,
        grid=(m // tm,),
        in_specs=[
            pl.BlockSpec((tm, D_MODEL), lambda i: (i, 0)),
            pl.BlockSpec((tm, GDN_WIDTH), lambda i: (i, 0)),
            pl.BlockSpec((tm, RWKV_WIDTH), lambda i: (i, 0)),
            pl.BlockSpec((GDN_WIDTH, D_MODEL), lambda i: (0, 0)),
            pl.BlockSpec((RWKV_WIDTH, D_MODEL), lambda i: (1, 0)),
            pl.BlockSpec((1, D_MODEL), lambda i: (0, 0)),
        ],
        out_specs=[pl.BlockSpec((tm, D_MODEL), lambda i: (i, 0)),
                   pl.BlockSpec((tm, D_MODEL), lambda i: (i, 0))],
        out_shape=[jax.ShapeDtypeStruct((m, D_MODEL), F32),
                   jax.ShapeDtypeStruct((m, D_MODEL), BF16)],
        compiler_params=pltpu.CompilerParams(
            dimension_semantics=("arbitrary",), vmem_limit_bytes=VMEM_LIMIT),
        name="outproj",
    )(x2d, oa, ob, wo_bf16, wo_bf16, g_row)


FFN_TF = 512
FFN_NF = D_FF // FFN_TF


def _ffn_pipeline(f, up_fn, wd_ref, x1_ref, fg_ref, y_ref, acc_ref):
    @pl.when(f == 0)
    def _():
        acc_ref[...] = jnp.zeros(acc_ref.shape, F32)

    acc_ref[...] += jnp.dot(up_fn().astype(BF16), wd_ref[...], preferred_element_type=F32)

    @pl.when(f == FFN_NF - 1)
    def _():
        xo = x1_ref[...] + acc_ref[...]
        ms = jnp.mean(xo * xo, axis=-1, keepdims=True)
        y_ref[...] = xo * lax.rsqrt(ms + RMS_EPS) * fg_ref[...]


def _ffn_long_kernel(hn_ref, x1_ref, wg0_ref, wu0_ref, cwg0_ref, cwu0_ref, wd0_ref,
                     wg1_ref, wu1_ref, cwg1_ref, cwu1_ref, wd1_ref, fg_ref,
                     y_ref, n_ref, acc_ref, carry_ref, hbuf_ref, *, tt):
    ti = pl.program_id(1)
    f = pl.program_id(2)
    steps = pl.num_programs(2)
    hn = hn_ref[...]

    def act(slot, chunk, wg_ref, wu_ref, cwg_ref, cwu_ref):
        convs = []
        for j, (w_ref, cw_ref) in enumerate(((wg_ref, cwg_ref), (wu_ref, cwu_ref))):
            h = jnp.dot(hn, w_ref[...], preferred_element_type=F32)
            hbuf_ref[slot, j, SUBLANES:SUBLANES + tt, :] = h
            prev = carry_ref[chunk, j]
            hbuf_ref[slot, j, 0:SUBLANES, :] = jnp.where(ti == 0, jnp.zeros_like(prev), prev)
            cw = cw_ref[...]
            conv = h * cw[FFN_CONV - 1:FFN_CONV, :]
            for i in range(FFN_CONV - 1):
                off = SUBLANES - (FFN_CONV - 1) + i
                conv = conv + hbuf_ref[slot, j, off:off + tt, :] * cw[i:i + 1, :]
            carry_ref[chunk, j] = h[tt - SUBLANES:, :]
            n_ref[j, :, pl.ds(pl.multiple_of(chunk * FFN_TF, FFN_TF), FFN_TF)] = h[tt - (FFN_CONV - 1):, :]
            convs.append(conv)
        return (_silu(convs[0]) * convs[1]).astype(BF16)

    @pl.when(f == 0)
    def _():
        acc_ref[...] = jnp.zeros(acc_ref.shape, F32)

    def both():
        a0 = act(0, 2 * f, wg0_ref, wu0_ref, cwg0_ref, cwu0_ref)
        a1 = act(1, 2 * f + 1, wg1_ref, wu1_ref, cwg1_ref, cwu1_ref)
        acc_ref[...] += (jnp.dot(a0, wd0_ref[...], preferred_element_type=F32)
                         + jnp.dot(a1, wd1_ref[...], preferred_element_type=F32))

    if FFN_NF % 2 == 0:
        both()
    else:
        pl.when(f < steps - 1)(both)

        @pl.when(f == steps - 1)
        def _():
            a0 = act(0, 2 * f, wg0_ref, wu0_ref, cwg0_ref, cwu0_ref)
            acc_ref[...] += jnp.dot(a0, wd0_ref[...], preferred_element_type=F32)

    @pl.when(f == steps - 1)
    def _():
        xo = x1_ref[...] + acc_ref[...]
        ms = jnp.mean(xo * xo, axis=-1, keepdims=True)
        y_ref[...] = xo * lax.rsqrt(ms + RMS_EPS) * fg_ref[...]


def _ffn_long(hn, x1, wup, cw, wdown, fg_row, tt):
    b, t, _ = hn.shape
    tf = FFN_TF
    nf = FFN_NF
    steps = -(-nf // 2)
    chunk = (lambda f: 2 * f, lambda f: jnp.minimum(2 * f + 1, nf - 1))
    weights = []
    for k in range(2):
        weights += [
            pl.BlockSpec((D_MODEL, tf), lambda i, s, f, k=k: (0, chunk[k](f))),
            pl.BlockSpec((D_MODEL, tf), lambda i, s, f, k=k: (0, nf + chunk[k](f))),
            pl.BlockSpec((FFN_CONV, tf), lambda i, s, f, k=k: (0, chunk[k](f))),
            pl.BlockSpec((FFN_CONV, tf), lambda i, s, f, k=k: (0, nf + chunk[k](f))),
            pl.BlockSpec((tf, D_MODEL), lambda i, s, f, k=k: (chunk[k](f), 0)),
        ]
    return pl.pallas_call(
        functools.partial(_ffn_long_kernel, tt=tt),
        grid=(b, t // tt, steps),
        in_specs=[
            pl.BlockSpec((None, tt, D_MODEL), lambda i, s, f: (i, s, 0)),
            pl.BlockSpec((None, tt, D_MODEL), lambda i, s, f: (i, s, 0)),
        ] + weights + [pl.BlockSpec((1, D_MODEL), lambda i, s, f: (0, 0))],
        out_specs=[
            pl.BlockSpec((None, tt, D_MODEL), lambda i, s, f: (i, s, 0)),
            pl.BlockSpec((None, None, 2, FFN_CONV - 1, D_FF), lambda i, s, f: (i, s, 0, 0, 0)),
        ],
        out_shape=[
            jax.ShapeDtypeStruct((b, t, D_MODEL), F32),
            jax.ShapeDtypeStruct((b, t // tt, 2, FFN_CONV - 1, D_FF), F32),
        ],
        scratch_shapes=[
            pltpu.VMEM((tt, D_MODEL), F32),
            pltpu.VMEM((nf, 2, SUBLANES, tf), F32),
            pltpu.VMEM((2, 2, SUBLANES + tt, tf), F32),
        ],
        compiler_params=pltpu.CompilerParams(
            dimension_semantics=("arbitrary", "arbitrary", "arbitrary"),
            vmem_limit_bytes=VMEM_LIMIT_FFN),
        name="ffn_long",
    )(hn, x1, wup, wup, cw, cw, wdown, wup, wup, cw, cw, wdown, fg_row)


def _ffn_short_kernel(hn_ref, x1_ref, wg_ref, wu_ref, cwg_ref, cwu_ref, wd_ref, fg_ref,
                      b0g_ref, b1g_ref, b0u_ref, b1u_ref,
                      y_ref, ng_ref, nu_ref, acc_ref, z_ref, hb_ref, *, tt, seq):
    f = pl.program_id(1)
    nseq = tt // seq

    def up_fn():
        hn = hn_ref[...]
        t_in_seq = lax.broadcasted_iota(jnp.int32, (tt, FFN_TF), 0) % seq
        convs = []
        groups = ((wg_ref, cwg_ref, b0g_ref, b1g_ref, ng_ref), (wu_ref, cwu_ref, b0u_ref, b1u_ref, nu_ref))
        for w_ref, cw_ref, b0_ref, b1_ref, n_ref in groups:
            h = jnp.dot(hn, w_ref[...], preferred_element_type=F32)
            z_ref[...] = jnp.zeros(z_ref.shape, F32)
            for lb in range(FFN_TF // LANES):
                cols = slice(lb * LANES, (lb + 1) * LANES)
                z_ref[lb, pl.ds(0, nseq, stride=seq), :] = b0_ref[:, cols]
                z_ref[lb, pl.ds(1, nseq, stride=seq), :] = b1_ref[:, cols]
                hb_ref[lb] = h[:, cols]
                n_ref[0, :, cols] = hb_ref[lb, pl.ds(seq - 2, nseq, stride=seq), :]
                n_ref[1, :, cols] = hb_ref[lb, pl.ds(seq - 1, nseq, stride=seq), :]
            z = jnp.concatenate([z_ref[lb] for lb in range(FFN_TF // LANES)], axis=1)
            s1 = jnp.where(t_in_seq == 0, pltpu.roll(z, tt - 1, 0), pltpu.roll(h, 1, 0))
            s2 = jnp.where(t_in_seq < 2, z, pltpu.roll(h, 2, 0))
            cw = cw_ref[...]
            convs.append(h * cw[2:3, :] + s1 * cw[1:2, :] + s2 * cw[0:1, :])
        return _silu(convs[0]) * convs[1]

    _ffn_pipeline(f, up_fn, wd_ref, x1_ref, fg_ref, y_ref, acc_ref)


def _ffn_short(hn, x1, wup, cw, wdown, fg_row, hist, tt, seq):
    m = hn.shape[0]
    tf = FFN_TF
    nf = FFN_NF
    nseq = tt // seq
    up = lambda f: f
    down = lambda f: f
    st = lambda k, col0: pl.BlockSpec((None, nseq, tf), lambda i, f: (k, i, col0 + up(f)))
    new = pl.BlockSpec((2, nseq, tf), lambda i, f: (0, i, up(f)))
    new_shape = jax.ShapeDtypeStruct((2, m // seq, D_FF), F32)
    return pl.pallas_call(
        functools.partial(_ffn_short_kernel, tt=tt, seq=seq),
        grid=(m // tt, nf),
        in_specs=[
            pl.BlockSpec((tt, D_MODEL), lambda i, f: (i, 0)),
            pl.BlockSpec((tt, D_MODEL), lambda i, f: (i, 0)),
            pl.BlockSpec((D_MODEL, tf), lambda i, f: (0, up(f))),
            pl.BlockSpec((D_MODEL, tf), lambda i, f: (0, nf + up(f))),
            pl.BlockSpec((FFN_CONV, tf), lambda i, f: (0, up(f))),
            pl.BlockSpec((FFN_CONV, tf), lambda i, f: (0, nf + up(f))),
            pl.BlockSpec((tf, D_MODEL), lambda i, f: (down(f), 0)),
            pl.BlockSpec((1, D_MODEL), lambda i, f: (0, 0)),
            st(0, 0), st(1, 0), st(0, nf), st(1, nf),
        ],
        out_specs=[pl.BlockSpec((tt, D_MODEL), lambda i, f: (i, 0)), new, new],
        out_shape=[jax.ShapeDtypeStruct((m, D_MODEL), F32), new_shape, new_shape],
        scratch_shapes=[
            pltpu.VMEM((tt, D_MODEL), F32),
            pltpu.VMEM((tf // LANES, tt, LANES), F32),
            pltpu.VMEM((tf // LANES, tt, LANES), F32),
        ],
        compiler_params=pltpu.CompilerParams(
            dimension_semantics=("arbitrary", "arbitrary"), vmem_limit_bytes=VMEM_LIMIT),
        name="ffn_short",
    )(hn, x1, wup, wup, cw, cw, wdown, fg_row, hist, hist, hist, hist)


def _pad_lanes(vec, offset):
    out = jnp.zeros((LANES,), F32)
    return out.at[offset:offset + vec.shape[0]].set(vec.astype(F32))


def _trunk(x, s_gdn, s_gconv, s_rwkv, s_shift, s_ffn, prm, *, long_seq):
    b, t, _ = x.shape
    m = b * t
    x2d = x.reshape(m, D_MODEL)
    tm = min(512, m)
    proj = _inproj(x2d, prm["ln1_g"], prm["w_in"], min(1024, m))
    gconv_new = proj.reshape(b, t, PROJ_WIDTH)[:, t - (GDN_CONV - 1):, COL_QKV:COL_QKV + 3 * GDN_WIDTH]

    o_a, gdn_new = _gdn_mixer(proj, s_gconv, s_gdn, prm["gdn_conv_w"], prm["alog_r"], prm["dtb_r"],
                              prm["alog_c"], prm["dtb_c"], prm["gdn_norm_g"], t)
    o_b, rwkv_new = _rwkv_mixer(proj, s_shift, s_shift, s_rwkv, prm["mu_rkv"], prm["mu_lora"],
                                prm["rwkv_w0"], prm["rwkv_a0"], prm["rwkv_wab"], prm["rwkv_g_b"],
                                prm["rwkv_k_k"], prm["rwkv_k_a"], prm["rwkv_r_k"], prm["rwkv_gn_w"],
                                prm["rwkv_gn_b"], t)

    x1, hn = _outproj(x2d, o_a, o_b, prm["w_o"], prm["ln2_g"], tm)
    if long_seq:
        tt = min(512, t)
        y, n_gu = _ffn_long(hn.reshape(b, t, D_MODEL), x1.reshape(b, t, D_MODEL), prm["ffn_w_up"],
                            prm["ffn_conv_w"], prm["ffn_w_down"], prm["final_g"], tt)
        ffn_new = jnp.concatenate([n_gu[:, -1, 0], n_gu[:, -1, 1]], axis=-1)
    else:
        tt = min(512, m)
        y, n_g, n_u = _ffn_short(hn, x1, prm["ffn_w_up"], prm["ffn_conv_w"], prm["ffn_w_down"],
                                 prm["final_g"], jnp.swapaxes(s_ffn, 0, 1), tt, t)
        y = y.reshape(b, t, D_MODEL)
        ffn_new = jnp.swapaxes(jnp.concatenate([n_g, n_u], axis=-1), 0, 1)

    proj = proj.reshape(b, t, PROJ_WIDTH)
    shift_new = jnp.concatenate([proj[:, t - 1, COL_RKV:COL_RKV + 3 * RWKV_WIDTH],
                                 proj[:, t - 1, COL_LORA:COL_LORA + RWKV_LORA]], axis=-1)
    return y, gdn_new[None], gconv_new[None], rwkv_new[None], shift_new[None], ffn_new[None]


def kernel(x_prompt, x_sample, state_gdn, state_gdn_conv, state_rwkv, state_rwkv_shift, state_ffn_conv, ln1_g, w_in, gdn_conv_w, gdn_a_log, gdn_dt_bias, gdn_norm_g, rwkv_mu, rwkv_w0, rwkv_w_b, rwkv_a0, rwkv_a_b, rwkv_g_b, rwkv_k_k, rwkv_k_a, rwkv_r_k, rwkv_gn_w, rwkv_gn_b, w_o, ln2_g, ffn_w_up, ffn_conv_w, ffn_w_down, final_g):
    assert ln1_g.shape[0] == 1, "single-layer trunk"
    w_perm = _permute_win(w_in[0])
    mu = rwkv_mu[0]
    zeros_w = jnp.zeros((RWKV_LORA_W, RWKV_WIDTH), F32)
    wab = jnp.concatenate([
        jnp.concatenate([rwkv_w_b[0], zeros_w], axis=1),
        jnp.concatenate([zeros_w, rwkv_a_b[0]], axis=1)], axis=0).astype(BF16)
    alog = _pad_lanes(gdn_a_log[0], GDN_HEADS)
    dtb = _pad_lanes(gdn_dt_bias[0], GDN_HEADS)
    prm = {
        "ln1_g": ln1_g[0][None], "w_in": w_perm, "gdn_conv_w": gdn_conv_w[0],
        "alog_r": alog[None], "dtb_r": dtb[None], "alog_c": alog[:, None], "dtb_c": dtb[:, None],
        "gdn_norm_g": gdn_norm_g[0][None],
        "mu_rkv": mu[None, :3 * RWKV_WIDTH], "mu_lora": mu[None, 3 * RWKV_WIDTH:],
        "rwkv_w0": rwkv_w0[0][None], "rwkv_a0": rwkv_a0[0][None], "rwkv_wab": wab,
        "rwkv_g_b": rwkv_g_b[0].astype(BF16), "rwkv_k_k": rwkv_k_k[0][None],
        "rwkv_k_a": rwkv_k_a[0][None], "rwkv_r_k": rwkv_r_k[0].reshape(1, RWKV_WIDTH),
        "rwkv_gn_w": rwkv_gn_w[0][None], "rwkv_gn_b": rwkv_gn_b[0][None],
        "w_o": w_o[0].astype(BF16), "ln2_g": ln2_g[0][None],
        "ffn_w_up": ffn_w_up[0].astype(BF16), "ffn_conv_w": ffn_conv_w[0],
        "ffn_w_down": ffn_w_down[0].astype(BF16), "final_g": final_g[None],
    }

    bp = x_prompt.shape[0]
    zero_states = (
        jnp.zeros((bp,) + state_gdn.shape[2:], F32),
        jnp.zeros((bp,) + state_gdn_conv.shape[2:], F32),
        jnp.zeros((bp,) + state_rwkv.shape[2:], F32),
        jnp.zeros((bp,) + state_rwkv_shift.shape[2:], F32),
        None,
    )
    outs_p = _trunk(x_prompt, *zero_states, prm, long_seq=True)
    outs_s = _trunk(x_sample, state_gdn[0], state_gdn_conv[0], state_rwkv[0], state_rwkv_shift[0],
                    state_ffn_conv[0], prm, long_seq=False)
    return (outs_p[0], outs_s[0]) + tuple(outs_p[1:]) + tuple(outs_s[1:])
```

```python
import functools

import jax
import jax.numpy as jnp
from jax import lax
from jax.experimental import pallas as pl
from jax.experimental.pallas import tpu as pltpu

F32 = jnp.float32
BF16 = jnp.bfloat16

D_MODEL = 2048
GDN_WIDTH = 1024
GDN_HEADS = 8
GDN_DK = 128
GDN_CONV = 4
RWKV_WIDTH = 1024
RWKV_HEAD = 64
RWKV_HEADS = 16
RWKV_LORA_W = 64
RWKV_LORA_A = 64
RWKV_LORA_G = 128
RWKV_LORA = RWKV_LORA_W + RWKV_LORA_A + RWKV_LORA_G
RWKV_PROJ = 3 * RWKV_WIDTH + RWKV_LORA
D_FF = 5632
FFN_CONV = 3
RMS_EPS = 1e-6
L2_EPS = 1e-12
GN_EPS = 64e-5

REF_OFF_Z = 3 * GDN_WIDTH
REF_OFF_B = 4 * GDN_WIDTH
REF_OFF_RWKV = REF_OFF_B + 2 * GDN_HEADS
REF_IN_WIDTH = REF_OFF_RWKV + RWKV_PROJ

LANES = 128
SUBLANES = 8
COL_QKV = 0
COL_RKV = 3 * GDN_WIDTH
COL_Z = COL_RKV + 3 * RWKV_WIDTH
COL_LORA = COL_Z + GDN_WIDTH
COL_BA = COL_LORA + RWKV_LORA
PROJ_WIDTH = 7680
PROJ_TN = 1280

MIX_ROWS = 64
MIX_GROUPS_LONG = 4
MIX_GROUPS_SHORT = 2

NN = (((1,), (0,)), ((), ()))
NT = (((1,), (1,)), ((), ()))
TN = (((0,), (0,)), ((), ()))

VMEM_LIMIT = 56 * 1024 * 1024
VMEM_LIMIT_FFN = 62 * 1024 * 1024

P_GRAM = "x1"
P_INV = "x1"
P_SOLVE = "x1"
P_STATE = "x1"
P_OUT = "x1"


def _split(x):
    hi = x.astype(BF16)
    return hi, (x - hi.astype(F32)).astype(BF16)


def _dot(a, b, dims=NN, mode="x1"):
    if mode == "hi":
        return lax.dot_general(a, b, dims, precision=lax.Precision.HIGHEST,
                               preferred_element_type=F32)
    if mode == "x3":
        a_hi, a_lo = _split(a)
        b_hi, b_lo = _split(b)
        d = lambda u, v: lax.dot_general(u, v, dims, preferred_element_type=F32)
        return d(a_hi, b_hi) + (d(a_hi, b_lo) + d(a_lo, b_hi))
    return lax.dot_general(a.astype(BF16), b.astype(BF16), dims, preferred_element_type=F32)


def _ones_dot(ones_mat, x, dims=NN):
    x1 = x.astype(BF16)
    r1 = x - x1.astype(F32)
    x2 = r1.astype(BF16)
    x3 = (r1 - x2.astype(F32)).astype(BF16)
    m = ones_mat.astype(BF16)
    if dims == NN:
        d = lambda v: lax.dot_general(m, v, dims, preferred_element_type=F32)
    else:
        d = lambda v: lax.dot_general(v, m, dims, preferred_element_type=F32)
    return d(x1) + (d(x2) + d(x3))


def _sigmoid(x):
    return 1.0 / (1.0 + jnp.exp(-x))


def _silu(x):
    return x * _sigmoid(x)


def _softplus(x):
    return jnp.maximum(x, 0.0) + jnp.log(1.0 + jnp.exp(-jnp.abs(x)))


def _seq_masks(rows, seq_len):
    r = lax.broadcasted_iota(jnp.int32, (rows, rows), 0)
    c = lax.broadcasted_iota(jnp.int32, (rows, rows), 1)
    if seq_len >= rows:
        return None, r >= c, r > c
    shift = seq_len.bit_length() - 1
    assert 1 << shift == seq_len
    same = jnp.right_shift(r, shift) == jnp.right_shift(c, shift)
    return same, same & (r >= c), same & (r > c)


def _wide_masks(rows, seq_len):
    r = lax.broadcasted_iota(jnp.int32, (rows, 2 * rows), 0)
    c = lax.broadcasted_iota(jnp.int32, (rows, 2 * rows), 1)
    right = c >= rows
    cc = jnp.where(right, c - rows, c)
    if seq_len >= rows:
        return r >= cc, right & (r > cc)
    shift = seq_len.bit_length() - 1
    same = jnp.right_shift(r, shift) == jnp.right_shift(cc, shift)
    return same & (r >= cc), same & right & (r > cc)


def _pair_masks(rows, seq_len):
    half = LANES // 2
    assert rows == half
    lane = lax.broadcasted_iota(jnp.int32, (rows, LANES), 1)
    row = lax.broadcasted_iota(jnp.int32, (rows, LANES), 0)
    col = jnp.bitwise_and(lane, half - 1)
    if seq_len >= rows:
        causal, strict = row >= col, row > col
    else:
        shift = seq_len.bit_length() - 1
        same = jnp.right_shift(row, shift) == jnp.right_shift(col, shift)
        causal, strict = same & (row >= col), same & (row > col)
    return causal, strict, jnp.where(row == col, 1.0, 0.0).astype(F32), lane < half


def _bd(x, first_half):
    return jnp.concatenate([jnp.where(first_half, x, 0.0), jnp.where(first_half, 0.0, x)], axis=0)


def _pair_inverses(neg_a, rows, nilpotency, eye2, first_half):
    qs = dict(neg_a)
    ts = {u: eye2 + q for u, q in qs.items()}
    n = 2
    if n < nilpotency:
        qs = {u: _dot(q, _bd(q, first_half), mode=P_INV) for u, q in qs.items()}
    while n < nilpotency:
        if 2 * n < nilpotency:
            tq = {u: _dot(jnp.concatenate([ts[u], qs[u]], axis=0), _bd(qs[u], first_half), mode=P_INV)
                  for u in qs}
            ts = {u: ts[u] + tq[u][:rows] for u in qs}
            qs = {u: tq[u][rows:] for u in qs}
        else:
            ts = {u: ts[u] + _dot(ts[u], _bd(qs[u], first_half), mode=P_INV) for u in qs}
        n *= 2
    return ts


def _lane_group_sums(x, group):
    tile = 2 * LANES
    shift = group.bit_length() - 1
    li = jnp.right_shift(lax.broadcasted_iota(jnp.int32, (tile, tile), 0), shift)
    lj = jnp.right_shift(lax.broadcasted_iota(jnp.int32, (tile, tile), 1), shift)
    ones = jnp.where(li == lj, 1.0, 0.0).astype(BF16)
    hi, lo = _split(x)
    d = lambda u: lax.dot_general(u, ones, NN, preferred_element_type=F32)
    return jnp.concatenate([d(hi[:, t:t + tile]) + d(lo[:, t:t + tile])
                            for t in range(0, x.shape[1], tile)], axis=1)


def _unit_lower_inverses(mats, rows, nilpotency):
    r = lax.broadcasted_iota(jnp.int32, (rows, rows), 0)
    c = lax.broadcasted_iota(jnp.int32, (rows, rows), 1)
    eye = jnp.where(r == c, 1.0, 0.0).astype(F32)
    qs = [-a for a in mats]
    ts = [eye + q for q in qs]
    n = 2
    if n < nilpotency:
        qs = [_dot(q, q, mode=P_INV) for q in qs]
    while n < nilpotency:
        if 2 * n < nilpotency:
            tq = [_dot(jnp.concatenate([t, q], axis=0), q, mode=P_INV) for t, q in zip(ts, qs)]
            ts = [t + p[:rows] for t, p in zip(ts, tq)]
            qs = [p[rows:] for p in tq]
        else:
            ts = [t + _dot(t, q, mode=P_INV) for t, q in zip(ts, qs)]
        n *= 2
    return ts


def _permute_win_kernel(w_ref, o_ref):
    rw = REF_OFF_RWKV
    rows = w_ref.shape[0]
    cast = lambda lo, hi: w_ref[:, lo:hi].astype(BF16)
    o_ref[:, COL_QKV:COL_RKV] = cast(0, REF_OFF_Z)
    o_ref[:, COL_RKV:COL_Z] = cast(rw, rw + 3 * RWKV_WIDTH)
    o_ref[:, COL_Z:COL_LORA] = cast(REF_OFF_Z, REF_OFF_B)
    o_ref[:, COL_LORA:COL_BA] = cast(rw + 3 * RWKV_WIDTH, REF_IN_WIDTH)
    tail = jnp.concatenate([w_ref[:, REF_OFF_B:REF_OFF_RWKV],
                            jnp.zeros((rows, PROJ_WIDTH - COL_BA - 2 * GDN_HEADS), F32)], axis=1)
    o_ref[:, COL_BA:] = tail.astype(BF16)


def _permute_win(w, tr=256):
    return pl.pallas_call(
        _permute_win_kernel,
        grid=(D_MODEL // tr,),
        in_specs=[pl.BlockSpec((tr, REF_IN_WIDTH), lambda i: (i, 0))],
        out_specs=pl.BlockSpec((tr, PROJ_WIDTH), lambda i: (i, 0)),
        out_shape=jax.ShapeDtypeStruct((D_MODEL, PROJ_WIDTH), BF16),
        compiler_params=pltpu.CompilerParams(
            dimension_semantics=("arbitrary",), vmem_limit_bytes=VMEM_LIMIT),
        name="permute_win",
    )(w)


def _inproj_kernel(x_ref, g_ref, w_ref, o_ref, xn_ref):
    @pl.when(pl.program_id(1) == 0)
    def _():
        x = x_ref[...]
        ms = jnp.mean(x * x, axis=-1, keepdims=True)
        xn_ref[...] = (x * lax.rsqrt(ms + RMS_EPS) * g_ref[...]).astype(BF16)

    o_ref[...] = jnp.dot(xn_ref[...], w_ref[...], preferred_element_type=F32)


def _inproj(x2d, g_row, w_bf16, tm):
    m = x2d.shape[0]
    return pl.pallas_call(
        _inproj_kernel,
        grid=(m // tm, PROJ_WIDTH // PROJ_TN),
        in_specs=[
            pl.BlockSpec((tm, D_MODEL), lambda i, j: (i, 0)),
            pl.BlockSpec((1, D_MODEL), lambda i, j: (0, 0)),
            pl.BlockSpec((D_MODEL, PROJ_TN), lambda i, j: (0, j)),
        ],
        out_specs=pl.BlockSpec((tm, PROJ_TN), lambda i, j: (i, j)),
        out_shape=jax.ShapeDtypeStruct((m, PROJ_WIDTH), F32),
        scratch_shapes=[pltpu.VMEM((tm, D_MODEL), BF16)],
        compiler_params=pltpu.CompilerParams(
            dimension_semantics=("arbitrary", "arbitrary"), vmem_limit_bytes=VMEM_LIMIT),
        name="inproj",
    )(x2d, g_row, w_bf16)


def _mixer_geometry(seq_len, n_seq, short_groups=1):
    rows = MIX_ROWS
    length = min(seq_len, rows)
    assert rows % length == 0 and seq_len % length == 0 and length % SUBLANES == 0
    per_group = rows // length
    groups = MIX_GROUPS_LONG if per_group == 1 else short_groups
    while n_seq % (groups * per_group):
        groups //= 2
    assert groups >= 1
    return rows, length, per_group, groups, seq_len // length


def _gdn_kernel(qkv_ref, z_ref, ba_ref, cbuf_ref, s0_ref, convw_ref, alog_r_ref, dtb_r_ref,
                alog_c_ref, dtb_c_ref, ng_ref, o_ref, s_ref, tail_ref, xp_ref, *, seq_len, n_seq):
    R, L, G, S, _ = _mixer_geometry(seq_len, n_seq, MIX_GROUPS_SHORT)
    RT = S * R
    c = pl.program_id(1)
    width = 3 * GDN_WIDTH
    hist = GDN_CONV - 1
    cw = convw_ref[...]
    groups = range(S)
    seqs = range(G)

    @pl.when(c == 0)
    def _():
        s_ref[...] = s0_ref[...]

    pieces = []
    if G == 1:
        @pl.when(c == 0)
        def _():
            for s in groups:
                xp_ref[s, 0:SUBLANES, :] = jnp.zeros((SUBLANES, width), F32)
                xp_ref[s, SUBLANES - hist:SUBLANES, :] = cbuf_ref[s]

        @pl.when(c > 0)
        def _():
            for s in groups:
                xp_ref[s, 0:SUBLANES, :] = xp_ref[s, R:R + SUBLANES, :]

        for s in groups:
            xp_ref[s, SUBLANES:SUBLANES + R, :] = qkv_ref[s]
            piece = qkv_ref[s] * cw[hist:hist + 1, :]
            for i in range(hist):
                off = SUBLANES - hist + i
                piece = piece + xp_ref[s, off:off + R, :] * cw[i:i + 1, :]
            pieces.append(piece)
    else:
        for s in groups:
            for g in seqs:
                q = s * G + g
                rows = slice(g * L, (g + 1) * L)
                xp_ref[q, SUBLANES - hist:SUBLANES, :] = cbuf_ref[q]
                xp_ref[q, SUBLANES:SUBLANES + L, :] = qkv_ref[s, rows, :]
                piece = qkv_ref[s, rows, :] * cw[hist:hist + 1, :]
                for i in range(hist):
                    off = SUBLANES - hist + i
                    piece = piece + xp_ref[q, off:off + L, :] * cw[i:i + 1, :]
                pieces.append(piece)
    qkv = _silu(pieces[0] if len(pieces) == 1 else jnp.concatenate(pieces, axis=0))

    same_t, causal_t, _ = _seq_masks(RT, L)
    causal01 = jnp.where(causal_t, 1.0, 0.0).astype(F32)

    ba = ba_ref[...].reshape(RT, LANES)
    ba_t = ba.T
    beta_c = _sigmoid(ba)
    g_c = -jnp.exp(alog_r_ref[...]) * _softplus(ba + dtb_r_ref[...])
    g_r = -jnp.exp(alog_c_ref[...]) * _softplus(ba_t + dtb_c_ref[...])
    gc_all = _ones_dot(causal01, g_c)
    gr_all = _ones_dot(causal01, g_r, NT)
    if same_t is None:
        gtot_all = jnp.broadcast_to(gc_all[RT - 1:RT, :], (RT, LANES))
    else:
        gtot_all = _ones_dot(jnp.where(same_t, 1.0, 0.0).astype(F32), g_c)

    qk_raw = qkv[:, :2 * GDN_WIDTH]
    qk_n = qk_raw * lax.rsqrt(_lane_group_sums(qk_raw * qk_raw, GDN_DK) + L2_EPS)
    q_all = qk_n[:, :GDN_WIDTH] * (GDN_DK ** -0.5)
    k_all = qk_n[:, GDN_WIDTH:]
    v_all = qkv[:, 2 * GDN_WIDTH:]

    causal2, strict2, eye2, first_half = _pair_masks(R, L)
    first_head = lax.broadcasted_iota(jnp.int32, (R, 2 * GDN_DK), 1) < GDN_DK
    zeros_u = jnp.zeros((R, GDN_DK), F32)
    zeros_rhs = jnp.zeros((R, 2 * GDN_DK), F32)
    chains = [(s, h) for s in groups for h in range(GDN_HEADS)]
    units = [(s, p) for s in groups for p in range(GDN_HEADS // 2)]
    seq_rows = [slice(g * L, (g + 1) * L) for g in seqs]
    qs, ks, vs, betas, gcols, gtots = {}, {}, {}, {}, {}, {}
    for s, h in chains:
        rs = slice(s * R, (s + 1) * R)
        lo = h * GDN_DK
        key = (s, h)
        qs[key] = q_all[rs, lo:lo + GDN_DK]
        ks[key] = k_all[rs, lo:lo + GDN_DK]
        vs[key] = v_all[rs, lo:lo + GDN_DK]
        betas[key] = beta_c[rs, h:h + 1]
        gcols[key] = gc_all[rs, GDN_HEADS + h:GDN_HEADS + h + 1]
        gtots[key] = gtot_all[rs, GDN_HEADS + h:GDN_HEADS + h + 1]
    kq2, decay2, a2 = {}, {}, {}
    for s, p in units:
        rs = slice(s * R, (s + 1) * R)
        cols = slice(2 * p * GDN_DK, (2 * p + 2) * GDN_DK)
        k_pair = k_all[rs, cols]
        kt_bd = jnp.concatenate([jnp.where(first_head, k_pair, 0.0),
                                 jnp.where(first_head, 0.0, k_pair)], axis=0).T
        kq = _dot(jnp.concatenate([k_pair, q_all[rs, cols]], axis=0), kt_bd, mode=P_GRAM)
        h0, h1 = (s, 2 * p), (s, 2 * p + 1)
        gcol2 = jnp.where(first_half, gcols[h0], gcols[h1])
        grow2 = jnp.concatenate([gr_all[GDN_HEADS + 2 * p:GDN_HEADS + 2 * p + 1, rs],
                                 gr_all[GDN_HEADS + 2 * p + 1:GDN_HEADS + 2 * p + 2, rs]], axis=1)
        dec = jnp.where(causal2, jnp.exp(jnp.where(causal2, gcol2 - grow2, 0.0)), 0.0)
        beta2 = jnp.where(first_half, betas[h0], betas[h1])
        kq2[(s, p)] = kq
        decay2[(s, p)] = dec
        a2[(s, p)] = jnp.where(strict2, -(beta2 * kq[:R] * dec), 0.0)
    t2 = _pair_inverses(a2, R, L, eye2, first_half)
    gammas = {key: jnp.exp(gcols[key]) for key in chains}

    def stacked(key, x, zeros):
        return jnp.concatenate([x, zeros] if key[1] % 2 == 0 else [zeros, x], axis=0)

    sols = {key: _dot(t2[(key[0], key[1] // 2)],
                      stacked(key, jnp.concatenate([(betas[key] * gammas[key]) * ks[key],
                                                    betas[key] * vs[key]], axis=1), zeros_rhs),
                      mode=P_SOLVE) for key in chains}
    states = {(s, h): [s_ref[s * G + g, h] for g in seqs] for s, h in chains}
    wss = {key: [_dot(jnp.concatenate([sols[key][rows, :GDN_DK], (qs[key] * gammas[key])[rows]], axis=0),
                      states[key][g], mode=P_STATE) for g, rows in enumerate(seq_rows)]
           for key in chains}
    us = {key: jnp.concatenate([sols[key][rows, GDN_DK:] - wss[key][g][:L]
                                for g, rows in enumerate(seq_rows)], axis=0) for key in chains}
    qk2 = {u: kq2[u][R:] * decay2[u] for u in units}
    outs = {key: jnp.concatenate([wss[key][g][L:] for g in seqs], axis=0)
            + _dot(qk2[(key[0], key[1] // 2)], stacked(key, us[key], zeros_u), mode=P_OUT)
            for key in chains}
    for s, h in chains:
        key = (s, h)
        kt = ks[key] * jnp.exp(gtots[key] - gcols[key])
        for g, rows in enumerate(seq_rows):
            gl = jnp.exp(gtots[key][g * L:g * L + 1, :])
            s_ref[s * G + g, h] = gl * states[key][g] + _dot(kt[rows], us[key][rows], TN, mode=P_STATE)
    ng = jnp.concatenate([ng_ref[...]] * GDN_HEADS, axis=1)
    for s in groups:
        o = jnp.concatenate([outs[(s, h)] for h in range(GDN_HEADS)], axis=1)
        ms = _lane_group_sums(o * o, GDN_DK) * (1.0 / GDN_DK)
        o = o * lax.rsqrt(ms + RMS_EPS) * ng
        o_ref[s] = (o * _silu(z_ref[s])).astype(o_ref.dtype)
    for s in groups:
        for g in seqs:
            tail_ref[s * G + g] = qkv_ref[s, (g + 1) * L - hist:(g + 1) * L, :]


def _gdn_mixer(proj2d, cbuf, s0, convw, alog_r, dtb_r, alog_c, dtb_c, ng, seq_len):
    m = proj2d.shape[0]
    nseq = m // seq_len
    R, L, G, S, sps = _mixer_geometry(seq_len, nseq, MIX_GROUPS_SHORT)
    width = 3 * GDN_WIDTH
    proj3d = proj2d.reshape(nseq // G, sps * R, PROJ_WIDTH)
    const2 = lambda i, c: (0, 0)
    rows_map = lambda col: (lambda i, c: (i, c, col))
    per_seq = lambda *dims: pl.BlockSpec((S * G,) + dims, lambda i, c: (i,) + (0,) * len(dims))
    xp_shape = (S, SUBLANES + R, width) if G == 1 else (S * G, SUBLANES + L, width)
    o, s_new, conv_new = pl.pallas_call(
        functools.partial(_gdn_kernel, seq_len=seq_len, n_seq=nseq),
        grid=(nseq // (S * G), sps),
        in_specs=[
            pl.BlockSpec((S, R, width), rows_map(COL_QKV // width)),
            pl.BlockSpec((S, R, GDN_WIDTH), rows_map(COL_Z // GDN_WIDTH)),
            pl.BlockSpec((S, R, LANES), rows_map(COL_BA // LANES)),
            per_seq(GDN_CONV - 1, width),
            per_seq(GDN_HEADS, GDN_DK, GDN_DK),
            pl.BlockSpec((GDN_CONV, width), const2),
            pl.BlockSpec((1, LANES), const2),
            pl.BlockSpec((1, LANES), const2),
            pl.BlockSpec((LANES, 1), const2),
            pl.BlockSpec((LANES, 1), const2),
            pl.BlockSpec((1, GDN_DK), const2),
        ],
        out_specs=[
            pl.BlockSpec((S, R, GDN_WIDTH), rows_map(0)),
            per_seq(GDN_HEADS, GDN_DK, GDN_DK),
            per_seq(GDN_CONV - 1, width),
        ],
        out_shape=[
            jax.ShapeDtypeStruct((nseq // G, sps * R, GDN_WIDTH), BF16),
            jax.ShapeDtypeStruct((nseq, GDN_HEADS, GDN_DK, GDN_DK), F32),
            jax.ShapeDtypeStruct((nseq, GDN_CONV - 1, width), F32),
        ],
        scratch_shapes=[pltpu.VMEM(xp_shape, F32)],
        compiler_params=pltpu.CompilerParams(
            dimension_semantics=("arbitrary", "arbitrary"), vmem_limit_bytes=VMEM_LIMIT),
        name="gdn_mixer",
    )(proj3d, proj3d, proj3d, cbuf, s0, convw, alog_r, dtb_r, alog_c, dtb_c, ng)
    return o.reshape(m, GDN_WIDTH), s_new, conv_new


def _shifted_rows(x_ref, prev_ref, carry_ref, c, R, L, G, S):
    width = x_ref.shape[-1]
    groups = range(S)
    if G == 1:
        @pl.when(c == 0)
        def _():
            for s in groups:
                carry_ref[s, 0:SUBLANES, :] = jnp.zeros((SUBLANES, width), F32)
                carry_ref[s, SUBLANES - 1:SUBLANES, :] = prev_ref[s:s + 1, :]

        @pl.when(c > 0)
        def _():
            for s in groups:
                carry_ref[s, 0:SUBLANES, :] = carry_ref[s, R:R + SUBLANES, :]

        xs, prevs = [], []
        for s in groups:
            carry_ref[s, SUBLANES:SUBLANES + R, :] = x_ref[s]
            xs.append(x_ref[s])
            prevs.append(carry_ref[s, SUBLANES - 1:SUBLANES - 1 + R, :])
    else:
        row = lax.broadcasted_iota(jnp.int32, (L, width), 0)
        xs, prevs = [], []
        for s in groups:
            x = x_ref[s]
            xs.append(x)
            for g in range(G):
                xg = x[g * L:(g + 1) * L]
                q = s * G + g
                prevs.append(jnp.where(row == 0, prev_ref[q:q + 1, :], pltpu.roll(xg, 1, 0)))
    cat = lambda parts: parts[0] if len(parts) == 1 else jnp.concatenate(parts, axis=0)
    return cat(xs), cat(prevs)


def _rwkv_kernel(rkv_ref, lora_ref, sh_rkv_ref, sh_lora_ref, s0_ref, mu_rkv_ref, mu_lora_ref,
                 w0_ref, a0_ref, wab_ref, gb_ref, kk_ref, ka_ref, rk_ref, gnw_ref, gnb_ref,
                 o_ref, s_ref, last_rkv_ref, last_lora_ref, xr_ref, xl_ref, m_ref, *, seq_len, n_seq):
    R, L, G, S, sps = _mixer_geometry(seq_len, n_seq)
    RT = S * R
    NQ = S * G
    c = pl.program_id(1)
    W = RWKV_WIDTH
    HD = RWKV_HEAD
    pairs = range(RWKV_HEADS // 2)
    groups = range(S)
    seqs = range(G)

    @pl.when(c == 0)
    def _():
        zero = jnp.zeros((HD, HD), F32)
        for q in range(NQ):
            for j in pairs:
                vk = jnp.concatenate([jnp.concatenate([s0_ref[q, 2 * j], zero], axis=1),
                                      jnp.concatenate([zero, s0_ref[q, 2 * j + 1]], axis=1)], axis=0)
                m_ref[q, j] = vk.T

    p, p_prev = _shifted_rows(rkv_ref, sh_rkv_ref, xr_ref, c, R, L, G, S)
    xs = p + (p_prev - p) * mu_rkv_ref[...]
    pl_, pl_prev = _shifted_rows(lora_ref, sh_lora_ref, xl_ref, c, R, L, G, S)
    xl = pl_ + (pl_prev - pl_) * mu_lora_ref[...]
    r = xs[:, :W]
    k = xs[:, W:2 * W]
    v = xs[:, 2 * W:]

    wa_in = xl[:, :LANES]
    lane = lax.broadcasted_iota(jnp.int32, (RT, LANES), 1)
    wa_in = jnp.where(lane < RWKV_LORA_W, jnp.tanh(wa_in), wa_in)
    wa = _dot(wa_in, wab_ref[...])
    w = -_softplus(-(w0_ref[...] + wa[:, :W])) - 0.5
    lw = -jnp.exp(w)
    a = _sigmoid(a0_ref[...] + wa[:, W:])
    gate = _dot(_sigmoid(xl[:, LANES:]), gb_ref[...])
    kk_raw = k * kk_ref[...]
    k2 = k * (1.0 + (a - 1.0) * ka_ref[...])

    same_t, causal_t, _ = _seq_masks(RT, L)
    lc = _ones_dot(jnp.where(causal_t, 1.0, 0.0).astype(F32), lw)
    if G == 1:
        ltot = jnp.concatenate([jnp.broadcast_to(lc[(s + 1) * R - 1:(s + 1) * R, :], (R, W))
                                for s in groups], axis=0) if S > 1 else \
            jnp.broadcast_to(lc[R - 1:R, :], (R, W))
    else:
        ltot = _ones_dot(jnp.where(same_t, 1.0, 0.0).astype(F32), lw)
    e_inv = jnp.exp(-lc)
    e_rem = jnp.exp(ltot - lc)

    kk = kk_raw * lax.rsqrt(_lane_group_sums(kk_raw * kk_raw, HD) + L2_EPS)
    kka = kk * a
    ct = kk * jnp.exp(lc - lw)
    rt = r * jnp.exp(lc)
    bh = kka * e_inv
    kh = k2 * e_inv
    bb = kka * e_rem
    kb = k2 * e_rem
    p_rows = [ltot[q * L:q * L + 1] for q in range(NQ)]
    p_rows = p_rows + [p_rows[0]] * (-NQ % SUBLANES)
    pt = jnp.exp(jnp.concatenate(p_rows, axis=0)).T

    causal2, strict2, eye2, even_half = _pair_masks(R, L)
    block_diag = ((lax.broadcasted_iota(jnp.int32, (LANES, LANES), 0) < HD)
                  == (lax.broadcasted_iota(jnp.int32, (LANES, LANES), 1) < HD))
    even_all = jnp.bitwise_and(lax.broadcasted_iota(jnp.int32, (R, W), 1), LANES - 1) < HD
    tile = lambda j: slice(j * LANES, (j + 1) * LANES)
    bd = lambda x: _bd(x, even_half)

    if G > 1:
        row2 = lax.broadcasted_iota(jnp.int32, (2 * R, LANES), 0)
        seq_of_row = jnp.right_shift(jnp.bitwise_and(row2, R - 1), L.bit_length() - 1)

    units = [(s, j) for s in groups for j in pairs]
    xps, vps, bdb, bdk, bkts = {}, {}, {}, {}, {}
    for s in groups:
        rs = slice(s * R, (s + 1) * R)
        x_s = jnp.concatenate([ct[rs], rt[rs]], axis=0)
        bt_s = jnp.concatenate([jnp.where(even_all, bh[rs], 0.0), jnp.where(even_all, 0.0, bh[rs])],
                               axis=0).T
        kt_s = jnp.concatenate([jnp.where(even_all, kh[rs], 0.0), jnp.where(even_all, 0.0, kh[rs])],
                               axis=0).T
        bkt_s = jnp.concatenate([bb[rs], kb[rs]], axis=0).T
        for j in pairs:
            xps[(s, j)] = x_s[:, tile(j)]
            vps[(s, j)] = v[rs, tile(j)]
            bdb[(s, j)] = bt_s[tile(j), :]
            bdk[(s, j)] = kt_s[tile(j), :]
            bkts[(s, j)] = bkt_s[tile(j), :]
    gb = {u: _dot(xps[u], bdb[u], mode=P_GRAM) for u in units}
    gk = {u: _dot(xps[u], bdk[u], mode=P_GRAM) for u in units}
    ms = {(s, j): [m_ref[s * G + g, j] for g in seqs] for s, j in units}
    xm_c, xm_r = {}, {}
    for u in units:
        if G == 1:
            xm = _dot(xps[u], ms[u][0], mode=P_STATE)
            xm_c[u], xm_r[u] = xm[:R], xm[R:]
        else:
            parts = [_dot(jnp.concatenate([xps[u][g * L:(g + 1) * L],
                                           xps[u][R + g * L:R + (g + 1) * L]], axis=0),
                          ms[u][g], mode=P_STATE) for g in seqs]
            xm_c[u] = jnp.concatenate([p_[:L] for p_ in parts], axis=0)
            xm_r[u] = jnp.concatenate([p_[L:] for p_ in parts], axis=0)
    ts = _pair_inverses({u: jnp.where(strict2, -gb[u][:R], 0.0) for u in units}, R, L, eye2, even_half)
    bdv = {u: bd(vps[u]) for u in units}
    akvs = {u: _dot(jnp.where(strict2, gk[u][:R], 0.0), bdv[u], mode=P_OUT) for u in units}
    us = {u: _dot(ts[u], bd(-(xm_c[u] + akvs[u])), mode=P_SOLVE) for u in units}
    uvs = {u: jnp.concatenate([us[u], vps[u]], axis=0) for u in units}
    ys = {u: xm_r[u] + _dot(
        jnp.concatenate([jnp.where(causal2, gb[u][R:], 0.0), jnp.where(causal2, gk[u][R:], 0.0)], axis=1),
        jnp.concatenate([bd(us[u]), bdv[u]], axis=0), mode=P_OUT) for u in units}
    for s, j in units:
        u = (s, j)
        for g in seqs:
            q = s * G + g
            uv_g = uvs[u] if G == 1 else jnp.where(seq_of_row == g, uvs[u], 0.0)
            upd = _dot(bkts[u], uv_g, mode=P_STATE)
            m_ref[q, j] = pt[tile(j), q:q + 1] * ms[u][g] + jnp.where(block_diag, upd, 0.0)

    y_rows = [jnp.concatenate([ys[(s, j)] for j in pairs], axis=1) for s in groups]
    y_all = y_rows[0] if S == 1 else jnp.concatenate(y_rows, axis=0)
    mean = _lane_group_sums(y_all, HD) * (1.0 / HD)
    yc = y_all - mean
    var = _lane_group_sums(yc * yc, HD) * (1.0 / HD)
    yn = yc * lax.rsqrt(var + GN_EPS) * gnw_ref[...] + gnb_ref[...]
    yn = yn + _lane_group_sums(r * k2 * rk_ref[...], HD) * v
    o_ref[...] = (yn * gate).reshape(S, R, W).astype(o_ref.dtype)
    for q in range(NQ):
        last = (q // G) * R + (q % G + 1) * L - 1
        last_rkv_ref[q:q + 1, :] = p[last:last + 1, :]
        last_lora_ref[q:q + 1, :] = pl_[last:last + 1, :]

    @pl.when(c == sps - 1)
    def _():
        for q in range(NQ):
            for j in pairs:
                vk = m_ref[q, j].T
                s_ref[q, 2 * j] = vk[:HD, :HD]
                s_ref[q, 2 * j + 1] = vk[HD:, HD:]


def _rwkv_mixer(proj2d, sh_rkv, sh_lora, s0, mu_rkv, mu_lora, w0, a0, wab, gb, kk, ka, rk, gnw, gnb,
                seq_len):
    m = proj2d.shape[0]
    nseq = m // seq_len
    R, L, G, S, sps = _mixer_geometry(seq_len, nseq)
    W = RWKV_WIDTH
    proj3d = proj2d.reshape(nseq // G, sps * R, PROJ_WIDTH)
    const2 = lambda i, c: (0, 0)
    row = lambda width: pl.BlockSpec((1, width), const2)
    rows_map = lambda col: (lambda i, c: (i, c, col))
    per_seq = lambda *dims: pl.BlockSpec((S * G,) + dims, lambda i, c: (i,) + (0,) * len(dims))
    carry = lambda width: pltpu.VMEM((S, SUBLANES + R, width) if G == 1 else (SUBLANES, LANES), F32)
    o, s_new, last_rkv, last_lora = pl.pallas_call(
        functools.partial(_rwkv_kernel, seq_len=seq_len, n_seq=nseq),
        grid=(nseq // (S * G), sps),
        in_specs=[
            pl.BlockSpec((S, R, 3 * W), rows_map(COL_RKV // (3 * W))),
            pl.BlockSpec((S, R, RWKV_LORA), rows_map(COL_LORA // RWKV_LORA)),
            pl.BlockSpec((S * G, 3 * W), lambda i, c: (i, 0)),
            pl.BlockSpec((S * G, RWKV_LORA), lambda i, c: (i, 3 * W // RWKV_LORA)),
            per_seq(RWKV_HEADS, RWKV_HEAD, RWKV_HEAD),
            row(3 * W), row(RWKV_LORA), row(W), row(W),
            pl.BlockSpec((LANES, 2 * W), const2),
            pl.BlockSpec((RWKV_LORA_G, W), const2),
            row(W), row(W), row(W), row(W), row(W),
        ],
        out_specs=[pl.BlockSpec((S, R, W), rows_map(0)), per_seq(RWKV_HEADS, RWKV_HEAD, RWKV_HEAD),
                   per_seq(3 * W), per_seq(RWKV_LORA)],
        out_shape=[
            jax.ShapeDtypeStruct((nseq // G, sps * R, W), BF16),
            jax.ShapeDtypeStruct((nseq, RWKV_HEADS, RWKV_HEAD, RWKV_HEAD), F32),
            jax.ShapeDtypeStruct((nseq, 3 * W), F32),
            jax.ShapeDtypeStruct((nseq, RWKV_LORA), F32),
        ],
        scratch_shapes=[carry(3 * W), carry(RWKV_LORA),
                        pltpu.VMEM((S * G, RWKV_HEADS // 2, LANES, LANES), F32)],
        compiler_params=pltpu.CompilerParams(
            dimension_semantics=("arbitrary", "arbitrary"), vmem_limit_bytes=VMEM_LIMIT),
        name="rwkv_mixer",
    )(proj3d, proj3d, sh_rkv, sh_lora, s0, mu_rkv, mu_lora, w0, a0, wab, gb, kk, ka, rk, gnw, gnb)
    return o.reshape(m, W), s_new, jnp.concatenate([last_rkv, last_lora], axis=-1)


def _outproj_kernel(x_ref, oa_ref, ob_ref, wa_ref, wb_ref, g_ref, x1_ref, hn_ref):
    x1 = (x_ref[...] + jnp.dot(oa_ref[...], wa_ref[...], preferred_element_type=F32)
          + jnp.dot(ob_ref[...], wb_ref[...], preferred_element_type=F32))
    x1_ref[...] = x1
    ms = jnp.mean(x1 * x1, axis=-1, keepdims=True)
    hn_ref[...] = (x1 * lax.rsqrt(ms + RMS_EPS) * g_ref[...]).astype(BF16)


def _outproj(x2d, oa, ob, wo_bf16, g_row, tm):
    m = x2d.shape[0]
    return pl.pallas_call(
        _outproj_kernel,
        grid=(m // tm,),
        in_specs=[
            pl.BlockSpec((tm, D_MODEL), lambda i: (i, 0)),
            pl.BlockSpec((tm, GDN_WIDTH), lambda i: (i, 0)),
            pl.BlockSpec((tm, RWKV_WIDTH), lambda i: (i, 0)),
            pl.BlockSpec((GDN_WIDTH, D_MODEL), lambda i: (0, 0)),
            pl.BlockSpec((RWKV_WIDTH, D_MODEL), lambda i: (1, 0)),
            pl.BlockSpec((1, D_MODEL), lambda i: (0, 0)),
        ],
        out_specs=[pl.BlockSpec((tm, D_MODEL), lambda i: (i, 0)),
                   pl.BlockSpec((tm, D_MODEL), lambda i: (i, 0))],
        out_shape=[jax.ShapeDtypeStruct((m, D_MODEL), F32),
                   jax.ShapeDtypeStruct((m, D_MODEL), BF16)],
        compiler_params=pltpu.CompilerParams(
            dimension_semantics=("arbitrary",), vmem_limit_bytes=VMEM_LIMIT),
        name="outproj",
    )(x2d, oa, ob, wo_bf16, wo_bf16, g_row)


FFN_TF = 512
FFN_NF = D_FF // FFN_TF


def _ffn_pipeline(f, up_fn, wd_ref, x1_ref, fg_ref, y_ref, acc_ref):
    @pl.when(f == 0)
    def _():
        acc_ref[...] = jnp.zeros(acc_ref.shape, F32)

    acc_ref[...] += jnp.dot(up_fn().astype(BF16), wd_ref[...], preferred_element_type=F32)

    @pl.when(f == FFN_NF - 1)
    def _():
        xo = x1_ref[...] + acc_ref[...]
        ms = jnp.mean(xo * xo, axis=-1, keepdims=True)
        y_ref[...] = xo * lax.rsqrt(ms + RMS_EPS) * fg_ref[...]


def _ffn_long_kernel(hn_ref, x1_ref, wg0_ref, wu0_ref, cwg0_ref, cwu0_ref, wd0_ref,
                     wg1_ref, wu1_ref, cwg1_ref, cwu1_ref, wd1_ref, fg_ref,
                     y_ref, n_ref, acc_ref, carry_ref, hbuf_ref, *, tt):
    ti = pl.program_id(1)
    f = pl.program_id(2)
    steps = pl.num_programs(2)
    hn = hn_ref[...]

    def act(slot, chunk, wg_ref, wu_ref, cwg_ref, cwu_ref):
        convs = []
        for j, (w_ref, cw_ref) in enumerate(((wg_ref, cwg_ref), (wu_ref, cwu_ref))):
            h = jnp.dot(hn, w_ref[...], preferred_element_type=F32)
            hbuf_ref[slot, j, SUBLANES:SUBLANES + tt, :] = h
            prev = carry_ref[chunk, j]
            hbuf_ref[slot, j, 0:SUBLANES, :] = jnp.where(ti == 0, jnp.zeros_like(prev), prev)
            cw = cw_ref[...]
            conv = h * cw[FFN_CONV - 1:FFN_CONV, :]
            for i in range(FFN_CONV - 1):
                off = SUBLANES - (FFN_CONV - 1) + i
                conv = conv + hbuf_ref[slot, j, off:off + tt, :] * cw[i:i + 1, :]
            carry_ref[chunk, j] = h[tt - SUBLANES:, :]
            n_ref[j, :, pl.ds(pl.multiple_of(chunk * FFN_TF, FFN_TF), FFN_TF)] = h[tt - (FFN_CONV - 1):, :]
            convs.append(conv)
        return (_silu(convs[0]) * convs[1]).astype(BF16)

    @pl.when(f == 0)
    def _():
        acc_ref[...] = jnp.zeros(acc_ref.shape, F32)

    def both():
        a0 = act(0, 2 * f, wg0_ref, wu0_ref, cwg0_ref, cwu0_ref)
        a1 = act(1, 2 * f + 1, wg1_ref, wu1_ref, cwg1_ref, cwu1_ref)
        acc_ref[...] += (jnp.dot(a0, wd0_ref[...], preferred_element_type=F32)
                         + jnp.dot(a1, wd1_ref[...], preferred_element_type=F32))

    if FFN_NF % 2 == 0:
        both()
    else:
        pl.when(f < steps - 1)(both)

        @pl.when(f == steps - 1)
        def _():
            a0 = act(0, 2 * f, wg0_ref, wu0_ref, cwg0_ref, cwu0_ref)
            acc_ref[...] += jnp.dot(a0, wd0_ref[...], preferred_element_type=F32)

    @pl.when(f == steps - 1)
    def _():
        xo = x1_ref[...] + acc_ref[...]
        ms = jnp.mean(xo * xo, axis=-1, keepdims=True)
        y_ref[...] = xo * lax.rsqrt(ms + RMS_EPS) * fg_ref[...]


def _ffn_long(hn, x1, wup, cw, wdown, fg_row, tt):
    b, t, _ = hn.shape
    tf = FFN_TF
    nf = FFN_NF
    steps = -(-nf // 2)
    chunk = (lambda f: 2 * f, lambda f: jnp.minimum(2 * f + 1, nf - 1))
    weights = []
    for k in range(2):
        weights += [
            pl.BlockSpec((D_MODEL, tf), lambda i, s, f, k=k: (0, chunk[k](f))),
            pl.BlockSpec((D_MODEL, tf), lambda i, s, f, k=k: (0, nf + chunk[k](f))),
            pl.BlockSpec((FFN_CONV, tf), lambda i, s, f, k=k: (0, chunk[k](f))),
            pl.BlockSpec((FFN_CONV, tf), lambda i, s, f, k=k: (0, nf + chunk[k](f))),
            pl.BlockSpec((tf, D_MODEL), lambda i, s, f, k=k: (chunk[k](f), 0)),
        ]
    return pl.pallas_call(
        functools.partial(_ffn_long_kernel, tt=tt),
        grid=(b, t // tt, steps),
        in_specs=[
            pl.BlockSpec((None, tt, D_MODEL), lambda i, s, f: (i, s, 0)),
            pl.BlockSpec((None, tt, D_MODEL), lambda i, s, f: (i, s, 0)),
        ] + weights + [pl.BlockSpec((1, D_MODEL), lambda i, s, f: (0, 0))],
        out_specs=[
            pl.BlockSpec((None, tt, D_MODEL), lambda i, s, f: (i, s, 0)),
            pl.BlockSpec((None, None, 2, FFN_CONV - 1, D_FF), lambda i, s, f: (i, s, 0, 0, 0)),
        ],
        out_shape=[
            jax.ShapeDtypeStruct((b, t, D_MODEL), F32),
            jax.ShapeDtypeStruct((b, t // tt, 2, FFN_CONV - 1, D_FF), F32),
        ],
        scratch_shapes=[
            pltpu.VMEM((tt, D_MODEL), F32),
            pltpu.VMEM((nf, 2, SUBLANES, tf), F32),
            pltpu.VMEM((2, 2, SUBLANES + tt, tf), F32),
        ],
        compiler_params=pltpu.CompilerParams(
            dimension_semantics=("arbitrary", "arbitrary", "arbitrary"),
            vmem_limit_bytes=VMEM_LIMIT_FFN),
        name="ffn_long",
    )(hn, x1, wup, wup, cw, cw, wdown, wup, wup, cw, cw, wdown, fg_row)


def _ffn_short_kernel(hn_ref, x1_ref, wg_ref, wu_ref, cwg_ref, cwu_ref, wd_ref, fg_ref,
                      b0g_ref, b1g_ref, b0u_ref, b1u_ref,
                      y_ref, ng_ref, nu_ref, acc_ref, z_ref, hb_ref, *, tt, seq):
    f = pl.program_id(1)
    nseq = tt // seq

    def up_fn():
        hn = hn_ref[...]
        t_in_seq = lax.broadcasted_iota(jnp.int32, (tt, FFN_TF), 0) % seq
        convs = []
        groups = ((wg_ref, cwg_ref, b0g_ref, b1g_ref, ng_ref), (wu_ref, cwu_ref, b0u_ref, b1u_ref, nu_ref))
        for w_ref, cw_ref, b0_ref, b1_ref, n_ref in groups:
            h = jnp.dot(hn, w_ref[...], preferred_element_type=F32)
            z_ref[...] = jnp.zeros(z_ref.shape, F32)
            for lb in range(FFN_TF // LANES):
                cols = slice(lb * LANES, (lb + 1) * LANES)
                z_ref[lb, pl.ds(0, nseq, stride=seq), :] = b0_ref[:, cols]
                z_ref[lb, pl.ds(1, nseq, stride=seq), :] = b1_ref[:, cols]
                hb_ref[lb] = h[:, cols]
                n_ref[0, :, cols] = hb_ref[lb, pl.ds(seq - 2, nseq, stride=seq), :]
                n_ref[1, :, cols] = hb_ref[lb, pl.ds(seq - 1, nseq, stride=seq), :]
            z = jnp.concatenate([z_ref[lb] for lb in range(FFN_TF // LANES)], axis=1)
            s1 = jnp.where(t_in_seq == 0, pltpu.roll(z, tt - 1, 0), pltpu.roll(h, 1, 0))
            s2 = jnp.where(t_in_seq < 2, z, pltpu.roll(h, 2, 0))
            cw = cw_ref[...]
            convs.append(h * cw[2:3, :] + s1 * cw[1:2, :] + s2 * cw[0:1, :])
        return _silu(convs[0]) * convs[1]

    _ffn_pipeline(f, up_fn, wd_ref, x1_ref, fg_ref, y_ref, acc_ref)


def _ffn_short(hn, x1, wup, cw, wdown, fg_row, hist, tt, seq):
    m = hn.shape[0]
    tf = FFN_TF
    nf = FFN_NF
    nseq = tt // seq
    up = lambda f: f
    down = lambda f: f
    st = lambda k, col0: pl.BlockSpec((None, nseq, tf), lambda i, f: (k, i, col0 + up(f)))
    new = pl.BlockSpec((2, nseq, tf), lambda i, f: (0, i, up(f)))
    new_shape = jax.ShapeDtypeStruct((2, m // seq, D_FF), F32)
    return pl.pallas_call(
        functools.partial(_ffn_short_kernel, tt=tt, seq=seq),
        grid=(m // tt, nf),
        in_specs=[
            pl.BlockSpec((tt, D_MODEL), lambda i, f: (i, 0)),
            pl.BlockSpec((tt, D_MODEL), lambda i, f: (i, 0)),
            pl.BlockSpec((D_MODEL, tf), lambda i, f: (0, up(f))),
            pl.BlockSpec((D_MODEL, tf), lambda i, f: (0, nf + up(f))),
            pl.BlockSpec((FFN_CONV, tf), lambda i, f: (0, up(f))),
            pl.BlockSpec((FFN_CONV, tf), lambda i, f: (0, nf + up(f))),
            pl.BlockSpec((tf, D_MODEL), lambda i, f: (down(f), 0)),
            pl.BlockSpec((1, D_MODEL), lambda i, f: (0, 0)),
            st(0, 0), st(1, 0), st(0, nf), st(1, nf),
        ],
        out_specs=[pl.BlockSpec((tt, D_MODEL), lambda i, f: (i, 0)), new, new],
        out_shape=[jax.ShapeDtypeStruct((m, D_MODEL), F32), new_shape, new_shape],
        scratch_shapes=[
            pltpu.VMEM((tt, D_MODEL), F32),
            pltpu.VMEM((tf // LANES, tt, LANES), F32),
            pltpu.VMEM((tf // LANES, tt, LANES), F32),
        ],
        compiler_params=pltpu.CompilerParams(
            dimension_semantics=("arbitrary", "arbitrary"), vmem_limit_bytes=VMEM_LIMIT),
        name="ffn_short",
    )(hn, x1, wup, wup, cw, cw, wdown, fg_row, hist, hist, hist, hist)


def _pad_lanes(vec, offset):
    out = jnp.zeros((LANES,), F32)
    return out.at[offset:offset + vec.shape[0]].set(vec.astype(F32))


def _trunk(x, s_gdn, s_gconv, s_rwkv, s_shift, s_ffn, prm, *, long_seq):
    b, t, _ = x.shape
    m = b * t
    x2d = x.reshape(m, D_MODEL)
    tm = min(512, m)
    proj = _inproj(x2d, prm["ln1_g"], prm["w_in"], min(1024, m))

    o_a, gdn_new, gconv_new = _gdn_mixer(proj, s_gconv, s_gdn, prm["gdn_conv_w"], prm["alog_r"],
                                         prm["dtb_r"], prm["alog_c"], prm["dtb_c"], prm["gdn_norm_g"], t)
    o_b, rwkv_new, shift_new = _rwkv_mixer(proj, s_shift, s_shift, s_rwkv, prm["mu_rkv"], prm["mu_lora"],
                                           prm["rwkv_w0"], prm["rwkv_a0"], prm["rwkv_wab"],
                                           prm["rwkv_g_b"], prm["rwkv_k_k"], prm["rwkv_k_a"],
                                           prm["rwkv_r_k"], prm["rwkv_gn_w"], prm["rwkv_gn_b"], t)

    x1, hn = _outproj(x2d, o_a, o_b, prm["w_o"], prm["ln2_g"], tm)
    if long_seq:
        tt = min(512, t)
        y, n_gu = _ffn_long(hn.reshape(b, t, D_MODEL), x1.reshape(b, t, D_MODEL), prm["ffn_w_up"],
                            prm["ffn_conv_w"], prm["ffn_w_down"], prm["final_g"], tt)
        ffn_new = jnp.concatenate([n_gu[:, -1, 0], n_gu[:, -1, 1]], axis=-1)
    else:
        tt = min(512, m)
        y, n_g, n_u = _ffn_short(hn, x1, prm["ffn_w_up"], prm["ffn_conv_w"], prm["ffn_w_down"],
                                 prm["final_g"], jnp.swapaxes(s_ffn, 0, 1), tt, t)
        y = y.reshape(b, t, D_MODEL)
        ffn_new = jnp.swapaxes(jnp.concatenate([n_g, n_u], axis=-1), 0, 1)

    return y, gdn_new[None], gconv_new[None], rwkv_new[None], shift_new[None], ffn_new[None]


def kernel(x_prompt, x_sample, state_gdn, state_gdn_conv, state_rwkv, state_rwkv_shift, state_ffn_conv, ln1_g, w_in, gdn_conv_w, gdn_a_log, gdn_dt_bias, gdn_norm_g, rwkv_mu, rwkv_w0, rwkv_w_b, rwkv_a0, rwkv_a_b, rwkv_g_b, rwkv_k_k, rwkv_k_a, rwkv_r_k, rwkv_gn_w, rwkv_gn_b, w_o, ln2_g, ffn_w_up, ffn_conv_w, ffn_w_down, final_g):
    assert ln1_g.shape[0] == 1, "single-layer trunk"
    w_perm = _permute_win(w_in[0])
    mu = rwkv_mu[0]
    zeros_w = jnp.zeros((RWKV_LORA_W, RWKV_WIDTH), F32)
    wab = jnp.concatenate([
        jnp.concatenate([rwkv_w_b[0], zeros_w], axis=1),
        jnp.concatenate([zeros_w, rwkv_a_b[0]], axis=1)], axis=0).astype(BF16)
    alog = _pad_lanes(gdn_a_log[0], GDN_HEADS)
    dtb = _pad_lanes(gdn_dt_bias[0], GDN_HEADS)
    prm = {
        "ln1_g": ln1_g[0][None], "w_in": w_perm, "gdn_conv_w": gdn_conv_w[0],
        "alog_r": alog[None], "dtb_r": dtb[None], "alog_c": alog[:, None], "dtb_c": dtb[:, None],
        "gdn_norm_g": gdn_norm_g[0][None],
        "mu_rkv": mu[None, :3 * RWKV_WIDTH], "mu_lora": mu[None, 3 * RWKV_WIDTH:],
        "rwkv_w0": rwkv_w0[0][None], "rwkv_a0": rwkv_a0[0][None], "rwkv_wab": wab,
        "rwkv_g_b": rwkv_g_b[0].astype(BF16), "rwkv_k_k": rwkv_k_k[0][None],
        "rwkv_k_a": rwkv_k_a[0][None], "rwkv_r_k": rwkv_r_k[0].reshape(1, RWKV_WIDTH),
        "rwkv_gn_w": rwkv_gn_w[0][None], "rwkv_gn_b": rwkv_gn_b[0][None],
        "w_o": w_o[0].astype(BF16), "ln2_g": ln2_g[0][None],
        "ffn_w_up": ffn_w_up[0].astype(BF16), "ffn_conv_w": ffn_conv_w[0],
        "ffn_w_down": ffn_w_down[0].astype(BF16), "final_g": final_g[None],
    }

    bp = x_prompt.shape[0]
    zero_states = (
        jnp.zeros((bp,) + state_gdn.shape[2:], F32),
        jnp.zeros((bp,) + state_gdn_conv.shape[2:], F32),
        jnp.zeros((bp,) + state_rwkv.shape[2:], F32),
        jnp.zeros((bp,) + state_rwkv_shift.shape[2:], F32),
        None,
    )
    outs_p = _trunk(x_prompt, *zero_states, prm, long_seq=True)
    outs_s = _trunk(x_sample, state_gdn[0], state_gdn_conv[0], state_rwkv[0], state_rwkv_shift[0],
                    state_ffn_conv[0], prm, long_seq=False)
    return (outs_p[0], outs_s[0]) + tuple(outs_p[1:]) + tuple(outs_s[1:])
```

```python
import functools

import jax
import jax.numpy as jnp
from jax import lax
from jax.experimental import pallas as pl
from jax.experimental.pallas import tpu as pltpu

F32 = jnp.float32
BF16 = jnp.bfloat16

D_MODEL = 2048
GDN_WIDTH = 1024
GDN_HEADS = 8
GDN_DK = 128
GDN_CONV = 4
RWKV_WIDTH = 1024
RWKV_HEAD = 64
RWKV_HEADS = 16
RWKV_LORA_W = 64
RWKV_LORA_A = 64
RWKV_LORA_G = 128
RWKV_LORA = RWKV_LORA_W + RWKV_LORA_A + RWKV_LORA_G
RWKV_PROJ = 3 * RWKV_WIDTH + RWKV_LORA
D_FF = 5632
FFN_CONV = 3
RMS_EPS = 1e-6
L2_EPS = 1e-12
GN_EPS = 64e-5

REF_OFF_Z = 3 * GDN_WIDTH
REF_OFF_B = 4 * GDN_WIDTH
REF_OFF_RWKV = REF_OFF_B + 2 * GDN_HEADS
REF_IN_WIDTH = REF_OFF_RWKV + RWKV_PROJ

LANES = 128
SUBLANES = 8
COL_QKV = 0
COL_RKV = 3 * GDN_WIDTH
COL_Z = COL_RKV + 3 * RWKV_WIDTH
COL_LORA = COL_Z + GDN_WIDTH
COL_BA = COL_LORA + RWKV_LORA
PROJ_WIDTH = 7680
PROJ_TM = 2048
PROJ_TN = 512
PROJ_TN_SHORT = 1280

MIX_ROWS = 64
MIX_GROUPS_LONG = 4
MIX_GROUPS_SHORT = 2

NN = (((1,), (0,)), ((), ()))
NT = (((1,), (1,)), ((), ()))
TN = (((0,), (0,)), ((), ()))

VMEM_LIMIT = 56 * 1024 * 1024
VMEM_LIMIT_FFN = 62 * 1024 * 1024

P_GRAM = "x1"
P_INV = "x1"
P_SOLVE = "x1"
P_STATE = "x1"
P_OUT = "x1"


def _split(x):
    hi = x.astype(BF16)
    return hi, (x - hi.astype(F32)).astype(BF16)


def _dot(a, b, dims=NN, mode="x1"):
    if mode == "hi":
        return lax.dot_general(a, b, dims, precision=lax.Precision.HIGHEST,
                               preferred_element_type=F32)
    if mode == "x3":
        a_hi, a_lo = _split(a)
        b_hi, b_lo = _split(b)
        d = lambda u, v: lax.dot_general(u, v, dims, preferred_element_type=F32)
        return d(a_hi, b_hi) + (d(a_hi, b_lo) + d(a_lo, b_hi))
    return lax.dot_general(a.astype(BF16), b.astype(BF16), dims, preferred_element_type=F32)


def _ones_dot(ones_mat, x, dims=NN):
    x1 = x.astype(BF16)
    r1 = x - x1.astype(F32)
    x2 = r1.astype(BF16)
    x3 = (r1 - x2.astype(F32)).astype(BF16)
    m = ones_mat.astype(BF16)
    if dims == NN:
        d = lambda v: lax.dot_general(m, v, dims, preferred_element_type=F32)
    else:
        d = lambda v: lax.dot_general(v, m, dims, preferred_element_type=F32)
    return d(x1) + (d(x2) + d(x3))


def _sigmoid(x):
    return 1.0 / (1.0 + jnp.exp(-x))


def _silu(x):
    return x * _sigmoid(x)


def _softplus(x):
    return jnp.maximum(x, 0.0) + jnp.log(1.0 + jnp.exp(-jnp.abs(x)))


def _seq_masks(rows, seq_len):
    r = lax.broadcasted_iota(jnp.int32, (rows, rows), 0)
    c = lax.broadcasted_iota(jnp.int32, (rows, rows), 1)
    if seq_len >= rows:
        return None, r >= c, r > c
    shift = seq_len.bit_length() - 1
    assert 1 << shift == seq_len
    same = jnp.right_shift(r, shift) == jnp.right_shift(c, shift)
    return same, same & (r >= c), same & (r > c)


def _wide_masks(rows, seq_len):
    r = lax.broadcasted_iota(jnp.int32, (rows, 2 * rows), 0)
    c = lax.broadcasted_iota(jnp.int32, (rows, 2 * rows), 1)
    right = c >= rows
    cc = jnp.where(right, c - rows, c)
    if seq_len >= rows:
        return r >= cc, right & (r > cc)
    shift = seq_len.bit_length() - 1
    same = jnp.right_shift(r, shift) == jnp.right_shift(cc, shift)
    return same & (r >= cc), same & right & (r > cc)


def _pair_masks(rows, seq_len):
    half = LANES // 2
    assert rows == half
    lane = lax.broadcasted_iota(jnp.int32, (rows, LANES), 1)
    row = lax.broadcasted_iota(jnp.int32, (rows, LANES), 0)
    col = jnp.bitwise_and(lane, half - 1)
    if seq_len >= rows:
        causal, strict = row >= col, row > col
    else:
        shift = seq_len.bit_length() - 1
        same = jnp.right_shift(row, shift) == jnp.right_shift(col, shift)
        causal, strict = same & (row >= col), same & (row > col)
    return causal, strict, jnp.where(row == col, 1.0, 0.0).astype(F32), lane < half


def _bd(x, first_half):
    return jnp.concatenate([jnp.where(first_half, x, 0.0), jnp.where(first_half, 0.0, x)], axis=0)


def _pair_inverses(neg_a, rows, nilpotency, eye2, first_half):
    qs = dict(neg_a)
    ts = {u: eye2 + q for u, q in qs.items()}
    n = 2
    if n < nilpotency:
        qs = {u: _dot(q, _bd(q, first_half), mode=P_INV) for u, q in qs.items()}
    while n < nilpotency:
        if 2 * n < nilpotency:
            tq = {u: _dot(jnp.concatenate([ts[u], qs[u]], axis=0), _bd(qs[u], first_half), mode=P_INV)
                  for u in qs}
            ts = {u: ts[u] + tq[u][:rows] for u in qs}
            qs = {u: tq[u][rows:] for u in qs}
        else:
            ts = {u: ts[u] + _dot(ts[u], _bd(qs[u], first_half), mode=P_INV) for u in qs}
        n *= 2
    return ts


def _lane_group_sums(x, group):
    tile = 2 * LANES
    shift = group.bit_length() - 1
    li = jnp.right_shift(lax.broadcasted_iota(jnp.int32, (tile, tile), 0), shift)
    lj = jnp.right_shift(lax.broadcasted_iota(jnp.int32, (tile, tile), 1), shift)
    ones = jnp.where(li == lj, 1.0, 0.0).astype(BF16)
    hi, lo = _split(x)
    d = lambda u: lax.dot_general(u, ones, NN, preferred_element_type=F32)
    return jnp.concatenate([d(hi[:, t:t + tile]) + d(lo[:, t:t + tile])
                            for t in range(0, x.shape[1], tile)], axis=1)


def _unit_lower_inverses(mats, rows, nilpotency):
    r = lax.broadcasted_iota(jnp.int32, (rows, rows), 0)
    c = lax.broadcasted_iota(jnp.int32, (rows, rows), 1)
    eye = jnp.where(r == c, 1.0, 0.0).astype(F32)
    qs = [-a for a in mats]
    ts = [eye + q for q in qs]
    n = 2
    if n < nilpotency:
        qs = [_dot(q, q, mode=P_INV) for q in qs]
    while n < nilpotency:
        if 2 * n < nilpotency:
            tq = [_dot(jnp.concatenate([t, q], axis=0), q, mode=P_INV) for t, q in zip(ts, qs)]
            ts = [t + p[:rows] for t, p in zip(ts, tq)]
            qs = [p[rows:] for p in tq]
        else:
            ts = [t + _dot(t, q, mode=P_INV) for t, q in zip(ts, qs)]
        n *= 2
    return ts


def _permute_win_kernel(w_ref, o_ref):
    rw = REF_OFF_RWKV
    rows = w_ref.shape[0]
    cast = lambda lo, hi: w_ref[:, lo:hi].astype(BF16)
    o_ref[:, COL_QKV:COL_RKV] = cast(0, REF_OFF_Z)
    o_ref[:, COL_RKV:COL_Z] = cast(rw, rw + 3 * RWKV_WIDTH)
    o_ref[:, COL_Z:COL_LORA] = cast(REF_OFF_Z, REF_OFF_B)
    o_ref[:, COL_LORA:COL_BA] = cast(rw + 3 * RWKV_WIDTH, REF_IN_WIDTH)
    tail = jnp.concatenate([w_ref[:, REF_OFF_B:REF_OFF_RWKV],
                            jnp.zeros((rows, PROJ_WIDTH - COL_BA - 2 * GDN_HEADS), F32)], axis=1)
    o_ref[:, COL_BA:] = tail.astype(BF16)


def _permute_win(w, tr=256):
    return pl.pallas_call(
        _permute_win_kernel,
        grid=(D_MODEL // tr,),
        in_specs=[pl.BlockSpec((tr, REF_IN_WIDTH), lambda i: (i, 0))],
        out_specs=pl.BlockSpec((tr, PROJ_WIDTH), lambda i: (i, 0)),
        out_shape=jax.ShapeDtypeStruct((D_MODEL, PROJ_WIDTH), BF16),
        compiler_params=pltpu.CompilerParams(
            dimension_semantics=("arbitrary",), vmem_limit_bytes=VMEM_LIMIT),
        name="permute_win",
    )(w)


def _inproj_kernel(x_ref, g_ref, w_ref, o_ref, xn_ref):
    @pl.when(pl.program_id(1) == 0)
    def _():
        x = x_ref[...]
        ms = jnp.mean(x * x, axis=-1, keepdims=True)
        xn_ref[...] = (x * lax.rsqrt(ms + RMS_EPS) * g_ref[...]).astype(BF16)

    o_ref[...] = jnp.dot(xn_ref[...], w_ref[...], preferred_element_type=F32)


def _inproj(x2d, g_row, w_bf16):
    m = x2d.shape[0]
    tm = min(PROJ_TM, m)
    tn = PROJ_TN if tm == PROJ_TM else PROJ_TN_SHORT
    assert m % tm == 0 and PROJ_WIDTH % tn == 0
    return pl.pallas_call(
        _inproj_kernel,
        grid=(m // tm, PROJ_WIDTH // tn),
        in_specs=[
            pl.BlockSpec((tm, D_MODEL), lambda i, j: (i, 0)),
            pl.BlockSpec((1, D_MODEL), lambda i, j: (0, 0)),
            pl.BlockSpec((D_MODEL, tn), lambda i, j: (0, j)),
        ],
        out_specs=pl.BlockSpec((tm, tn), lambda i, j: (i, j)),
        out_shape=jax.ShapeDtypeStruct((m, PROJ_WIDTH), F32),
        scratch_shapes=[pltpu.VMEM((tm, D_MODEL), BF16)],
        compiler_params=pltpu.CompilerParams(
            dimension_semantics=("arbitrary", "arbitrary"), vmem_limit_bytes=VMEM_LIMIT_FFN),
        name="inproj",
    )(x2d, g_row, w_bf16)


def _mixer_geometry(seq_len, n_seq, short_groups=1):
    rows = MIX_ROWS
    length = min(seq_len, rows)
    assert rows % length == 0 and seq_len % length == 0 and length % SUBLANES == 0
    per_group = rows // length
    groups = MIX_GROUPS_LONG if per_group == 1 else short_groups
    while n_seq % (groups * per_group):
        groups //= 2
    assert groups >= 1
    return rows, length, per_group, groups, seq_len // length


def _gdn_kernel(qkv_ref, z_ref, ba_ref, cbuf_ref, s0_ref, convw_ref, alog_r_ref, dtb_r_ref,
                alog_c_ref, dtb_c_ref, ng_ref, o_ref, s_ref, tail_ref, xp_ref, *, seq_len, n_seq):
    R, L, G, S, _ = _mixer_geometry(seq_len, n_seq, MIX_GROUPS_SHORT)
    RT = S * R
    c = pl.program_id(1)
    width = 3 * GDN_WIDTH
    hist = GDN_CONV - 1
    cw = convw_ref[...]
    groups = range(S)
    seqs = range(G)

    @pl.when(c == 0)
    def _():
        s_ref[...] = s0_ref[...]

    pieces = []
    if G == 1:
        @pl.when(c == 0)
        def _():
            for s in groups:
                xp_ref[s, 0:SUBLANES, :] = jnp.zeros((SUBLANES, width), F32)
                xp_ref[s, SUBLANES - hist:SUBLANES, :] = cbuf_ref[s]

        @pl.when(c > 0)
        def _():
            for s in groups:
                xp_ref[s, 0:SUBLANES, :] = xp_ref[s, R:R + SUBLANES, :]

        for s in groups:
            xp_ref[s, SUBLANES:SUBLANES + R, :] = qkv_ref[s]
            piece = qkv_ref[s] * cw[hist:hist + 1, :]
            for i in range(hist):
                off = SUBLANES - hist + i
                piece = piece + xp_ref[s, off:off + R, :] * cw[i:i + 1, :]
            pieces.append(piece)
    else:
        for s in groups:
            for g in seqs:
                q = s * G + g
                rows = slice(g * L, (g + 1) * L)
                xp_ref[q, SUBLANES - hist:SUBLANES, :] = cbuf_ref[q]
                xp_ref[q, SUBLANES:SUBLANES + L, :] = qkv_ref[s, rows, :]
                piece = qkv_ref[s, rows, :] * cw[hist:hist + 1, :]
                for i in range(hist):
                    off = SUBLANES - hist + i
                    piece = piece + xp_ref[q, off:off + L, :] * cw[i:i + 1, :]
                pieces.append(piece)
    qkv = _silu(pieces[0] if len(pieces) == 1 else jnp.concatenate(pieces, axis=0))

    same_t, causal_t, _ = _seq_masks(RT, L)
    causal01 = jnp.where(causal_t, 1.0, 0.0).astype(F32)

    ba = ba_ref[...].reshape(RT, LANES)
    ba_t = ba.T
    beta_c = _sigmoid(ba)
    g_c = -jnp.exp(alog_r_ref[...]) * _softplus(ba + dtb_r_ref[...])
    g_r = -jnp.exp(alog_c_ref[...]) * _softplus(ba_t + dtb_c_ref[...])
    gc_all = _ones_dot(causal01, g_c)
    gr_all = _ones_dot(causal01, g_r, NT)
    if same_t is None:
        gtot_all = jnp.broadcast_to(gc_all[RT - 1:RT, :], (RT, LANES))
    else:
        gtot_all = _ones_dot(jnp.where(same_t, 1.0, 0.0).astype(F32), g_c)

    qk_raw = qkv[:, :2 * GDN_WIDTH]
    qk_n = qk_raw * lax.rsqrt(_lane_group_sums(qk_raw * qk_raw, GDN_DK) + L2_EPS)
    q_all = qk_n[:, :GDN_WIDTH] * (GDN_DK ** -0.5)
    k_all = qk_n[:, GDN_WIDTH:]
    v_all = qkv[:, 2 * GDN_WIDTH:]

    causal2, strict2, eye2, first_half = _pair_masks(R, L)
    first_head = lax.broadcasted_iota(jnp.int32, (R, 2 * GDN_DK), 1) < GDN_DK
    zeros_u = jnp.zeros((R, GDN_DK), F32)
    zeros_rhs = jnp.zeros((R, 2 * GDN_DK), F32)
    chains = [(s, h) for s in groups for h in range(GDN_HEADS)]
    units = [(s, p) for s in groups for p in range(GDN_HEADS // 2)]
    seq_rows = [slice(g * L, (g + 1) * L) for g in seqs]
    qs, ks, vs, betas, gcols, gtots = {}, {}, {}, {}, {}, {}
    for s, h in chains:
        rs = slice(s * R, (s + 1) * R)
        lo = h * GDN_DK
        key = (s, h)
        qs[key] = q_all[rs, lo:lo + GDN_DK]
        ks[key] = k_all[rs, lo:lo + GDN_DK]
        vs[key] = v_all[rs, lo:lo + GDN_DK]
        betas[key] = beta_c[rs, h:h + 1]
        gcols[key] = gc_all[rs, GDN_HEADS + h:GDN_HEADS + h + 1]
        gtots[key] = gtot_all[rs, GDN_HEADS + h:GDN_HEADS + h + 1]
    kq2, decay2, a2 = {}, {}, {}
    for s, p in units:
        rs = slice(s * R, (s + 1) * R)
        cols = slice(2 * p * GDN_DK, (2 * p + 2) * GDN_DK)
        k_pair = k_all[rs, cols]
        kt_bd = jnp.concatenate([jnp.where(first_head, k_pair, 0.0),
                                 jnp.where(first_head, 0.0, k_pair)], axis=0).T
        kq = _dot(jnp.concatenate([k_pair, q_all[rs, cols]], axis=0), kt_bd, mode=P_GRAM)
        h0, h1 = (s, 2 * p), (s, 2 * p + 1)
        gcol2 = jnp.where(first_half, gcols[h0], gcols[h1])
        grow2 = jnp.concatenate([gr_all[GDN_HEADS + 2 * p:GDN_HEADS + 2 * p + 1, rs],
                                 gr_all[GDN_HEADS + 2 * p + 1:GDN_HEADS + 2 * p + 2, rs]], axis=1)
        dec = jnp.where(causal2, jnp.exp(jnp.where(causal2, gcol2 - grow2, 0.0)), 0.0)
        beta2 = jnp.where(first_half, betas[h0], betas[h1])
        kq2[(s, p)] = kq
        decay2[(s, p)] = dec
        a2[(s, p)] = jnp.where(strict2, -(beta2 * kq[:R] * dec), 0.0)
    t2 = _pair_inverses(a2, R, L, eye2, first_half)
    gammas = {key: jnp.exp(gcols[key]) for key in chains}

    def stacked(key, x, zeros):
        return jnp.concatenate([x, zeros] if key[1] % 2 == 0 else [zeros, x], axis=0)

    sols = {key: _dot(t2[(key[0], key[1] // 2)],
                      stacked(key, jnp.concatenate([(betas[key] * gammas[key]) * ks[key],
                                                    betas[key] * vs[key]], axis=1), zeros_rhs),
                      mode=P_SOLVE) for key in chains}
    states = {(s, h): [s_ref[s * G + g, h] for g in seqs] for s, h in chains}
    wss = {key: [_dot(jnp.concatenate([sols[key][rows, :GDN_DK], (qs[key] * gammas[key])[rows]], axis=0),
                      states[key][g], mode=P_STATE) for g, rows in enumerate(seq_rows)]
           for key in chains}
    us = {key: jnp.concatenate([sols[key][rows, GDN_DK:] - wss[key][g][:L]
                                for g, rows in enumerate(seq_rows)], axis=0) for key in chains}
    qk2 = {u: kq2[u][R:] * decay2[u] for u in units}
    outs = {key: jnp.concatenate([wss[key][g][L:] for g in seqs], axis=0)
            + _dot(qk2[(key[0], key[1] // 2)], stacked(key, us[key], zeros_u), mode=P_OUT)
            for key in chains}
    for s, h in chains:
        key = (s, h)
        kt = ks[key] * jnp.exp(gtots[key] - gcols[key])
        for g, rows in enumerate(seq_rows):
            gl = jnp.exp(gtots[key][g * L:g * L + 1, :])
            s_ref[s * G + g, h] = gl * states[key][g] + _dot(kt[rows], us[key][rows], TN, mode=P_STATE)
    ng = jnp.concatenate([ng_ref[...]] * GDN_HEADS, axis=1)
    for s in groups:
        o = jnp.concatenate([outs[(s, h)] for h in range(GDN_HEADS)], axis=1)
        ms = _lane_group_sums(o * o, GDN_DK) * (1.0 / GDN_DK)
        o = o * lax.rsqrt(ms + RMS_EPS) * ng
        o_ref[s] = (o * _silu(z_ref[s])).astype(o_ref.dtype)
    for s in groups:
        for g in seqs:
            tail_ref[s * G + g] = qkv_ref[s, (g + 1) * L - hist:(g + 1) * L, :]


def _gdn_mixer(proj2d, cbuf, s0, convw, alog_r, dtb_r, alog_c, dtb_c, ng, seq_len):
    m = proj2d.shape[0]
    nseq = m // seq_len
    R, L, G, S, sps = _mixer_geometry(seq_len, nseq, MIX_GROUPS_SHORT)
    width = 3 * GDN_WIDTH
    proj3d = proj2d.reshape(nseq // G, sps * R, PROJ_WIDTH)
    const2 = lambda i, c: (0, 0)
    rows_map = lambda col: (lambda i, c: (i, c, col))
    per_seq = lambda *dims: pl.BlockSpec((S * G,) + dims, lambda i, c: (i,) + (0,) * len(dims))
    xp_shape = (S, SUBLANES + R, width) if G == 1 else (S * G, SUBLANES + L, width)
    o, s_new, conv_new = pl.pallas_call(
        functools.partial(_gdn_kernel, seq_len=seq_len, n_seq=nseq),
        grid=(nseq // (S * G), sps),
        in_specs=[
            pl.BlockSpec((S, R, width), rows_map(COL_QKV // width)),
            pl.BlockSpec((S, R, GDN_WIDTH), rows_map(COL_Z // GDN_WIDTH)),
            pl.BlockSpec((S, R, LANES), rows_map(COL_BA // LANES)),
            per_seq(GDN_CONV - 1, width),
            per_seq(GDN_HEADS, GDN_DK, GDN_DK),
            pl.BlockSpec((GDN_CONV, width), const2),
            pl.BlockSpec((1, LANES), const2),
            pl.BlockSpec((1, LANES), const2),
            pl.BlockSpec((LANES, 1), const2),
            pl.BlockSpec((LANES, 1), const2),
            pl.BlockSpec((1, GDN_DK), const2),
        ],
        out_specs=[
            pl.BlockSpec((S, R, GDN_WIDTH), rows_map(0)),
            per_seq(GDN_HEADS, GDN_DK, GDN_DK),
            per_seq(GDN_CONV - 1, width),
        ],
        out_shape=[
            jax.ShapeDtypeStruct((nseq // G, sps * R, GDN_WIDTH), BF16),
            jax.ShapeDtypeStruct((nseq, GDN_HEADS, GDN_DK, GDN_DK), F32),
            jax.ShapeDtypeStruct((nseq, GDN_CONV - 1, width), F32),
        ],
        scratch_shapes=[pltpu.VMEM(xp_shape, F32)],
        compiler_params=pltpu.CompilerParams(
            dimension_semantics=("arbitrary", "arbitrary"), vmem_limit_bytes=VMEM_LIMIT),
        name="gdn_mixer",
    )(proj3d, proj3d, proj3d, cbuf, s0, convw, alog_r, dtb_r, alog_c, dtb_c, ng)
    return o.reshape(m, GDN_WIDTH), s_new, conv_new


def _shifted_rows(x_ref, prev_ref, carry_ref, c, R, L, G, S):
    width = x_ref.shape[-1]
    groups = range(S)
    if G == 1:
        @pl.when(c == 0)
        def _():
            for s in groups:
                carry_ref[s, 0:SUBLANES, :] = jnp.zeros((SUBLANES, width), F32)
                carry_ref[s, SUBLANES - 1:SUBLANES, :] = prev_ref[s:s + 1, :]

        @pl.when(c > 0)
        def _():
            for s in groups:
                carry_ref[s, 0:SUBLANES, :] = carry_ref[s, R:R + SUBLANES, :]

        xs, prevs = [], []
        for s in groups:
            carry_ref[s, SUBLANES:SUBLANES + R, :] = x_ref[s]
            xs.append(x_ref[s])
            prevs.append(carry_ref[s, SUBLANES - 1:SUBLANES - 1 + R, :])
    else:
        row = lax.broadcasted_iota(jnp.int32, (L, width), 0)
        xs, prevs = [], []
        for s in groups:
            x = x_ref[s]
            xs.append(x)
            for g in range(G):
                xg = x[g * L:(g + 1) * L]
                q = s * G + g
                prevs.append(jnp.where(row == 0, prev_ref[q:q + 1, :], pltpu.roll(xg, 1, 0)))
    cat = lambda parts: parts[0] if len(parts) == 1 else jnp.concatenate(parts, axis=0)
    return cat(xs), cat(prevs)


def _rwkv_kernel(rkv_ref, lora_ref, sh_rkv_ref, sh_lora_ref, s0_ref, mu_rkv_ref, mu_lora_ref,
                 w0_ref, a0_ref, wab_ref, gb_ref, kk_ref, ka_ref, rk_ref, gnw_ref, gnb_ref,
                 o_ref, s_ref, last_rkv_ref, last_lora_ref, xr_ref, xl_ref, m_ref, *, seq_len, n_seq):
    R, L, G, S, sps = _mixer_geometry(seq_len, n_seq)
    RT = S * R
    NQ = S * G
    c = pl.program_id(1)
    W = RWKV_WIDTH
    HD = RWKV_HEAD
    pairs = range(RWKV_HEADS // 2)
    groups = range(S)
    seqs = range(G)

    @pl.when(c == 0)
    def _():
        zero = jnp.zeros((HD, HD), F32)
        for q in range(NQ):
            for j in pairs:
                vk = jnp.concatenate([jnp.concatenate([s0_ref[q, 2 * j], zero], axis=1),
                                      jnp.concatenate([zero, s0_ref[q, 2 * j + 1]], axis=1)], axis=0)
                m_ref[q, j] = vk.T

    p, p_prev = _shifted_rows(rkv_ref, sh_rkv_ref, xr_ref, c, R, L, G, S)
    xs = p + (p_prev - p) * mu_rkv_ref[...]
    pl_, pl_prev = _shifted_rows(lora_ref, sh_lora_ref, xl_ref, c, R, L, G, S)
    xl = pl_ + (pl_prev - pl_) * mu_lora_ref[...]
    r = xs[:, :W]
    k = xs[:, W:2 * W]
    v = xs[:, 2 * W:]

    wa_in = xl[:, :LANES]
    lane = lax.broadcasted_iota(jnp.int32, (RT, LANES), 1)
    wa_in = jnp.where(lane < RWKV_LORA_W, jnp.tanh(wa_in), wa_in)
    wa = _dot(wa_in, wab_ref[...])
    w = -_softplus(-(w0_ref[...] + wa[:, :W])) - 0.5
    lw = -jnp.exp(w)
    a = _sigmoid(a0_ref[...] + wa[:, W:])
    gate = _dot(_sigmoid(xl[:, LANES:]), gb_ref[...])
    kk_raw = k * kk_ref[...]
    k2 = k * (1.0 + (a - 1.0) * ka_ref[...])

    same_t, causal_t, _ = _seq_masks(RT, L)
    lc = _ones_dot(jnp.where(causal_t, 1.0, 0.0).astype(F32), lw)
    if G == 1:
        ltot = jnp.concatenate([jnp.broadcast_to(lc[(s + 1) * R - 1:(s + 1) * R, :], (R, W))
                                for s in groups], axis=0) if S > 1 else \
            jnp.broadcast_to(lc[R - 1:R, :], (R, W))
    else:
        ltot = _ones_dot(jnp.where(same_t, 1.0, 0.0).astype(F32), lw)
    e_inv = jnp.exp(-lc)
    e_rem = jnp.exp(ltot - lc)

    kk = kk_raw * lax.rsqrt(_lane_group_sums(kk_raw * kk_raw, HD) + L2_EPS)
    kka = kk * a
    ct = kk * jnp.exp(lc - lw)
    rt = r * jnp.exp(lc)
    bh = kka * e_inv
    kh = k2 * e_inv
    bb = kka * e_rem
    kb = k2 * e_rem
    p_rows = [ltot[q * L:q * L + 1] for q in range(NQ)]
    p_rows = p_rows + [p_rows[0]] * (-NQ % SUBLANES)
    pt = jnp.exp(jnp.concatenate(p_rows, axis=0)).T

    causal2, strict2, eye2, even_half = _pair_masks(R, L)
    block_diag = ((lax.broadcasted_iota(jnp.int32, (LANES, LANES), 0) < HD)
                  == (lax.broadcasted_iota(jnp.int32, (LANES, LANES), 1) < HD))
    even_all = jnp.bitwise_and(lax.broadcasted_iota(jnp.int32, (R, W), 1), LANES - 1) < HD
    tile = lambda j: slice(j * LANES, (j + 1) * LANES)
    bd = lambda x: _bd(x, even_half)

    if G > 1:
        row2 = lax.broadcasted_iota(jnp.int32, (2 * R, LANES), 0)
        seq_of_row = jnp.right_shift(jnp.bitwise_and(row2, R - 1), L.bit_length() - 1)

    units = [(s, j) for s in groups for j in pairs]
    xps, vps, bdb, bdk, bkts = {}, {}, {}, {}, {}
    for s in groups:
        rs = slice(s * R, (s + 1) * R)
        x_s = jnp.concatenate([ct[rs], rt[rs]], axis=0)
        bt_s = jnp.concatenate([jnp.where(even_all, bh[rs], 0.0), jnp.where(even_all, 0.0, bh[rs])],
                               axis=0).T
        kt_s = jnp.concatenate([jnp.where(even_all, kh[rs], 0.0), jnp.where(even_all, 0.0, kh[rs])],
                               axis=0).T
        bkt_s = jnp.concatenate([bb[rs], kb[rs]], axis=0).T
        for j in pairs:
            xps[(s, j)] = x_s[:, tile(j)]
            vps[(s, j)] = v[rs, tile(j)]
            bdb[(s, j)] = bt_s[tile(j), :]
            bdk[(s, j)] = kt_s[tile(j), :]
            bkts[(s, j)] = bkt_s[tile(j), :]
    gb = {u: _dot(xps[u], bdb[u], mode=P_GRAM) for u in units}
    gk = {u: _dot(xps[u], bdk[u], mode=P_GRAM) for u in units}
    ms = {(s, j): [m_ref[s * G + g, j] for g in seqs] for s, j in units}
    xm_c, xm_r = {}, {}
    for u in units:
        if G == 1:
            xm = _dot(xps[u], ms[u][0], mode=P_STATE)
            xm_c[u], xm_r[u] = xm[:R], xm[R:]
        else:
            parts = [_dot(jnp.concatenate([xps[u][g * L:(g + 1) * L],
                                           xps[u][R + g * L:R + (g + 1) * L]], axis=0),
                          ms[u][g], mode=P_STATE) for g in seqs]
            xm_c[u] = jnp.concatenate([p_[:L] for p_ in parts], axis=0)
            xm_r[u] = jnp.concatenate([p_[L:] for p_ in parts], axis=0)
    ts = _pair_inverses({u: jnp.where(strict2, -gb[u][:R], 0.0) for u in units}, R, L, eye2, even_half)
    bdv = {u: bd(vps[u]) for u in units}
    akvs = {u: _dot(jnp.where(strict2, gk[u][:R], 0.0), bdv[u], mode=P_OUT) for u in units}
    us = {u: _dot(ts[u], bd(-(xm_c[u] + akvs[u])), mode=P_SOLVE) for u in units}
    uvs = {u: jnp.concatenate([us[u], vps[u]], axis=0) for u in units}
    ys = {u: xm_r[u] + _dot(
        jnp.concatenate([jnp.where(causal2, gb[u][R:], 0.0), jnp.where(causal2, gk[u][R:], 0.0)], axis=1),
        jnp.concatenate([bd(us[u]), bdv[u]], axis=0), mode=P_OUT) for u in units}
    for s, j in units:
        u = (s, j)
        for g in seqs:
            q = s * G + g
            uv_g = uvs[u] if G == 1 else jnp.where(seq_of_row == g, uvs[u], 0.0)
            upd = _dot(bkts[u], uv_g, mode=P_STATE)
            m_ref[q, j] = pt[tile(j), q:q + 1] * ms[u][g] + jnp.where(block_diag, upd, 0.0)

    y_rows = [jnp.concatenate([ys[(s, j)] for j in pairs], axis=1) for s in groups]
    y_all = y_rows[0] if S == 1 else jnp.concatenate(y_rows, axis=0)
    mean = _lane_group_sums(y_all, HD) * (1.0 / HD)
    yc = y_all - mean
    var = _lane_group_sums(yc * yc, HD) * (1.0 / HD)
    yn = yc * lax.rsqrt(var + GN_EPS) * gnw_ref[...] + gnb_ref[...]
    yn = yn + _lane_group_sums(r * k2 * rk_ref[...], HD) * v
    o_ref[...] = (yn * gate).reshape(S, R, W).astype(o_ref.dtype)
    for q in range(NQ):
        last = (q // G) * R + (q % G + 1) * L - 1
        last_rkv_ref[q:q + 1, :] = p[last:last + 1, :]
        last_lora_ref[q:q + 1, :] = pl_[last:last + 1, :]

    @pl.when(c == sps - 1)
    def _():
        for q in range(NQ):
            for j in pairs:
                vk = m_ref[q, j].T
                s_ref[q, 2 * j] = vk[:HD, :HD]
                s_ref[q, 2 * j + 1] = vk[HD:, HD:]


def _rwkv_mixer(proj2d, sh_rkv, sh_lora, s0, mu_rkv, mu_lora, w0, a0, wab, gb, kk, ka, rk, gnw, gnb,
                seq_len):
    m = proj2d.shape[0]
    nseq = m // seq_len
    R, L, G, S, sps = _mixer_geometry(seq_len, nseq)
    W = RWKV_WIDTH
    proj3d = proj2d.reshape(nseq // G, sps * R, PROJ_WIDTH)
    const2 = lambda i, c: (0, 0)
    row = lambda width: pl.BlockSpec((1, width), const2)
    rows_map = lambda col: (lambda i, c: (i, c, col))
    per_seq = lambda *dims: pl.BlockSpec((S * G,) + dims, lambda i, c: (i,) + (0,) * len(dims))
    carry = lambda width: pltpu.VMEM((S, SUBLANES + R, width) if G == 1 else (SUBLANES, LANES), F32)
    o, s_new, last_rkv, last_lora = pl.pallas_call(
        functools.partial(_rwkv_kernel, seq_len=seq_len, n_seq=nseq),
        grid=(nseq // (S * G), sps),
        in_specs=[
            pl.BlockSpec((S, R, 3 * W), rows_map(COL_RKV // (3 * W))),
            pl.BlockSpec((S, R, RWKV_LORA), rows_map(COL_LORA // RWKV_LORA)),
            pl.BlockSpec((S * G, 3 * W), lambda i, c: (i, 0)),
            pl.BlockSpec((S * G, RWKV_LORA), lambda i, c: (i, 3 * W // RWKV_LORA)),
            per_seq(RWKV_HEADS, RWKV_HEAD, RWKV_HEAD),
            row(3 * W), row(RWKV_LORA), row(W), row(W),
            pl.BlockSpec((LANES, 2 * W), const2),
            pl.BlockSpec((RWKV_LORA_G, W), const2),
            row(W), row(W), row(W), row(W), row(W),
        ],
        out_specs=[pl.BlockSpec((S, R, W), rows_map(0)), per_seq(RWKV_HEADS, RWKV_HEAD, RWKV_HEAD),
                   per_seq(3 * W), per_seq(RWKV_LORA)],
        out_shape=[
            jax.ShapeDtypeStruct((nseq // G, sps * R, W), BF16),
            jax.ShapeDtypeStruct((nseq, RWKV_HEADS, RWKV_HEAD, RWKV_HEAD), F32),
            jax.ShapeDtypeStruct((nseq, 3 * W), F32),
            jax.ShapeDtypeStruct((nseq, RWKV_LORA), F32),
        ],
        scratch_shapes=[carry(3 * W), carry(RWKV_LORA),
                        pltpu.VMEM((S * G, RWKV_HEADS // 2, LANES, LANES), F32)],
        compiler_params=pltpu.CompilerParams(
            dimension_semantics=("arbitrary", "arbitrary"), vmem_limit_bytes=VMEM_LIMIT),
        name="rwkv_mixer",
    )(proj3d, proj3d, sh_rkv, sh_lora, s0, mu_rkv, mu_lora, w0, a0, wab, gb, kk, ka, rk, gnw, gnb)
    return o.reshape(m, W), s_new, jnp.concatenate([last_rkv, last_lora], axis=-1)


def _outproj_kernel(x_ref, oa_ref, ob_ref, wa_ref, wb_ref, g_ref, x1_ref, hn_ref):
    x1 = (x_ref[...] + jnp.dot(oa_ref[...], wa_ref[...], preferred_element_type=F32)
          + jnp.dot(ob_ref[...], wb_ref[...], preferred_element_type=F32))
    x1_ref[...] = x1
    ms = jnp.mean(x1 * x1, axis=-1, keepdims=True)
    hn_ref[...] = (x1 * lax.rsqrt(ms + RMS_EPS) * g_ref[...]).astype(BF16)


def _outproj(x2d, oa, ob, wo_bf16, g_row, tm):
    m = x2d.shape[0]
    return pl.pallas_call(
        _outproj_kernel,
        grid=(m // tm,),
        in_specs=[
            pl.BlockSpec((tm, D_MODEL), lambda i: (i, 0)),
            pl.BlockSpec((tm, GDN_WIDTH), lambda i: (i, 0)),
            pl.BlockSpec((tm, RWKV_WIDTH), lambda i: (i, 0)),
            pl.BlockSpec((GDN_WIDTH, D_MODEL), lambda i: (0, 0)),
            pl.BlockSpec((RWKV_WIDTH, D_MODEL), lambda i: (1, 0)),
            pl.BlockSpec((1, D_MODEL), lambda i: (0, 0)),
        ],
        out_specs=[pl.BlockSpec((tm, D_MODEL), lambda i: (i, 0)),
                   pl.BlockSpec((tm, D_MODEL), lambda i: (i, 0))],
        out_shape=[jax.ShapeDtypeStruct((m, D_MODEL), F32),
                   jax.ShapeDtypeStruct((m, D_MODEL), BF16)],
        compiler_params=pltpu.CompilerParams(
            dimension_semantics=("arbitrary",), vmem_limit_bytes=VMEM_LIMIT),
        name="outproj",
    )(x2d, oa, ob, wo_bf16, wo_bf16, g_row)


FFN_TF = 512
FFN_NF = D_FF // FFN_TF


def _ffn_pipeline(f, up_fn, wd_ref, x1_ref, fg_ref, y_ref, acc_ref):
    @pl.when(f == 0)
    def _():
        acc_ref[...] = jnp.zeros(acc_ref.shape, F32)

    acc_ref[...] += jnp.dot(up_fn().astype(BF16), wd_ref[...], preferred_element_type=F32)

    @pl.when(f == FFN_NF - 1)
    def _():
        xo = x1_ref[...] + acc_ref[...]
        ms = jnp.mean(xo * xo, axis=-1, keepdims=True)
        y_ref[...] = xo * lax.rsqrt(ms + RMS_EPS) * fg_ref[...]


def _ffn_long_kernel(hn_ref, x1_ref, wg0_ref, wu0_ref, cwg0_ref, cwu0_ref, wd0_ref,
                     wg1_ref, wu1_ref, cwg1_ref, cwu1_ref, wd1_ref, fg_ref,
                     y_ref, n_ref, acc_ref, carry_ref, hbuf_ref, *, tt):
    ti = pl.program_id(1)
    f = pl.program_id(2)
    steps = pl.num_programs(2)
    hn = hn_ref[...]

    def act(slot, chunk, wg_ref, wu_ref, cwg_ref, cwu_ref):
        convs = []
        for j, (w_ref, cw_ref) in enumerate(((wg_ref, cwg_ref), (wu_ref, cwu_ref))):
            h = jnp.dot(hn, w_ref[...], preferred_element_type=F32)
            hbuf_ref[slot, j, SUBLANES:SUBLANES + tt, :] = h
            prev = carry_ref[chunk, j]
            hbuf_ref[slot, j, 0:SUBLANES, :] = jnp.where(ti == 0, jnp.zeros_like(prev), prev)
            cw = cw_ref[...]
            conv = h * cw[FFN_CONV - 1:FFN_CONV, :]
            for i in range(FFN_CONV - 1):
                off = SUBLANES - (FFN_CONV - 1) + i
                conv = conv + hbuf_ref[slot, j, off:off + tt, :] * cw[i:i + 1, :]
            carry_ref[chunk, j] = h[tt - SUBLANES:, :]
            n_ref[j, :, pl.ds(pl.multiple_of(chunk * FFN_TF, FFN_TF), FFN_TF)] = h[tt - (FFN_CONV - 1):, :]
            convs.append(conv)
        return (_silu(convs[0]) * convs[1]).astype(BF16)

    @pl.when(f == 0)
    def _():
        acc_ref[...] = jnp.zeros(acc_ref.shape, F32)

    def both():
        a0 = act(0, 2 * f, wg0_ref, wu0_ref, cwg0_ref, cwu0_ref)
        a1 = act(1, 2 * f + 1, wg1_ref, wu1_ref, cwg1_ref, cwu1_ref)
        acc_ref[...] += (jnp.dot(a0, wd0_ref[...], preferred_element_type=F32)
                         + jnp.dot(a1, wd1_ref[...], preferred_element_type=F32))

    if FFN_NF % 2 == 0:
        both()
    else:
        pl.when(f < steps - 1)(both)

        @pl.when(f == steps - 1)
        def _():
            a0 = act(0, 2 * f, wg0_ref, wu0_ref, cwg0_ref, cwu0_ref)
            acc_ref[...] += jnp.dot(a0, wd0_ref[...], preferred_element_type=F32)

    @pl.when(f == steps - 1)
    def _():
        xo = x1_ref[...] + acc_ref[...]
        ms = jnp.mean(xo * xo, axis=-1, keepdims=True)
        y_ref[...] = xo * lax.rsqrt(ms + RMS_EPS) * fg_ref[...]


def _ffn_long(hn, x1, wup, cw, wdown, fg_row, tt):
    b, t, _ = hn.shape
    tf = FFN_TF
    nf = FFN_NF
    steps = -(-nf // 2)
    chunk = (lambda f: 2 * f, lambda f: jnp.minimum(2 * f + 1, nf - 1))
    weights = []
    for k in range(2):
        weights += [
            pl.BlockSpec((D_MODEL, tf), lambda i, s, f, k=k: (0, chunk[k](f))),
            pl.BlockSpec((D_MODEL, tf), lambda i, s, f, k=k: (0, nf + chunk[k](f))),
            pl.BlockSpec((FFN_CONV, tf), lambda i, s, f, k=k: (0, chunk[k](f))),
            pl.BlockSpec((FFN_CONV, tf), lambda i, s, f, k=k: (0, nf + chunk[k](f))),
            pl.BlockSpec((tf, D_MODEL), lambda i, s, f, k=k: (chunk[k](f), 0)),
        ]
    return pl.pallas_call(
        functools.partial(_ffn_long_kernel, tt=tt),
        grid=(b, t // tt, steps),
        in_specs=[
            pl.BlockSpec((None, tt, D_MODEL), lambda i, s, f: (i, s, 0)),
            pl.BlockSpec((None, tt, D_MODEL), lambda i, s, f: (i, s, 0)),
        ] + weights + [pl.BlockSpec((1, D_MODEL), lambda i, s, f: (0, 0))],
        out_specs=[
            pl.BlockSpec((None, tt, D_MODEL), lambda i, s, f: (i, s, 0)),
            pl.BlockSpec((None, None, 2, FFN_CONV - 1, D_FF), lambda i, s, f: (i, s, 0, 0, 0)),
        ],
        out_shape=[
            jax.ShapeDtypeStruct((b, t, D_MODEL), F32),
            jax.ShapeDtypeStruct((b, t // tt, 2, FFN_CONV - 1, D_FF), F32),
        ],
        scratch_shapes=[
            pltpu.VMEM((tt, D_MODEL), F32),
            pltpu.VMEM((nf, 2, SUBLANES, tf), F32),
            pltpu.VMEM((2, 2, SUBLANES + tt, tf), F32),
        ],
        compiler_params=pltpu.CompilerParams(
            dimension_semantics=("arbitrary", "arbitrary", "arbitrary"),
            vmem_limit_bytes=VMEM_LIMIT_FFN),
        name="ffn_long",
    )(hn, x1, wup, wup, cw, cw, wdown, wup, wup, cw, cw, wdown, fg_row)


def _ffn_short_kernel(hn_ref, x1_ref, wg_ref, wu_ref, cwg_ref, cwu_ref, wd_ref, fg_ref,
                      b0g_ref, b1g_ref, b0u_ref, b1u_ref,
                      y_ref, ng_ref, nu_ref, acc_ref, z_ref, hb_ref, *, tt, seq):
    f = pl.program_id(1)
    nseq = tt // seq

    def up_fn():
        hn = hn_ref[...]
        t_in_seq = lax.broadcasted_iota(jnp.int32, (tt, FFN_TF), 0) % seq
        convs = []
        groups = ((wg_ref, cwg_ref, b0g_ref, b1g_ref, ng_ref), (wu_ref, cwu_ref, b0u_ref, b1u_ref, nu_ref))
        for w_ref, cw_ref, b0_ref, b1_ref, n_ref in groups:
            h = jnp.dot(hn, w_ref[...], preferred_element_type=F32)
            z_ref[...] = jnp.zeros(z_ref.shape, F32)
            for lb in range(FFN_TF // LANES):
                cols = slice(lb * LANES, (lb + 1) * LANES)
                z_ref[lb, pl.ds(0, nseq, stride=seq), :] = b0_ref[:, cols]
                z_ref[lb, pl.ds(1, nseq, stride=seq), :] = b1_ref[:, cols]
                hb_ref[lb] = h[:, cols]
                n_ref[0, :, cols] = hb_ref[lb, pl.ds(seq - 2, nseq, stride=seq), :]
                n_ref[1, :, cols] = hb_ref[lb, pl.ds(seq - 1, nseq, stride=seq), :]
            z = jnp.concatenate([z_ref[lb] for lb in range(FFN_TF // LANES)], axis=1)
            s1 = jnp.where(t_in_seq == 0, pltpu.roll(z, tt - 1, 0), pltpu.roll(h, 1, 0))
            s2 = jnp.where(t_in_seq < 2, z, pltpu.roll(h, 2, 0))
            cw = cw_ref[...]
            convs.append(h * cw[2:3, :] + s1 * cw[1:2, :] + s2 * cw[0:1, :])
        return _silu(convs[0]) * convs[1]

    _ffn_pipeline(f, up_fn, wd_ref, x1_ref, fg_ref, y_ref, acc_ref)


def _ffn_short(hn, x1, wup, cw, wdown, fg_row, hist, tt, seq):
    m = hn.shape[0]
    tf = FFN_TF
    nf = FFN_NF
    nseq = tt // seq
    up = lambda f: f
    down = lambda f: f
    st = lambda k, col0: pl.BlockSpec((None, nseq, tf), lambda i, f: (k, i, col0 + up(f)))
    new = pl.BlockSpec((2, nseq, tf), lambda i, f: (0, i, up(f)))
    new_shape = jax.ShapeDtypeStruct((2, m // seq, D_FF), F32)
    return pl.pallas_call(
        functools.partial(_ffn_short_kernel, tt=tt, seq=seq),
        grid=(m // tt, nf),
        in_specs=[
            pl.BlockSpec((tt, D_MODEL), lambda i, f: (i, 0)),
            pl.BlockSpec((tt, D_MODEL), lambda i, f: (i, 0)),
            pl.BlockSpec((D_MODEL, tf), lambda i, f: (0, up(f))),
            pl.BlockSpec((D_MODEL, tf), lambda i, f: (0, nf + up(f))),
            pl.BlockSpec((FFN_CONV, tf), lambda i, f: (0, up(f))),
            pl.BlockSpec((FFN_CONV, tf), lambda i, f: (0, nf + up(f))),
            pl.BlockSpec((tf, D_MODEL), lambda i, f: (down(f), 0)),
            pl.BlockSpec((1, D_MODEL), lambda i, f: (0, 0)),
            st(0, 0), st(1, 0), st(0, nf), st(1, nf),
        ],
        out_specs=[pl.BlockSpec((tt, D_MODEL), lambda i, f: (i, 0)), new, new],
        out_shape=[jax.ShapeDtypeStruct((m, D_MODEL), F32), new_shape, new_shape],
        scratch_shapes=[
            pltpu.VMEM((tt, D_MODEL), F32),
            pltpu.VMEM((tf // LANES, tt, LANES), F32),
            pltpu.VMEM((tf // LANES, tt, LANES), F32),
        ],
        compiler_params=pltpu.CompilerParams(
            dimension_semantics=("arbitrary", "arbitrary"), vmem_limit_bytes=VMEM_LIMIT),
        name="ffn_short",
    )(hn, x1, wup, wup, cw, cw, wdown, fg_row, hist, hist, hist, hist)


def _pad_lanes(vec, offset):
    out = jnp.zeros((LANES,), F32)
    return out.at[offset:offset + vec.shape[0]].set(vec.astype(F32))


def _trunk(x, s_gdn, s_gconv, s_rwkv, s_shift, s_ffn, prm, *, long_seq):
    b, t, _ = x.shape
    m = b * t
    x2d = x.reshape(m, D_MODEL)
    tm = min(512, m)
    proj = _inproj(x2d, prm["ln1_g"], prm["w_in"])

    o_a, gdn_new, gconv_new = _gdn_mixer(proj, s_gconv, s_gdn, prm["gdn_conv_w"], prm["alog_r"],
                                         prm["dtb_r"], prm["alog_c"], prm["dtb_c"], prm["gdn_norm_g"], t)
    o_b, rwkv_new, shift_new = _rwkv_mixer(proj, s_shift, s_shift, s_rwkv, prm["mu_rkv"], prm["mu_lora"],
                                           prm["rwkv_w0"], prm["rwkv_a0"], prm["rwkv_wab"],
                                           prm["rwkv_g_b"], prm["rwkv_k_k"], prm["rwkv_k_a"],
                                           prm["rwkv_r_k"], prm["rwkv_gn_w"], prm["rwkv_gn_b"], t)

    x1, hn = _outproj(x2d, o_a, o_b, prm["w_o"], prm["ln2_g"], tm)
    if long_seq:
        tt = min(512, t)
        y, n_gu = _ffn_long(hn.reshape(b, t, D_MODEL), x1.reshape(b, t, D_MODEL), prm["ffn_w_up"],
                            prm["ffn_conv_w"], prm["ffn_w_down"], prm["final_g"], tt)
        ffn_new = jnp.concatenate([n_gu[:, -1, 0], n_gu[:, -1, 1]], axis=-1)
    else:
        tt = min(512, m)
        y, n_g, n_u = _ffn_short(hn, x1, prm["ffn_w_up"], prm["ffn_conv_w"], prm["ffn_w_down"],
                                 prm["final_g"], jnp.swapaxes(s_ffn, 0, 1), tt, t)
        y = y.reshape(b, t, D_MODEL)
        ffn_new = jnp.swapaxes(jnp.concatenate([n_g, n_u], axis=-1), 0, 1)

    return y, gdn_new[None], gconv_new[None], rwkv_new[None], shift_new[None], ffn_new[None]


def kernel(x_prompt, x_sample, state_gdn, state_gdn_conv, state_rwkv, state_rwkv_shift, state_ffn_conv, ln1_g, w_in, gdn_conv_w, gdn_a_log, gdn_dt_bias, gdn_norm_g, rwkv_mu, rwkv_w0, rwkv_w_b, rwkv_a0, rwkv_a_b, rwkv_g_b, rwkv_k_k, rwkv_k_a, rwkv_r_k, rwkv_gn_w, rwkv_gn_b, w_o, ln2_g, ffn_w_up, ffn_conv_w, ffn_w_down, final_g):
    assert ln1_g.shape[0] == 1, "single-layer trunk"
    w_perm = _permute_win(w_in[0])
    mu = rwkv_mu[0]
    zeros_w = jnp.zeros((RWKV_LORA_W, RWKV_WIDTH), F32)
    wab = jnp.concatenate([
        jnp.concatenate([rwkv_w_b[0], zeros_w], axis=1),
        jnp.concatenate([zeros_w, rwkv_a_b[0]], axis=1)], axis=0).astype(BF16)
    alog = _pad_lanes(gdn_a_log[0], GDN_HEADS)
    dtb = _pad_lanes(gdn_dt_bias[0], GDN_HEADS)
    prm = {
        "ln1_g": ln1_g[0][None], "w_in": w_perm, "gdn_conv_w": gdn_conv_w[0],
        "alog_r": alog[None], "dtb_r": dtb[None], "alog_c": alog[:, None], "dtb_c": dtb[:, None],
        "gdn_norm_g": gdn_norm_g[0][None],
        "mu_rkv": mu[None, :3 * RWKV_WIDTH], "mu_lora": mu[None, 3 * RWKV_WIDTH:],
        "rwkv_w0": rwkv_w0[0][None], "rwkv_a0": rwkv_a0[0][None], "rwkv_wab": wab,
        "rwkv_g_b": rwkv_g_b[0].astype(BF16), "rwkv_k_k": rwkv_k_k[0][None],
        "rwkv_k_a": rwkv_k_a[0][None], "rwkv_r_k": rwkv_r_k[0].reshape(1, RWKV_WIDTH),
        "rwkv_gn_w": rwkv_gn_w[0][None], "rwkv_gn_b": rwkv_gn_b[0][None],
        "w_o": w_o[0].astype(BF16), "ln2_g": ln2_g[0][None],
        "ffn_w_up": ffn_w_up[0].astype(BF16), "ffn_conv_w": ffn_conv_w[0],
        "ffn_w_down": ffn_w_down[0].astype(BF16), "final_g": final_g[None],
    }

    bp = x_prompt.shape[0]
    zero_states = (
        jnp.zeros((bp,) + state_gdn.shape[2:], F32),
        jnp.zeros((bp,) + state_gdn_conv.shape[2:], F32),
        jnp.zeros((bp,) + state_rwkv.shape[2:], F32),
        jnp.zeros((bp,) + state_rwkv_shift.shape[2:], F32),
        None,
    )
    outs_p = _trunk(x_prompt, *zero_states, prm, long_seq=True)
    outs_s = _trunk(x_sample, state_gdn[0], state_gdn_conv[0], state_rwkv[0], state_rwkv_shift[0],
                    state_ffn_conv[0], prm, long_seq=False)
    return (outs_p[0], outs_s[0]) + tuple(outs_p[1:]) + tuple(outs_s[1:])
```

```python
import functools

import jax
import jax.numpy as jnp
from jax import lax
from jax.experimental import pallas as pl
from jax.experimental.pallas import tpu as pltpu

F32 = jnp.float32
BF16 = jnp.bfloat16

D_MODEL = 2048
GDN_WIDTH = 1024
GDN_HEADS = 8
GDN_DK = 128
GDN_CONV = 4
RWKV_WIDTH = 1024
RWKV_HEAD = 64
RWKV_HEADS = 16
RWKV_LORA_W = 64
RWKV_LORA_A = 64
RWKV_LORA_G = 128
RWKV_LORA = RWKV_LORA_W + RWKV_LORA_A + RWKV_LORA_G
RWKV_PROJ = 3 * RWKV_WIDTH + RWKV_LORA
D_FF = 5632
FFN_CONV = 3
RMS_EPS = 1e-6
L2_EPS = 1e-12
GN_EPS = 64e-5

REF_OFF_Z = 3 * GDN_WIDTH
REF_OFF_B = 4 * GDN_WIDTH
REF_OFF_RWKV = REF_OFF_B + 2 * GDN_HEADS
REF_IN_WIDTH = REF_OFF_RWKV + RWKV_PROJ

LANES = 128
SUBLANES = 8
COL_QKV = 0
COL_RKV = 3 * GDN_WIDTH
COL_Z = COL_RKV + 3 * RWKV_WIDTH
COL_LORA = COL_Z + GDN_WIDTH
COL_BA = COL_LORA + RWKV_LORA
PROJ_WIDTH = 7680
PROJ_TM = 1024
PROJ_TN = 1280

MIX_ROWS = 64
MIX_GROUPS_LONG = 4
MIX_GROUPS_SHORT = 2

NN = (((1,), (0,)), ((), ()))
NT = (((1,), (1,)), ((), ()))
TN = (((0,), (0,)), ((), ()))

VMEM_LIMIT = 56 * 1024 * 1024
VMEM_LIMIT_FFN = 62 * 1024 * 1024

P_GRAM = "x1"
P_INV = "x1"
P_SOLVE = "x1"
P_STATE = "x1"
P_OUT = "x1"


def _split(x):
    hi = x.astype(BF16)
    return hi, (x - hi.astype(F32)).astype(BF16)


def _dot(a, b, dims=NN, mode="x1"):
    if mode == "hi":
        return lax.dot_general(a, b, dims, precision=lax.Precision.HIGHEST,
                               preferred_element_type=F32)
    if mode == "x3":
        a_hi, a_lo = _split(a)
        b_hi, b_lo = _split(b)
        d = lambda u, v: lax.dot_general(u, v, dims, preferred_element_type=F32)
        return d(a_hi, b_hi) + (d(a_hi, b_lo) + d(a_lo, b_hi))
    return lax.dot_general(a.astype(BF16), b.astype(BF16), dims, preferred_element_type=F32)


def _ones_dot(ones_mat, x, dims=NN):
    x1, x2 = _split(x)
    m = ones_mat.astype(BF16)
    if dims == NN:
        d = lambda v: lax.dot_general(m, v, dims, preferred_element_type=F32)
    else:
        d = lambda v: lax.dot_general(v, m, dims, preferred_element_type=F32)
    return d(x1) + d(x2)


def _sigmoid(x):
    return 1.0 / (1.0 + jnp.exp(-x))


def _silu(x):
    return x * _sigmoid(x)


def _softplus(x):
    return jnp.maximum(x, 0.0) + jnp.log(1.0 + jnp.exp(-jnp.abs(x)))


def _seq_masks(rows, seq_len):
    r = lax.broadcasted_iota(jnp.int32, (rows, rows), 0)
    c = lax.broadcasted_iota(jnp.int32, (rows, rows), 1)
    if seq_len >= rows:
        return None, r >= c, r > c
    shift = seq_len.bit_length() - 1
    assert 1 << shift == seq_len
    same = jnp.right_shift(r, shift) == jnp.right_shift(c, shift)
    return same, same & (r >= c), same & (r > c)


def _wide_masks(rows, seq_len):
    r = lax.broadcasted_iota(jnp.int32, (rows, 2 * rows), 0)
    c = lax.broadcasted_iota(jnp.int32, (rows, 2 * rows), 1)
    right = c >= rows
    cc = jnp.where(right, c - rows, c)
    if seq_len >= rows:
        return r >= cc, right & (r > cc)
    shift = seq_len.bit_length() - 1
    same = jnp.right_shift(r, shift) == jnp.right_shift(cc, shift)
    return same & (r >= cc), same & right & (r > cc)


def _pair_masks(rows, seq_len):
    half = LANES // 2
    assert rows == half
    lane = lax.broadcasted_iota(jnp.int32, (rows, LANES), 1)
    row = lax.broadcasted_iota(jnp.int32, (rows, LANES), 0)
    col = jnp.bitwise_and(lane, half - 1)
    if seq_len >= rows:
        causal, strict = row >= col, row > col
    else:
        shift = seq_len.bit_length() - 1
        same = jnp.right_shift(row, shift) == jnp.right_shift(col, shift)
        causal, strict = same & (row >= col), same & (row > col)
    return causal, strict, jnp.where(row == col, 1.0, 0.0).astype(F32), lane < half


def _bd(x, first_half):
    return jnp.concatenate([jnp.where(first_half, x, 0.0), jnp.where(first_half, 0.0, x)], axis=0)


def _pair_inverses(neg_a, rows, nilpotency, eye2, first_half):
    qs = dict(neg_a)
    ts = {u: eye2 + q for u, q in qs.items()}
    n = 2
    if n < nilpotency:
        qs = {u: _dot(q, _bd(q, first_half), mode=P_INV) for u, q in qs.items()}
    while n < nilpotency:
        if 2 * n < nilpotency:
            tq = {u: _dot(jnp.concatenate([ts[u], qs[u]], axis=0), _bd(qs[u], first_half), mode=P_INV)
                  for u in qs}
            ts = {u: ts[u] + tq[u][:rows] for u in qs}
            qs = {u: tq[u][rows:] for u in qs}
        else:
            ts = {u: ts[u] + _dot(ts[u], _bd(qs[u], first_half), mode=P_INV) for u in qs}
        n *= 2
    return ts


def _lane_group_sums(x, group):
    tile = 2 * LANES
    shift = group.bit_length() - 1
    li = jnp.right_shift(lax.broadcasted_iota(jnp.int32, (tile, tile), 0), shift)
    lj = jnp.right_shift(lax.broadcasted_iota(jnp.int32, (tile, tile), 1), shift)
    ones = jnp.where(li == lj, 1.0, 0.0).astype(BF16)
    hi, lo = _split(x)
    d = lambda u: lax.dot_general(u, ones, NN, preferred_element_type=F32)
    return jnp.concatenate([d(hi[:, t:t + tile]) + d(lo[:, t:t + tile])
                            for t in range(0, x.shape[1], tile)], axis=1)


def _unit_lower_inverses(mats, rows, nilpotency):
    r = lax.broadcasted_iota(jnp.int32, (rows, rows), 0)
    c = lax.broadcasted_iota(jnp.int32, (rows, rows), 1)
    eye = jnp.where(r == c, 1.0, 0.0).astype(F32)
    qs = [-a for a in mats]
    ts = [eye + q for q in qs]
    n = 2
    if n < nilpotency:
        qs = [_dot(q, q, mode=P_INV) for q in qs]
    while n < nilpotency:
        if 2 * n < nilpotency:
            tq = [_dot(jnp.concatenate([t, q], axis=0), q, mode=P_INV) for t, q in zip(ts, qs)]
            ts = [t + p[:rows] for t, p in zip(ts, tq)]
            qs = [p[rows:] for p in tq]
        else:
            ts = [t + _dot(t, q, mode=P_INV) for t, q in zip(ts, qs)]
        n *= 2
    return ts


def _permute_win_kernel(w_ref, o_ref):
    rw = REF_OFF_RWKV
    rows = w_ref.shape[0]
    cast = lambda lo, hi: w_ref[:, lo:hi].astype(BF16)
    o_ref[:, COL_QKV:COL_RKV] = cast(0, REF_OFF_Z)
    o_ref[:, COL_RKV:COL_Z] = cast(rw, rw + 3 * RWKV_WIDTH)
    o_ref[:, COL_Z:COL_LORA] = cast(REF_OFF_Z, REF_OFF_B)
    o_ref[:, COL_LORA:COL_BA] = cast(rw + 3 * RWKV_WIDTH, REF_IN_WIDTH)
    tail = jnp.concatenate([w_ref[:, REF_OFF_B:REF_OFF_RWKV],
                            jnp.zeros((rows, PROJ_WIDTH - COL_BA - 2 * GDN_HEADS), F32)], axis=1)
    o_ref[:, COL_BA:] = tail.astype(BF16)


def _permute_win(w, tr=256):
    return pl.pallas_call(
        _permute_win_kernel,
        grid=(D_MODEL // tr,),
        in_specs=[pl.BlockSpec((tr, REF_IN_WIDTH), lambda i: (i, 0))],
        out_specs=pl.BlockSpec((tr, PROJ_WIDTH), lambda i: (i, 0)),
        out_shape=jax.ShapeDtypeStruct((D_MODEL, PROJ_WIDTH), BF16),
        compiler_params=pltpu.CompilerParams(
            dimension_semantics=("arbitrary",), vmem_limit_bytes=VMEM_LIMIT),
        name="permute_win",
    )(w)


def _inproj_kernel(x_ref, g_ref, w_ref, o_ref, xn_ref):
    @pl.when(pl.program_id(1) == 0)
    def _():
        x = x_ref[...]
        ms = jnp.mean(x * x, axis=-1, keepdims=True)
        xn_ref[...] = (x * lax.rsqrt(ms + RMS_EPS) * g_ref[...]).astype(BF16)

    o_ref[...] = jnp.dot(xn_ref[...], w_ref[...], preferred_element_type=F32)


def _inproj(x2d, g_row, w_bf16):
    m = x2d.shape[0]
    tm = min(PROJ_TM, m)
    tn = PROJ_TN
    assert m % tm == 0 and PROJ_WIDTH % tn == 0
    return pl.pallas_call(
        _inproj_kernel,
        grid=(m // tm, PROJ_WIDTH // tn),
        in_specs=[
            pl.BlockSpec((tm, D_MODEL), lambda i, j: (i, 0)),
            pl.BlockSpec((1, D_MODEL), lambda i, j: (0, 0)),
            pl.BlockSpec((D_MODEL, tn), lambda i, j: (0, j)),
        ],
        out_specs=pl.BlockSpec((tm, tn), lambda i, j: (i, j)),
        out_shape=jax.ShapeDtypeStruct((m, PROJ_WIDTH), F32),
        scratch_shapes=[pltpu.VMEM((tm, D_MODEL), BF16)],
        compiler_params=pltpu.CompilerParams(
            dimension_semantics=("arbitrary", "arbitrary"), vmem_limit_bytes=VMEM_LIMIT),
        name="inproj",
    )(x2d, g_row, w_bf16)


def _mixer_geometry(seq_len, n_seq, short_groups=1):
    rows = MIX_ROWS
    length = min(seq_len, rows)
    assert rows % length == 0 and seq_len % length == 0 and length % SUBLANES == 0
    per_group = rows // length
    groups = MIX_GROUPS_LONG if per_group == 1 else short_groups
    while n_seq % (groups * per_group):
        groups //= 2
    assert groups >= 1
    return rows, length, per_group, groups, seq_len // length


def _gdn_kernel(qkv_ref, z_ref, ba_ref, cbuf_ref, s0_ref, convw_ref, alog_r_ref, dtb_r_ref,
                alog_c_ref, dtb_c_ref, ng_ref, o_ref, s_ref, tail_ref, xp_ref, *, seq_len, n_seq):
    R, L, G, S, _ = _mixer_geometry(seq_len, n_seq, MIX_GROUPS_SHORT)
    RT = S * R
    c = pl.program_id(1)
    width = 3 * GDN_WIDTH
    hist = GDN_CONV - 1
    cw = convw_ref[...]
    groups = range(S)
    seqs = range(G)

    @pl.when(c == 0)
    def _():
        s_ref[...] = s0_ref[...]

    pieces = []
    if G == 1:
        @pl.when(c == 0)
        def _():
            for s in groups:
                xp_ref[s, 0:SUBLANES, :] = jnp.zeros((SUBLANES, width), F32)
                xp_ref[s, SUBLANES - hist:SUBLANES, :] = cbuf_ref[s]

        @pl.when(c > 0)
        def _():
            for s in groups:
                xp_ref[s, 0:SUBLANES, :] = xp_ref[s, R:R + SUBLANES, :]

        for s in groups:
            xp_ref[s, SUBLANES:SUBLANES + R, :] = qkv_ref[s]
            piece = qkv_ref[s] * cw[hist:hist + 1, :]
            for i in range(hist):
                off = SUBLANES - hist + i
                piece = piece + xp_ref[s, off:off + R, :] * cw[i:i + 1, :]
            pieces.append(piece)
    else:
        for s in groups:
            for g in seqs:
                q = s * G + g
                rows = slice(g * L, (g + 1) * L)
                xp_ref[q, SUBLANES - hist:SUBLANES, :] = cbuf_ref[q]
                xp_ref[q, SUBLANES:SUBLANES + L, :] = qkv_ref[s, rows, :]
                piece = qkv_ref[s, rows, :] * cw[hist:hist + 1, :]
                for i in range(hist):
                    off = SUBLANES - hist + i
                    piece = piece + xp_ref[q, off:off + L, :] * cw[i:i + 1, :]
                pieces.append(piece)
    qkv = _silu(pieces[0] if len(pieces) == 1 else jnp.concatenate(pieces, axis=0))

    same_t, causal_t, _ = _seq_masks(RT, L)
    causal01 = jnp.where(causal_t, 1.0, 0.0).astype(F32)

    ba = ba_ref[...].reshape(RT, LANES)
    ba_t = ba.T
    beta_c = _sigmoid(ba)
    g_c = -jnp.exp(alog_r_ref[...]) * _softplus(ba + dtb_r_ref[...])
    g_r = -jnp.exp(alog_c_ref[...]) * _softplus(ba_t + dtb_c_ref[...])
    gc_all = _ones_dot(causal01, g_c)
    gr_all = _ones_dot(causal01, g_r, NT)
    if same_t is None:
        gtot_all = jnp.broadcast_to(gc_all[RT - 1:RT, :], (RT, LANES))
    else:
        gtot_all = _ones_dot(jnp.where(same_t, 1.0, 0.0).astype(F32), g_c)

    qk_raw = qkv[:, :2 * GDN_WIDTH]
    qk_n = qk_raw * lax.rsqrt(_lane_group_sums(qk_raw * qk_raw, GDN_DK) + L2_EPS)
    q_all = qk_n[:, :GDN_WIDTH] * (GDN_DK ** -0.5)
    k_all = qk_n[:, GDN_WIDTH:]
    v_all = qkv[:, 2 * GDN_WIDTH:]

    causal2, strict2, eye2, first_half = _pair_masks(R, L)
    first_head = lax.broadcasted_iota(jnp.int32, (R, 2 * GDN_DK), 1) < GDN_DK
    zeros_u = jnp.zeros((R, GDN_DK), F32)
    zeros_rhs = jnp.zeros((R, 2 * GDN_DK), F32)
    chains = [(s, h) for s in groups for h in range(GDN_HEADS)]
    units = [(s, p) for s in groups for p in range(GDN_HEADS // 2)]
    seq_rows = [slice(g * L, (g + 1) * L) for g in seqs]
    qs, ks, vs, betas, gcols, gtots = {}, {}, {}, {}, {}, {}
    for s, h in chains:
        rs = slice(s * R, (s + 1) * R)
        lo = h * GDN_DK
        key = (s, h)
        qs[key] = q_all[rs, lo:lo + GDN_DK]
        ks[key] = k_all[rs, lo:lo + GDN_DK]
        vs[key] = v_all[rs, lo:lo + GDN_DK]
        betas[key] = beta_c[rs, h:h + 1]
        gcols[key] = gc_all[rs, GDN_HEADS + h:GDN_HEADS + h + 1]
        gtots[key] = gtot_all[rs, GDN_HEADS + h:GDN_HEADS + h + 1]
    kq2, decay2, a2 = {}, {}, {}
    for s, p in units:
        rs = slice(s * R, (s + 1) * R)
        cols = slice(2 * p * GDN_DK, (2 * p + 2) * GDN_DK)
        k_pair = k_all[rs, cols]
        kt_bd = jnp.concatenate([jnp.where(first_head, k_pair, 0.0),
                                 jnp.where(first_head, 0.0, k_pair)], axis=0).T
        kq = _dot(jnp.concatenate([k_pair, q_all[rs, cols]], axis=0), kt_bd, mode=P_GRAM)
        h0, h1 = (s, 2 * p), (s, 2 * p + 1)
        gcol2 = jnp.where(first_half, gcols[h0], gcols[h1])
        grow2 = jnp.concatenate([gr_all[GDN_HEADS + 2 * p:GDN_HEADS + 2 * p + 1, rs],
                                 gr_all[GDN_HEADS + 2 * p + 1:GDN_HEADS + 2 * p + 2, rs]], axis=1)
        dec = jnp.where(causal2, jnp.exp(jnp.where(causal2, gcol2 - grow2, 0.0)), 0.0)
        beta2 = jnp.where(first_half, betas[h0], betas[h1])
        kq2[(s, p)] = kq
        decay2[(s, p)] = dec
        a2[(s, p)] = jnp.where(strict2, -(beta2 * kq[:R] * dec), 0.0)
    t2 = _pair_inverses(a2, R, L, eye2, first_half)
    gammas = {key: jnp.exp(gcols[key]) for key in chains}

    def stacked(key, x, zeros):
        return jnp.concatenate([x, zeros] if key[1] % 2 == 0 else [zeros, x], axis=0)

    sols = {key: _dot(t2[(key[0], key[1] // 2)],
                      stacked(key, jnp.concatenate([(betas[key] * gammas[key]) * ks[key],
                                                    betas[key] * vs[key]], axis=1), zeros_rhs),
                      mode=P_SOLVE) for key in chains}
    states = {(s, h): [s_ref[s * G + g, h] for g in seqs] for s, h in chains}
    wss = {key: [_dot(jnp.concatenate([sols[key][rows, :GDN_DK], (qs[key] * gammas[key])[rows]], axis=0),
                      states[key][g], mode=P_STATE) for g, rows in enumerate(seq_rows)]
           for key in chains}
    us = {key: jnp.concatenate([sols[key][rows, GDN_DK:] - wss[key][g][:L]
                                for g, rows in enumerate(seq_rows)], axis=0) for key in chains}
    qk2 = {u: kq2[u][R:] * decay2[u] for u in units}
    outs = {key: jnp.concatenate([wss[key][g][L:] for g in seqs], axis=0)
            + _dot(qk2[(key[0], key[1] // 2)], stacked(key, us[key], zeros_u), mode=P_OUT)
            for key in chains}
    for s, h in chains:
        key = (s, h)
        kt = ks[key] * jnp.exp(gtots[key] - gcols[key])
        for g, rows in enumerate(seq_rows):
            gl = jnp.exp(gtots[key][g * L:g * L + 1, :])
            s_ref[s * G + g, h] = gl * states[key][g] + _dot(kt[rows], us[key][rows], TN, mode=P_STATE)
    ng = jnp.concatenate([ng_ref[...]] * GDN_HEADS, axis=1)
    for s in groups:
        o = jnp.concatenate([outs[(s, h)] for h in range(GDN_HEADS)], axis=1)
        ms = _lane_group_sums(o * o, GDN_DK) * (1.0 / GDN_DK)
        o = o * lax.rsqrt(ms + RMS_EPS) * ng
        o_ref[s] = (o * _silu(z_ref[s])).astype(o_ref.dtype)
    for s in groups:
        for g in seqs:
            tail_ref[s * G + g] = qkv_ref[s, (g + 1) * L - hist:(g + 1) * L, :]


def _gdn_mixer(proj2d, cbuf, s0, convw, alog_r, dtb_r, alog_c, dtb_c, ng, seq_len):
    m = proj2d.shape[0]
    nseq = m // seq_len
    R, L, G, S, sps = _mixer_geometry(seq_len, nseq, MIX_GROUPS_SHORT)
    width = 3 * GDN_WIDTH
    proj3d = proj2d.reshape(nseq // G, sps * R, PROJ_WIDTH)
    const2 = lambda i, c: (0, 0)
    rows_map = lambda col: (lambda i, c: (i, c, col))
    per_seq = lambda *dims: pl.BlockSpec((S * G,) + dims, lambda i, c: (i,) + (0,) * len(dims))
    xp_shape = (S, SUBLANES + R, width) if G == 1 else (S * G, SUBLANES + L, width)
    o, s_new, conv_new = pl.pallas_call(
        functools.partial(_gdn_kernel, seq_len=seq_len, n_seq=nseq),
        grid=(nseq // (S * G), sps),
        in_specs=[
            pl.BlockSpec((S, R, width), rows_map(COL_QKV // width)),
            pl.BlockSpec((S, R, GDN_WIDTH), rows_map(COL_Z // GDN_WIDTH)),
            pl.BlockSpec((S, R, LANES), rows_map(COL_BA // LANES)),
            per_seq(GDN_CONV - 1, width),
            per_seq(GDN_HEADS, GDN_DK, GDN_DK),
            pl.BlockSpec((GDN_CONV, width), const2),
            pl.BlockSpec((1, LANES), const2),
            pl.BlockSpec((1, LANES), const2),
            pl.BlockSpec((LANES, 1), const2),
            pl.BlockSpec((LANES, 1), const2),
            pl.BlockSpec((1, GDN_DK), const2),
        ],
        out_specs=[
            pl.BlockSpec((S, R, GDN_WIDTH), rows_map(0)),
            per_seq(GDN_HEADS, GDN_DK, GDN_DK),
            per_seq(GDN_CONV - 1, width),
        ],
        out_shape=[
            jax.ShapeDtypeStruct((nseq // G, sps * R, GDN_WIDTH), BF16),
            jax.ShapeDtypeStruct((nseq, GDN_HEADS, GDN_DK, GDN_DK), F32),
            jax.ShapeDtypeStruct((nseq, GDN_CONV - 1, width), F32),
        ],
        scratch_shapes=[pltpu.VMEM(xp_shape, F32)],
        compiler_params=pltpu.CompilerParams(
            dimension_semantics=("arbitrary", "arbitrary"), vmem_limit_bytes=VMEM_LIMIT),
        name="gdn_mixer",
    )(proj3d, proj3d, proj3d, cbuf, s0, convw, alog_r, dtb_r, alog_c, dtb_c, ng)
    return o.reshape(m, GDN_WIDTH), s_new, conv_new


def _shifted_rows(x_ref, prev_ref, carry_ref, c, R, L, G, S):
    width = x_ref.shape[-1]
    groups = range(S)
    if G == 1:
        @pl.when(c == 0)
        def _():
            for s in groups:
                carry_ref[s, 0:SUBLANES, :] = jnp.zeros((SUBLANES, width), F32)
                carry_ref[s, SUBLANES - 1:SUBLANES, :] = prev_ref[s:s + 1, :]

        @pl.when(c > 0)
        def _():
            for s in groups:
                carry_ref[s, 0:SUBLANES, :] = carry_ref[s, R:R + SUBLANES, :]

        xs, prevs = [], []
        for s in groups:
            carry_ref[s, SUBLANES:SUBLANES + R, :] = x_ref[s]
            xs.append(x_ref[s])
            prevs.append(carry_ref[s, SUBLANES - 1:SUBLANES - 1 + R, :])
    else:
        row = lax.broadcasted_iota(jnp.int32, (L, width), 0)
        xs, prevs = [], []
        for s in groups:
            x = x_ref[s]
            xs.append(x)
            for g in range(G):
                xg = x[g * L:(g + 1) * L]
                q = s * G + g
                prevs.append(jnp.where(row == 0, prev_ref[q:q + 1, :], pltpu.roll(xg, 1, 0)))
    cat = lambda parts: parts[0] if len(parts) == 1 else jnp.concatenate(parts, axis=0)
    return cat(xs), cat(prevs)


def _rwkv_kernel(rkv_ref, lora_ref, sh_rkv_ref, sh_lora_ref, s0_ref, mu_rkv_ref, mu_lora_ref,
                 w0_ref, a0_ref, wab_ref, gb_ref, kk_ref, ka_ref, rk_ref, gnw_ref, gnb_ref,
                 o_ref, s_ref, last_rkv_ref, last_lora_ref, xr_ref, xl_ref, m_ref, *, seq_len, n_seq):
    R, L, G, S, sps = _mixer_geometry(seq_len, n_seq)
    RT = S * R
    NQ = S * G
    c = pl.program_id(1)
    W = RWKV_WIDTH
    HD = RWKV_HEAD
    pairs = range(RWKV_HEADS // 2)
    groups = range(S)
    seqs = range(G)

    @pl.when(c == 0)
    def _():
        zero = jnp.zeros((HD, HD), F32)
        for q in range(NQ):
            for j in pairs:
                vk = jnp.concatenate([jnp.concatenate([s0_ref[q, 2 * j], zero], axis=1),
                                      jnp.concatenate([zero, s0_ref[q, 2 * j + 1]], axis=1)], axis=0)
                m_ref[q, j] = vk.T

    p, p_prev = _shifted_rows(rkv_ref, sh_rkv_ref, xr_ref, c, R, L, G, S)
    xs = p + (p_prev - p) * mu_rkv_ref[...]
    pl_, pl_prev = _shifted_rows(lora_ref, sh_lora_ref, xl_ref, c, R, L, G, S)
    xl = pl_ + (pl_prev - pl_) * mu_lora_ref[...]
    r = xs[:, :W]
    k = xs[:, W:2 * W]
    v = xs[:, 2 * W:]

    wa_in = xl[:, :LANES]
    lane = lax.broadcasted_iota(jnp.int32, (RT, LANES), 1)
    wa_in = jnp.where(lane < RWKV_LORA_W, jnp.tanh(wa_in), wa_in)
    wa = _dot(wa_in, wab_ref[...])
    w = -_softplus(-(w0_ref[...] + wa[:, :W])) - 0.5
    lw = -jnp.exp(w)
    a = _sigmoid(a0_ref[...] + wa[:, W:])
    gate = _dot(_sigmoid(xl[:, LANES:]), gb_ref[...])
    kk_raw = k * kk_ref[...]
    k2 = k * (1.0 + (a - 1.0) * ka_ref[...])

    same_t, causal_t, _ = _seq_masks(RT, L)
    lc = _ones_dot(jnp.where(causal_t, 1.0, 0.0).astype(F32), lw)
    if G == 1:
        ltot = jnp.concatenate([jnp.broadcast_to(lc[(s + 1) * R - 1:(s + 1) * R, :], (R, W))
                                for s in groups], axis=0) if S > 1 else \
            jnp.broadcast_to(lc[R - 1:R, :], (R, W))
    else:
        ltot = _ones_dot(jnp.where(same_t, 1.0, 0.0).astype(F32), lw)
    e_inv = jnp.exp(-lc)
    e_rem = jnp.exp(ltot - lc)

    kk = kk_raw * lax.rsqrt(_lane_group_sums(kk_raw * kk_raw, HD) + L2_EPS)
    kka = kk * a
    ct = kk * jnp.exp(lc - lw)
    rt = r * jnp.exp(lc)
    bh = kka * e_inv
    kh = k2 * e_inv
    bb = kka * e_rem
    kb = k2 * e_rem
    p_rows = [ltot[q * L:q * L + 1] for q in range(NQ)]
    p_rows = p_rows + [p_rows[0]] * (-NQ % SUBLANES)
    pt = jnp.exp(jnp.concatenate(p_rows, axis=0)).T

    causal2, strict2, eye2, even_half = _pair_masks(R, L)
    block_diag = ((lax.broadcasted_iota(jnp.int32, (LANES, LANES), 0) < HD)
                  == (lax.broadcasted_iota(jnp.int32, (LANES, LANES), 1) < HD))
    even_all = jnp.bitwise_and(lax.broadcasted_iota(jnp.int32, (R, W), 1), LANES - 1) < HD
    tile = lambda j: slice(j * LANES, (j + 1) * LANES)
    bd = lambda x: _bd(x, even_half)

    if G > 1:
        row2 = lax.broadcasted_iota(jnp.int32, (2 * R, LANES), 0)
        seq_of_row = jnp.right_shift(jnp.bitwise_and(row2, R - 1), L.bit_length() - 1)

    units = [(s, j) for s in groups for j in pairs]
    xps, vps, bdb, bdk, bkts = {}, {}, {}, {}, {}
    for s in groups:
        rs = slice(s * R, (s + 1) * R)
        x_s = jnp.concatenate([ct[rs], rt[rs]], axis=0)
        bt_s = jnp.concatenate([jnp.where(even_all, bh[rs], 0.0), jnp.where(even_all, 0.0, bh[rs])],
                               axis=0).T
        kt_s = jnp.concatenate([jnp.where(even_all, kh[rs], 0.0), jnp.where(even_all, 0.0, kh[rs])],
                               axis=0).T
        bkt_s = jnp.concatenate([bb[rs], kb[rs]], axis=0).T
        for j in pairs:
            xps[(s, j)] = x_s[:, tile(j)]
            vps[(s, j)] = v[rs, tile(j)]
            bdb[(s, j)] = bt_s[tile(j), :]
            bdk[(s, j)] = kt_s[tile(j), :]
            bkts[(s, j)] = bkt_s[tile(j), :]
    gb = {u: _dot(xps[u], bdb[u], mode=P_GRAM) for u in units}
    gk = {u: _dot(xps[u], bdk[u], mode=P_GRAM) for u in units}
    ms = {(s, j): [m_ref[s * G + g, j] for g in seqs] for s, j in units}
    xm_c, xm_r = {}, {}
    for u in units:
        if G == 1:
            xm = _dot(xps[u], ms[u][0], mode=P_STATE)
            xm_c[u], xm_r[u] = xm[:R], xm[R:]
        else:
            parts = [_dot(jnp.concatenate([xps[u][g * L:(g + 1) * L],
                                           xps[u][R + g * L:R + (g + 1) * L]], axis=0),
                          ms[u][g], mode=P_STATE) for g in seqs]
            xm_c[u] = jnp.concatenate([p_[:L] for p_ in parts], axis=0)
            xm_r[u] = jnp.concatenate([p_[L:] for p_ in parts], axis=0)
    ts = _pair_inverses({u: jnp.where(strict2, -gb[u][:R], 0.0) for u in units}, R, L, eye2, even_half)
    bdv = {u: bd(vps[u]) for u in units}
    akvs = {u: _dot(jnp.where(strict2, gk[u][:R], 0.0), bdv[u], mode=P_OUT) for u in units}
    us = {u: _dot(ts[u], bd(-(xm_c[u] + akvs[u])), mode=P_SOLVE) for u in units}
    uvs = {u: jnp.concatenate([us[u], vps[u]], axis=0) for u in units}
    ys = {u: xm_r[u] + _dot(
        jnp.concatenate([jnp.where(causal2, gb[u][R:], 0.0), jnp.where(causal2, gk[u][R:], 0.0)], axis=1),
        jnp.concatenate([bd(us[u]), bdv[u]], axis=0), mode=P_OUT) for u in units}
    for s, j in units:
        u = (s, j)
        for g in seqs:
            q = s * G + g
            uv_g = uvs[u] if G == 1 else jnp.where(seq_of_row == g, uvs[u], 0.0)
            upd = _dot(bkts[u], uv_g, mode=P_STATE)
            m_ref[q, j] = pt[tile(j), q:q + 1] * ms[u][g] + jnp.where(block_diag, upd, 0.0)

    y_rows = [jnp.concatenate([ys[(s, j)] for j in pairs], axis=1) for s in groups]
    y_all = y_rows[0] if S == 1 else jnp.concatenate(y_rows, axis=0)
    mean = _lane_group_sums(y_all, HD) * (1.0 / HD)
    yc = y_all - mean
    var = _lane_group_sums(yc * yc, HD) * (1.0 / HD)
    yn = yc * lax.rsqrt(var + GN_EPS) * gnw_ref[...] + gnb_ref[...]
    yn = yn + _lane_group_sums(r * k2 * rk_ref[...], HD) * v
    o_ref[...] = (yn * gate).reshape(S, R, W).astype(o_ref.dtype)
    for q in range(NQ):
        last = (q // G) * R + (q % G + 1) * L - 1
        last_rkv_ref[q:q + 1, :] = p[last:last + 1, :]
        last_lora_ref[q:q + 1, :] = pl_[last:last + 1, :]

    @pl.when(c == sps - 1)
    def _():
        for q in range(NQ):
            for j in pairs:
                vk = m_ref[q, j].T
                s_ref[q, 2 * j] = vk[:HD, :HD]
                s_ref[q, 2 * j + 1] = vk[HD:, HD:]


def _rwkv_mixer(proj2d, sh_rkv, sh_lora, s0, mu_rkv, mu_lora, w0, a0, wab, gb, kk, ka, rk, gnw, gnb,
                seq_len):
    m = proj2d.shape[0]
    nseq = m // seq_len
    R, L, G, S, sps = _mixer_geometry(seq_len, nseq)
    W = RWKV_WIDTH
    proj3d = proj2d.reshape(nseq // G, sps * R, PROJ_WIDTH)
    const2 = lambda i, c: (0, 0)
    row = lambda width: pl.BlockSpec((1, width), const2)
    rows_map = lambda col: (lambda i, c: (i, c, col))
    per_seq = lambda *dims: pl.BlockSpec((S * G,) + dims, lambda i, c: (i,) + (0,) * len(dims))
    carry = lambda width: pltpu.VMEM((S, SUBLANES + R, width) if G == 1 else (SUBLANES, LANES), F32)
    o, s_new, last_rkv, last_lora = pl.pallas_call(
        functools.partial(_rwkv_kernel, seq_len=seq_len, n_seq=nseq),
        grid=(nseq // (S * G), sps),
        in_specs=[
            pl.BlockSpec((S, R, 3 * W), rows_map(COL_RKV // (3 * W))),
            pl.BlockSpec((S, R, RWKV_LORA), rows_map(COL_LORA // RWKV_LORA)),
            pl.BlockSpec((S * G, 3 * W), lambda i, c: (i, 0)),
            pl.BlockSpec((S * G, RWKV_LORA), lambda i, c: (i, 3 * W // RWKV_LORA)),
            per_seq(RWKV_HEADS, RWKV_HEAD, RWKV_HEAD),
            row(3 * W), row(RWKV_LORA), row(W), row(W),
            pl.BlockSpec((LANES, 2 * W), const2),
            pl.BlockSpec((RWKV_LORA_G, W), const2),
            row(W), row(W), row(W), row(W), row(W),
        ],
        out_specs=[pl.BlockSpec((S, R, W), rows_map(0)), per_seq(RWKV_HEADS, RWKV_HEAD, RWKV_HEAD),
                   per_seq(3 * W), per_seq(RWKV_LORA)],
        out_shape=[
            jax.ShapeDtypeStruct((nseq // G, sps * R, W), BF16),
            jax.ShapeDtypeStruct((nseq, RWKV_HEADS, RWKV_HEAD, RWKV_HEAD), F32),
            jax.ShapeDtypeStruct((nseq, 3 * W), F32),
            jax.ShapeDtypeStruct((nseq, RWKV_LORA), F32),
        ],
        scratch_shapes=[carry(3 * W), carry(RWKV_LORA),
                        pltpu.VMEM((S * G, RWKV_HEADS // 2, LANES, LANES), F32)],
        compiler_params=pltpu.CompilerParams(
            dimension_semantics=("arbitrary", "arbitrary"), vmem_limit_bytes=VMEM_LIMIT),
        name="rwkv_mixer",
    )(proj3d, proj3d, sh_rkv, sh_lora, s0, mu_rkv, mu_lora, w0, a0, wab, gb, kk, ka, rk, gnw, gnb)
    return o.reshape(m, W), s_new, jnp.concatenate([last_rkv, last_lora], axis=-1)


def _outproj_kernel(x_ref, oa_ref, ob_ref, wa_ref, wb_ref, g_ref, x1_ref, hn_ref):
    x1 = (x_ref[...] + jnp.dot(oa_ref[...], wa_ref[...], preferred_element_type=F32)
          + jnp.dot(ob_ref[...], wb_ref[...], preferred_element_type=F32))
    x1_ref[...] = x1
    ms = jnp.mean(x1 * x1, axis=-1, keepdims=True)
    hn_ref[...] = (x1 * lax.rsqrt(ms + RMS_EPS) * g_ref[...]).astype(BF16)


def _outproj(x2d, oa, ob, wo_bf16, g_row, tm):
    m = x2d.shape[0]
    return pl.pallas_call(
        _outproj_kernel,
        grid=(m // tm,),
        in_specs=[
            pl.BlockSpec((tm, D_MODEL), lambda i: (i, 0)),
            pl.BlockSpec((tm, GDN_WIDTH), lambda i: (i, 0)),
            pl.BlockSpec((tm, RWKV_WIDTH), lambda i: (i, 0)),
            pl.BlockSpec((GDN_WIDTH, D_MODEL), lambda i: (0, 0)),
            pl.BlockSpec((RWKV_WIDTH, D_MODEL), lambda i: (1, 0)),
            pl.BlockSpec((1, D_MODEL), lambda i: (0, 0)),
        ],
        out_specs=[pl.BlockSpec((tm, D_MODEL), lambda i: (i, 0)),
                   pl.BlockSpec((tm, D_MODEL), lambda i: (i, 0))],
        out_shape=[jax.ShapeDtypeStruct((m, D_MODEL), F32),
                   jax.ShapeDtypeStruct((m, D_MODEL), BF16)],
        compiler_params=pltpu.CompilerParams(
            dimension_semantics=("arbitrary",), vmem_limit_bytes=VMEM_LIMIT),
        name="outproj",
    )(x2d, oa, ob, wo_bf16, wo_bf16, g_row)


FFN_TF = 512
FFN_NF = D_FF // FFN_TF


def _ffn_pipeline(f, up_fn, wd_ref, x1_ref, fg_ref, y_ref, acc_ref):
    @pl.when(f == 0)
    def _():
        acc_ref[...] = jnp.zeros(acc_ref.shape, F32)

    acc_ref[...] += jnp.dot(up_fn().astype(BF16), wd_ref[...], preferred_element_type=F32)

    @pl.when(f == FFN_NF - 1)
    def _():
        xo = x1_ref[...] + acc_ref[...]
        ms = jnp.mean(xo * xo, axis=-1, keepdims=True)
        y_ref[...] = xo * lax.rsqrt(ms + RMS_EPS) * fg_ref[...]


def _ffn_long_kernel(hn_ref, x1_ref, wg0_ref, wu0_ref, cwg0_ref, cwu0_ref, wd0_ref,
                     wg1_ref, wu1_ref, cwg1_ref, cwu1_ref, wd1_ref, fg_ref,
                     y_ref, n_ref, acc_ref, carry_ref, hbuf_ref, *, tt):
    ti = pl.program_id(1)
    f = pl.program_id(2)
    steps = pl.num_programs(2)
    hn = hn_ref[...]

    def act(slot, chunk, wg_ref, wu_ref, cwg_ref, cwu_ref):
        convs = []
        for j, (w_ref, cw_ref) in enumerate(((wg_ref, cwg_ref), (wu_ref, cwu_ref))):
            h = jnp.dot(hn, w_ref[...], preferred_element_type=F32)
            hbuf_ref[slot, j, SUBLANES:SUBLANES + tt, :] = h
            prev = carry_ref[chunk, j]
            hbuf_ref[slot, j, 0:SUBLANES, :] = jnp.where(ti == 0, jnp.zeros_like(prev), prev)
            cw = cw_ref[...]
            conv = h * cw[FFN_CONV - 1:FFN_CONV, :]
            for i in range(FFN_CONV - 1):
                off = SUBLANES - (FFN_CONV - 1) + i
                conv = conv + hbuf_ref[slot, j, off:off + tt, :] * cw[i:i + 1, :]
            carry_ref[chunk, j] = h[tt - SUBLANES:, :]
            n_ref[j, :, pl.ds(pl.multiple_of(chunk * FFN_TF, FFN_TF), FFN_TF)] = h[tt - (FFN_CONV - 1):, :]
            convs.append(conv)
        return (_silu(convs[0]) * convs[1]).astype(BF16)

    @pl.when(f == 0)
    def _():
        acc_ref[...] = jnp.zeros(acc_ref.shape, F32)

    def both():
        a0 = act(0, 2 * f, wg0_ref, wu0_ref, cwg0_ref, cwu0_ref)
        a1 = act(1, 2 * f + 1, wg1_ref, wu1_ref, cwg1_ref, cwu1_ref)
        acc_ref[...] += (jnp.dot(a0, wd0_ref[...], preferred_element_type=F32)
                         + jnp.dot(a1, wd1_ref[...], preferred_element_type=F32))

    if FFN_NF % 2 == 0:
        both()
    else:
        pl.when(f < steps - 1)(both)

        @pl.when(f == steps - 1)
        def _():
            a0 = act(0, 2 * f, wg0_ref, wu0_ref, cwg0_ref, cwu0_ref)
            acc_ref[...] += jnp.dot(a0, wd0_ref[...], preferred_element_type=F32)

    @pl.when(f == steps - 1)
    def _():
        xo = x1_ref[...] + acc_ref[...]
        ms = jnp.mean(xo * xo, axis=-1, keepdims=True)
        y_ref[...] = xo * lax.rsqrt(ms + RMS_EPS) * fg_ref[...]


def _ffn_long(hn, x1, wup, cw, wdown, fg_row, tt):
    b, t, _ = hn.shape
    tf = FFN_TF
    nf = FFN_NF
    steps = -(-nf // 2)
    chunk = (lambda f: 2 * f, lambda f: jnp.minimum(2 * f + 1, nf - 1))
    weights = []
    for k in range(2):
        weights += [
            pl.BlockSpec((D_MODEL, tf), lambda i, s, f, k=k: (0, chunk[k](f))),
            pl.BlockSpec((D_MODEL, tf), lambda i, s, f, k=k: (0, nf + chunk[k](f))),
            pl.BlockSpec((FFN_CONV, tf), lambda i, s, f, k=k: (0, chunk[k](f))),
            pl.BlockSpec((FFN_CONV, tf), lambda i, s, f, k=k: (0, nf + chunk[k](f))),
            pl.BlockSpec((tf, D_MODEL), lambda i, s, f, k=k: (chunk[k](f), 0)),
        ]
    return pl.pallas_call(
        functools.partial(_ffn_long_kernel, tt=tt),
        grid=(b, t // tt, steps),
        in_specs=[
            pl.BlockSpec((None, tt, D_MODEL), lambda i, s, f: (i, s, 0)),
            pl.BlockSpec((None, tt, D_MODEL), lambda i, s, f: (i, s, 0)),
        ] + weights + [pl.BlockSpec((1, D_MODEL), lambda i, s, f: (0, 0))],
        out_specs=[
            pl.BlockSpec((None, tt, D_MODEL), lambda i, s, f: (i, s, 0)),
            pl.BlockSpec((None, None, 2, FFN_CONV - 1, D_FF), lambda i, s, f: (i, s, 0, 0, 0)),
        ],
        out_shape=[
            jax.ShapeDtypeStruct((b, t, D_MODEL), F32),
            jax.ShapeDtypeStruct((b, t // tt, 2, FFN_CONV - 1, D_FF), F32),
        ],
        scratch_shapes=[
            pltpu.VMEM((tt, D_MODEL), F32),
            pltpu.VMEM((nf, 2, SUBLANES, tf), F32),
            pltpu.VMEM((2, 2, SUBLANES + tt, tf), F32),
        ],
        compiler_params=pltpu.CompilerParams(
            dimension_semantics=("arbitrary", "arbitrary", "arbitrary"),
            vmem_limit_bytes=VMEM_LIMIT_FFN),
        name="ffn_long",
    )(hn, x1, wup, wup, cw, cw, wdown, wup, wup, cw, cw, wdown, fg_row)


def _ffn_short_kernel(hn_ref, x1_ref, wg_ref, wu_ref, cwg_ref, cwu_ref, wd_ref, fg_ref,
                      b0g_ref, b1g_ref, b0u_ref, b1u_ref,
                      y_ref, ng_ref, nu_ref, acc_ref, z_ref, hb_ref, *, tt, seq):
    f = pl.program_id(1)
    nseq = tt // seq

    def up_fn():
        hn = hn_ref[...]
        t_in_seq = lax.broadcasted_iota(jnp.int32, (tt, FFN_TF), 0) % seq
        convs = []
        groups = ((wg_ref, cwg_ref, b0g_ref, b1g_ref, ng_ref), (wu_ref, cwu_ref, b0u_ref, b1u_ref, nu_ref))
        for w_ref, cw_ref, b0_ref, b1_ref, n_ref in groups:
            h = jnp.dot(hn, w_ref[...], preferred_element_type=F32)
            z_ref[...] = jnp.zeros(z_ref.shape, F32)
            for lb in range(FFN_TF // LANES):
                cols = slice(lb * LANES, (lb + 1) * LANES)
                z_ref[lb, pl.ds(0, nseq, stride=seq), :] = b0_ref[:, cols]
                z_ref[lb, pl.ds(1, nseq, stride=seq), :] = b1_ref[:, cols]
                hb_ref[lb] = h[:, cols]
                n_ref[0, :, cols] = hb_ref[lb, pl.ds(seq - 2, nseq, stride=seq), :]
                n_ref[1, :, cols] = hb_ref[lb, pl.ds(seq - 1, nseq, stride=seq), :]
            z = jnp.concatenate([z_ref[lb] for lb in range(FFN_TF // LANES)], axis=1)
            s1 = jnp.where(t_in_seq == 0, pltpu.roll(z, tt - 1, 0), pltpu.roll(h, 1, 0))
            s2 = jnp.where(t_in_seq < 2, z, pltpu.roll(h, 2, 0))
            cw = cw_ref[...]
            convs.append(h * cw[2:3, :] + s1 * cw[1:2, :] + s2 * cw[0:1, :])
        return _silu(convs[0]) * convs[1]

    _ffn_pipeline(f, up_fn, wd_ref, x1_ref, fg_ref, y_ref, acc_ref)


def _ffn_short(hn, x1, wup, cw, wdown, fg_row, hist, tt, seq):
    m = hn.shape[0]
    tf = FFN_TF
    nf = FFN_NF
    nseq = tt // seq
    up = lambda f: f
    down = lambda f: f
    st = lambda k, col0: pl.BlockSpec((None, nseq, tf), lambda i, f: (k, i, col0 + up(f)))
    new = pl.BlockSpec((2, nseq, tf), lambda i, f: (0, i, up(f)))
    new_shape = jax.ShapeDtypeStruct((2, m // seq, D_FF), F32)
    return pl.pallas_call(
        functools.partial(_ffn_short_kernel, tt=tt, seq=seq),
        grid=(m // tt, nf),
        in_specs=[
            pl.BlockSpec((tt, D_MODEL), lambda i, f: (i, 0)),
            pl.BlockSpec((tt, D_MODEL), lambda i, f: (i, 0)),
            pl.BlockSpec((D_MODEL, tf), lambda i, f: (0, up(f))),
            pl.BlockSpec((D_MODEL, tf), lambda i, f: (0, nf + up(f))),
            pl.BlockSpec((FFN_CONV, tf), lambda i, f: (0, up(f))),
            pl.BlockSpec((FFN_CONV, tf), lambda i, f: (0, nf + up(f))),
            pl.BlockSpec((tf, D_MODEL), lambda i, f: (down(f), 0)),
            pl.BlockSpec((1, D_MODEL), lambda i, f: (0, 0)),
            st(0, 0), st(1, 0), st(0, nf), st(1, nf),
        ],
        out_specs=[pl.BlockSpec((tt, D_MODEL), lambda i, f: (i, 0)), new, new],
        out_shape=[jax.ShapeDtypeStruct((m, D_MODEL), F32), new_shape, new_shape],
        scratch_shapes=[
            pltpu.VMEM((tt, D_MODEL), F32),
            pltpu.VMEM((tf // LANES, tt, LANES), F32),
            pltpu.VMEM((tf // LANES, tt, LANES), F32),
        ],
        compiler_params=pltpu.CompilerParams(
            dimension_semantics=("arbitrary", "arbitrary"), vmem_limit_bytes=VMEM_LIMIT),
        name="ffn_short",
    )(hn, x1, wup, wup, cw, cw, wdown, fg_row, hist, hist, hist, hist)


def _pad_lanes(vec, offset):
    out = jnp.zeros((LANES,), F32)
    return out.at[offset:offset + vec.shape[0]].set(vec.astype(F32))


def _trunk(x, s_gdn, s_gconv, s_rwkv, s_shift, s_ffn, prm, *, long_seq):
    b, t, _ = x.shape
    m = b * t
    x2d = x.reshape(m, D_MODEL)
    tm = min(512, m)
    proj = _inproj(x2d, prm["ln1_g"], prm["w_in"])

    o_a, gdn_new, gconv_new = _gdn_mixer(proj, s_gconv, s_gdn, prm["gdn_conv_w"], prm["alog_r"],
                                         prm["dtb_r"], prm["alog_c"], prm["dtb_c"], prm["gdn_norm_g"], t)
    o_b, rwkv_new, shift_new = _rwkv_mixer(proj, s_shift, s_shift, s_rwkv, prm["mu_rkv"], prm["mu_lora"],
                                           prm["rwkv_w0"], prm["rwkv_a0"], prm["rwkv_wab"],
                                           prm["rwkv_g_b"], prm["rwkv_k_k"], prm["rwkv_k_a"],
                                           prm["rwkv_r_k"], prm["rwkv_gn_w"], prm["rwkv_gn_b"], t)

    x1, hn = _outproj(x2d, o_a, o_b, prm["w_o"], prm["ln2_g"], tm)
    if long_seq:
        tt = min(512, t)
        y, n_gu = _ffn_long(hn.reshape(b, t, D_MODEL), x1.reshape(b, t, D_MODEL), prm["ffn_w_up"],
                            prm["ffn_conv_w"], prm["ffn_w_down"], prm["final_g"], tt)
        ffn_new = jnp.concatenate([n_gu[:, -1, 0], n_gu[:, -1, 1]], axis=-1)
    else:
        tt = min(512, m)
        y, n_g, n_u = _ffn_short(hn, x1, prm["ffn_w_up"], prm["ffn_conv_w"], prm["ffn_w_down"],
                                 prm["final_g"], jnp.swapaxes(s_ffn, 0, 1), tt, t)
        y = y.reshape(b, t, D_MODEL)
        ffn_new = jnp.swapaxes(jnp.concatenate([n_g, n_u], axis=-1), 0, 1)

    return y, gdn_new[None], gconv_new[None], rwkv_new[None], shift_new[None], ffn_new[None]


def kernel(x_prompt, x_sample, state_gdn, state_gdn_conv, state_rwkv, state_rwkv_shift, state_ffn_conv, ln1_g, w_in, gdn_conv_w, gdn_a_log, gdn_dt_bias, gdn_norm_g, rwkv_mu, rwkv_w0, rwkv_w_b, rwkv_a0, rwkv_a_b, rwkv_g_b, rwkv_k_k, rwkv_k_a, rwkv_r_k, rwkv_gn_w, rwkv_gn_b, w_o, ln2_g, ffn_w_up, ffn_conv_w, ffn_w_down, final_g):
    assert ln1_g.shape[0] == 1, "single-layer trunk"
    w_perm = _permute_win(w_in[0])
    mu = rwkv_mu[0]
    zeros_w = jnp.zeros((RWKV_LORA_W, RWKV_WIDTH), F32)
    wab = jnp.concatenate([
        jnp.concatenate([rwkv_w_b[0], zeros_w], axis=1),
        jnp.concatenate([zeros_w, rwkv_a_b[0]], axis=1)], axis=0).astype(BF16)
    alog = _pad_lanes(gdn_a_log[0], GDN_HEADS)
    dtb = _pad_lanes(gdn_dt_bias[0], GDN_HEADS)
    prm = {
        "ln1_g": ln1_g[0][None], "w_in": w_perm, "gdn_conv_w": gdn_conv_w[0],
        "alog_r": alog[None], "dtb_r": dtb[None], "alog_c": alog[:, None], "dtb_c": dtb[:, None],
        "gdn_norm_g": gdn_norm_g[0][None],
        "mu_rkv": mu[None, :3 * RWKV_WIDTH], "mu_lora": mu[None, 3 * RWKV_WIDTH:],
        "rwkv_w0": rwkv_w0[0][None], "rwkv_a0": rwkv_a0[0][None], "rwkv_wab": wab,
        "rwkv_g_b": rwkv_g_b[0].astype(BF16), "rwkv_k_k": rwkv_k_k[0][None],
        "rwkv_k_a": rwkv_k_a[0][None], "rwkv_r_k": rwkv_r_k[0].reshape(1, RWKV_WIDTH),
        "rwkv_gn_w": rwkv_gn_w[0][None], "rwkv_gn_b": rwkv_gn_b[0][None],
        "w_o": w_o[0].astype(BF16), "ln2_g": ln2_g[0][None],
        "ffn_w_up": ffn_w_up[0].astype(BF16), "ffn_conv_w": ffn_conv_w[0],
        "ffn_w_down": ffn_w_down[0].astype(BF16), "final_g": final_g[None],
    }

    bp = x_prompt.shape[0]
    zero_states = (
        jnp.zeros((bp,) + state_gdn.shape[2:], F32),
        jnp.zeros((bp,) + state_gdn_conv.shape[2:], F32),
        jnp.zeros((bp,) + state_rwkv.shape[2:], F32),
        jnp.zeros((bp,) + state_rwkv_shift.shape[2:], F32),
        None,
    )
    outs_p = _trunk(x_prompt, *zero_states, prm, long_seq=True)
    outs_s = _trunk(x_sample, state_gdn[0], state_gdn_conv[0], state_rwkv[0], state_rwkv_shift[0],
                    state_ffn_conv[0], prm, long_seq=False)
    return (outs_p[0], outs_s[0]) + tuple(outs_p[1:]) + tuple(outs_s[1:])
```

```python
import functools

import jax
import jax.numpy as jnp
from jax import lax
from jax.experimental import pallas as pl
from jax.experimental.pallas import tpu as pltpu

F32 = jnp.float32
BF16 = jnp.bfloat16

D_MODEL = 2048
GDN_WIDTH = 1024
GDN_HEADS = 8
GDN_DK = 128
GDN_CONV = 4
RWKV_WIDTH = 1024
RWKV_HEAD = 64
RWKV_HEADS = 16
RWKV_LORA_W = 64
RWKV_LORA_A = 64
RWKV_LORA_G = 128
RWKV_LORA = RWKV_LORA_W + RWKV_LORA_A + RWKV_LORA_G
RWKV_PROJ = 3 * RWKV_WIDTH + RWKV_LORA
D_FF = 5632
FFN_CONV = 3
RMS_EPS = 1e-6
L2_EPS = 1e-12
GN_EPS = 64e-5

REF_OFF_Z = 3 * GDN_WIDTH
REF_OFF_B = 4 * GDN_WIDTH
REF_OFF_RWKV = REF_OFF_B + 2 * GDN_HEADS
REF_IN_WIDTH = REF_OFF_RWKV + RWKV_PROJ

LANES = 128
SUBLANES = 8
COL_QKV = 0
COL_RKV = 3 * GDN_WIDTH
COL_Z = COL_RKV + 3 * RWKV_WIDTH
COL_LORA = COL_Z + GDN_WIDTH
COL_BA = COL_LORA + RWKV_LORA
PROJ_WIDTH = 7680
PROJ_TN = 1536

MIX_ROWS = 64
MIX_GROUPS_LONG = 4
MIX_GROUPS_SHORT = 2

NN = (((1,), (0,)), ((), ()))
NT = (((1,), (1,)), ((), ()))
TN = (((0,), (0,)), ((), ()))

VMEM_LIMIT = 56 * 1024 * 1024
VMEM_LIMIT_FFN = 62 * 1024 * 1024

P_GRAM = "x1"
P_INV = "x1"
P_SOLVE = "x1"
P_STATE = "x1"
P_OUT = "x1"


def _split(x):
    hi = x.astype(BF16)
    return hi, (x - hi.astype(F32)).astype(BF16)


def _dot(a, b, dims=NN, mode="x1"):
    if mode == "hi":
        return lax.dot_general(a, b, dims, precision=lax.Precision.HIGHEST,
                               preferred_element_type=F32)
    if mode == "x3":
        a_hi, a_lo = _split(a)
        b_hi, b_lo = _split(b)
        d = lambda u, v: lax.dot_general(u, v, dims, preferred_element_type=F32)
        return d(a_hi, b_hi) + (d(a_hi, b_lo) + d(a_lo, b_hi))
    return lax.dot_general(a.astype(BF16), b.astype(BF16), dims, preferred_element_type=F32)


def _ones_dot(ones_mat, x, dims=NN):
    x1 = x.astype(BF16)
    r1 = x - x1.astype(F32)
    x2 = r1.astype(BF16)
    x3 = (r1 - x2.astype(F32)).astype(BF16)
    m = ones_mat.astype(BF16)
    if dims == NN:
        d = lambda v: lax.dot_general(m, v, dims, preferred_element_type=F32)
    else:
        d = lambda v: lax.dot_general(v, m, dims, preferred_element_type=F32)
    return d(x1) + (d(x2) + d(x3))


def _sigmoid(x):
    return 1.0 / (1.0 + jnp.exp(-x))


def _silu(x):
    return x * _sigmoid(x)


def _softplus(x):
    return jnp.maximum(x, 0.0) + jnp.log(1.0 + jnp.exp(-jnp.abs(x)))


def _seq_masks(rows, seq_len):
    r = lax.broadcasted_iota(jnp.int32, (rows, rows), 0)
    c = lax.broadcasted_iota(jnp.int32, (rows, rows), 1)
    if seq_len >= rows:
        return None, r >= c, r > c
    shift = seq_len.bit_length() - 1
    assert 1 << shift == seq_len
    same = jnp.right_shift(r, shift) == jnp.right_shift(c, shift)
    return same, same & (r >= c), same & (r > c)


def _wide_masks(rows, seq_len):
    r = lax.broadcasted_iota(jnp.int32, (rows, 2 * rows), 0)
    c = lax.broadcasted_iota(jnp.int32, (rows, 2 * rows), 1)
    right = c >= rows
    cc = jnp.where(right, c - rows, c)
    if seq_len >= rows:
        return r >= cc, right & (r > cc)
    shift = seq_len.bit_length() - 1
    same = jnp.right_shift(r, shift) == jnp.right_shift(cc, shift)
    return same & (r >= cc), same & right & (r > cc)


def _pair_masks(rows, seq_len):
    half = LANES // 2
    assert rows == half
    lane = lax.broadcasted_iota(jnp.int32, (rows, LANES), 1)
    row = lax.broadcasted_iota(jnp.int32, (rows, LANES), 0)
    col = jnp.bitwise_and(lane, half - 1)
    if seq_len >= rows:
        causal, strict = row >= col, row > col
    else:
        shift = seq_len.bit_length() - 1
        same = jnp.right_shift(row, shift) == jnp.right_shift(col, shift)
        causal, strict = same & (row >= col), same & (row > col)
    return causal, strict, jnp.where(row == col, 1.0, 0.0).astype(F32), lane < half


def _bd(x, first_half):
    return jnp.concatenate([jnp.where(first_half, x, 0.0), jnp.where(first_half, 0.0, x)], axis=0)


def _pair_inverses(neg_a, rows, nilpotency, eye2, first_half):
    qs = dict(neg_a)
    ts = {u: eye2 + q for u, q in qs.items()}
    n = 2
    if n < nilpotency:
        qs = {u: _dot(q, _bd(q, first_half), mode=P_INV) for u, q in qs.items()}
    while n < nilpotency:
        if 2 * n < nilpotency:
            tq = {u: _dot(jnp.concatenate([ts[u], qs[u]], axis=0), _bd(qs[u], first_half), mode=P_INV)
                  for u in qs}
            ts = {u: ts[u] + tq[u][:rows] for u in qs}
            qs = {u: tq[u][rows:] for u in qs}
        else:
            ts = {u: ts[u] + _dot(ts[u], _bd(qs[u], first_half), mode=P_INV) for u in qs}
        n *= 2
    return ts


def _lane_group_sums(x, group):
    tile = 2 * LANES
    shift = group.bit_length() - 1
    li = jnp.right_shift(lax.broadcasted_iota(jnp.int32, (tile, tile), 0), shift)
    lj = jnp.right_shift(lax.broadcasted_iota(jnp.int32, (tile, tile), 1), shift)
    ones = jnp.where(li == lj, 1.0, 0.0).astype(BF16)
    hi, lo = _split(x)
    d = lambda u: lax.dot_general(u, ones, NN, preferred_element_type=F32)
    return jnp.concatenate([d(hi[:, t:t + tile]) + d(lo[:, t:t + tile])
                            for t in range(0, x.shape[1], tile)], axis=1)


def _unit_lower_inverses(mats, rows, nilpotency):
    r = lax.broadcasted_iota(jnp.int32, (rows, rows), 0)
    c = lax.broadcasted_iota(jnp.int32, (rows, rows), 1)
    eye = jnp.where(r == c, 1.0, 0.0).astype(F32)
    qs = [-a for a in mats]
    ts = [eye + q for q in qs]
    n = 2
    if n < nilpotency:
        qs = [_dot(q, q, mode=P_INV) for q in qs]
    while n < nilpotency:
        if 2 * n < nilpotency:
            tq = [_dot(jnp.concatenate([t, q], axis=0), q, mode=P_INV) for t, q in zip(ts, qs)]
            ts = [t + p[:rows] for t, p in zip(ts, tq)]
            qs = [p[rows:] for p in tq]
        else:
            ts = [t + _dot(t, q, mode=P_INV) for t, q in zip(ts, qs)]
        n *= 2
    return ts


def _permute_win_kernel(w_ref, o_ref):
    rw = REF_OFF_RWKV
    rows = w_ref.shape[0]
    cast = lambda lo, hi: w_ref[:, lo:hi].astype(BF16)
    o_ref[:, COL_QKV:COL_RKV] = cast(0, REF_OFF_Z)
    o_ref[:, COL_RKV:COL_Z] = cast(rw, rw + 3 * RWKV_WIDTH)
    o_ref[:, COL_Z:COL_LORA] = cast(REF_OFF_Z, REF_OFF_B)
    o_ref[:, COL_LORA:COL_BA] = cast(rw + 3 * RWKV_WIDTH, REF_IN_WIDTH)
    tail = jnp.concatenate([w_ref[:, REF_OFF_B:REF_OFF_RWKV],
                            jnp.zeros((rows, PROJ_WIDTH - COL_BA - 2 * GDN_HEADS), F32)], axis=1)
    o_ref[:, COL_BA:] = tail.astype(BF16)


def _permute_win(w, tr=256):
    return pl.pallas_call(
        _permute_win_kernel,
        grid=(D_MODEL // tr,),
        in_specs=[pl.BlockSpec((tr, REF_IN_WIDTH), lambda i: (i, 0))],
        out_specs=pl.BlockSpec((tr, PROJ_WIDTH), lambda i: (i, 0)),
        out_shape=jax.ShapeDtypeStruct((D_MODEL, PROJ_WIDTH), BF16),
        compiler_params=pltpu.CompilerParams(
            dimension_semantics=("arbitrary",), vmem_limit_bytes=VMEM_LIMIT),
        name="permute_win",
    )(w)


def _inproj_kernel(x_ref, g_ref, w_ref, o_ref, xn_ref):
    @pl.when(pl.program_id(1) == 0)
    def _():
        x = x_ref[...]
        ms = jnp.mean(x * x, axis=-1, keepdims=True)
        xn_ref[...] = (x * lax.rsqrt(ms + RMS_EPS) * g_ref[...]).astype(BF16)

    o_ref[...] = jnp.dot(xn_ref[...], w_ref[...], preferred_element_type=F32)


def _inproj(x2d, g_row, w_bf16, tm):
    m = x2d.shape[0]
    return pl.pallas_call(
        _inproj_kernel,
        grid=(m // tm, PROJ_WIDTH // PROJ_TN),
        in_specs=[
            pl.BlockSpec((tm, D_MODEL), lambda i, j: (i, 0)),
            pl.BlockSpec((1, D_MODEL), lambda i, j: (0, 0)),
            pl.BlockSpec((D_MODEL, PROJ_TN), lambda i, j: (0, j)),
        ],
        out_specs=pl.BlockSpec((tm, PROJ_TN), lambda i, j: (i, j)),
        out_shape=jax.ShapeDtypeStruct((m, PROJ_WIDTH), F32),
        scratch_shapes=[pltpu.VMEM((tm, D_MODEL), BF16)],
        compiler_params=pltpu.CompilerParams(
            dimension_semantics=("arbitrary", "arbitrary"), vmem_limit_bytes=VMEM_LIMIT),
        name="inproj",
    )(x2d, g_row, w_bf16)


def _mixer_geometry(seq_len, n_seq, short_groups=1):
    rows = MIX_ROWS
    length = min(seq_len, rows)
    assert rows % length == 0 and seq_len % length == 0 and length % SUBLANES == 0
    per_group = rows // length
    groups = MIX_GROUPS_LONG if per_group == 1 else short_groups
    while n_seq % (groups * per_group):
        groups //= 2
    assert groups >= 1
    return rows, length, per_group, groups, seq_len // length


def _gdn_kernel(qkv_ref, z_ref, ba_ref, cbuf_ref, s0_ref, convw_ref, alog_r_ref, dtb_r_ref,
                alog_c_ref, dtb_c_ref, ng_ref, o_ref, s_ref, tail_ref, xp_ref, *, seq_len, n_seq):
    R, L, G, S, _ = _mixer_geometry(seq_len, n_seq, MIX_GROUPS_SHORT)
    RT = S * R
    c = pl.program_id(1)
    width = 3 * GDN_WIDTH
    hist = GDN_CONV - 1
    cw = convw_ref[...]
    groups = range(S)
    seqs = range(G)

    @pl.when(c == 0)
    def _():
        s_ref[...] = s0_ref[...]

    pieces = []
    if G == 1:
        @pl.when(c == 0)
        def _():
            for s in groups:
                xp_ref[s, 0:SUBLANES, :] = jnp.zeros((SUBLANES, width), F32)
                xp_ref[s, SUBLANES - hist:SUBLANES, :] = cbuf_ref[s]

        @pl.when(c > 0)
        def _():
            for s in groups:
                xp_ref[s, 0:SUBLANES, :] = xp_ref[s, R:R + SUBLANES, :]

        for s in groups:
            xp_ref[s, SUBLANES:SUBLANES + R, :] = qkv_ref[s]
            piece = qkv_ref[s] * cw[hist:hist + 1, :]
            for i in range(hist):
                off = SUBLANES - hist + i
                piece = piece + xp_ref[s, off:off + R, :] * cw[i:i + 1, :]
            pieces.append(piece)
    else:
        for s in groups:
            for g in seqs:
                q = s * G + g
                rows = slice(g * L, (g + 1) * L)
                xp_ref[q, SUBLANES - hist:SUBLANES, :] = cbuf_ref[q]
                xp_ref[q, SUBLANES:SUBLANES + L, :] = qkv_ref[s, rows, :]
                piece = qkv_ref[s, rows, :] * cw[hist:hist + 1, :]
                for i in range(hist):
                    off = SUBLANES - hist + i
                    piece = piece + xp_ref[q, off:off + L, :] * cw[i:i + 1, :]
                pieces.append(piece)
    qkv = _silu(pieces[0] if len(pieces) == 1 else jnp.concatenate(pieces, axis=0))

    same_t, causal_t, _ = _seq_masks(RT, L)
    causal01 = jnp.where(causal_t, 1.0, 0.0).astype(F32)

    ba = ba_ref[...].reshape(RT, LANES)
    ba_t = ba.T
    beta_c = _sigmoid(ba)
    g_c = -jnp.exp(alog_r_ref[...]) * _softplus(ba + dtb_r_ref[...])
    g_r = -jnp.exp(alog_c_ref[...]) * _softplus(ba_t + dtb_c_ref[...])
    gc_all = _ones_dot(causal01, g_c)
    gr_all = _ones_dot(causal01, g_r, NT)
    if same_t is None:
        gtot_all = jnp.broadcast_to(gc_all[RT - 1:RT, :], (RT, LANES))
    else:
        gtot_all = _ones_dot(jnp.where(same_t, 1.0, 0.0).astype(F32), g_c)

    qk_raw = qkv[:, :2 * GDN_WIDTH]
    qk_n = qk_raw * lax.rsqrt(_lane_group_sums(qk_raw * qk_raw, GDN_DK) + L2_EPS)
    q_all = qk_n[:, :GDN_WIDTH] * (GDN_DK ** -0.5)
    k_all = qk_n[:, GDN_WIDTH:]
    v_all = qkv[:, 2 * GDN_WIDTH:]

    causal2, strict2, eye2, first_half = _pair_masks(R, L)
    first_head = lax.broadcasted_iota(jnp.int32, (R, 2 * GDN_DK), 1) < GDN_DK
    zeros_u = jnp.zeros((R, GDN_DK), F32)
    zeros_rhs = jnp.zeros((R, 2 * GDN_DK), F32)
    chains = [(s, h) for s in groups for h in range(GDN_HEADS)]
    units = [(s, p) for s in groups for p in range(GDN_HEADS // 2)]
    seq_rows = [slice(g * L, (g + 1) * L) for g in seqs]
    qs, ks, vs, betas, gcols, gtots = {}, {}, {}, {}, {}, {}
    for s, h in chains:
        rs = slice(s * R, (s + 1) * R)
        lo = h * GDN_DK
        key = (s, h)
        qs[key] = q_all[rs, lo:lo + GDN_DK]
        ks[key] = k_all[rs, lo:lo + GDN_DK]
        vs[key] = v_all[rs, lo:lo + GDN_DK]
        betas[key] = beta_c[rs, h:h + 1]
        gcols[key] = gc_all[rs, GDN_HEADS + h:GDN_HEADS + h + 1]
        gtots[key] = gtot_all[rs, GDN_HEADS + h:GDN_HEADS + h + 1]
    kq2, decay2, a2 = {}, {}, {}
    for s, p in units:
        rs = slice(s * R, (s + 1) * R)
        cols = slice(2 * p * GDN_DK, (2 * p + 2) * GDN_DK)
        k_pair = k_all[rs, cols]
        kt_bd = jnp.concatenate([jnp.where(first_head, k_pair, 0.0),
                                 jnp.where(first_head, 0.0, k_pair)], axis=0).T
        kq = _dot(jnp.concatenate([k_pair, q_all[rs, cols]], axis=0), kt_bd, mode=P_GRAM)
        h0, h1 = (s, 2 * p), (s, 2 * p + 1)
        gcol2 = jnp.where(first_half, gcols[h0], gcols[h1])
        grow2 = jnp.concatenate([gr_all[GDN_HEADS + 2 * p:GDN_HEADS + 2 * p + 1, rs],
                                 gr_all[GDN_HEADS + 2 * p + 1:GDN_HEADS + 2 * p + 2, rs]], axis=1)
        dec = jnp.where(causal2, jnp.exp(jnp.where(causal2, gcol2 - grow2, 0.0)), 0.0)
        beta2 = jnp.where(first_half, betas[h0], betas[h1])
        kq2[(s, p)] = kq
        decay2[(s, p)] = dec
        a2[(s, p)] = jnp.where(strict2, -(beta2 * kq[:R] * dec), 0.0)
    t2 = _pair_inverses(a2, R, L, eye2, first_half)
    gammas = {key: jnp.exp(gcols[key]) for key in chains}

    def stacked(key, x, zeros):
        return jnp.concatenate([x, zeros] if key[1] % 2 == 0 else [zeros, x], axis=0)

    sols = {key: _dot(t2[(key[0], key[1] // 2)],
                      stacked(key, jnp.concatenate([(betas[key] * gammas[key]) * ks[key],
                                                    betas[key] * vs[key]], axis=1), zeros_rhs),
                      mode=P_SOLVE) for key in chains}
    states = {(s, h): [s_ref[s * G + g, h] for g in seqs] for s, h in chains}
    wss = {key: [_dot(jnp.concatenate([sols[key][rows, :GDN_DK], (qs[key] * gammas[key])[rows]], axis=0),
                      states[key][g], mode=P_STATE) for g, rows in enumerate(seq_rows)]
           for key in chains}
    us = {key: jnp.concatenate([sols[key][rows, GDN_DK:] - wss[key][g][:L]
                                for g, rows in enumerate(seq_rows)], axis=0) for key in chains}
    qk2 = {u: kq2[u][R:] * decay2[u] for u in units}
    outs = {key: jnp.concatenate([wss[key][g][L:] for g in seqs], axis=0)
            + _dot(qk2[(key[0], key[1] // 2)], stacked(key, us[key], zeros_u), mode=P_OUT)
            for key in chains}
    for s, h in chains:
        key = (s, h)
        kt = ks[key] * jnp.exp(gtots[key] - gcols[key])
        for g, rows in enumerate(seq_rows):
            gl = jnp.exp(gtots[key][g * L:g * L + 1, :])
            s_ref[s * G + g, h] = gl * states[key][g] + _dot(kt[rows], us[key][rows], TN, mode=P_STATE)
    ng = jnp.concatenate([ng_ref[...]] * GDN_HEADS, axis=1)
    for s in groups:
        o = jnp.concatenate([outs[(s, h)] for h in range(GDN_HEADS)], axis=1)
        ms = _lane_group_sums(o * o, GDN_DK) * (1.0 / GDN_DK)
        o = o * lax.rsqrt(ms + RMS_EPS) * ng
        o_ref[s] = (o * _silu(z_ref[s])).astype(o_ref.dtype)
    for s in groups:
        for g in seqs:
            tail_ref[s * G + g] = qkv_ref[s, (g + 1) * L - hist:(g + 1) * L, :]


def _gdn_mixer(proj2d, cbuf, s0, convw, alog_r, dtb_r, alog_c, dtb_c, ng, seq_len):
    m = proj2d.shape[0]
    nseq = m // seq_len
    R, L, G, S, sps = _mixer_geometry(seq_len, nseq, MIX_GROUPS_SHORT)
    width = 3 * GDN_WIDTH
    proj3d = proj2d.reshape(nseq // G, sps * R, PROJ_WIDTH)
    const2 = lambda i, c: (0, 0)
    rows_map = lambda col: (lambda i, c: (i, c, col))
    per_seq = lambda *dims: pl.BlockSpec((S * G,) + dims, lambda i, c: (i,) + (0,) * len(dims))
    xp_shape = (S, SUBLANES + R, width) if G == 1 else (S * G, SUBLANES + L, width)
    o, s_new, conv_new = pl.pallas_call(
        functools.partial(_gdn_kernel, seq_len=seq_len, n_seq=nseq),
        grid=(nseq // (S * G), sps),
        in_specs=[
            pl.BlockSpec((S, R, width), rows_map(COL_QKV // width)),
            pl.BlockSpec((S, R, GDN_WIDTH), rows_map(COL_Z // GDN_WIDTH)),
            pl.BlockSpec((S, R, LANES), rows_map(COL_BA // LANES)),
            per_seq(GDN_CONV - 1, width),
            per_seq(GDN_HEADS, GDN_DK, GDN_DK),
            pl.BlockSpec((GDN_CONV, width), const2),
            pl.BlockSpec((1, LANES), const2),
            pl.BlockSpec((1, LANES), const2),
            pl.BlockSpec((LANES, 1), const2),
            pl.BlockSpec((LANES, 1), const2),
            pl.BlockSpec((1, GDN_DK), const2),
        ],
        out_specs=[
            pl.BlockSpec((S, R, GDN_WIDTH), rows_map(0)),
            per_seq(GDN_HEADS, GDN_DK, GDN_DK),
            per_seq(GDN_CONV - 1, width),
        ],
        out_shape=[
            jax.ShapeDtypeStruct((nseq // G, sps * R, GDN_WIDTH), BF16),
            jax.ShapeDtypeStruct((nseq, GDN_HEADS, GDN_DK, GDN_DK), F32),
            jax.ShapeDtypeStruct((nseq, GDN_CONV - 1, width), F32),
        ],
        scratch_shapes=[pltpu.VMEM(xp_shape, F32)],
        compiler_params=pltpu.CompilerParams(
            dimension_semantics=("arbitrary", "arbitrary"), vmem_limit_bytes=VMEM_LIMIT),
        name="gdn_mixer",
    )(proj3d, proj3d, proj3d, cbuf, s0, convw, alog_r, dtb_r, alog_c, dtb_c, ng)
    return o.reshape(m, GDN_WIDTH), s_new, conv_new


def _shifted_rows(x_ref, prev_ref, carry_ref, c, R, L, G, S):
    width = x_ref.shape[-1]
    groups = range(S)
    if G == 1:
        @pl.when(c == 0)
        def _():
            for s in groups:
                carry_ref[s, 0:SUBLANES, :] = jnp.zeros((SUBLANES, width), F32)
                carry_ref[s, SUBLANES - 1:SUBLANES, :] = prev_ref[s:s + 1, :]

        @pl.when(c > 0)
        def _():
            for s in groups:
                carry_ref[s, 0:SUBLANES, :] = carry_ref[s, R:R + SUBLANES, :]

        xs, prevs = [], []
        for s in groups:
            carry_ref[s, SUBLANES:SUBLANES + R, :] = x_ref[s]
            xs.append(x_ref[s])
            prevs.append(carry_ref[s, SUBLANES - 1:SUBLANES - 1 + R, :])
    else:
        row = lax.broadcasted_iota(jnp.int32, (L, width), 0)
        xs, prevs = [], []
        for s in groups:
            x = x_ref[s]
            xs.append(x)
            for g in range(G):
                xg = x[g * L:(g + 1) * L]
                q = s * G + g
                prevs.append(jnp.where(row == 0, prev_ref[q:q + 1, :], pltpu.roll(xg, 1, 0)))
    cat = lambda parts: parts[0] if len(parts) == 1 else jnp.concatenate(parts, axis=0)
    return cat(xs), cat(prevs)


def _rwkv_kernel(rkv_ref, lora_ref, sh_rkv_ref, sh_lora_ref, s0_ref, mu_rkv_ref, mu_lora_ref,
                 w0_ref, a0_ref, wab_ref, gb_ref, kk_ref, ka_ref, rk_ref, gnw_ref, gnb_ref,
                 o_ref, s_ref, last_rkv_ref, last_lora_ref, xr_ref, xl_ref, m_ref, *, seq_len, n_seq):
    R, L, G, S, sps = _mixer_geometry(seq_len, n_seq)
    RT = S * R
    NQ = S * G
    c = pl.program_id(1)
    W = RWKV_WIDTH
    HD = RWKV_HEAD
    pairs = range(RWKV_HEADS // 2)
    groups = range(S)
    seqs = range(G)

    @pl.when(c == 0)
    def _():
        zero = jnp.zeros((HD, HD), F32)
        for q in range(NQ):
            for j in pairs:
                vk = jnp.concatenate([jnp.concatenate([s0_ref[q, 2 * j], zero], axis=1),
                                      jnp.concatenate([zero, s0_ref[q, 2 * j + 1]], axis=1)], axis=0)
                m_ref[q, j] = vk.T

    p, p_prev = _shifted_rows(rkv_ref, sh_rkv_ref, xr_ref, c, R, L, G, S)
    xs = p + (p_prev - p) * mu_rkv_ref[...]
    pl_, pl_prev = _shifted_rows(lora_ref, sh_lora_ref, xl_ref, c, R, L, G, S)
    xl = pl_ + (pl_prev - pl_) * mu_lora_ref[...]
    r = xs[:, :W]
    k = xs[:, W:2 * W]
    v = xs[:, 2 * W:]

    wa_in = xl[:, :LANES]
    lane = lax.broadcasted_iota(jnp.int32, (RT, LANES), 1)
    wa_in = jnp.where(lane < RWKV_LORA_W, jnp.tanh(wa_in), wa_in)
    wa = _dot(wa_in, wab_ref[...])
    w = -_softplus(-(w0_ref[...] + wa[:, :W])) - 0.5
    lw = -jnp.exp(w)
    a = _sigmoid(a0_ref[...] + wa[:, W:])
    gate = _dot(_sigmoid(xl[:, LANES:]), gb_ref[...])
    kk_raw = k * kk_ref[...]
    k2 = k * (1.0 + (a - 1.0) * ka_ref[...])

    same_t, causal_t, _ = _seq_masks(RT, L)
    lc = _ones_dot(jnp.where(causal_t, 1.0, 0.0).astype(F32), lw)
    if G == 1:
        ltot = jnp.concatenate([jnp.broadcast_to(lc[(s + 1) * R - 1:(s + 1) * R, :], (R, W))
                                for s in groups], axis=0) if S > 1 else \
            jnp.broadcast_to(lc[R - 1:R, :], (R, W))
    else:
        ltot = _ones_dot(jnp.where(same_t, 1.0, 0.0).astype(F32), lw)
    e_inv = jnp.exp(-lc)
    e_rem = jnp.exp(ltot - lc)

    kk = kk_raw * lax.rsqrt(_lane_group_sums(kk_raw * kk_raw, HD) + L2_EPS)
    kka = kk * a
    ct = kk * jnp.exp(lc - lw)
    rt = r * jnp.exp(lc)
    bh = kka * e_inv
    kh = k2 * e_inv
    bb = kka * e_rem
    kb = k2 * e_rem
    p_rows = [ltot[q * L:q * L + 1] for q in range(NQ)]
    p_rows = p_rows + [p_rows[0]] * (-NQ % SUBLANES)
    pt = jnp.exp(jnp.concatenate(p_rows, axis=0)).T

    causal2, strict2, eye2, even_half = _pair_masks(R, L)
    block_diag = ((lax.broadcasted_iota(jnp.int32, (LANES, LANES), 0) < HD)
                  == (lax.broadcasted_iota(jnp.int32, (LANES, LANES), 1) < HD))
    even_all = jnp.bitwise_and(lax.broadcasted_iota(jnp.int32, (R, W), 1), LANES - 1) < HD
    tile = lambda j: slice(j * LANES, (j + 1) * LANES)
    bd = lambda x: _bd(x, even_half)

    if G > 1:
        row2 = lax.broadcasted_iota(jnp.int32, (2 * R, LANES), 0)
        seq_of_row = jnp.right_shift(jnp.bitwise_and(row2, R - 1), L.bit_length() - 1)

    units = [(s, j) for s in groups for j in pairs]
    xps, vps, bdb, bdk, bkts = {}, {}, {}, {}, {}
    for s in groups:
        rs = slice(s * R, (s + 1) * R)
        x_s = jnp.concatenate([ct[rs], rt[rs]], axis=0)
        bt_s = jnp.concatenate([jnp.where(even_all, bh[rs], 0.0), jnp.where(even_all, 0.0, bh[rs])],
                               axis=0).T
        kt_s = jnp.concatenate([jnp.where(even_all, kh[rs], 0.0), jnp.where(even_all, 0.0, kh[rs])],
                               axis=0).T
        bkt_s = jnp.concatenate([bb[rs], kb[rs]], axis=0).T
        for j in pairs:
            xps[(s, j)] = x_s[:, tile(j)]
            vps[(s, j)] = v[rs, tile(j)]
            bdb[(s, j)] = bt_s[tile(j), :]
            bdk[(s, j)] = kt_s[tile(j), :]
            bkts[(s, j)] = bkt_s[tile(j), :]
    gb = {u: _dot(xps[u], bdb[u], mode=P_GRAM) for u in units}
    gk = {u: _dot(xps[u], bdk[u], mode=P_GRAM) for u in units}
    ms = {(s, j): [m_ref[s * G + g, j] for g in seqs] for s, j in units}
    xm_c, xm_r = {}, {}
    for u in units:
        if G == 1:
            xm = _dot(xps[u], ms[u][0], mode=P_STATE)
            xm_c[u], xm_r[u] = xm[:R], xm[R:]
        else:
            parts = [_dot(jnp.concatenate([xps[u][g * L:(g + 1) * L],
                                           xps[u][R + g * L:R + (g + 1) * L]], axis=0),
                          ms[u][g], mode=P_STATE) for g in seqs]
            xm_c[u] = jnp.concatenate([p_[:L] for p_ in parts], axis=0)
            xm_r[u] = jnp.concatenate([p_[L:] for p_ in parts], axis=0)
    ts = _pair_inverses({u: jnp.where(strict2, -gb[u][:R], 0.0) for u in units}, R, L, eye2, even_half)
    bdv = {u: bd(vps[u]) for u in units}
    akvs = {u: _dot(jnp.where(strict2, gk[u][:R], 0.0), bdv[u], mode=P_OUT) for u in units}
    us = {u: _dot(ts[u], bd(-(xm_c[u] + akvs[u])), mode=P_SOLVE) for u in units}
    uvs = {u: jnp.concatenate([us[u], vps[u]], axis=0) for u in units}
    ys = {u: xm_r[u] + _dot(
        jnp.concatenate([jnp.where(causal2, gb[u][R:], 0.0), jnp.where(causal2, gk[u][R:], 0.0)], axis=1),
        jnp.concatenate([bd(us[u]), bdv[u]], axis=0), mode=P_OUT) for u in units}
    for s, j in units:
        u = (s, j)
        for g in seqs:
            q = s * G + g
            uv_g = uvs[u] if G == 1 else jnp.where(seq_of_row == g, uvs[u], 0.0)
            upd = _dot(bkts[u], uv_g, mode=P_STATE)
            m_ref[q, j] = pt[tile(j), q:q + 1] * ms[u][g] + jnp.where(block_diag, upd, 0.0)

    y_rows = [jnp.concatenate([ys[(s, j)] for j in pairs], axis=1) for s in groups]
    y_all = y_rows[0] if S == 1 else jnp.concatenate(y_rows, axis=0)
    mean = _lane_group_sums(y_all, HD) * (1.0 / HD)
    yc = y_all - mean
    var = _lane_group_sums(yc * yc, HD) * (1.0 / HD)
    yn = yc * lax.rsqrt(var + GN_EPS) * gnw_ref[...] + gnb_ref[...]
    yn = yn + _lane_group_sums(r * k2 * rk_ref[...], HD) * v
    o_ref[...] = (yn * gate).reshape(S, R, W).astype(o_ref.dtype)
    for q in range(NQ):
        last = (q // G) * R + (q % G + 1) * L - 1
        last_rkv_ref[q:q + 1, :] = p[last:last + 1, :]
        last_lora_ref[q:q + 1, :] = pl_[last:last + 1, :]

    @pl.when(c == sps - 1)
    def _():
        for q in range(NQ):
            for j in pairs:
                vk = m_ref[q, j].T
                s_ref[q, 2 * j] = vk[:HD, :HD]
                s_ref[q, 2 * j + 1] = vk[HD:, HD:]


def _rwkv_mixer(proj2d, sh_rkv, sh_lora, s0, mu_rkv, mu_lora, w0, a0, wab, gb, kk, ka, rk, gnw, gnb,
                seq_len):
    m = proj2d.shape[0]
    nseq = m // seq_len
    R, L, G, S, sps = _mixer_geometry(seq_len, nseq)
    W = RWKV_WIDTH
    proj3d = proj2d.reshape(nseq // G, sps * R, PROJ_WIDTH)
    const2 = lambda i, c: (0, 0)
    row = lambda width: pl.BlockSpec((1, width), const2)
    rows_map = lambda col: (lambda i, c: (i, c, col))
    per_seq = lambda *dims: pl.BlockSpec((S * G,) + dims, lambda i, c: (i,) + (0,) * len(dims))
    carry = lambda width: pltpu.VMEM((S, SUBLANES + R, width) if G == 1 else (SUBLANES, LANES), F32)
    o, s_new, last_rkv, last_lora = pl.pallas_call(
        functools.partial(_rwkv_kernel, seq_len=seq_len, n_seq=nseq),
        grid=(nseq // (S * G), sps),
        in_specs=[
            pl.BlockSpec((S, R, 3 * W), rows_map(COL_RKV // (3 * W))),
            pl.BlockSpec((S, R, RWKV_LORA), rows_map(COL_LORA // RWKV_LORA)),
            pl.BlockSpec((S * G, 3 * W), lambda i, c: (i, 0)),
            pl.BlockSpec((S * G, RWKV_LORA), lambda i, c: (i, 3 * W // RWKV_LORA)),
            per_seq(RWKV_HEADS, RWKV_HEAD, RWKV_HEAD),
            row(3 * W), row(RWKV_LORA), row(W), row(W),
            pl.BlockSpec((LANES, 2 * W), const2),
            pl.BlockSpec((RWKV_LORA_G, W), const2),
            row(W), row(W), row(W), row(W), row(W),
        ],
        out_specs=[pl.BlockSpec((S, R, W), rows_map(0)), per_seq(RWKV_HEADS, RWKV_HEAD, RWKV_HEAD),
                   per_seq(3 * W), per_seq(RWKV_LORA)],
        out_shape=[
            jax.ShapeDtypeStruct((nseq // G, sps * R, W), BF16),
            jax.ShapeDtypeStruct((nseq, RWKV_HEADS, RWKV_HEAD, RWKV_HEAD), F32),
            jax.ShapeDtypeStruct((nseq, 3 * W), F32),
            jax.ShapeDtypeStruct((nseq, RWKV_LORA), F32),
        ],
        scratch_shapes=[carry(3 * W), carry(RWKV_LORA),
                        pltpu.VMEM((S * G, RWKV_HEADS // 2, LANES, LANES), F32)],
        compiler_params=pltpu.CompilerParams(
            dimension_semantics=("arbitrary", "arbitrary"), vmem_limit_bytes=VMEM_LIMIT),
        name="rwkv_mixer",
    )(proj3d, proj3d, sh_rkv, sh_lora, s0, mu_rkv, mu_lora, w0, a0, wab, gb, kk, ka, rk, gnw, gnb)
    return o.reshape(m, W), s_new, jnp.concatenate([last_rkv, last_lora], axis=-1)


def _outproj_kernel(x_ref, oa_ref, ob_ref, wa_ref, wb_ref, g_ref, x1_ref, hn_ref):
    x1 = (x_ref[...] + jnp.dot(oa_ref[...], wa_ref[...], preferred_element_type=F32)
          + jnp.dot(ob_ref[...], wb_ref[...], preferred_element_type=F32))
    x1_ref[...] = x1
    ms = jnp.mean(x1 * x1, axis=-1, keepdims=True)
    hn_ref[...] = (x1 * lax.rsqrt(ms + RMS_EPS) * g_ref[...]).astype(BF16)


def _outproj(x2d, oa, ob, wo_bf16, g_row, tm):
    m = x2d.shape[0]
    return pl.pallas_call(
        _outproj_kernel,
        grid=(m // tm,),
        in_specs=[
            pl.BlockSpec((tm, D_MODEL), lambda i: (i, 0)),
            pl.BlockSpec((tm, GDN_WIDTH), lambda i: (i, 0)),
            pl.BlockSpec((tm, RWKV_WIDTH), lambda i: (i, 0)),
            pl.BlockSpec((GDN_WIDTH, D_MODEL), lambda i: (0, 0)),
            pl.BlockSpec((RWKV_WIDTH, D_MODEL), lambda i: (1, 0)),
            pl.BlockSpec((1, D_MODEL), lambda i: (0, 0)),
        ],
        out_specs=[pl.BlockSpec((tm, D_MODEL), lambda i: (i, 0)),
                   pl.BlockSpec((tm, D_MODEL), lambda i: (i, 0))],
        out_shape=[jax.ShapeDtypeStruct((m, D_MODEL), F32),
                   jax.ShapeDtypeStruct((m, D_MODEL), BF16)],
        compiler_params=pltpu.CompilerParams(
            dimension_semantics=("arbitrary",), vmem_limit_bytes=VMEM_LIMIT),
        name="outproj",
    )(x2d, oa, ob, wo_bf16, wo_bf16, g_row)


FFN_TF = 512
FFN_NF = D_FF // FFN_TF


def _ffn_pipeline(f, up_fn, wd_ref, x1_ref, fg_ref, y_ref, acc_ref):
    @pl.when(f == 0)
    def _():
        acc_ref[...] = jnp.zeros(acc_ref.shape, F32)

    acc_ref[...] += jnp.dot(up_fn().astype(BF16), wd_ref[...], preferred_element_type=F32)

    @pl.when(f == FFN_NF - 1)
    def _():
        xo = x1_ref[...] + acc_ref[...]
        ms = jnp.mean(xo * xo, axis=-1, keepdims=True)
        y_ref[...] = xo * lax.rsqrt(ms + RMS_EPS) * fg_ref[...]


def _ffn_long_kernel(hn_ref, x1_ref, wg0_ref, wu0_ref, cwg0_ref, cwu0_ref, wd0_ref,
                     wg1_ref, wu1_ref, cwg1_ref, cwu1_ref, wd1_ref, fg_ref,
                     y_ref, n_ref, acc_ref, carry_ref, hbuf_ref, *, tt):
    ti = pl.program_id(1)
    f = pl.program_id(2)
    steps = pl.num_programs(2)
    hn = hn_ref[...]

    def act(slot, chunk, wg_ref, wu_ref, cwg_ref, cwu_ref):
        convs = []
        for j, (w_ref, cw_ref) in enumerate(((wg_ref, cwg_ref), (wu_ref, cwu_ref))):
            h = jnp.dot(hn, w_ref[...], preferred_element_type=F32)
            hbuf_ref[slot, j, SUBLANES:SUBLANES + tt, :] = h
            prev = carry_ref[chunk, j]
            hbuf_ref[slot, j, 0:SUBLANES, :] = jnp.where(ti == 0, jnp.zeros_like(prev), prev)
            cw = cw_ref[...]
            conv = h * cw[FFN_CONV - 1:FFN_CONV, :]
            for i in range(FFN_CONV - 1):
                off = SUBLANES - (FFN_CONV - 1) + i
                conv = conv + hbuf_ref[slot, j, off:off + tt, :] * cw[i:i + 1, :]
            carry_ref[chunk, j] = h[tt - SUBLANES:, :]
            n_ref[j, :, pl.ds(pl.multiple_of(chunk * FFN_TF, FFN_TF), FFN_TF)] = h[tt - (FFN_CONV - 1):, :]
            convs.append(conv)
        return (_silu(convs[0]) * convs[1]).astype(BF16)

    @pl.when(f == 0)
    def _():
        acc_ref[...] = jnp.zeros(acc_ref.shape, F32)

    def both():
        a0 = act(0, 2 * f, wg0_ref, wu0_ref, cwg0_ref, cwu0_ref)
        a1 = act(1, 2 * f + 1, wg1_ref, wu1_ref, cwg1_ref, cwu1_ref)
        acc_ref[...] += (jnp.dot(a0, wd0_ref[...], preferred_element_type=F32)
                         + jnp.dot(a1, wd1_ref[...], preferred_element_type=F32))

    if FFN_NF % 2 == 0:
        both()
    else:
        pl.when(f < steps - 1)(both)

        @pl.when(f == steps - 1)
        def _():
            a0 = act(0, 2 * f, wg0_ref, wu0_ref, cwg0_ref, cwu0_ref)
            acc_ref[...] += jnp.dot(a0, wd0_ref[...], preferred_element_type=F32)

    @pl.when(f == steps - 1)
    def _():
        xo = x1_ref[...] + acc_ref[...]
        ms = jnp.mean(xo * xo, axis=-1, keepdims=True)
        y_ref[...] = xo * lax.rsqrt(ms + RMS_EPS) * fg_ref[...]


def _ffn_long(hn, x1, wup, cw, wdown, fg_row, tt):
    b, t, _ = hn.shape
    tf = FFN_TF
    nf = FFN_NF
    steps = -(-nf // 2)
    chunk = (lambda f: 2 * f, lambda f: jnp.minimum(2 * f + 1, nf - 1))
    weights = []
    for k in range(2):
        weights += [
            pl.BlockSpec((D_MODEL, tf), lambda i, s, f, k=k: (0, chunk[k](f))),
            pl.BlockSpec((D_MODEL, tf), lambda i, s, f, k=k: (0, nf + chunk[k](f))),
            pl.BlockSpec((FFN_CONV, tf), lambda i, s, f, k=k: (0, chunk[k](f))),
            pl.BlockSpec((FFN_CONV, tf), lambda i, s, f, k=k: (0, nf + chunk[k](f))),
            pl.BlockSpec((tf, D_MODEL), lambda i, s, f, k=k: (chunk[k](f), 0)),
        ]
    return pl.pallas_call(
        functools.partial(_ffn_long_kernel, tt=tt),
        grid=(b, t // tt, steps),
        in_specs=[
            pl.BlockSpec((None, tt, D_MODEL), lambda i, s, f: (i, s, 0)),
            pl.BlockSpec((None, tt, D_MODEL), lambda i, s, f: (i, s, 0)),
        ] + weights + [pl.BlockSpec((1, D_MODEL), lambda i, s, f: (0, 0))],
        out_specs=[
            pl.BlockSpec((None, tt, D_MODEL), lambda i, s, f: (i, s, 0)),
            pl.BlockSpec((None, None, 2, FFN_CONV - 1, D_FF), lambda i, s, f: (i, s, 0, 0, 0)),
        ],
        out_shape=[
            jax.ShapeDtypeStruct((b, t, D_MODEL), F32),
            jax.ShapeDtypeStruct((b, t // tt, 2, FFN_CONV - 1, D_FF), F32),
        ],
        scratch_shapes=[
            pltpu.VMEM((tt, D_MODEL), F32),
            pltpu.VMEM((nf, 2, SUBLANES, tf), F32),
            pltpu.VMEM((2, 2, SUBLANES + tt, tf), F32),
        ],
        compiler_params=pltpu.CompilerParams(
            dimension_semantics=("arbitrary", "arbitrary", "arbitrary"),
            vmem_limit_bytes=VMEM_LIMIT_FFN),
        name="ffn_long",
    )(hn, x1, wup, wup, cw, cw, wdown, wup, wup, cw, cw, wdown, fg_row)


def _ffn_short_kernel(hn_ref, x1_ref, wg_ref, wu_ref, cwg_ref, cwu_ref, wd_ref, fg_ref,
                      b0g_ref, b1g_ref, b0u_ref, b1u_ref,
                      y_ref, ng_ref, nu_ref, acc_ref, z_ref, hb_ref, *, tt, seq):
    f = pl.program_id(1)
    nseq = tt // seq

    def up_fn():
        hn = hn_ref[...]
        t_in_seq = lax.broadcasted_iota(jnp.int32, (tt, FFN_TF), 0) % seq
        convs = []
        groups = ((wg_ref, cwg_ref, b0g_ref, b1g_ref, ng_ref), (wu_ref, cwu_ref, b0u_ref, b1u_ref, nu_ref))
        for w_ref, cw_ref, b0_ref, b1_ref, n_ref in groups:
            h = jnp.dot(hn, w_ref[...], preferred_element_type=F32)
            z_ref[...] = jnp.zeros(z_ref.shape, F32)
            for lb in range(FFN_TF // LANES):
                cols = slice(lb * LANES, (lb + 1) * LANES)
                z_ref[lb, pl.ds(0, nseq, stride=seq), :] = b0_ref[:, cols]
                z_ref[lb, pl.ds(1, nseq, stride=seq), :] = b1_ref[:, cols]
                hb_ref[lb] = h[:, cols]
                n_ref[0, :, cols] = hb_ref[lb, pl.ds(seq - 2, nseq, stride=seq), :]
                n_ref[1, :, cols] = hb_ref[lb, pl.ds(seq - 1, nseq, stride=seq), :]
            z = jnp.concatenate([z_ref[lb] for lb in range(FFN_TF // LANES)], axis=1)
            s1 = jnp.where(t_in_seq == 0, pltpu.roll(z, tt - 1, 0), pltpu.roll(h, 1, 0))
            s2 = jnp.where(t_in_seq < 2, z, pltpu.roll(h, 2, 0))
            cw = cw_ref[...]
            convs.append(h * cw[2:3, :] + s1 * cw[1:2, :] + s2 * cw[0:1, :])
        return _silu(convs[0]) * convs[1]

    _ffn_pipeline(f, up_fn, wd_ref, x1_ref, fg_ref, y_ref, acc_ref)


def _ffn_short(hn, x1, wup, cw, wdown, fg_row, hist, tt, seq):
    m = hn.shape[0]
    tf = FFN_TF
    nf = FFN_NF
    nseq = tt // seq
    up = lambda f: f
    down = lambda f: f
    st = lambda k, col0: pl.BlockSpec((None, nseq, tf), lambda i, f: (k, i, col0 + up(f)))
    new = pl.BlockSpec((2, nseq, tf), lambda i, f: (0, i, up(f)))
    new_shape = jax.ShapeDtypeStruct((2, m // seq, D_FF), F32)
    return pl.pallas_call(
        functools.partial(_ffn_short_kernel, tt=tt, seq=seq),
        grid=(m // tt, nf),
        in_specs=[
            pl.BlockSpec((tt, D_MODEL), lambda i, f: (i, 0)),
            pl.BlockSpec((tt, D_MODEL), lambda i, f: (i, 0)),
            pl.BlockSpec((D_MODEL, tf), lambda i, f: (0, up(f))),
            pl.BlockSpec((D_MODEL, tf), lambda i, f: (0, nf + up(f))),
            pl.BlockSpec((FFN_CONV, tf), lambda i, f: (0, up(f))),
            pl.BlockSpec((FFN_CONV, tf), lambda i, f: (0, nf + up(f))),
            pl.BlockSpec((tf, D_MODEL), lambda i, f: (down(f), 0)),
            pl.BlockSpec((1, D_MODEL), lambda i, f: (0, 0)),
            st(0, 0), st(1, 0), st(0, nf), st(1, nf),
        ],
        out_specs=[pl.BlockSpec((tt, D_MODEL), lambda i, f: (i, 0)), new, new],
        out_shape=[jax.ShapeDtypeStruct((m, D_MODEL), F32), new_shape, new_shape],
        scratch_shapes=[
            pltpu.VMEM((tt, D_MODEL), F32),
            pltpu.VMEM((tf // LANES, tt, LANES), F32),
            pltpu.VMEM((tf // LANES, tt, LANES), F32),
        ],
        compiler_params=pltpu.CompilerParams(
            dimension_semantics=("arbitrary", "arbitrary"), vmem_limit_bytes=VMEM_LIMIT),
        name="ffn_short",
    )(hn, x1, wup, wup, cw, cw, wdown, fg_row, hist, hist, hist, hist)


def _pad_lanes(vec, offset):
    out = jnp.zeros((LANES,), F32)
    return out.at[offset:offset + vec.shape[0]].set(vec.astype(F32))


def _trunk(x, s_gdn, s_gconv, s_rwkv, s_shift, s_ffn, prm, *, long_seq):
    b, t, _ = x.shape
    m = b * t
    x2d = x.reshape(m, D_MODEL)
    tm = min(512, m)
    proj = _inproj(x2d, prm["ln1_g"], prm["w_in"], min(1024, m))

    o_a, gdn_new, gconv_new = _gdn_mixer(proj, s_gconv, s_gdn, prm["gdn_conv_w"], prm["alog_r"],
                                         prm["dtb_r"], prm["alog_c"], prm["dtb_c"], prm["gdn_norm_g"], t)
    o_b, rwkv_new, shift_new = _rwkv_mixer(proj, s_shift, s_shift, s_rwkv, prm["mu_rkv"], prm["mu_lora"],
                                           prm["rwkv_w0"], prm["rwkv_a0"], prm["rwkv_wab"],
                                           prm["rwkv_g_b"], prm["rwkv_k_k"], prm["rwkv_k_a"],
                                           prm["rwkv_r_k"], prm["rwkv_gn_w"], prm["rwkv_gn_b"], t)

    x1, hn = _outproj(x2d, o_a, o_b, prm["w_o"], prm["ln2_g"], tm)
    if long_seq:
        tt = min(512, t)
        y, n_gu = _ffn_long(hn.reshape(b, t, D_MODEL), x1.reshape(b, t, D_MODEL), prm["ffn_w_up"],
                            prm["ffn_conv_w"], prm["ffn_w_down"], prm["final_g"], tt)
        ffn_new = jnp.concatenate([n_gu[:, -1, 0], n_gu[:, -1, 1]], axis=-1)
    else:
        tt = min(512, m)
        y, n_g, n_u = _ffn_short(hn, x1, prm["ffn_w_up"], prm["ffn_conv_w"], prm["ffn_w_down"],
                                 prm["final_g"], jnp.swapaxes(s_ffn, 0, 1), tt, t)
        y = y.reshape(b, t, D_MODEL)
        ffn_new = jnp.swapaxes(jnp.concatenate([n_g, n_u], axis=-1), 0, 1)

    return y, gdn_new[None], gconv_new[None], rwkv_new[None], shift_new[None], ffn_new[None]


def kernel(x_prompt, x_sample, state_gdn, state_gdn_conv, state_rwkv, state_rwkv_shift, state_ffn_conv, ln1_g, w_in, gdn_conv_w, gdn_a_log, gdn_dt_bias, gdn_norm_g, rwkv_mu, rwkv_w0, rwkv_w_b, rwkv_a0, rwkv_a_b, rwkv_g_b, rwkv_k_k, rwkv_k_a, rwkv_r_k, rwkv_gn_w, rwkv_gn_b, w_o, ln2_g, ffn_w_up, ffn_conv_w, ffn_w_down, final_g):
    assert ln1_g.shape[0] == 1, "single-layer trunk"
    w_perm = _permute_win(w_in[0])
    mu = rwkv_mu[0]
    zeros_w = jnp.zeros((RWKV_LORA_W, RWKV_WIDTH), F32)
    wab = jnp.concatenate([
        jnp.concatenate([rwkv_w_b[0], zeros_w], axis=1),
        jnp.concatenate([zeros_w, rwkv_a_b[0]], axis=1)], axis=0).astype(BF16)
    alog = _pad_lanes(gdn_a_log[0], GDN_HEADS)
    dtb = _pad_lanes(gdn_dt_bias[0], GDN_HEADS)
    prm = {
        "ln1_g": ln1_g[0][None], "w_in": w_perm, "gdn_conv_w": gdn_conv_w[0],
        "alog_r": alog[None], "dtb_r": dtb[None], "alog_c": alog[:, None], "dtb_c": dtb[:, None],
        "gdn_norm_g": gdn_norm_g[0][None],
        "mu_rkv": mu[None, :3 * RWKV_WIDTH], "mu_lora": mu[None, 3 * RWKV_WIDTH:],
        "rwkv_w0": rwkv_w0[0][None], "rwkv_a0": rwkv_a0[0][None], "rwkv_wab": wab,
        "rwkv_g_b": rwkv_g_b[0].astype(BF16), "rwkv_k_k": rwkv_k_k[0][None],
        "rwkv_k_a": rwkv_k_a[0][None], "rwkv_r_k": rwkv_r_k[0].reshape(1, RWKV_WIDTH),
        "rwkv_gn_w": rwkv_gn_w[0][None], "rwkv_gn_b": rwkv_gn_b[0][None],
        "w_o": w_o[0].astype(BF16), "ln2_g": ln2_g[0][None],
        "ffn_w_up": ffn_w_up[0].astype(BF16), "ffn_conv_w": ffn_conv_w[0],
        "ffn_w_down": ffn_w_down[0].astype(BF16), "final_g": final_g[None],
    }

    bp = x_prompt.shape[0]
    zero_states = (
        jnp.zeros((bp,) + state_gdn.shape[2:], F32),
        jnp.zeros((bp,) + state_gdn_conv.shape[2:], F32),
        jnp.zeros((bp,) + state_rwkv.shape[2:], F32),
        jnp.zeros((bp,) + state_rwkv_shift.shape[2:], F32),
        None,
    )
    outs_p = _trunk(x_prompt, *zero_states, prm, long_seq=True)
    outs_s = _trunk(x_sample, state_gdn[0], state_gdn_conv[0], state_rwkv[0], state_rwkv_shift[0],
                    state_ffn_conv[0], prm, long_seq=False)
    return (outs_p[0], outs_s[0]) + tuple(outs_p[1:]) + tuple(outs_s[1:])
```

```python
import functools

import jax
import jax.numpy as jnp
from jax import lax
from jax.experimental import pallas as pl
from jax.experimental.pallas import tpu as pltpu

F32 = jnp.float32
BF16 = jnp.bfloat16

D_MODEL = 2048
GDN_WIDTH = 1024
GDN_HEADS = 8
GDN_DK = 128
GDN_CONV = 4
RWKV_WIDTH = 1024
RWKV_HEAD = 64
RWKV_HEADS = 16
RWKV_LORA_W = 64
RWKV_LORA_A = 64
RWKV_LORA_G = 128
RWKV_LORA = RWKV_LORA_W + RWKV_LORA_A + RWKV_LORA_G
RWKV_PROJ = 3 * RWKV_WIDTH + RWKV_LORA
D_FF = 5632
FFN_CONV = 3
RMS_EPS = 1e-6
L2_EPS = 1e-12
GN_EPS = 64e-5

REF_OFF_Z = 3 * GDN_WIDTH
REF_OFF_B = 4 * GDN_WIDTH
REF_OFF_RWKV = REF_OFF_B + 2 * GDN_HEADS
REF_IN_WIDTH = REF_OFF_RWKV + RWKV_PROJ

LANES = 128
SUBLANES = 8
COL_QKV = 0
COL_RKV = 3 * GDN_WIDTH
COL_Z = COL_RKV + 3 * RWKV_WIDTH
COL_LORA = COL_Z + GDN_WIDTH
COL_BA = COL_LORA + RWKV_LORA
PROJ_WIDTH = 7680
PROJ_TN = 1536

MIX_ROWS = 64
MIX_GROUPS_LONG = 4
MIX_GROUPS_SHORT = 2

NN = (((1,), (0,)), ((), ()))
NT = (((1,), (1,)), ((), ()))
TN = (((0,), (0,)), ((), ()))

VMEM_LIMIT = 56 * 1024 * 1024
VMEM_LIMIT_FFN = 62 * 1024 * 1024

P_GRAM = "x1"
P_INV = "x1"
P_SOLVE = "x1"
P_STATE = "x1"
P_OUT = "x1"


def _split(x):
    hi = x.astype(BF16)
    return hi, (x - hi.astype(F32)).astype(BF16)


def _dot(a, b, dims=NN, mode="x1"):
    if mode == "hi":
        return lax.dot_general(a, b, dims, precision=lax.Precision.HIGHEST,
                               preferred_element_type=F32)
    if mode == "x3":
        a_hi, a_lo = _split(a)
        b_hi, b_lo = _split(b)
        d = lambda u, v: lax.dot_general(u, v, dims, preferred_element_type=F32)
        return d(a_hi, b_hi) + (d(a_hi, b_lo) + d(a_lo, b_hi))
    return lax.dot_general(a.astype(BF16), b.astype(BF16), dims, preferred_element_type=F32)


def _ones_dot(ones_mat, x, dims=NN):
    x1, x2 = _split(x)
    m = ones_mat.astype(BF16)
    if dims == NN:
        d = lambda v: lax.dot_general(m, v, dims, preferred_element_type=F32)
    else:
        d = lambda v: lax.dot_general(v, m, dims, preferred_element_type=F32)
    return d(x1) + d(x2)


def _sigmoid(x):
    return 1.0 / (1.0 + jnp.exp(-x))


def _silu(x):
    return x * _sigmoid(x)


def _softplus(x):
    return jnp.maximum(x, 0.0) + jnp.log(1.0 + jnp.exp(-jnp.abs(x)))


def _seq_masks(rows, seq_len):
    r = lax.broadcasted_iota(jnp.int32, (rows, rows), 0)
    c = lax.broadcasted_iota(jnp.int32, (rows, rows), 1)
    if seq_len >= rows:
        return None, r >= c, r > c
    shift = seq_len.bit_length() - 1
    assert 1 << shift == seq_len
    same = jnp.right_shift(r, shift) == jnp.right_shift(c, shift)
    return same, same & (r >= c), same & (r > c)


def _wide_masks(rows, seq_len):
    r = lax.broadcasted_iota(jnp.int32, (rows, 2 * rows), 0)
    c = lax.broadcasted_iota(jnp.int32, (rows, 2 * rows), 1)
    right = c >= rows
    cc = jnp.where(right, c - rows, c)
    if seq_len >= rows:
        return r >= cc, right & (r > cc)
    shift = seq_len.bit_length() - 1
    same = jnp.right_shift(r, shift) == jnp.right_shift(cc, shift)
    return same & (r >= cc), same & right & (r > cc)


def _pair_masks(rows, seq_len):
    half = LANES // 2
    assert rows == half
    lane = lax.broadcasted_iota(jnp.int32, (rows, LANES), 1)
    row = lax.broadcasted_iota(jnp.int32, (rows, LANES), 0)
    col = jnp.bitwise_and(lane, half - 1)
    if seq_len >= rows:
        causal, strict = row >= col, row > col
    else:
        shift = seq_len.bit_length() - 1
        same = jnp.right_shift(row, shift) == jnp.right_shift(col, shift)
        causal, strict = same & (row >= col), same & (row > col)
    return causal, strict, jnp.where(row == col, 1.0, 0.0).astype(F32), lane < half


def _bd(x, first_half):
    return jnp.concatenate([jnp.where(first_half, x, 0.0), jnp.where(first_half, 0.0, x)], axis=0)


def _pair_inverses(neg_a, rows, nilpotency, eye2, first_half):
    qs = dict(neg_a)
    ts = {u: eye2 + q for u, q in qs.items()}
    n = 2
    if n < nilpotency:
        qs = {u: _dot(q, _bd(q, first_half), mode=P_INV) for u, q in qs.items()}
    while n < nilpotency:
        if 2 * n < nilpotency:
            tq = {u: _dot(jnp.concatenate([ts[u], qs[u]], axis=0), _bd(qs[u], first_half), mode=P_INV)
                  for u in qs}
            ts = {u: ts[u] + tq[u][:rows] for u in qs}
            qs = {u: tq[u][rows:] for u in qs}
        else:
            ts = {u: ts[u] + _dot(ts[u], _bd(qs[u], first_half), mode=P_INV) for u in qs}
        n *= 2
    return ts


def _lane_group_sums(x, group):
    tile = 2 * LANES
    shift = group.bit_length() - 1
    li = jnp.right_shift(lax.broadcasted_iota(jnp.int32, (tile, tile), 0), shift)
    lj = jnp.right_shift(lax.broadcasted_iota(jnp.int32, (tile, tile), 1), shift)
    ones = jnp.where(li == lj, 1.0, 0.0).astype(BF16)
    hi, lo = _split(x)
    d = lambda u: lax.dot_general(u, ones, NN, preferred_element_type=F32)
    return jnp.concatenate([d(hi[:, t:t + tile]) + d(lo[:, t:t + tile])
                            for t in range(0, x.shape[1], tile)], axis=1)


def _unit_lower_inverses(mats, rows, nilpotency):
    r = lax.broadcasted_iota(jnp.int32, (rows, rows), 0)
    c = lax.broadcasted_iota(jnp.int32, (rows, rows), 1)
    eye = jnp.where(r == c, 1.0, 0.0).astype(F32)
    qs = [-a for a in mats]
    ts = [eye + q for q in qs]
    n = 2
    if n < nilpotency:
        qs = [_dot(q, q, mode=P_INV) for q in qs]
    while n < nilpotency:
        if 2 * n < nilpotency:
            tq = [_dot(jnp.concatenate([t, q], axis=0), q, mode=P_INV) for t, q in zip(ts, qs)]
            ts = [t + p[:rows] for t, p in zip(ts, tq)]
            qs = [p[rows:] for p in tq]
        else:
            ts = [t + _dot(t, q, mode=P_INV) for t, q in zip(ts, qs)]
        n *= 2
    return ts


def _permute_win_kernel(w_ref, o_ref):
    rw = REF_OFF_RWKV
    rows = w_ref.shape[0]
    cast = lambda lo, hi: w_ref[:, lo:hi].astype(BF16)
    o_ref[:, COL_QKV:COL_RKV] = cast(0, REF_OFF_Z)
    o_ref[:, COL_RKV:COL_Z] = cast(rw, rw + 3 * RWKV_WIDTH)
    o_ref[:, COL_Z:COL_LORA] = cast(REF_OFF_Z, REF_OFF_B)
    o_ref[:, COL_LORA:COL_BA] = cast(rw + 3 * RWKV_WIDTH, REF_IN_WIDTH)
    tail = jnp.concatenate([w_ref[:, REF_OFF_B:REF_OFF_RWKV],
                            jnp.zeros((rows, PROJ_WIDTH - COL_BA - 2 * GDN_HEADS), F32)], axis=1)
    o_ref[:, COL_BA:] = tail.astype(BF16)


def _permute_win(w, tr=256):
    return pl.pallas_call(
        _permute_win_kernel,
        grid=(D_MODEL // tr,),
        in_specs=[pl.BlockSpec((tr, REF_IN_WIDTH), lambda i: (i, 0))],
        out_specs=pl.BlockSpec((tr, PROJ_WIDTH), lambda i: (i, 0)),
        out_shape=jax.ShapeDtypeStruct((D_MODEL, PROJ_WIDTH), BF16),
        compiler_params=pltpu.CompilerParams(
            dimension_semantics=("arbitrary",), vmem_limit_bytes=VMEM_LIMIT),
        name="permute_win",
    )(w)


def _inproj_kernel(x_ref, g_ref, w_ref, o_ref, xn_ref):
    @pl.when(pl.program_id(1) == 0)
    def _():
        x = x_ref[...]
        ms = jnp.mean(x * x, axis=-1, keepdims=True)
        xn_ref[...] = (x * lax.rsqrt(ms + RMS_EPS) * g_ref[...]).astype(BF16)

    o_ref[...] = jnp.dot(xn_ref[...], w_ref[...], preferred_element_type=F32)


def _inproj(x2d, g_row, w_bf16, tm):
    m = x2d.shape[0]
    return pl.pallas_call(
        _inproj_kernel,
        grid=(m // tm, PROJ_WIDTH // PROJ_TN),
        in_specs=[
            pl.BlockSpec((tm, D_MODEL), lambda i, j: (i, 0)),
            pl.BlockSpec((1, D_MODEL), lambda i, j: (0, 0)),
            pl.BlockSpec((D_MODEL, PROJ_TN), lambda i, j: (0, j)),
        ],
        out_specs=pl.BlockSpec((tm, PROJ_TN), lambda i, j: (i, j)),
        out_shape=jax.ShapeDtypeStruct((m, PROJ_WIDTH), F32),
        scratch_shapes=[pltpu.VMEM((tm, D_MODEL), BF16)],
        compiler_params=pltpu.CompilerParams(
            dimension_semantics=("arbitrary", "arbitrary"), vmem_limit_bytes=VMEM_LIMIT),
        name="inproj",
    )(x2d, g_row, w_bf16)


def _mixer_geometry(seq_len, n_seq, short_groups=1):
    rows = MIX_ROWS
    length = min(seq_len, rows)
    assert rows % length == 0 and seq_len % length == 0 and length % SUBLANES == 0
    per_group = rows // length
    groups = MIX_GROUPS_LONG if per_group == 1 else short_groups
    while n_seq % (groups * per_group):
        groups //= 2
    assert groups >= 1
    return rows, length, per_group, groups, seq_len // length


def _gdn_kernel(qkv_ref, z_ref, ba_ref, cbuf_ref, s0_ref, convw_ref, alog_r_ref, dtb_r_ref,
                alog_c_ref, dtb_c_ref, ng_ref, o_ref, s_ref, tail_ref, xp_ref, *, seq_len, n_seq):
    R, L, G, S, _ = _mixer_geometry(seq_len, n_seq, MIX_GROUPS_SHORT)
    RT = S * R
    c = pl.program_id(1)
    width = 3 * GDN_WIDTH
    hist = GDN_CONV - 1
    cw = convw_ref[...]
    groups = range(S)
    seqs = range(G)

    @pl.when(c == 0)
    def _():
        s_ref[...] = s0_ref[...]

    pieces = []
    if G == 1:
        @pl.when(c == 0)
        def _():
            for s in groups:
                xp_ref[s, 0:SUBLANES, :] = jnp.zeros((SUBLANES, width), F32)
                xp_ref[s, SUBLANES - hist:SUBLANES, :] = cbuf_ref[s]

        @pl.when(c > 0)
        def _():
            for s in groups:
                xp_ref[s, 0:SUBLANES, :] = xp_ref[s, R:R + SUBLANES, :]

        for s in groups:
            xp_ref[s, SUBLANES:SUBLANES + R, :] = qkv_ref[s]
            piece = qkv_ref[s] * cw[hist:hist + 1, :]
            for i in range(hist):
                off = SUBLANES - hist + i
                piece = piece + xp_ref[s, off:off + R, :] * cw[i:i + 1, :]
            pieces.append(piece)
    else:
        for s in groups:
            for g in seqs:
                q = s * G + g
                rows = slice(g * L, (g + 1) * L)
                xp_ref[q, SUBLANES - hist:SUBLANES, :] = cbuf_ref[q]
                xp_ref[q, SUBLANES:SUBLANES + L, :] = qkv_ref[s, rows, :]
                piece = qkv_ref[s, rows, :] * cw[hist:hist + 1, :]
                for i in range(hist):
                    off = SUBLANES - hist + i
                    piece = piece + xp_ref[q, off:off + L, :] * cw[i:i + 1, :]
                pieces.append(piece)
    qkv = _silu(pieces[0] if len(pieces) == 1 else jnp.concatenate(pieces, axis=0))

    same_t, causal_t, _ = _seq_masks(RT, L)
    causal01 = jnp.where(causal_t, 1.0, 0.0).astype(F32)

    ba = ba_ref[...].reshape(RT, LANES)
    ba_t = ba.T
    beta_c = _sigmoid(ba)
    g_c = -jnp.exp(alog_r_ref[...]) * _softplus(ba + dtb_r_ref[...])
    g_r = -jnp.exp(alog_c_ref[...]) * _softplus(ba_t + dtb_c_ref[...])
    gc_all = _ones_dot(causal01, g_c)
    gr_all = _ones_dot(causal01, g_r, NT)
    if same_t is None:
        gtot_all = jnp.broadcast_to(gc_all[RT - 1:RT, :], (RT, LANES))
    else:
        gtot_all = _ones_dot(jnp.where(same_t, 1.0, 0.0).astype(F32), g_c)

    qk_raw = qkv[:, :2 * GDN_WIDTH]
    qk_n = qk_raw * lax.rsqrt(_lane_group_sums(qk_raw * qk_raw, GDN_DK) + L2_EPS)
    q_all = qk_n[:, :GDN_WIDTH] * (GDN_DK ** -0.5)
    k_all = qk_n[:, GDN_WIDTH:]
    v_all = qkv[:, 2 * GDN_WIDTH:]

    causal2, strict2, eye2, first_half = _pair_masks(R, L)
    first_head = lax.broadcasted_iota(jnp.int32, (R, 2 * GDN_DK), 1) < GDN_DK
    zeros_u = jnp.zeros((R, GDN_DK), F32)
    zeros_rhs = jnp.zeros((R, 2 * GDN_DK), F32)
    chains = [(s, h) for s in groups for h in range(GDN_HEADS)]
    units = [(s, p) for s in groups for p in range(GDN_HEADS // 2)]
    seq_rows = [slice(g * L, (g + 1) * L) for g in seqs]
    qs, ks, vs, betas, gcols, gtots = {}, {}, {}, {}, {}, {}
    for s, h in chains:
        rs = slice(s * R, (s + 1) * R)
        lo = h * GDN_DK
        key = (s, h)
        qs[key] = q_all[rs, lo:lo + GDN_DK]
        ks[key] = k_all[rs, lo:lo + GDN_DK]
        vs[key] = v_all[rs, lo:lo + GDN_DK]
        betas[key] = beta_c[rs, h:h + 1]
        gcols[key] = gc_all[rs, GDN_HEADS + h:GDN_HEADS + h + 1]
        gtots[key] = gtot_all[rs, GDN_HEADS + h:GDN_HEADS + h + 1]
    kq2, decay2, a2 = {}, {}, {}
    for s, p in units:
        rs = slice(s * R, (s + 1) * R)
        cols = slice(2 * p * GDN_DK, (2 * p + 2) * GDN_DK)
        k_pair = k_all[rs, cols]
        kt_bd = jnp.concatenate([jnp.where(first_head, k_pair, 0.0),
                                 jnp.where(first_head, 0.0, k_pair)], axis=0).T
        kq = _dot(jnp.concatenate([k_pair, q_all[rs, cols]], axis=0), kt_bd, mode=P_GRAM)
        h0, h1 = (s, 2 * p), (s, 2 * p + 1)
        gcol2 = jnp.where(first_half, gcols[h0], gcols[h1])
        grow2 = jnp.concatenate([gr_all[GDN_HEADS + 2 * p:GDN_HEADS + 2 * p + 1, rs],
                                 gr_all[GDN_HEADS + 2 * p + 1:GDN_HEADS + 2 * p + 2, rs]], axis=1)
        dec = jnp.where(causal2, jnp.exp(jnp.where(causal2, gcol2 - grow2, 0.0)), 0.0)
        beta2 = jnp.where(first_half, betas[h0], betas[h1])
        kq2[(s, p)] = kq
        decay2[(s, p)] = dec
        a2[(s, p)] = jnp.where(strict2, -(beta2 * kq[:R] * dec), 0.0)
    t2 = _pair_inverses(a2, R, L, eye2, first_half)
    gammas = {key: jnp.exp(gcols[key]) for key in chains}

    def stacked(key, x, zeros):
        return jnp.concatenate([x, zeros] if key[1] % 2 == 0 else [zeros, x], axis=0)

    sols = {key: _dot(t2[(key[0], key[1] // 2)],
                      stacked(key, jnp.concatenate([(betas[key] * gammas[key]) * ks[key],
                                                    betas[key] * vs[key]], axis=1), zeros_rhs),
                      mode=P_SOLVE) for key in chains}
    states = {(s, h): [s_ref[s * G + g, h] for g in seqs] for s, h in chains}
    wss = {key: [_dot(jnp.concatenate([sols[key][rows, :GDN_DK], (qs[key] * gammas[key])[rows]], axis=0),
                      states[key][g], mode=P_STATE) for g, rows in enumerate(seq_rows)]
           for key in chains}
    us = {key: jnp.concatenate([sols[key][rows, GDN_DK:] - wss[key][g][:L]
                                for g, rows in enumerate(seq_rows)], axis=0) for key in chains}
    qk2 = {u: kq2[u][R:] * decay2[u] for u in units}
    outs = {key: jnp.concatenate([wss[key][g][L:] for g in seqs], axis=0)
            + _dot(qk2[(key[0], key[1] // 2)], stacked(key, us[key], zeros_u), mode=P_OUT)
            for key in chains}
    for s, h in chains:
        key = (s, h)
        kt = ks[key] * jnp.exp(gtots[key] - gcols[key])
        for g, rows in enumerate(seq_rows):
            gl = jnp.exp(gtots[key][g * L:g * L + 1, :])
            s_ref[s * G + g, h] = gl * states[key][g] + _dot(kt[rows], us[key][rows], TN, mode=P_STATE)
    ng = jnp.concatenate([ng_ref[...]] * GDN_HEADS, axis=1)
    for s in groups:
        o = jnp.concatenate([outs[(s, h)] for h in range(GDN_HEADS)], axis=1)
        ms = _lane_group_sums(o * o, GDN_DK) * (1.0 / GDN_DK)
        o = o * lax.rsqrt(ms + RMS_EPS) * ng
        o_ref[s] = (o * _silu(z_ref[s])).astype(o_ref.dtype)
    for s in groups:
        for g in seqs:
            tail_ref[s * G + g] = qkv_ref[s, (g + 1) * L - hist:(g + 1) * L, :]


def _gdn_mixer(proj2d, cbuf, s0, convw, alog_r, dtb_r, alog_c, dtb_c, ng, seq_len):
    m = proj2d.shape[0]
    nseq = m // seq_len
    R, L, G, S, sps = _mixer_geometry(seq_len, nseq, MIX_GROUPS_SHORT)
    width = 3 * GDN_WIDTH
    proj3d = proj2d.reshape(nseq // G, sps * R, PROJ_WIDTH)
    const2 = lambda i, c: (0, 0)
    rows_map = lambda col: (lambda i, c: (i, c, col))
    per_seq = lambda *dims: pl.BlockSpec((S * G,) + dims, lambda i, c: (i,) + (0,) * len(dims))
    xp_shape = (S, SUBLANES + R, width) if G == 1 else (S * G, SUBLANES + L, width)
    o, s_new, conv_new = pl.pallas_call(
        functools.partial(_gdn_kernel, seq_len=seq_len, n_seq=nseq),
        grid=(nseq // (S * G), sps),
        in_specs=[
            pl.BlockSpec((S, R, width), rows_map(COL_QKV // width)),
            pl.BlockSpec((S, R, GDN_WIDTH), rows_map(COL_Z // GDN_WIDTH)),
            pl.BlockSpec((S, R, LANES), rows_map(COL_BA // LANES)),
            per_seq(GDN_CONV - 1, width),
            per_seq(GDN_HEADS, GDN_DK, GDN_DK),
            pl.BlockSpec((GDN_CONV, width), const2),
            pl.BlockSpec((1, LANES), const2),
            pl.BlockSpec((1, LANES), const2),
            pl.BlockSpec((LANES, 1), const2),
            pl.BlockSpec((LANES, 1), const2),
            pl.BlockSpec((1, GDN_DK), const2),
        ],
        out_specs=[
            pl.BlockSpec((S, R, GDN_WIDTH), rows_map(0)),
            per_seq(GDN_HEADS, GDN_DK, GDN_DK),
            per_seq(GDN_CONV - 1, width),
        ],
        out_shape=[
            jax.ShapeDtypeStruct((nseq // G, sps * R, GDN_WIDTH), BF16),
            jax.ShapeDtypeStruct((nseq, GDN_HEADS, GDN_DK, GDN_DK), F32),
            jax.ShapeDtypeStruct((nseq, GDN_CONV - 1, width), F32),
        ],
        scratch_shapes=[pltpu.VMEM(xp_shape, F32)],
        compiler_params=pltpu.CompilerParams(
            dimension_semantics=("arbitrary", "arbitrary"), vmem_limit_bytes=VMEM_LIMIT),
        name="gdn_mixer",
    )(proj3d, proj3d, proj3d, cbuf, s0, convw, alog_r, dtb_r, alog_c, dtb_c, ng)
    return o.reshape(m, GDN_WIDTH), s_new, conv_new


def _shifted_rows(x_ref, prev_ref, carry_ref, c, R, L, G, S):
    width = x_ref.shape[-1]
    groups = range(S)
    if G == 1:
        @pl.when(c == 0)
        def _():
            for s in groups:
                carry_ref[s, 0:SUBLANES, :] = jnp.zeros((SUBLANES, width), F32)
                carry_ref[s, SUBLANES - 1:SUBLANES, :] = prev_ref[s:s + 1, :]

        @pl.when(c > 0)
        def _():
            for s in groups:
                carry_ref[s, 0:SUBLANES, :] = carry_ref[s, R:R + SUBLANES, :]

        xs, prevs = [], []
        for s in groups:
            carry_ref[s, SUBLANES:SUBLANES + R, :] = x_ref[s]
            xs.append(x_ref[s])
            prevs.append(carry_ref[s, SUBLANES - 1:SUBLANES - 1 + R, :])
    else:
        row = lax.broadcasted_iota(jnp.int32, (L, width), 0)
        xs, prevs = [], []
        for s in groups:
            x = x_ref[s]
            xs.append(x)
            for g in range(G):
                xg = x[g * L:(g + 1) * L]
                q = s * G + g
                prevs.append(jnp.where(row == 0, prev_ref[q:q + 1, :], pltpu.roll(xg, 1, 0)))
    cat = lambda parts: parts[0] if len(parts) == 1 else jnp.concatenate(parts, axis=0)
    return cat(xs), cat(prevs)


def _rwkv_kernel(rkv_ref, lora_ref, sh_rkv_ref, sh_lora_ref, s0_ref, mu_rkv_ref, mu_lora_ref,
                 w0_ref, a0_ref, wab_ref, gb_ref, kk_ref, ka_ref, rk_ref, gnw_ref, gnb_ref,
                 o_ref, s_ref, last_rkv_ref, last_lora_ref, xr_ref, xl_ref, m_ref, *, seq_len, n_seq):
    R, L, G, S, sps = _mixer_geometry(seq_len, n_seq)
    RT = S * R
    NQ = S * G
    c = pl.program_id(1)
    W = RWKV_WIDTH
    HD = RWKV_HEAD
    pairs = range(RWKV_HEADS // 2)
    groups = range(S)
    seqs = range(G)

    @pl.when(c == 0)
    def _():
        zero = jnp.zeros((HD, HD), F32)
        for q in range(NQ):
            for j in pairs:
                vk = jnp.concatenate([jnp.concatenate([s0_ref[q, 2 * j], zero], axis=1),
                                      jnp.concatenate([zero, s0_ref[q, 2 * j + 1]], axis=1)], axis=0)
                m_ref[q, j] = vk.T

    p, p_prev = _shifted_rows(rkv_ref, sh_rkv_ref, xr_ref, c, R, L, G, S)
    xs = p + (p_prev - p) * mu_rkv_ref[...]
    pl_, pl_prev = _shifted_rows(lora_ref, sh_lora_ref, xl_ref, c, R, L, G, S)
    xl = pl_ + (pl_prev - pl_) * mu_lora_ref[...]
    r = xs[:, :W]
    k = xs[:, W:2 * W]
    v = xs[:, 2 * W:]

    wa_in = xl[:, :LANES]
    lane = lax.broadcasted_iota(jnp.int32, (RT, LANES), 1)
    wa_in = jnp.where(lane < RWKV_LORA_W, jnp.tanh(wa_in), wa_in)
    wa = _dot(wa_in, wab_ref[...])
    w = -_softplus(-(w0_ref[...] + wa[:, :W])) - 0.5
    lw = -jnp.exp(w)
    a = _sigmoid(a0_ref[...] + wa[:, W:])
    gate = _dot(_sigmoid(xl[:, LANES:]), gb_ref[...])
    kk_raw = k * kk_ref[...]
    k2 = k * (1.0 + (a - 1.0) * ka_ref[...])

    same_t, causal_t, _ = _seq_masks(RT, L)
    lc = _ones_dot(jnp.where(causal_t, 1.0, 0.0).astype(F32), lw)
    if G == 1:
        ltot = jnp.concatenate([jnp.broadcast_to(lc[(s + 1) * R - 1:(s + 1) * R, :], (R, W))
                                for s in groups], axis=0) if S > 1 else \
            jnp.broadcast_to(lc[R - 1:R, :], (R, W))
    else:
        ltot = _ones_dot(jnp.where(same_t, 1.0, 0.0).astype(F32), lw)
    e_inv = jnp.exp(-lc)
    e_rem = jnp.exp(ltot - lc)

    kk = kk_raw * lax.rsqrt(_lane_group_sums(kk_raw * kk_raw, HD) + L2_EPS)
    kka = kk * a
    ct = kk * jnp.exp(lc - lw)
    rt = r * jnp.exp(lc)
    bh = kka * e_inv
    kh = k2 * e_inv
    bb = kka * e_rem
    kb = k2 * e_rem
    p_rows = [ltot[q * L:q * L + 1] for q in range(NQ)]
    p_rows = p_rows + [p_rows[0]] * (-NQ % SUBLANES)
    pt = jnp.exp(jnp.concatenate(p_rows, axis=0)).T

    causal2, strict2, eye2, even_half = _pair_masks(R, L)
    block_diag = ((lax.broadcasted_iota(jnp.int32, (LANES, LANES), 0) < HD)
                  == (lax.broadcasted_iota(jnp.int32, (LANES, LANES), 1) < HD))
    even_all = jnp.bitwise_and(lax.broadcasted_iota(jnp.int32, (R, W), 1), LANES - 1) < HD
    tile = lambda j: slice(j * LANES, (j + 1) * LANES)
    bd = lambda x: _bd(x, even_half)

    if G > 1:
        row2 = lax.broadcasted_iota(jnp.int32, (2 * R, LANES), 0)
        seq_of_row = jnp.right_shift(jnp.bitwise_and(row2, R - 1), L.bit_length() - 1)

    units = [(s, j) for s in groups for j in pairs]
    xps, vps, bdb, bdk, bkts = {}, {}, {}, {}, {}
    for s in groups:
        rs = slice(s * R, (s + 1) * R)
        x_s = jnp.concatenate([ct[rs], rt[rs]], axis=0)
        bt_s = jnp.concatenate([jnp.where(even_all, bh[rs], 0.0), jnp.where(even_all, 0.0, bh[rs])],
                               axis=0).T
        kt_s = jnp.concatenate([jnp.where(even_all, kh[rs], 0.0), jnp.where(even_all, 0.0, kh[rs])],
                               axis=0).T
        bkt_s = jnp.concatenate([bb[rs], kb[rs]], axis=0).T
        for j in pairs:
            xps[(s, j)] = x_s[:, tile(j)]
            vps[(s, j)] = v[rs, tile(j)]
            bdb[(s, j)] = bt_s[tile(j), :]
            bdk[(s, j)] = kt_s[tile(j), :]
            bkts[(s, j)] = bkt_s[tile(j), :]
    gb = {u: _dot(xps[u], bdb[u], mode=P_GRAM) for u in units}
    gk = {u: _dot(xps[u], bdk[u], mode=P_GRAM) for u in units}
    ms = {(s, j): [m_ref[s * G + g, j] for g in seqs] for s, j in units}
    xm_c, xm_r = {}, {}
    for u in units:
        if G == 1:
            xm = _dot(xps[u], ms[u][0], mode=P_STATE)
            xm_c[u], xm_r[u] = xm[:R], xm[R:]
        else:
            parts = [_dot(jnp.concatenate([xps[u][g * L:(g + 1) * L],
                                           xps[u][R + g * L:R + (g + 1) * L]], axis=0),
                          ms[u][g], mode=P_STATE) for g in seqs]
            xm_c[u] = jnp.concatenate([p_[:L] for p_ in parts], axis=0)
            xm_r[u] = jnp.concatenate([p_[L:] for p_ in parts], axis=0)
    ts = _pair_inverses({u: jnp.where(strict2, -gb[u][:R], 0.0) for u in units}, R, L, eye2, even_half)
    bdv = {u: bd(vps[u]) for u in units}
    akvs = {u: _dot(jnp.where(strict2, gk[u][:R], 0.0), bdv[u], mode=P_OUT) for u in units}
    us = {u: _dot(ts[u], bd(-(xm_c[u] + akvs[u])), mode=P_SOLVE) for u in units}
    uvs = {u: jnp.concatenate([us[u], vps[u]], axis=0) for u in units}
    ys = {u: xm_r[u] + _dot(
        jnp.concatenate([jnp.where(causal2, gb[u][R:], 0.0), jnp.where(causal2, gk[u][R:], 0.0)], axis=1),
        jnp.concatenate([bd(us[u]), bdv[u]], axis=0), mode=P_OUT) for u in units}
    for s, j in units:
        u = (s, j)
        for g in seqs:
            q = s * G + g
            uv_g = uvs[u] if G == 1 else jnp.where(seq_of_row == g, uvs[u], 0.0)
            upd = _dot(bkts[u], uv_g, mode=P_STATE)
            m_ref[q, j] = pt[tile(j), q:q + 1] * ms[u][g] + jnp.where(block_diag, upd, 0.0)

    y_rows = [jnp.concatenate([ys[(s, j)] for j in pairs], axis=1) for s in groups]
    y_all = y_rows[0] if S == 1 else jnp.concatenate(y_rows, axis=0)
    mean = _lane_group_sums(y_all, HD) * (1.0 / HD)
    yc = y_all - mean
    var = _lane_group_sums(yc * yc, HD) * (1.0 / HD)
    yn = yc * lax.rsqrt(var + GN_EPS) * gnw_ref[...] + gnb_ref[...]
    yn = yn + _lane_group_sums(r * k2 * rk_ref[...], HD) * v
    o_ref[...] = (yn * gate).reshape(S, R, W).astype(o_ref.dtype)
    for q in range(NQ):
        last = (q // G) * R + (q % G + 1) * L - 1
        last_rkv_ref[q:q + 1, :] = p[last:last + 1, :]
        last_lora_ref[q:q + 1, :] = pl_[last:last + 1, :]

    @pl.when(c == sps - 1)
    def _():
        for q in range(NQ):
            for j in pairs:
                vk = m_ref[q, j].T
                s_ref[q, 2 * j] = vk[:HD, :HD]
                s_ref[q, 2 * j + 1] = vk[HD:, HD:]


def _rwkv_mixer(proj2d, sh_rkv, sh_lora, s0, mu_rkv, mu_lora, w0, a0, wab, gb, kk, ka, rk, gnw, gnb,
                seq_len):
    m = proj2d.shape[0]
    nseq = m // seq_len
    R, L, G, S, sps = _mixer_geometry(seq_len, nseq)
    W = RWKV_WIDTH
    proj3d = proj2d.reshape(nseq // G, sps * R, PROJ_WIDTH)
    const2 = lambda i, c: (0, 0)
    row = lambda width: pl.BlockSpec((1, width), const2)
    rows_map = lambda col: (lambda i, c: (i, c, col))
    per_seq = lambda *dims: pl.BlockSpec((S * G,) + dims, lambda i, c: (i,) + (0,) * len(dims))
    carry = lambda width: pltpu.VMEM((S, SUBLANES + R, width) if G == 1 else (SUBLANES, LANES), F32)
    o, s_new, last_rkv, last_lora = pl.pallas_call(
        functools.partial(_rwkv_kernel, seq_len=seq_len, n_seq=nseq),
        grid=(nseq // (S * G), sps),
        in_specs=[
            pl.BlockSpec((S, R, 3 * W), rows_map(COL_RKV // (3 * W))),
            pl.BlockSpec((S, R, RWKV_LORA), rows_map(COL_LORA // RWKV_LORA)),
            pl.BlockSpec((S * G, 3 * W), lambda i, c: (i, 0)),
            pl.BlockSpec((S * G, RWKV_LORA), lambda i, c: (i, 3 * W // RWKV_LORA)),
            per_seq(RWKV_HEADS, RWKV_HEAD, RWKV_HEAD),
            row(3 * W), row(RWKV_LORA), row(W), row(W),
            pl.BlockSpec((LANES, 2 * W), const2),
            pl.BlockSpec((RWKV_LORA_G, W), const2),
            row(W), row(W), row(W), row(W), row(W),
        ],
        out_specs=[pl.BlockSpec((S, R, W), rows_map(0)), per_seq(RWKV_HEADS, RWKV_HEAD, RWKV_HEAD),
                   per_seq(3 * W), per_seq(RWKV_LORA)],
        out_shape=[
            jax.ShapeDtypeStruct((nseq // G, sps * R, W), BF16),
            jax.ShapeDtypeStruct((nseq, RWKV_HEADS, RWKV_HEAD, RWKV_HEAD), F32),
            jax.ShapeDtypeStruct((nseq, 3 * W), F32),
            jax.ShapeDtypeStruct((nseq, RWKV_LORA), F32),
        ],
        scratch_shapes=[carry(3 * W), carry(RWKV_LORA),
                        pltpu.VMEM((S * G, RWKV_HEADS // 2, LANES, LANES), F32)],
        compiler_params=pltpu.CompilerParams(
            dimension_semantics=("arbitrary", "arbitrary"), vmem_limit_bytes=VMEM_LIMIT),
        name="rwkv_mixer",
    )(proj3d, proj3d, sh_rkv, sh_lora, s0, mu_rkv, mu_lora, w0, a0, wab, gb, kk, ka, rk, gnw, gnb)
    return o.reshape(m, W), s_new, jnp.concatenate([last_rkv, last_lora], axis=-1)


def _outproj_kernel(x_ref, oa_ref, ob_ref, wa_ref, wb_ref, g_ref, x1_ref, hn_ref):
    x1 = (x_ref[...] + jnp.dot(oa_ref[...], wa_ref[...], preferred_element_type=F32)
          + jnp.dot(ob_ref[...], wb_ref[...], preferred_element_type=F32))
    x1_ref[...] = x1
    ms = jnp.mean(x1 * x1, axis=-1, keepdims=True)
    hn_ref[...] = (x1 * lax.rsqrt(ms + RMS_EPS) * g_ref[...]).astype(BF16)


def _outproj(x2d, oa, ob, wo_bf16, g_row, tm):
    m = x2d.shape[0]
    return pl.pallas_call(
        _outproj_kernel,
        grid=(m // tm,),
        in_specs=[
            pl.BlockSpec((tm, D_MODEL), lambda i: (i, 0)),
            pl.BlockSpec((tm, GDN_WIDTH), lambda i: (i, 0)),
            pl.BlockSpec((tm, RWKV_WIDTH), lambda i: (i, 0)),
            pl.BlockSpec((GDN_WIDTH, D_MODEL), lambda i: (0, 0)),
            pl.BlockSpec((RWKV_WIDTH, D_MODEL), lambda i: (1, 0)),
            pl.BlockSpec((1, D_MODEL), lambda i: (0, 0)),
        ],
        out_specs=[pl.BlockSpec((tm, D_MODEL), lambda i: (i, 0)),
                   pl.BlockSpec((tm, D_MODEL), lambda i: (i, 0))],
        out_shape=[jax.ShapeDtypeStruct((m, D_MODEL), F32),
                   jax.ShapeDtypeStruct((m, D_MODEL), BF16)],
        compiler_params=pltpu.CompilerParams(
            dimension_semantics=("arbitrary",), vmem_limit_bytes=VMEM_LIMIT),
        name="outproj",
    )(x2d, oa, ob, wo_bf16, wo_bf16, g_row)


FFN_TF = 512
FFN_NF = D_FF // FFN_TF


def _ffn_pipeline(f, up_fn, wd_ref, x1_ref, fg_ref, y_ref, acc_ref):
    @pl.when(f == 0)
    def _():
        acc_ref[...] = jnp.zeros(acc_ref.shape, F32)

    acc_ref[...] += jnp.dot(up_fn().astype(BF16), wd_ref[...], preferred_element_type=F32)

    @pl.when(f == FFN_NF - 1)
    def _():
        xo = x1_ref[...] + acc_ref[...]
        ms = jnp.mean(xo * xo, axis=-1, keepdims=True)
        y_ref[...] = xo * lax.rsqrt(ms + RMS_EPS) * fg_ref[...]


def _ffn_long_kernel(hn_ref, x1_ref, wg0_ref, wu0_ref, cwg0_ref, cwu0_ref, wd0_ref,
                     wg1_ref, wu1_ref, cwg1_ref, cwu1_ref, wd1_ref, fg_ref,
                     y_ref, n_ref, acc_ref, carry_ref, hbuf_ref, *, tt):
    ti = pl.program_id(1)
    f = pl.program_id(2)
    steps = pl.num_programs(2)
    hn = hn_ref[...]

    def act(slot, chunk, wg_ref, wu_ref, cwg_ref, cwu_ref):
        convs = []
        for j, (w_ref, cw_ref) in enumerate(((wg_ref, cwg_ref), (wu_ref, cwu_ref))):
            h = jnp.dot(hn, w_ref[...], preferred_element_type=F32)
            hbuf_ref[slot, j, SUBLANES:SUBLANES + tt, :] = h
            prev = carry_ref[chunk, j]
            hbuf_ref[slot, j, 0:SUBLANES, :] = jnp.where(ti == 0, jnp.zeros_like(prev), prev)
            cw = cw_ref[...]
            conv = h * cw[FFN_CONV - 1:FFN_CONV, :]
            for i in range(FFN_CONV - 1):
                off = SUBLANES - (FFN_CONV - 1) + i
                conv = conv + hbuf_ref[slot, j, off:off + tt, :] * cw[i:i + 1, :]
            carry_ref[chunk, j] = h[tt - SUBLANES:, :]
            n_ref[j, :, pl.ds(pl.multiple_of(chunk * FFN_TF, FFN_TF), FFN_TF)] = h[tt - (FFN_CONV - 1):, :]
            convs.append(conv)
        return (_silu(convs[0]) * convs[1]).astype(BF16)

    @pl.when(f == 0)
    def _():
        acc_ref[...] = jnp.zeros(acc_ref.shape, F32)

    def both():
        a0 = act(0, 2 * f, wg0_ref, wu0_ref, cwg0_ref, cwu0_ref)
        a1 = act(1, 2 * f + 1, wg1_ref, wu1_ref, cwg1_ref, cwu1_ref)
        acc_ref[...] += (jnp.dot(a0, wd0_ref[...], preferred_element_type=F32)
                         + jnp.dot(a1, wd1_ref[...], preferred_element_type=F32))

    if FFN_NF % 2 == 0:
        both()
    else:
        pl.when(f < steps - 1)(both)

        @pl.when(f == steps - 1)
        def _():
            a0 = act(0, 2 * f, wg0_ref, wu0_ref, cwg0_ref, cwu0_ref)
            acc_ref[...] += jnp.dot(a0, wd0_ref[...], preferred_element_type=F32)

    @pl.when(f == steps - 1)
    def _():
        xo = x1_ref[...] + acc_ref[...]
        ms = jnp.mean(xo * xo, axis=-1, keepdims=True)
        y_ref[...] = xo * lax.rsqrt(ms + RMS_EPS) * fg_ref[...]


def _ffn_long(hn, x1, wup, cw, wdown, fg_row, tt):
    b, t, _ = hn.shape
    tf = FFN_TF
    nf = FFN_NF
    steps = -(-nf // 2)
    chunk = (lambda f: 2 * f, lambda f: jnp.minimum(2 * f + 1, nf - 1))
    weights = []
    for k in range(2):
        weights += [
            pl.BlockSpec((D_MODEL, tf), lambda i, s, f, k=k: (0, chunk[k](f))),
            pl.BlockSpec((D_MODEL, tf), lambda i, s, f, k=k: (0, nf + chunk[k](f))),
            pl.BlockSpec((FFN_CONV, tf), lambda i, s, f, k=k: (0, chunk[k](f))),
            pl.BlockSpec((FFN_CONV, tf), lambda i, s, f, k=k: (0, nf + chunk[k](f))),
            pl.BlockSpec((tf, D_MODEL), lambda i, s, f, k=k: (chunk[k](f), 0)),
        ]
    return pl.pallas_call(
        functools.partial(_ffn_long_kernel, tt=tt),
        grid=(b, t // tt, steps),
        in_specs=[
            pl.BlockSpec((None, tt, D_MODEL), lambda i, s, f: (i, s, 0)),
            pl.BlockSpec((None, tt, D_MODEL), lambda i, s, f: (i, s, 0)),
        ] + weights + [pl.BlockSpec((1, D_MODEL), lambda i, s, f: (0, 0))],
        out_specs=[
            pl.BlockSpec((None, tt, D_MODEL), lambda i, s, f: (i, s, 0)),
            pl.BlockSpec((None, None, 2, FFN_CONV - 1, D_FF), lambda i, s, f: (i, s, 0, 0, 0)),
        ],
        out_shape=[
            jax.ShapeDtypeStruct((b, t, D_MODEL), F32),
            jax.ShapeDtypeStruct((b, t // tt, 2, FFN_CONV - 1, D_FF), F32),
        ],
        scratch_shapes=[
            pltpu.VMEM((tt, D_MODEL), F32),
            pltpu.VMEM((nf, 2, SUBLANES, tf), F32),
            pltpu.VMEM((2, 2, SUBLANES + tt, tf), F32),
        ],
        compiler_params=pltpu.CompilerParams(
            dimension_semantics=("arbitrary", "arbitrary", "arbitrary"),
            vmem_limit_bytes=VMEM_LIMIT_FFN),
        name="ffn_long",
    )(hn, x1, wup, wup, cw, cw, wdown, wup, wup, cw, cw, wdown, fg_row)


def _ffn_short_kernel(hn_ref, x1_ref, wg_ref, wu_ref, cwg_ref, cwu_ref, wd_ref, fg_ref,
                      b0g_ref, b1g_ref, b0u_ref, b1u_ref,
                      y_ref, ng_ref, nu_ref, acc_ref, z_ref, hb_ref, *, tt, seq):
    f = pl.program_id(1)
    nseq = tt // seq

    def up_fn():
        hn = hn_ref[...]
        t_in_seq = lax.broadcasted_iota(jnp.int32, (tt, FFN_TF), 0) % seq
        convs = []
        groups = ((wg_ref, cwg_ref, b0g_ref, b1g_ref, ng_ref), (wu_ref, cwu_ref, b0u_ref, b1u_ref, nu_ref))
        for w_ref, cw_ref, b0_ref, b1_ref, n_ref in groups:
            h = jnp.dot(hn, w_ref[...], preferred_element_type=F32)
            z_ref[...] = jnp.zeros(z_ref.shape, F32)
            for lb in range(FFN_TF // LANES):
                cols = slice(lb * LANES, (lb + 1) * LANES)
                z_ref[lb, pl.ds(0, nseq, stride=seq), :] = b0_ref[:, cols]
                z_ref[lb, pl.ds(1, nseq, stride=seq), :] = b1_ref[:, cols]
                hb_ref[lb] = h[:, cols]
                n_ref[0, :, cols] = hb_ref[lb, pl.ds(seq - 2, nseq, stride=seq), :]
                n_ref[1, :, cols] = hb_ref[lb, pl.ds(seq - 1, nseq, stride=seq), :]
            z = jnp.concatenate([z_ref[lb] for lb in range(FFN_TF // LANES)], axis=1)
            s1 = jnp.where(t_in_seq == 0, pltpu.roll(z, tt - 1, 0), pltpu.roll(h, 1, 0))
            s2 = jnp.where(t_in_seq < 2, z, pltpu.roll(h, 2, 0))
            cw = cw_ref[...]
            convs.append(h * cw[2:3, :] + s1 * cw[1:2, :] + s2 * cw[0:1, :])
        return _silu(convs[0]) * convs[1]

    _ffn_pipeline(f, up_fn, wd_ref, x1_ref, fg_ref, y_ref, acc_ref)


def _ffn_short(hn, x1, wup, cw, wdown, fg_row, hist, tt, seq):
    m = hn.shape[0]
    tf = FFN_TF
    nf = FFN_NF
    nseq = tt // seq
    up = lambda f: f
    down = lambda f: f
    st = lambda k, col0: pl.BlockSpec((None, nseq, tf), lambda i, f: (k, i, col0 + up(f)))
    new = pl.BlockSpec((2, nseq, tf), lambda i, f: (0, i, up(f)))
    new_shape = jax.ShapeDtypeStruct((2, m // seq, D_FF), F32)
    return pl.pallas_call(
        functools.partial(_ffn_short_kernel, tt=tt, seq=seq),
        grid=(m // tt, nf),
        in_specs=[
            pl.BlockSpec((tt, D_MODEL), lambda i, f: (i, 0)),
            pl.BlockSpec((tt, D_MODEL), lambda i, f: (i, 0)),
            pl.BlockSpec((D_MODEL, tf), lambda i, f: (0, up(f))),
            pl.BlockSpec((D_MODEL, tf), lambda i, f: (0, nf + up(f))),
            pl.BlockSpec((FFN_CONV, tf), lambda i, f: (0, up(f))),
            pl.BlockSpec((FFN_CONV, tf), lambda i, f: (0, nf + up(f))),
            pl.BlockSpec((tf, D_MODEL), lambda i, f: (down(f), 0)),
            pl.BlockSpec((1, D_MODEL), lambda i, f: (0, 0)),
            st(0, 0), st(1, 0), st(0, nf), st(1, nf),
        ],
        out_specs=[pl.BlockSpec((tt, D_MODEL), lambda i, f: (i, 0)), new, new],
        out_shape=[jax.ShapeDtypeStruct((m, D_MODEL), F32), new_shape, new_shape],
        scratch_shapes=[
            pltpu.VMEM((tt, D_MODEL), F32),
            pltpu.VMEM((tf // LANES, tt, LANES), F32),
            pltpu.VMEM((tf // LANES, tt, LANES), F32),
        ],
        compiler_params=pltpu.CompilerParams(
            dimension_semantics=("arbitrary", "arbitrary"), vmem_limit_bytes=VMEM_LIMIT),
        name="ffn_short",
    )(hn, x1, wup, wup, cw, cw, wdown, fg_row, hist, hist, hist, hist)


def _pad_lanes(vec, offset):
    out = jnp.zeros((LANES,), F32)
    return out.at[offset:offset + vec.shape[0]].set(vec.astype(F32))


def _trunk(x, s_gdn, s_gconv, s_rwkv, s_shift, s_ffn, prm, *, long_seq):
    b, t, _ = x.shape
    m = b * t
    x2d = x.reshape(m, D_MODEL)
    tm = min(512, m)
    proj = _inproj(x2d, prm["ln1_g"], prm["w_in"], min(1024, m))

    o_a, gdn_new, gconv_new = _gdn_mixer(proj, s_gconv, s_gdn, prm["gdn_conv_w"], prm["alog_r"],
                                         prm["dtb_r"], prm["alog_c"], prm["dtb_c"], prm["gdn_norm_g"], t)
    o_b, rwkv_new, shift_new = _rwkv_mixer(proj, s_shift, s_shift, s_rwkv, prm["mu_rkv"], prm["mu_lora"],
                                           prm["rwkv_w0"], prm["rwkv_a0"], prm["rwkv_wab"],
                                           prm["rwkv_g_b"], prm["rwkv_k_k"], prm["rwkv_k_a"],
                                           prm["rwkv_r_k"], prm["rwkv_gn_w"], prm["rwkv_gn_b"], t)

    x1, hn = _outproj(x2d, o_a, o_b, prm["w_o"], prm["ln2_g"], tm)
    if long_seq:
        tt = min(512, t)
        y, n_gu = _ffn_long(hn.reshape(b, t, D_MODEL), x1.reshape(b, t, D_MODEL), prm["ffn_w_up"],
                            prm["ffn_conv_w"], prm["ffn_w_down"], prm["final_g"], tt)
        ffn_new = jnp.concatenate([n_gu[:, -1, 0], n_gu[:, -1, 1]], axis=-1)
    else:
        tt = min(512, m)
        y, n_g, n_u = _ffn_short(hn, x1, prm["ffn_w_up"], prm["ffn_conv_w"], prm["ffn_w_down"],
                                 prm["final_g"], jnp.swapaxes(s_ffn, 0, 1), tt, t)
        y = y.reshape(b, t, D_MODEL)
        ffn_new = jnp.swapaxes(jnp.concatenate([n_g, n_u], axis=-1), 0, 1)

    return y, gdn_new[None], gconv_new[None], rwkv_new[None], shift_new[None], ffn_new[None]


def kernel(x_prompt, x_sample, state_gdn, state_gdn_conv, state_rwkv, state_rwkv_shift, state_ffn_conv, ln1_g, w_in, gdn_conv_w, gdn_a_log, gdn_dt_bias, gdn_norm_g, rwkv_mu, rwkv_w0, rwkv_w_b, rwkv_a0, rwkv_a_b, rwkv_g_b, rwkv_k_k, rwkv_k_a, rwkv_r_k, rwkv_gn_w, rwkv_gn_b, w_o, ln2_g, ffn_w_up, ffn_conv_w, ffn_w_down, final_g):
    assert ln1_g.shape[0] == 1, "single-layer trunk"
    w_perm = _permute_win(w_in[0])
    mu = rwkv_mu[0]
    zeros_w = jnp.zeros((RWKV_LORA_W, RWKV_WIDTH), F32)
    wab = jnp.concatenate([
        jnp.concatenate([rwkv_w_b[0], zeros_w], axis=1),
        jnp.concatenate([zeros_w, rwkv_a_b[0]], axis=1)], axis=0).astype(BF16)
    alog = _pad_lanes(gdn_a_log[0], GDN_HEADS)
    dtb = _pad_lanes(gdn_dt_bias[0], GDN_HEADS)
    prm = {
        "ln1_g": ln1_g[0][None], "w_in": w_perm, "gdn_conv_w": gdn_conv_w[0],
        "alog_r": alog[None], "dtb_r": dtb[None], "alog_c": alog[:, None], "dtb_c": dtb[:, None],
        "gdn_norm_g": gdn_norm_g[0][None],
        "mu_rkv": mu[None, :3 * RWKV_WIDTH], "mu_lora": mu[None, 3 * RWKV_WIDTH:],
        "rwkv_w0": rwkv_w0[0][None], "rwkv_a0": rwkv_a0[0][None], "rwkv_wab": wab,
        "rwkv_g_b": rwkv_g_b[0].astype(BF16), "rwkv_k_k": rwkv_k_k[0][None],
        "rwkv_k_a": rwkv_k_a[0][None], "rwkv_r_k": rwkv_r_k[0].reshape(1, RWKV_WIDTH),
        "rwkv_gn_w": rwkv_gn_w[0][None], "rwkv_gn_b": rwkv_gn_b[0][None],
        "w_o": w_o[0].astype(BF16), "ln2_g": ln2_g[0][None],
        "ffn_w_up": ffn_w_up[0].astype(BF16), "ffn_conv_w": ffn_conv_w[0],
        "ffn_w_down": ffn_w_down[0].astype(BF16), "final_g": final_g[None],
    }

    bp = x_prompt.shape[0]
    zero_states = (
        jnp.zeros((bp,) + state_gdn.shape[2:], F32),
        jnp.zeros((bp,) + state_gdn_conv.shape[2:], F32),
        jnp.zeros((bp,) + state_rwkv.shape[2:], F32),
        jnp.zeros((bp,) + state_rwkv_shift.shape[2:], F32),
        None,
    )
    outs_p = _trunk(x_prompt, *zero_states, prm, long_seq=True)
    outs_s = _trunk(x_sample, state_gdn[0], state_gdn_conv[0], state_rwkv[0], state_rwkv_shift[0],
                    state_ffn_conv[0], prm, long_seq=False)
    return (outs_p[0], outs_s[0]) + tuple(outs_p[1:]) + tuple(outs_s[1:])
```

```python
import functools

import jax
import jax.numpy as jnp
from jax import lax
from jax.experimental import pallas as pl
from jax.experimental.pallas import tpu as pltpu

F32 = jnp.float32
BF16 = jnp.bfloat16

D_MODEL = 2048
GDN_WIDTH = 1024
GDN_HEADS = 8
GDN_DK = 128
GDN_CONV = 4
RWKV_WIDTH = 1024
RWKV_HEAD = 64
RWKV_HEADS = 16
RWKV_LORA_W = 64
RWKV_LORA_A = 64
RWKV_LORA_G = 128
RWKV_LORA = RWKV_LORA_W + RWKV_LORA_A + RWKV_LORA_G
RWKV_PROJ = 3 * RWKV_WIDTH + RWKV_LORA
D_FF = 5632
FFN_CONV = 3
RMS_EPS = 1e-6
L2_EPS = 1e-12
GN_EPS = 64e-5

REF_OFF_Z = 3 * GDN_WIDTH
REF_OFF_B = 4 * GDN_WIDTH
REF_OFF_RWKV = REF_OFF_B + 2 * GDN_HEADS
REF_IN_WIDTH = REF_OFF_RWKV + RWKV_PROJ

LANES = 128
SUBLANES = 8
COL_QKV = 0
COL_RKV = 3 * GDN_WIDTH
COL_Z = COL_RKV + 3 * RWKV_WIDTH
COL_LORA = COL_Z + GDN_WIDTH
COL_BA = COL_LORA + RWKV_LORA
PROJ_WIDTH = 7680
PROJ_TN = 1536

MIX_ROWS = 64
MIX_GROUPS_LONG = 4
MIX_GROUPS_SHORT = 2

NN = (((1,), (0,)), ((), ()))
NT = (((1,), (1,)), ((), ()))
TN = (((0,), (0,)), ((), ()))

VMEM_LIMIT = 56 * 1024 * 1024
VMEM_LIMIT_FFN = 62 * 1024 * 1024

P_GRAM = "x1"
P_INV = "x1"
P_SOLVE = "x1"
P_STATE = "x1"
P_OUT = "x1"


def _split(x):
    hi = x.astype(BF16)
    return hi, (x - hi.astype(F32)).astype(BF16)


def _dot(a, b, dims=NN, mode="x1"):
    if mode == "hi":
        return lax.dot_general(a, b, dims, precision=lax.Precision.HIGHEST,
                               preferred_element_type=F32)
    if mode == "x3":
        a_hi, a_lo = _split(a)
        b_hi, b_lo = _split(b)
        d = lambda u, v: lax.dot_general(u, v, dims, preferred_element_type=F32)
        return d(a_hi, b_hi) + (d(a_hi, b_lo) + d(a_lo, b_hi))
    return lax.dot_general(a.astype(BF16), b.astype(BF16), dims, preferred_element_type=F32)


def _ones_dot(ones_mat, x, dims=NN):
    x1, x2 = _split(x)
    m = ones_mat.astype(BF16)
    if dims == NN:
        d = lambda v: lax.dot_general(m, v, dims, preferred_element_type=F32)
    else:
        d = lambda v: lax.dot_general(v, m, dims, preferred_element_type=F32)
    return d(x1) + d(x2)


def _sigmoid(x):
    return 1.0 / (1.0 + jnp.exp(-x))


def _silu(x):
    return x * _sigmoid(x)


def _softplus(x):
    return jnp.maximum(x, 0.0) + jnp.log(1.0 + jnp.exp(-jnp.abs(x)))


def _seq_masks(rows, seq_len):
    r = lax.broadcasted_iota(jnp.int32, (rows, rows), 0)
    c = lax.broadcasted_iota(jnp.int32, (rows, rows), 1)
    if seq_len >= rows:
        return None, r >= c, r > c
    shift = seq_len.bit_length() - 1
    assert 1 << shift == seq_len
    same = jnp.right_shift(r, shift) == jnp.right_shift(c, shift)
    return same, same & (r >= c), same & (r > c)


def _wide_masks(rows, seq_len):
    r = lax.broadcasted_iota(jnp.int32, (rows, 2 * rows), 0)
    c = lax.broadcasted_iota(jnp.int32, (rows, 2 * rows), 1)
    right = c >= rows
    cc = jnp.where(right, c - rows, c)
    if seq_len >= rows:
        return r >= cc, right & (r > cc)
    shift = seq_len.bit_length() - 1
    same = jnp.right_shift(r, shift) == jnp.right_shift(cc, shift)
    return same & (r >= cc), same & right & (r > cc)


def _pair_masks(rows, seq_len):
    half = LANES // 2
    assert rows == half
    lane = lax.broadcasted_iota(jnp.int32, (rows, LANES), 1)
    row = lax.broadcasted_iota(jnp.int32, (rows, LANES), 0)
    col = jnp.bitwise_and(lane, half - 1)
    if seq_len >= rows:
        causal, strict = row >= col, row > col
    else:
        shift = seq_len.bit_length() - 1
        same = jnp.right_shift(row, shift) == jnp.right_shift(col, shift)
        causal, strict = same & (row >= col), same & (row > col)
    return causal, strict, jnp.where(row == col, 1.0, 0.0).astype(F32), lane < half


def _bd(x, first_half):
    return jnp.concatenate([jnp.where(first_half, x, 0.0), jnp.where(first_half, 0.0, x)], axis=0)


def _pair_inverses(neg_a, rows, nilpotency, eye2, first_half):
    qs = dict(neg_a)
    ts = {u: eye2 + q for u, q in qs.items()}
    n = 2
    if n < nilpotency:
        qs = {u: _dot(q, _bd(q, first_half), mode=P_INV) for u, q in qs.items()}
    while n < nilpotency:
        if 2 * n < nilpotency:
            tq = {u: _dot(jnp.concatenate([ts[u], qs[u]], axis=0), _bd(qs[u], first_half), mode=P_INV)
                  for u in qs}
            ts = {u: ts[u] + tq[u][:rows] for u in qs}
            qs = {u: tq[u][rows:] for u in qs}
        else:
            ts = {u: ts[u] + _dot(ts[u], _bd(qs[u], first_half), mode=P_INV) for u in qs}
        n *= 2
    return ts


def _lane_group_sums(x, group):
    tile = 2 * LANES
    shift = group.bit_length() - 1
    li = jnp.right_shift(lax.broadcasted_iota(jnp.int32, (tile, tile), 0), shift)
    lj = jnp.right_shift(lax.broadcasted_iota(jnp.int32, (tile, tile), 1), shift)
    ones = jnp.where(li == lj, 1.0, 0.0).astype(BF16)
    hi, lo = _split(x)
    d = lambda u: lax.dot_general(u, ones, NN, preferred_element_type=F32)
    return jnp.concatenate([d(hi[:, t:t + tile]) + d(lo[:, t:t + tile])
                            for t in range(0, x.shape[1], tile)], axis=1)


def _unit_lower_inverses(mats, rows, nilpotency):
    r = lax.broadcasted_iota(jnp.int32, (rows, rows), 0)
    c = lax.broadcasted_iota(jnp.int32, (rows, rows), 1)
    eye = jnp.where(r == c, 1.0, 0.0).astype(F32)
    qs = [-a for a in mats]
    ts = [eye + q for q in qs]
    n = 2
    if n < nilpotency:
        qs = [_dot(q, q, mode=P_INV) for q in qs]
    while n < nilpotency:
        if 2 * n < nilpotency:
            tq = [_dot(jnp.concatenate([t, q], axis=0), q, mode=P_INV) for t, q in zip(ts, qs)]
            ts = [t + p[:rows] for t, p in zip(ts, tq)]
            qs = [p[rows:] for p in tq]
        else:
            ts = [t + _dot(t, q, mode=P_INV) for t, q in zip(ts, qs)]
        n *= 2
    return ts


def _permute_win_kernel(w_ref, o_ref):
    rw = REF_OFF_RWKV
    rows = w_ref.shape[0]
    cast = lambda lo, hi: w_ref[:, lo:hi].astype(BF16)
    o_ref[:, COL_QKV:COL_RKV] = cast(0, REF_OFF_Z)
    o_ref[:, COL_RKV:COL_Z] = cast(rw, rw + 3 * RWKV_WIDTH)
    o_ref[:, COL_Z:COL_LORA] = cast(REF_OFF_Z, REF_OFF_B)
    o_ref[:, COL_LORA:COL_BA] = cast(rw + 3 * RWKV_WIDTH, REF_IN_WIDTH)
    tail = jnp.concatenate([w_ref[:, REF_OFF_B:REF_OFF_RWKV],
                            jnp.zeros((rows, PROJ_WIDTH - COL_BA - 2 * GDN_HEADS), F32)], axis=1)
    o_ref[:, COL_BA:] = tail.astype(BF16)


def _permute_win(w, tr=256):
    return pl.pallas_call(
        _permute_win_kernel,
        grid=(D_MODEL // tr,),
        in_specs=[pl.BlockSpec((tr, REF_IN_WIDTH), lambda i: (i, 0))],
        out_specs=pl.BlockSpec((tr, PROJ_WIDTH), lambda i: (i, 0)),
        out_shape=jax.ShapeDtypeStruct((D_MODEL, PROJ_WIDTH), BF16),
        compiler_params=pltpu.CompilerParams(
            dimension_semantics=("arbitrary",), vmem_limit_bytes=VMEM_LIMIT),
        name="permute_win",
    )(w)


def _inproj_kernel(x_ref, g_ref, w_hbm, o_hbm, xn_ref, *, tm):
    i = pl.program_id(0)
    x = x_ref[...]
    ms = jnp.mean(x * x, axis=-1, keepdims=True)
    xn_ref[...] = (x * lax.rsqrt(ms + RMS_EPS) * g_ref[...]).astype(BF16)

    def column_tile(w_ref, o_ref):
        o_ref[...] = jnp.dot(xn_ref[...], w_ref[...], preferred_element_type=F32)

    pltpu.emit_pipeline(
        column_tile,
        grid=(PROJ_WIDTH // PROJ_TN,),
        in_specs=[pl.BlockSpec((D_MODEL, PROJ_TN), lambda j: (0, j), pipeline_mode=pl.Buffered(3))],
        out_specs=[pl.BlockSpec((tm, PROJ_TN), lambda j: (i, j))],
    )(w_hbm, o_hbm)


def _inproj(x2d, g_row, w_bf16, tm):
    m = x2d.shape[0]
    return pl.pallas_call(
        functools.partial(_inproj_kernel, tm=tm),
        grid=(m // tm,),
        in_specs=[
            pl.BlockSpec((tm, D_MODEL), lambda i: (i, 0)),
            pl.BlockSpec((1, D_MODEL), lambda i: (0, 0)),
            pl.BlockSpec(memory_space=pl.ANY),
        ],
        out_specs=pl.BlockSpec(memory_space=pl.ANY),
        out_shape=jax.ShapeDtypeStruct((m, PROJ_WIDTH), F32),
        scratch_shapes=[pltpu.VMEM((tm, D_MODEL), BF16)],
        compiler_params=pltpu.CompilerParams(
            dimension_semantics=("arbitrary",), vmem_limit_bytes=VMEM_LIMIT),
        name="inproj",
    )(x2d, g_row, w_bf16)


def _mixer_geometry(seq_len, n_seq, short_groups=1):
    rows = MIX_ROWS
    length = min(seq_len, rows)
    assert rows % length == 0 and seq_len % length == 0 and length % SUBLANES == 0
    per_group = rows // length
    groups = MIX_GROUPS_LONG if per_group == 1 else short_groups
    while n_seq % (groups * per_group):
        groups //= 2
    assert groups >= 1
    return rows, length, per_group, groups, seq_len // length


def _gdn_kernel(qkv_ref, z_ref, ba_ref, cbuf_ref, s0_ref, convw_ref, alog_r_ref, dtb_r_ref,
                alog_c_ref, dtb_c_ref, ng_ref, o_ref, s_ref, tail_ref, xp_ref, *, seq_len, n_seq):
    R, L, G, S, _ = _mixer_geometry(seq_len, n_seq, MIX_GROUPS_SHORT)
    RT = S * R
    c = pl.program_id(1)
    width = 3 * GDN_WIDTH
    hist = GDN_CONV - 1
    cw = convw_ref[...]
    groups = range(S)
    seqs = range(G)

    @pl.when(c == 0)
    def _():
        s_ref[...] = s0_ref[...]

    pieces = []
    if G == 1:
        @pl.when(c == 0)
        def _():
            for s in groups:
                xp_ref[s, 0:SUBLANES, :] = jnp.zeros((SUBLANES, width), F32)
                xp_ref[s, SUBLANES - hist:SUBLANES, :] = cbuf_ref[s]

        @pl.when(c > 0)
        def _():
            for s in groups:
                xp_ref[s, 0:SUBLANES, :] = xp_ref[s, R:R + SUBLANES, :]

        for s in groups:
            xp_ref[s, SUBLANES:SUBLANES + R, :] = qkv_ref[s]
            piece = qkv_ref[s] * cw[hist:hist + 1, :]
            for i in range(hist):
                off = SUBLANES - hist + i
                piece = piece + xp_ref[s, off:off + R, :] * cw[i:i + 1, :]
            pieces.append(piece)
    else:
        for s in groups:
            for g in seqs:
                q = s * G + g
                rows = slice(g * L, (g + 1) * L)
                xp_ref[q, SUBLANES - hist:SUBLANES, :] = cbuf_ref[q]
                xp_ref[q, SUBLANES:SUBLANES + L, :] = qkv_ref[s, rows, :]
                piece = qkv_ref[s, rows, :] * cw[hist:hist + 1, :]
                for i in range(hist):
                    off = SUBLANES - hist + i
                    piece = piece + xp_ref[q, off:off + L, :] * cw[i:i + 1, :]
                pieces.append(piece)
    qkv = _silu(pieces[0] if len(pieces) == 1 else jnp.concatenate(pieces, axis=0))

    same_t, causal_t, _ = _seq_masks(RT, L)
    causal01 = jnp.where(causal_t, 1.0, 0.0).astype(F32)

    ba = ba_ref[...].reshape(RT, LANES)
    ba_t = ba.T
    beta_c = _sigmoid(ba)
    g_c = -jnp.exp(alog_r_ref[...]) * _softplus(ba + dtb_r_ref[...])
    g_r = -jnp.exp(alog_c_ref[...]) * _softplus(ba_t + dtb_c_ref[...])
    gc_all = _ones_dot(causal01, g_c)
    gr_all = _ones_dot(causal01, g_r, NT)
    if same_t is None:
        gtot_all = jnp.broadcast_to(gc_all[RT - 1:RT, :], (RT, LANES))
    else:
        gtot_all = _ones_dot(jnp.where(same_t, 1.0, 0.0).astype(F32), g_c)

    qk_raw = qkv[:, :2 * GDN_WIDTH]
    qk_n = qk_raw * lax.rsqrt(_lane_group_sums(qk_raw * qk_raw, GDN_DK) + L2_EPS)
    q_all = qk_n[:, :GDN_WIDTH] * (GDN_DK ** -0.5)
    k_all = qk_n[:, GDN_WIDTH:]
    v_all = qkv[:, 2 * GDN_WIDTH:]

    causal2, strict2, eye2, first_half = _pair_masks(R, L)
    first_head = lax.broadcasted_iota(jnp.int32, (R, 2 * GDN_DK), 1) < GDN_DK
    zeros_u = jnp.zeros((R, GDN_DK), F32)
    zeros_rhs = jnp.zeros((R, 2 * GDN_DK), F32)
    chains = [(s, h) for s in groups for h in range(GDN_HEADS)]
    units = [(s, p) for s in groups for p in range(GDN_HEADS // 2)]
    seq_rows = [slice(g * L, (g + 1) * L) for g in seqs]
    qs, ks, vs, betas, gcols, gtots = {}, {}, {}, {}, {}, {}
    for s, h in chains:
        rs = slice(s * R, (s + 1) * R)
        lo = h * GDN_DK
        key = (s, h)
        qs[key] = q_all[rs, lo:lo + GDN_DK]
        ks[key] = k_all[rs, lo:lo + GDN_DK]
        vs[key] = v_all[rs, lo:lo + GDN_DK]
        betas[key] = beta_c[rs, h:h + 1]
        gcols[key] = gc_all[rs, GDN_HEADS + h:GDN_HEADS + h + 1]
        gtots[key] = gtot_all[rs, GDN_HEADS + h:GDN_HEADS + h + 1]
    kq2, decay2, a2 = {}, {}, {}
    for s, p in units:
        rs = slice(s * R, (s + 1) * R)
        cols = slice(2 * p * GDN_DK, (2 * p + 2) * GDN_DK)
        k_pair = k_all[rs, cols]
        kt_bd = jnp.concatenate([jnp.where(first_head, k_pair, 0.0),
                                 jnp.where(first_head, 0.0, k_pair)], axis=0).T
        kq = _dot(jnp.concatenate([k_pair, q_all[rs, cols]], axis=0), kt_bd, mode=P_GRAM)
        h0, h1 = (s, 2 * p), (s, 2 * p + 1)
        gcol2 = jnp.where(first_half, gcols[h0], gcols[h1])
        grow2 = jnp.concatenate([gr_all[GDN_HEADS + 2 * p:GDN_HEADS + 2 * p + 1, rs],
                                 gr_all[GDN_HEADS + 2 * p + 1:GDN_HEADS + 2 * p + 2, rs]], axis=1)
        dec = jnp.where(causal2, jnp.exp(jnp.where(causal2, gcol2 - grow2, 0.0)), 0.0)
        beta2 = jnp.where(first_half, betas[h0], betas[h1])
        kq2[(s, p)] = kq
        decay2[(s, p)] = dec
        a2[(s, p)] = jnp.where(strict2, -(beta2 * kq[:R] * dec), 0.0)
    t2 = _pair_inverses(a2, R, L, eye2, first_half)
    gammas = {key: jnp.exp(gcols[key]) for key in chains}

    def stacked(key, x, zeros):
        return jnp.concatenate([x, zeros] if key[1] % 2 == 0 else [zeros, x], axis=0)

    sols = {key: _dot(t2[(key[0], key[1] // 2)],
                      stacked(key, jnp.concatenate([(betas[key] * gammas[key]) * ks[key],
                                                    betas[key] * vs[key]], axis=1), zeros_rhs),
                      mode=P_SOLVE) for key in chains}
    states = {(s, h): [s_ref[s * G + g, h] for g in seqs] for s, h in chains}
    wss = {key: [_dot(jnp.concatenate([sols[key][rows, :GDN_DK], (qs[key] * gammas[key])[rows]], axis=0),
                      states[key][g], mode=P_STATE) for g, rows in enumerate(seq_rows)]
           for key in chains}
    us = {key: jnp.concatenate([sols[key][rows, GDN_DK:] - wss[key][g][:L]
                                for g, rows in enumerate(seq_rows)], axis=0) for key in chains}
    qk2 = {u: kq2[u][R:] * decay2[u] for u in units}
    outs = {key: jnp.concatenate([wss[key][g][L:] for g in seqs], axis=0)
            + _dot(qk2[(key[0], key[1] // 2)], stacked(key, us[key], zeros_u), mode=P_OUT)
            for key in chains}
    for s, h in chains:
        key = (s, h)
        kt = ks[key] * jnp.exp(gtots[key] - gcols[key])
        for g, rows in enumerate(seq_rows):
            gl = jnp.exp(gtots[key][g * L:g * L + 1, :])
            s_ref[s * G + g, h] = gl * states[key][g] + _dot(kt[rows], us[key][rows], TN, mode=P_STATE)
    ng = jnp.concatenate([ng_ref[...]] * GDN_HEADS, axis=1)
    for s in groups:
        o = jnp.concatenate([outs[(s, h)] for h in range(GDN_HEADS)], axis=1)
        ms = _lane_group_sums(o * o, GDN_DK) * (1.0 / GDN_DK)
        o = o * lax.rsqrt(ms + RMS_EPS) * ng
        o_ref[s] = (o * _silu(z_ref[s])).astype(o_ref.dtype)
    for s in groups:
        for g in seqs:
            tail_ref[s * G + g] = qkv_ref[s, (g + 1) * L - hist:(g + 1) * L, :]


def _gdn_mixer(proj2d, cbuf, s0, convw, alog_r, dtb_r, alog_c, dtb_c, ng, seq_len):
    m = proj2d.shape[0]
    nseq = m // seq_len
    R, L, G, S, sps = _mixer_geometry(seq_len, nseq, MIX_GROUPS_SHORT)
    width = 3 * GDN_WIDTH
    proj3d = proj2d.reshape(nseq // G, sps * R, PROJ_WIDTH)
    const2 = lambda i, c: (0, 0)
    rows_map = lambda col: (lambda i, c: (i, c, col))
    per_seq = lambda *dims: pl.BlockSpec((S * G,) + dims, lambda i, c: (i,) + (0,) * len(dims))
    xp_shape = (S, SUBLANES + R, width) if G == 1 else (S * G, SUBLANES + L, width)
    o, s_new, conv_new = pl.pallas_call(
        functools.partial(_gdn_kernel, seq_len=seq_len, n_seq=nseq),
        grid=(nseq // (S * G), sps),
        in_specs=[
            pl.BlockSpec((S, R, width), rows_map(COL_QKV // width)),
            pl.BlockSpec((S, R, GDN_WIDTH), rows_map(COL_Z // GDN_WIDTH)),
            pl.BlockSpec((S, R, LANES), rows_map(COL_BA // LANES)),
            per_seq(GDN_CONV - 1, width),
            per_seq(GDN_HEADS, GDN_DK, GDN_DK),
            pl.BlockSpec((GDN_CONV, width), const2),
            pl.BlockSpec((1, LANES), const2),
            pl.BlockSpec((1, LANES), const2),
            pl.BlockSpec((LANES, 1), const2),
            pl.BlockSpec((LANES, 1), const2),
            pl.BlockSpec((1, GDN_DK), const2),
        ],
        out_specs=[
            pl.BlockSpec((S, R, GDN_WIDTH), rows_map(0)),
            per_seq(GDN_HEADS, GDN_DK, GDN_DK),
            per_seq(GDN_CONV - 1, width),
        ],
        out_shape=[
            jax.ShapeDtypeStruct((nseq // G, sps * R, GDN_WIDTH), BF16),
            jax.ShapeDtypeStruct((nseq, GDN_HEADS, GDN_DK, GDN_DK), F32),
            jax.ShapeDtypeStruct((nseq, GDN_CONV - 1, width), F32),
        ],
        scratch_shapes=[pltpu.VMEM(xp_shape, F32)],
        compiler_params=pltpu.CompilerParams(
            dimension_semantics=("arbitrary", "arbitrary"), vmem_limit_bytes=VMEM_LIMIT),
        name="gdn_mixer",
    )(proj3d, proj3d, proj3d, cbuf, s0, convw, alog_r, dtb_r, alog_c, dtb_c, ng)
    return o.reshape(m, GDN_WIDTH), s_new, conv_new


def _shifted_rows(x_ref, prev_ref, carry_ref, c, R, L, G, S):
    width = x_ref.shape[-1]
    groups = range(S)
    if G == 1:
        @pl.when(c == 0)
        def _():
            for s in groups:
                carry_ref[s, 0:SUBLANES, :] = jnp.zeros((SUBLANES, width), F32)
                carry_ref[s, SUBLANES - 1:SUBLANES, :] = prev_ref[s:s + 1, :]

        @pl.when(c > 0)
        def _():
            for s in groups:
                carry_ref[s, 0:SUBLANES, :] = carry_ref[s, R:R + SUBLANES, :]

        xs, prevs = [], []
        for s in groups:
            carry_ref[s, SUBLANES:SUBLANES + R, :] = x_ref[s]
            xs.append(x_ref[s])
            prevs.append(carry_ref[s, SUBLANES - 1:SUBLANES - 1 + R, :])
    else:
        row = lax.broadcasted_iota(jnp.int32, (L, width), 0)
        xs, prevs = [], []
        for s in groups:
            x = x_ref[s]
            xs.append(x)
            for g in range(G):
                xg = x[g * L:(g + 1) * L]
                q = s * G + g
                prevs.append(jnp.where(row == 0, prev_ref[q:q + 1, :], pltpu.roll(xg, 1, 0)))
    cat = lambda parts: parts[0] if len(parts) == 1 else jnp.concatenate(parts, axis=0)
    return cat(xs), cat(prevs)


def _rwkv_kernel(rkv_ref, lora_ref, sh_rkv_ref, sh_lora_ref, s0_ref, mu_rkv_ref, mu_lora_ref,
                 w0_ref, a0_ref, wab_ref, gb_ref, kk_ref, ka_ref, rk_ref, gnw_ref, gnb_ref,
                 o_ref, s_ref, last_rkv_ref, last_lora_ref, xr_ref, xl_ref, m_ref, *, seq_len, n_seq):
    R, L, G, S, sps = _mixer_geometry(seq_len, n_seq)
    RT = S * R
    NQ = S * G
    c = pl.program_id(1)
    W = RWKV_WIDTH
    HD = RWKV_HEAD
    pairs = range(RWKV_HEADS // 2)
    groups = range(S)
    seqs = range(G)

    @pl.when(c == 0)
    def _():
        zero = jnp.zeros((HD, HD), F32)
        for q in range(NQ):
            for j in pairs:
                vk = jnp.concatenate([jnp.concatenate([s0_ref[q, 2 * j], zero], axis=1),
                                      jnp.concatenate([zero, s0_ref[q, 2 * j + 1]], axis=1)], axis=0)
                m_ref[q, j] = vk.T

    p, p_prev = _shifted_rows(rkv_ref, sh_rkv_ref, xr_ref, c, R, L, G, S)
    xs = p + (p_prev - p) * mu_rkv_ref[...]
    pl_, pl_prev = _shifted_rows(lora_ref, sh_lora_ref, xl_ref, c, R, L, G, S)
    xl = pl_ + (pl_prev - pl_) * mu_lora_ref[...]
    r = xs[:, :W]
    k = xs[:, W:2 * W]
    v = xs[:, 2 * W:]

    wa_in = xl[:, :LANES]
    lane = lax.broadcasted_iota(jnp.int32, (RT, LANES), 1)
    wa_in = jnp.where(lane < RWKV_LORA_W, jnp.tanh(wa_in), wa_in)
    wa = _dot(wa_in, wab_ref[...])
    w = -_softplus(-(w0_ref[...] + wa[:, :W])) - 0.5
    lw = -jnp.exp(w)
    a = _sigmoid(a0_ref[...] + wa[:, W:])
    gate = _dot(_sigmoid(xl[:, LANES:]), gb_ref[...])
    kk_raw = k * kk_ref[...]
    k2 = k * (1.0 + (a - 1.0) * ka_ref[...])

    same_t, causal_t, _ = _seq_masks(RT, L)
    lc = _ones_dot(jnp.where(causal_t, 1.0, 0.0).astype(F32), lw)
    if G == 1:
        ltot = jnp.concatenate([jnp.broadcast_to(lc[(s + 1) * R - 1:(s + 1) * R, :], (R, W))
                                for s in groups], axis=0) if S > 1 else \
            jnp.broadcast_to(lc[R - 1:R, :], (R, W))
    else:
        ltot = _ones_dot(jnp.where(same_t, 1.0, 0.0).astype(F32), lw)
    e_inv = jnp.exp(-lc)
    e_rem = jnp.exp(ltot - lc)

    kk = kk_raw * lax.rsqrt(_lane_group_sums(kk_raw * kk_raw, HD) + L2_EPS)
    kka = kk * a
    ct = kk * jnp.exp(lc - lw)
    rt = r * jnp.exp(lc)
    bh = kka * e_inv
    kh = k2 * e_inv
    bb = kka * e_rem
    kb = k2 * e_rem
    p_rows = [ltot[q * L:q * L + 1] for q in range(NQ)]
    p_rows = p_rows + [p_rows[0]] * (-NQ % SUBLANES)
    pt = jnp.exp(jnp.concatenate(p_rows, axis=0)).T

    causal2, strict2, eye2, even_half = _pair_masks(R, L)
    block_diag = ((lax.broadcasted_iota(jnp.int32, (LANES, LANES), 0) < HD)
                  == (lax.broadcasted_iota(jnp.int32, (LANES, LANES), 1) < HD))
    even_all = jnp.bitwise_and(lax.broadcasted_iota(jnp.int32, (R, W), 1), LANES - 1) < HD
    tile = lambda j: slice(j * LANES, (j + 1) * LANES)
    bd = lambda x: _bd(x, even_half)

    if G > 1:
        row2 = lax.broadcasted_iota(jnp.int32, (2 * R, LANES), 0)
        seq_of_row = jnp.right_shift(jnp.bitwise_and(row2, R - 1), L.bit_length() - 1)

    units = [(s, j) for s in groups for j in pairs]
    xps, vps, bdb, bdk, bkts = {}, {}, {}, {}, {}
    for s in groups:
        rs = slice(s * R, (s + 1) * R)
        x_s = jnp.concatenate([ct[rs], rt[rs]], axis=0)
        bt_s = jnp.concatenate([jnp.where(even_all, bh[rs], 0.0), jnp.where(even_all, 0.0, bh[rs])],
                               axis=0).T
        kt_s = jnp.concatenate([jnp.where(even_all, kh[rs], 0.0), jnp.where(even_all, 0.0, kh[rs])],
                               axis=0).T
        bkt_s = jnp.concatenate([bb[rs], kb[rs]], axis=0).T
        for j in pairs:
            xps[(s, j)] = x_s[:, tile(j)]
            vps[(s, j)] = v[rs, tile(j)]
            bdb[(s, j)] = bt_s[tile(j), :]
            bdk[(s, j)] = kt_s[tile(j), :]
            bkts[(s, j)] = bkt_s[tile(j), :]
    gb = {u: _dot(xps[u], bdb[u], mode=P_GRAM) for u in units}
    gk = {u: _dot(xps[u], bdk[u], mode=P_GRAM) for u in units}
    ms = {(s, j): [m_ref[s * G + g, j] for g in seqs] for s, j in units}
    xm_c, xm_r = {}, {}
    for u in units:
        if G == 1:
            xm = _dot(xps[u], ms[u][0], mode=P_STATE)
            xm_c[u], xm_r[u] = xm[:R], xm[R:]
        else:
            parts = [_dot(jnp.concatenate([xps[u][g * L:(g + 1) * L],
                                           xps[u][R + g * L:R + (g + 1) * L]], axis=0),
                          ms[u][g], mode=P_STATE) for g in seqs]
            xm_c[u] = jnp.concatenate([p_[:L] for p_ in parts], axis=0)
            xm_r[u] = jnp.concatenate([p_[L:] for p_ in parts], axis=0)
    ts = _pair_inverses({u: jnp.where(strict2, -gb[u][:R], 0.0) for u in units}, R, L, eye2, even_half)
    bdv = {u: bd(vps[u]) for u in units}
    akvs = {u: _dot(jnp.where(strict2, gk[u][:R], 0.0), bdv[u], mode=P_OUT) for u in units}
    us = {u: _dot(ts[u], bd(-(xm_c[u] + akvs[u])), mode=P_SOLVE) for u in units}
    uvs = {u: jnp.concatenate([us[u], vps[u]], axis=0) for u in units}
    ys = {u: xm_r[u] + _dot(
        jnp.concatenate([jnp.where(causal2, gb[u][R:], 0.0), jnp.where(causal2, gk[u][R:], 0.0)], axis=1),
        jnp.concatenate([bd(us[u]), bdv[u]], axis=0), mode=P_OUT) for u in units}
    for s, j in units:
        u = (s, j)
        for g in seqs:
            q = s * G + g
            uv_g = uvs[u] if G == 1 else jnp.where(seq_of_row == g, uvs[u], 0.0)
            upd = _dot(bkts[u], uv_g, mode=P_STATE)
            m_ref[q, j] = pt[tile(j), q:q + 1] * ms[u][g] + jnp.where(block_diag, upd, 0.0)

    y_rows = [jnp.concatenate([ys[(s, j)] for j in pairs], axis=1) for s in groups]
    y_all = y_rows[0] if S == 1 else jnp.concatenate(y_rows, axis=0)
    mean = _lane_group_sums(y_all, HD) * (1.0 / HD)
    yc = y_all - mean
    var = _lane_group_sums(yc * yc, HD) * (1.0 / HD)
    yn = yc * lax.rsqrt(var + GN_EPS) * gnw_ref[...] + gnb_ref[...]
    yn = yn + _lane_group_sums(r * k2 * rk_ref[...], HD) * v
    o_ref[...] = (yn * gate).reshape(S, R, W).astype(o_ref.dtype)
    for q in range(NQ):
        last = (q // G) * R + (q % G + 1) * L - 1
        last_rkv_ref[q:q + 1, :] = p[last:last + 1, :]
        last_lora_ref[q:q + 1, :] = pl_[last:last + 1, :]

    @pl.when(c == sps - 1)
    def _():
        for q in range(NQ):
            for j in pairs:
                vk = m_ref[q, j].T
                s_ref[q, 2 * j] = vk[:HD, :HD]
                s_ref[q, 2 * j + 1] = vk[HD:, HD:]


def _rwkv_mixer(proj2d, sh_rkv, sh_lora, s0, mu_rkv, mu_lora, w0, a0, wab, gb, kk, ka, rk, gnw, gnb,
                seq_len):
    m = proj2d.shape[0]
    nseq = m // seq_len
    R, L, G, S, sps = _mixer_geometry(seq_len, nseq)
    W = RWKV_WIDTH
    proj3d = proj2d.reshape(nseq // G, sps * R, PROJ_WIDTH)
    const2 = lambda i, c: (0, 0)
    row = lambda width: pl.BlockSpec((1, width), const2)
    rows_map = lambda col: (lambda i, c: (i, c, col))
    per_seq = lambda *dims: pl.BlockSpec((S * G,) + dims, lambda i, c: (i,) + (0,) * len(dims))
    carry = lambda width: pltpu.VMEM((S, SUBLANES + R, width) if G == 1 else (SUBLANES, LANES), F32)
    o, s_new, last_rkv, last_lora = pl.pallas_call(
        functools.partial(_rwkv_kernel, seq_len=seq_len, n_seq=nseq),
        grid=(nseq // (S * G), sps),
        in_specs=[
            pl.BlockSpec((S, R, 3 * W), rows_map(COL_RKV // (3 * W))),
            pl.BlockSpec((S, R, RWKV_LORA), rows_map(COL_LORA // RWKV_LORA)),
            pl.BlockSpec((S * G, 3 * W), lambda i, c: (i, 0)),
            pl.BlockSpec((S * G, RWKV_LORA), lambda i, c: (i, 3 * W // RWKV_LORA)),
            per_seq(RWKV_HEADS, RWKV_HEAD, RWKV_HEAD),
            row(3 * W), row(RWKV_LORA), row(W), row(W),
            pl.BlockSpec((LANES, 2 * W), const2),
            pl.BlockSpec((RWKV_LORA_G, W), const2),
            row(W), row(W), row(W), row(W), row(W),
        ],
        out_specs=[pl.BlockSpec((S, R, W), rows_map(0)), per_seq(RWKV_HEADS, RWKV_HEAD, RWKV_HEAD),
                   per_seq(3 * W), per_seq(RWKV_LORA)],
        out_shape=[
            jax.ShapeDtypeStruct((nseq // G, sps * R, W), BF16),
            jax.ShapeDtypeStruct((nseq, RWKV_HEADS, RWKV_HEAD, RWKV_HEAD), F32),
            jax.ShapeDtypeStruct((nseq, 3 * W), F32),
            jax.ShapeDtypeStruct((nseq, RWKV_LORA), F32),
        ],
        scratch_shapes=[carry(3 * W), carry(RWKV_LORA),
                        pltpu.VMEM((S * G, RWKV_HEADS // 2, LANES, LANES), F32)],
        compiler_params=pltpu.CompilerParams(
            dimension_semantics=("arbitrary", "arbitrary"), vmem_limit_bytes=VMEM_LIMIT),
        name="rwkv_mixer",
    )(proj3d, proj3d, sh_rkv, sh_lora, s0, mu_rkv, mu_lora, w0, a0, wab, gb, kk, ka, rk, gnw, gnb)
    return o.reshape(m, W), s_new, jnp.concatenate([last_rkv, last_lora], axis=-1)


def _outproj_kernel(x_ref, oa_ref, ob_ref, wa_ref, wb_ref, g_ref, x1_ref, hn_ref):
    x1 = (x_ref[...] + jnp.dot(oa_ref[...], wa_ref[...], preferred_element_type=F32)
          + jnp.dot(ob_ref[...], wb_ref[...], preferred_element_type=F32))
    x1_ref[...] = x1
    ms = jnp.mean(x1 * x1, axis=-1, keepdims=True)
    hn_ref[...] = (x1 * lax.rsqrt(ms + RMS_EPS) * g_ref[...]).astype(BF16)


def _outproj(x2d, oa, ob, wo_bf16, g_row, tm):
    m = x2d.shape[0]
    return pl.pallas_call(
        _outproj_kernel,
        grid=(m // tm,),
        in_specs=[
            pl.BlockSpec((tm, D_MODEL), lambda i: (i, 0)),
            pl.BlockSpec((tm, GDN_WIDTH), lambda i: (i, 0)),
            pl.BlockSpec((tm, RWKV_WIDTH), lambda i: (i, 0)),
            pl.BlockSpec((GDN_WIDTH, D_MODEL), lambda i: (0, 0)),
            pl.BlockSpec((RWKV_WIDTH, D_MODEL), lambda i: (1, 0)),
            pl.BlockSpec((1, D_MODEL), lambda i: (0, 0)),
        ],
        out_specs=[pl.BlockSpec((tm, D_MODEL), lambda i: (i, 0)),
                   pl.BlockSpec((tm, D_MODEL), lambda i: (i, 0))],
        out_shape=[jax.ShapeDtypeStruct((m, D_MODEL), F32),
                   jax.ShapeDtypeStruct((m, D_MODEL), BF16)],
        compiler_params=pltpu.CompilerParams(
            dimension_semantics=("arbitrary",), vmem_limit_bytes=VMEM_LIMIT),
        name="outproj",
    )(x2d, oa, ob, wo_bf16, wo_bf16, g_row)


FFN_TF = 512
FFN_NF = D_FF // FFN_TF


def _ffn_pipeline(f, up_fn, wd_ref, x1_ref, fg_ref, y_ref, acc_ref):
    @pl.when(f == 0)
    def _():
        acc_ref[...] = jnp.zeros(acc_ref.shape, F32)

    acc_ref[...] += jnp.dot(up_fn().astype(BF16), wd_ref[...], preferred_element_type=F32)

    @pl.when(f == FFN_NF - 1)
    def _():
        xo = x1_ref[...] + acc_ref[...]
        ms = jnp.mean(xo * xo, axis=-1, keepdims=True)
        y_ref[...] = xo * lax.rsqrt(ms + RMS_EPS) * fg_ref[...]


def _ffn_long_kernel(hn_ref, x1_ref, wg0_ref, wu0_ref, cwg0_ref, cwu0_ref, wd0_ref,
                     wg1_ref, wu1_ref, cwg1_ref, cwu1_ref, wd1_ref, fg_ref,
                     y_ref, n_ref, acc_ref, carry_ref, hbuf_ref, *, tt):
    ti = pl.program_id(1)
    f = pl.program_id(2)
    steps = pl.num_programs(2)
    hn = hn_ref[...]

    def act(slot, chunk, wg_ref, wu_ref, cwg_ref, cwu_ref):
        convs = []
        for j, (w_ref, cw_ref) in enumerate(((wg_ref, cwg_ref), (wu_ref, cwu_ref))):
            h = jnp.dot(hn, w_ref[...], preferred_element_type=F32)
            hbuf_ref[slot, j, SUBLANES:SUBLANES + tt, :] = h
            prev = carry_ref[chunk, j]
            hbuf_ref[slot, j, 0:SUBLANES, :] = jnp.where(ti == 0, jnp.zeros_like(prev), prev)
            cw = cw_ref[...]
            conv = h * cw[FFN_CONV - 1:FFN_CONV, :]
            for i in range(FFN_CONV - 1):
                off = SUBLANES - (FFN_CONV - 1) + i
                conv = conv + hbuf_ref[slot, j, off:off + tt, :] * cw[i:i + 1, :]
            carry_ref[chunk, j] = h[tt - SUBLANES:, :]
            n_ref[j, :, pl.ds(pl.multiple_of(chunk * FFN_TF, FFN_TF), FFN_TF)] = h[tt - (FFN_CONV - 1):, :]
            convs.append(conv)
        return (_silu(convs[0]) * convs[1]).astype(BF16)

    @pl.when(f == 0)
    def _():
        acc_ref[...] = jnp.zeros(acc_ref.shape, F32)

    def both():
        a0 = act(0, 2 * f, wg0_ref, wu0_ref, cwg0_ref, cwu0_ref)
        a1 = act(1, 2 * f + 1, wg1_ref, wu1_ref, cwg1_ref, cwu1_ref)
        acc_ref[...] += (jnp.dot(a0, wd0_ref[...], preferred_element_type=F32)
                         + jnp.dot(a1, wd1_ref[...], preferred_element_type=F32))

    if FFN_NF % 2 == 0:
        both()
    else:
        pl.when(f < steps - 1)(both)

        @pl.when(f == steps - 1)
        def _():
            a0 = act(0, 2 * f, wg0_ref, wu0_ref, cwg0_ref, cwu0_ref)
            acc_ref[...] += jnp.dot(a0, wd0_ref[...], preferred_element_type=F32)

    @pl.when(f == steps - 1)
    def _():
        xo = x1_ref[...] + acc_ref[...]
        ms = jnp.mean(xo * xo, axis=-1, keepdims=True)
        y_ref[...] = xo * lax.rsqrt(ms + RMS_EPS) * fg_ref[...]


def _ffn_long(hn, x1, wup, cw, wdown, fg_row, tt):
    b, t, _ = hn.shape
    tf = FFN_TF
    nf = FFN_NF
    steps = -(-nf // 2)
    chunk = (lambda f: 2 * f, lambda f: jnp.minimum(2 * f + 1, nf - 1))
    weights = []
    for k in range(2):
        weights += [
            pl.BlockSpec((D_MODEL, tf), lambda i, s, f, k=k: (0, chunk[k](f))),
            pl.BlockSpec((D_MODEL, tf), lambda i, s, f, k=k: (0, nf + chunk[k](f))),
            pl.BlockSpec((FFN_CONV, tf), lambda i, s, f, k=k: (0, chunk[k](f))),
            pl.BlockSpec((FFN_CONV, tf), lambda i, s, f, k=k: (0, nf + chunk[k](f))),
            pl.BlockSpec((tf, D_MODEL), lambda i, s, f, k=k: (chunk[k](f), 0)),
        ]
    return pl.pallas_call(
        functools.partial(_ffn_long_kernel, tt=tt),
        grid=(b, t // tt, steps),
        in_specs=[
            pl.BlockSpec((None, tt, D_MODEL), lambda i, s, f: (i, s, 0)),
            pl.BlockSpec((None, tt, D_MODEL), lambda i, s, f: (i, s, 0)),
        ] + weights + [pl.BlockSpec((1, D_MODEL), lambda i, s, f: (0, 0))],
        out_specs=[
            pl.BlockSpec((None, tt, D_MODEL), lambda i, s, f: (i, s, 0)),
            pl.BlockSpec((None, None, 2, FFN_CONV - 1, D_FF), lambda i, s, f: (i, s, 0, 0, 0)),
        ],
        out_shape=[
            jax.ShapeDtypeStruct((b, t, D_MODEL), F32),
            jax.ShapeDtypeStruct((b, t // tt, 2, FFN_CONV - 1, D_FF), F32),
        ],
        scratch_shapes=[
            pltpu.VMEM((tt, D_MODEL), F32),
            pltpu.VMEM((nf, 2, SUBLANES, tf), F32),
            pltpu.VMEM((2, 2, SUBLANES + tt, tf), F32),
        ],
        compiler_params=pltpu.CompilerParams(
            dimension_semantics=("arbitrary", "arbitrary", "arbitrary"),
            vmem_limit_bytes=VMEM_LIMIT_FFN),
        name="ffn_long",
    )(hn, x1, wup, wup, cw, cw, wdown, wup, wup, cw, cw, wdown, fg_row)


def _ffn_short_kernel(hn_ref, x1_ref, wg_ref, wu_ref, cwg_ref, cwu_ref, wd_ref, fg_ref,
                      b0g_ref, b1g_ref, b0u_ref, b1u_ref,
                      y_ref, ng_ref, nu_ref, acc_ref, z_ref, hb_ref, *, tt, seq):
    f = pl.program_id(1)
    nseq = tt // seq

    def up_fn():
        hn = hn_ref[...]
        t_in_seq = lax.broadcasted_iota(jnp.int32, (tt, FFN_TF), 0) % seq
        convs = []
        groups = ((wg_ref, cwg_ref, b0g_ref, b1g_ref, ng_ref), (wu_ref, cwu_ref, b0u_ref, b1u_ref, nu_ref))
        for w_ref, cw_ref, b0_ref, b1_ref, n_ref in groups:
            h = jnp.dot(hn, w_ref[...], preferred_element_type=F32)
            z_ref[...] = jnp.zeros(z_ref.shape, F32)
            for lb in range(FFN_TF // LANES):
                cols = slice(lb * LANES, (lb + 1) * LANES)
                z_ref[lb, pl.ds(0, nseq, stride=seq), :] = b0_ref[:, cols]
                z_ref[lb, pl.ds(1, nseq, stride=seq), :] = b1_ref[:, cols]
                hb_ref[lb] = h[:, cols]
                n_ref[0, :, cols] = hb_ref[lb, pl.ds(seq - 2, nseq, stride=seq), :]
                n_ref[1, :, cols] = hb_ref[lb, pl.ds(seq - 1, nseq, stride=seq), :]
            z = jnp.concatenate([z_ref[lb] for lb in range(FFN_TF // LANES)], axis=1)
            s1 = jnp.where(t_in_seq == 0, pltpu.roll(z, tt - 1, 0), pltpu.roll(h, 1, 0))
            s2 = jnp.where(t_in_seq < 2, z, pltpu.roll(h, 2, 0))
            cw = cw_ref[...]
            convs.append(h * cw[2:3, :] + s1 * cw[1:2, :] + s2 * cw[0:1, :])
        return _silu(convs[0]) * convs[1]

    _ffn_pipeline(f, up_fn, wd_ref, x1_ref, fg_ref, y_ref, acc_ref)


def _ffn_short(hn, x1, wup, cw, wdown, fg_row, hist, tt, seq):
    m = hn.shape[0]
    tf = FFN_TF
    nf = FFN_NF
    nseq = tt // seq
    up = lambda f: f
    down = lambda f: f
    st = lambda k, col0: pl.BlockSpec((None, nseq, tf), lambda i, f: (k, i, col0 + up(f)))
    new = pl.BlockSpec((2, nseq, tf), lambda i, f: (0, i, up(f)))
    new_shape = jax.ShapeDtypeStruct((2, m // seq, D_FF), F32)
    return pl.pallas_call(
        functools.partial(_ffn_short_kernel, tt=tt, seq=seq),
        grid=(m // tt, nf),
        in_specs=[
            pl.BlockSpec((tt, D_MODEL), lambda i, f: (i, 0)),
            pl.BlockSpec((tt, D_MODEL), lambda i, f: (i, 0)),
            pl.BlockSpec((D_MODEL, tf), lambda i, f: (0, up(f))),
            pl.BlockSpec((D_MODEL, tf), lambda i, f: (0, nf + up(f))),
            pl.BlockSpec((FFN_CONV, tf), lambda i, f: (0, up(f))),
            pl.BlockSpec((FFN_CONV, tf), lambda i, f: (0, nf + up(f))),
            pl.BlockSpec((tf, D_MODEL), lambda i, f: (down(f), 0)),
            pl.BlockSpec((1, D_MODEL), lambda i, f: (0, 0)),
            st(0, 0), st(1, 0), st(0, nf), st(1, nf),
        ],
        out_specs=[pl.BlockSpec((tt, D_MODEL), lambda i, f: (i, 0)), new, new],
        out_shape=[jax.ShapeDtypeStruct((m, D_MODEL), F32), new_shape, new_shape],
        scratch_shapes=[
            pltpu.VMEM((tt, D_MODEL), F32),
            pltpu.VMEM((tf // LANES, tt, LANES), F32),
            pltpu.VMEM((tf // LANES, tt, LANES), F32),
        ],
        compiler_params=pltpu.CompilerParams(
            dimension_semantics=("arbitrary", "arbitrary"), vmem_limit_bytes=VMEM_LIMIT),
        name="ffn_short",
    )(hn, x1, wup, wup, cw, cw, wdown, fg_row, hist, hist, hist, hist)


def _pad_lanes(vec, offset):
    out = jnp.zeros((LANES,), F32)
    return out.at[offset:offset + vec.shape[0]].set(vec.astype(F32))


def _trunk(x, s_gdn, s_gconv, s_rwkv, s_shift, s_ffn, prm, *, long_seq):
    b, t, _ = x.shape
    m = b * t
    x2d = x.reshape(m, D_MODEL)
    tm = min(512, m)
    proj = _inproj(x2d, prm["ln1_g"], prm["w_in"], min(1024, m))

    o_a, gdn_new, gconv_new = _gdn_mixer(proj, s_gconv, s_gdn, prm["gdn_conv_w"], prm["alog_r"],
                                         prm["dtb_r"], prm["alog_c"], prm["dtb_c"], prm["gdn_norm_g"], t)
    o_b, rwkv_new, shift_new = _rwkv_mixer(proj, s_shift, s_shift, s_rwkv, prm["mu_rkv"], prm["mu_lora"],
                                           prm["rwkv_w0"], prm["rwkv_a0"], prm["rwkv_wab"],
                                           prm["rwkv_g_b"], prm["rwkv_k_k"], prm["rwkv_k_a"],
                                           prm["rwkv_r_k"], prm["rwkv_gn_w"], prm["rwkv_gn_b"], t)

    x1, hn = _outproj(x2d, o_a, o_b, prm["w_o"], prm["ln2_g"], tm)
    if long_seq:
        tt = min(512, t)
        y, n_gu = _ffn_long(hn.reshape(b, t, D_MODEL), x1.reshape(b, t, D_MODEL), prm["ffn_w_up"],
                            prm["ffn_conv_w"], prm["ffn_w_down"], prm["final_g"], tt)
        ffn_new = jnp.concatenate([n_gu[:, -1, 0], n_gu[:, -1, 1]], axis=-1)
    else:
        tt = min(512, m)
        y, n_g, n_u = _ffn_short(hn, x1, prm["ffn_w_up"], prm["ffn_conv_w"], prm["ffn_w_down"],
                                 prm["final_g"], jnp.swapaxes(s_ffn, 0, 1), tt, t)
        y = y.reshape(b, t, D_MODEL)
        ffn_new = jnp.swapaxes(jnp.concatenate([n_g, n_u], axis=-1), 0, 1)

    return y, gdn_new[None], gconv_new[None], rwkv_new[None], shift_new[None], ffn_new[None]


def kernel(x_prompt, x_sample, state_gdn, state_gdn_conv, state_rwkv, state_rwkv_shift, state_ffn_conv, ln1_g, w_in, gdn_conv_w, gdn_a_log, gdn_dt_bias, gdn_norm_g, rwkv_mu, rwkv_w0, rwkv_w_b, rwkv_a0, rwkv_a_b, rwkv_g_b, rwkv_k_k, rwkv_k_a, rwkv_r_k, rwkv_gn_w, rwkv_gn_b, w_o, ln2_g, ffn_w_up, ffn_conv_w, ffn_w_down, final_g):
    assert ln1_g.shape[0] == 1, "single-layer trunk"
    w_perm = _permute_win(w_in[0])
    mu = rwkv_mu[0]
    zeros_w = jnp.zeros((RWKV_LORA_W, RWKV_WIDTH), F32)
    wab = jnp.concatenate([
        jnp.concatenate([rwkv_w_b[0], zeros_w], axis=1),
        jnp.concatenate([zeros_w, rwkv_a_b[0]], axis=1)], axis=0).astype(BF16)
    alog = _pad_lanes(gdn_a_log[0], GDN_HEADS)
    dtb = _pad_lanes(gdn_dt_bias[0], GDN_HEADS)
    prm = {
        "ln1_g": ln1_g[0][None], "w_in": w_perm, "gdn_conv_w": gdn_conv_w[0],
        "alog_r": alog[None], "dtb_r": dtb[None], "alog_c": alog[:, None], "dtb_c": dtb[:, None],
        "gdn_norm_g": gdn_norm_g[0][None],
        "mu_rkv": mu[None, :3 * RWKV_WIDTH], "mu_lora": mu[None, 3 * RWKV_WIDTH:],
        "rwkv_w0": rwkv_w0[0][None], "rwkv_a0": rwkv_a0[0][None], "rwkv_wab": wab,
        "rwkv_g_b": rwkv_g_b[0].astype(BF16), "rwkv_k_k": rwkv_k_k[0][None],
        "rwkv_k_a": rwkv_k_a[0][None], "rwkv_r_k": rwkv_r_k[0].reshape(1, RWKV_WIDTH),
        "rwkv_gn_w": rwkv_gn_w[0][None], "rwkv_gn_b": rwkv_gn_b[0][None],
        "w_o": w_o[0].astype(BF16), "ln2_g": ln2_g[0][None],
        "ffn_w_up": ffn_w_up[0].astype(BF16), "ffn_conv_w": ffn_conv_w[0],
        "ffn_w_down": ffn_w_down[0].astype(BF16), "final_g": final_g[None],
    }

    bp = x_prompt.shape[0]
    zero_states = (
        jnp.zeros((bp,) + state_gdn.shape[2:], F32),
        jnp.zeros((bp,) + state_gdn_conv.shape[2:], F32),
        jnp.zeros((bp,) + state_rwkv.shape[2:], F32),
        jnp.zeros((bp,) + state_rwkv_shift.shape[2:], F32),
        None,
    )
    outs_p = _trunk(x_prompt, *zero_states, prm, long_seq=True)
    outs_s = _trunk(x_sample, state_gdn[0], state_gdn_conv[0], state_rwkv[0], state_rwkv_shift[0],
                    state_ffn_conv[0], prm, long_seq=False)
    return (outs_p[0], outs_s[0]) + tuple(outs_p[1:]) + tuple(outs_s[1:])
```
